```python
import math, functools
import jax, jax.numpy as jnp
from jax import lax
import numpy as np


D_MODEL = 1024
BATCH = 8
SEQ = 2048
DEPTH = 1
DEC_BATCH = 128
DEC_SEQ = 1
PAST_LEN = 16384
PAGE_SIZE = 128

N_META = 16
D_MIX = D_MODEL
HG_WIDTH = D_MIX // 2
HG_HEAD_DIM = 128
HG_HEADS = HG_WIDTH // HG_HEAD_DIM
HG_CHUNK = 64
SSM_WIDTH = D_MIX - HG_WIDTH
SSM_HEAD_DIM = 64
SSM_HEADS = SSM_WIDTH // SSM_HEAD_DIM
SSM_GROUPS = 2
SSM_STATE = 128
SSM_CHUNK = 128
CONV_WIDTH = 4
CONV_DIM = SSM_WIDTH + 2 * SSM_GROUPS * SSM_STATE
D_FF = 2816
D_IN_PROJ = 4 * HG_WIDTH + SSM_WIDTH + CONV_DIM + SSM_HEADS
SPLITS = (HG_WIDTH, 2 * HG_WIDTH, 3 * HG_WIDTH, 4 * HG_WIDTH,
          4 * HG_WIDTH + SSM_WIDTH, 4 * HG_WIDTH + SSM_WIDTH + CONV_DIM)
EPS = 1e-6

kernel_name = 'hymba_hgrn2_mamba2_macaron_step'


def rmsnorm(x, w):
    xf = x.astype(jnp.float32)
    y = xf * lax.rsqrt(jnp.mean(xf * xf, axis=-1, keepdims=True) + EPS)
    return (y * w.astype(jnp.float32)).astype(x.dtype)


def swiglu(x, wg, wu, wd):
    return (jax.nn.silu(x @ wg) * (x @ wu)) @ wd


def forget_lower_bound(lb_logits, layer):
    return jnp.cumsum(jax.nn.softmax(lb_logits.astype(jnp.float32), axis=0), axis=0)[layer]


def hgrn2_chunked(q, k, v, logf, s0, chunk):
    bsz, seqlen, nh, _ = q.shape
    dv = v.shape[-1]
    n = seqlen // chunk

    def to_chunks(t):
        return jnp.moveaxis(t.reshape((bsz, n, chunk) + t.shape[2:]), 1, 0)

    causal = jnp.tril(jnp.ones((chunk, chunk), dtype=bool))[None, :, :, None, None]

    def step(s, inp):
        qc, kc, vc, lc = inp
        b = jnp.cumsum(lc, axis=1)
        decay = jnp.exp(jnp.where(causal, b[:, :, None] - b[:, None, :], -jnp.inf))
        scores = jnp.einsum('bthk,bshk,btshk->bhts', qc, kc, decay)
        o = (jnp.einsum('bhts,bshv->bthv', scores, vc)
             + jnp.einsum('bthk,bhkv->bthv', qc * jnp.exp(b), s))
        b_last = b[:, -1]
        s = (s * jnp.exp(b_last)[..., None]
             + jnp.einsum('bshk,bshv->bhkv', kc * jnp.exp(b_last[:, None] - b), vc))
        return s, o

    s, o = lax.scan(step, s0, (to_chunks(q), to_chunks(k), to_chunks(v), to_chunks(logf)))
    return jnp.moveaxis(o, 0, 1).reshape(bsz, seqlen, nh, dv), s


def ssd_chunked(x, dt, bm, cm, h0, chunk, a_neg):
    bsz, seqlen, nh, hp = x.shape
    ng, ns = bm.shape[2], bm.shape[3]
    rep = nh // ng
    n = seqlen // chunk

    def to_chunks(t):
        return jnp.moveaxis(t.reshape((bsz, n, chunk) + t.shape[2:]), 1, 0)

    causal = jnp.tril(jnp.ones((chunk, chunk), dtype=bool))
    a_g = a_neg.reshape(ng, rep)

    def step(h, inp):
        xc, dtc, bc, cc = inp
        cum = jnp.cumsum(dtc * a_g, axis=1)
        cum_t = jnp.moveaxis(cum, 1, -1)
        seg = jnp.exp(jnp.where(causal, cum_t[..., :, None] - cum_t[..., None, :], -jnp.inf))
        cb = jnp.einsum('btgn,bsgn->bgts', cc, bc)
        y = jnp.einsum('bgts,bgrts,bsgr,bsgrp->btgrp', cb, seg, dtc, xc)
        y = y + jnp.einsum('btgn,bgrpn,btgr->btgrp', cc, h, jnp.exp(cum))
        last = cum[:, -1]
        h = (h * jnp.exp(last)[..., None, None]
             + jnp.einsum('bsgn,bsgr,bsgrp->bgrpn', bc, dtc * jnp.exp(last[:, None] - cum), xc))
        return h, y

    xg = x.reshape(bsz, seqlen, ng, rep, hp)
    dtg = dt.reshape(bsz, seqlen, ng, rep)
    h, y = lax.scan(step, h0.reshape(bsz, ng, rep, hp, ns),
                    (to_chunks(xg), to_chunks(dtg), to_chunks(bm), to_chunks(cm)))
    return jnp.moveaxis(y, 0, 1).reshape(bsz, seqlen, nh, hp), h.reshape(bsz, nh, hp, ns)


def run_segments(scan_fn, chunk, segs, state, *seqs):
    outs = []
    off = 0
    for n in segs:
        o, state = scan_fn(*[s[:, off:off + n] for s in seqs], state, math.gcd(n, chunk))
        outs.append(o)
        off += n
    return jnp.concatenate(outs, axis=1), state


def causal_conv(xbc, buf, w, b):
    xp = jnp.concatenate([buf, xbc], axis=1)
    seqlen = xbc.shape[1]
    out = b + xp[:, 0:seqlen] * w[0]
    for j in range(1, CONV_WIDTH):
        out = out + xp[:, j:j + seqlen] * w[j]
    return jax.nn.silu(out), xp[:, xp.shape[1] - (CONV_WIDTH - 1):]


def decoder_layer(h, segs, s_hg, s_ssm, s_conv, lb, n1, w1g, w1u, w1d, nm, w_in, hg_norm,
                  conv_w, conv_b, dt_bias, a_log, d_skip, ssm_norm, w_out, n2, w2g, w2u, w2d):
    f32 = jnp.float32
    bsz, seqlen, _ = h.shape
    h = h + 0.5 * swiglu(rmsnorm(h, n1), w1g, w1u, w1d)
    u = rmsnorm(h, nm) @ w_in
    q, fz, iv, g, z, xbc, dt_raw = jnp.split(u, SPLITS, axis=-1)

    fz = fz.astype(f32)
    logf = jnp.log(lb + (1.0 - lb) * jax.nn.sigmoid(fz))
    kk = (1.0 - lb) * jax.nn.sigmoid(-fz)
    hg_shape = (bsz, seqlen, HG_HEADS, HG_HEAD_DIM)
    o_hg, s_hg = run_segments(hgrn2_chunked, HG_CHUNK, segs, s_hg.astype(f32),
                              jax.nn.silu(q.astype(f32)).reshape(hg_shape), kk.reshape(hg_shape),
                              iv.astype(f32).reshape(hg_shape), logf.reshape(hg_shape))
    o_hg = o_hg * lax.rsqrt(jnp.mean(o_hg * o_hg, axis=-1, keepdims=True) + EPS)
    o_hg = o_hg.reshape(bsz, seqlen, HG_WIDTH) * hg_norm.astype(f32) * jax.nn.silu(g.astype(f32))

    xbc_act, conv_buf = causal_conv(xbc.astype(f32), s_conv.astype(f32), conv_w.astype(f32), conv_b.astype(f32))
    xs, bm, cm = jnp.split(xbc_act, (SSM_WIDTH, SSM_WIDTH + SSM_GROUPS * SSM_STATE), axis=-1)
    xs = xs.reshape(bsz, seqlen, SSM_HEADS, SSM_HEAD_DIM)
    bm = bm.reshape(bsz, seqlen, SSM_GROUPS, SSM_STATE)
    cm = cm.reshape(bsz, seqlen, SSM_GROUPS, SSM_STATE)
    dt = jax.nn.softplus(dt_raw.astype(f32) + dt_bias.astype(f32))
    a_neg = -jnp.exp(a_log.astype(f32))
    y, s_ssm = run_segments(functools.partial(ssd_chunked, a_neg=a_neg), SSM_CHUNK, segs,
                            s_ssm.astype(f32), xs, dt, bm, cm)
    y = y + d_skip.astype(f32)[:, None] * xs
    yz = (y.reshape(bsz, seqlen, SSM_WIDTH) * jax.nn.silu(z.astype(f32))).reshape(bsz, seqlen, SSM_GROUPS, -1)
    yz = yz * lax.rsqrt(jnp.mean(yz * yz, axis=-1, keepdims=True) + EPS)
    yz = yz.reshape(bsz, seqlen, SSM_WIDTH) * ssm_norm.astype(f32)

    mix = jnp.concatenate([o_hg, yz], axis=-1).astype(h.dtype) @ w_out
    h = h + mix
    h = h + 0.5 * swiglu(rmsnorm(h, n2), w2g, w2u, w2d)
    return h, s_hg, s_ssm, conv_buf


def setup_inputs(seed: int = 0) -> dict:
    key = jax.random.key(seed)
    ks = jax.random.split(key, 32)
    f32 = jnp.float32

    def nrm(k, shape, scale):
        return scale * jax.random.normal(k, shape, f32)

    L = DEPTH
    dt0 = jnp.exp(jax.random.uniform(ks[20], (L, SSM_HEADS), f32, math.log(1e-3), math.log(1e-1)))
    return {
        'x_prompt': nrm(ks[0], (BATCH, SEQ, D_MODEL), 1.0),
        'x_sample': nrm(ks[1], (DEC_BATCH, DEC_SEQ, D_MODEL), 1.0),
        'state_hgrn': nrm(ks[2], (L, DEC_BATCH, HG_HEADS, HG_HEAD_DIM, HG_HEAD_DIM), 0.3),
        'state_ssm': nrm(ks[3], (L, DEC_BATCH, SSM_HEADS, SSM_HEAD_DIM, SSM_STATE), 0.1),
        'state_conv': nrm(ks[4], (L, DEC_BATCH, CONV_WIDTH - 1, CONV_DIM), 1.0),
        'meta_tokens': nrm(ks[5], (N_META, D_MODEL), 1.0),
        'lb_logits': nrm(ks[6], (L + 1, HG_WIDTH), 0.5),
        'norm_ffn1': 1.0 + nrm(ks[7], (L, D_MODEL), 0.02),
        'w_ffn1_gate': nrm(ks[8], (L, D_MODEL, D_FF), D_MODEL ** -0.5),
        'w_ffn1_up': nrm(ks[9], (L, D_MODEL, D_FF), D_MODEL ** -0.5),
        'w_ffn1_down': nrm(ks[10], (L, D_FF, D_MODEL), D_FF ** -0.5),
        'norm_mix': 1.0 + nrm(ks[11], (L, D_MODEL), 0.02),
        'w_in': nrm(ks[12], (L, D_MODEL, D_IN_PROJ), D_MODEL ** -0.5),
        'hg_norm': 1.0 + nrm(ks[13], (L, HG_WIDTH), 0.02),
        'conv_w': nrm(ks[14], (L, CONV_WIDTH, CONV_DIM), CONV_WIDTH ** -0.5),
        'conv_b': nrm(ks[15], (L, CONV_DIM), 0.01),
        'dt_bias': dt0 + jnp.log(-jnp.expm1(-dt0)),
        'a_log': jnp.log(jax.random.uniform(ks[21], (L, SSM_HEADS), f32, 1.0, 16.0)),
        'd_skip': 1.0 + nrm(ks[16], (L, SSM_HEADS), 0.1),
        'ssm_norm': 1.0 + nrm(ks[17], (L, SSM_WIDTH), 0.02),
        'w_out': nrm(ks[18], (L, D_MIX, D_MODEL), D_MIX ** -0.5),
        'norm_ffn2': 1.0 + nrm(ks[19], (L, D_MODEL), 0.02),
        'w_ffn2_gate': nrm(ks[22], (L, D_MODEL, D_FF), D_MODEL ** -0.5),
        'w_ffn2_up': nrm(ks[23], (L, D_MODEL, D_FF), D_MODEL ** -0.5),
        'w_ffn2_down': nrm(ks[24], (L, D_FF, D_MODEL), D_FF ** -0.5),
        'norm_final': 1.0 + nrm(ks[25], (D_MODEL,), 0.02),
    }


def reference(x_prompt, x_sample, state_hgrn, state_ssm, state_conv, meta_tokens, lb_logits,
              norm_ffn1, w_ffn1_gate, w_ffn1_up, w_ffn1_down, norm_mix, w_in, hg_norm, conv_w, conv_b,
              dt_bias, a_log, d_skip, ssm_norm, w_out, norm_ffn2, w_ffn2_gate, w_ffn2_up, w_ffn2_down,
              norm_final):
    f32 = jnp.float32
    bp, seq_p, _ = x_prompt.shape
    seq_s = x_sample.shape[1]
    meta = jnp.broadcast_to(meta_tokens.astype(x_prompt.dtype)[None], (bp, N_META, D_MODEL))
    hp = jnp.concatenate([meta, x_prompt], axis=1)
    hs = x_sample
    hg_p, ssm_p, conv_p, hg_s, ssm_s, conv_s = [], [], [], [], [], []
    for l in range(DEPTH):
        lb = forget_lower_bound(lb_logits, l)
        w = (norm_ffn1[l], w_ffn1_gate[l], w_ffn1_up[l], w_ffn1_down[l], norm_mix[l], w_in[l], hg_norm[l],
             conv_w[l], conv_b[l], dt_bias[l], a_log[l], d_skip[l], ssm_norm[l], w_out[l],
             norm_ffn2[l], w_ffn2_gate[l], w_ffn2_up[l], w_ffn2_down[l])
        hp, a, b, c = decoder_layer(
            hp, (N_META, seq_p),
            jnp.zeros((bp, HG_HEADS, HG_HEAD_DIM, HG_HEAD_DIM), f32),
            jnp.zeros((bp, SSM_HEADS, SSM_HEAD_DIM, SSM_STATE), f32),
            jnp.zeros((bp, CONV_WIDTH - 1, CONV_DIM), f32),
            lb, *w)
        hg_p.append(a)
        ssm_p.append(b)
        conv_p.append(c)
        hs, a, b, c = decoder_layer(hs, (seq_s,), state_hgrn[l], state_ssm[l], state_conv[l], lb, *w)
        hg_s.append(a)
        ssm_s.append(b)
        conv_s.append(c)
    y_prompt = rmsnorm(hp[:, N_META:], norm_final)
    y_sample = rmsnorm(hs, norm_final)
    dt_out = x_prompt.dtype
    hgrn_prompt = jnp.stack(hg_p).astype(dt_out)
    ssm_prompt = jnp.stack(ssm_p).astype(dt_out)
    conv_prompt = jnp.stack(conv_p).astype(dt_out)
    hgrn_sample = jnp.stack(hg_s).astype(dt_out)
    ssm_sample = jnp.stack(ssm_s).astype(dt_out)
    conv_sample = jnp.stack(conv_s).astype(dt_out)
    return (y_prompt, y_sample, hgrn_prompt, ssm_prompt, conv_prompt, hgrn_sample, ssm_sample, conv_sample)
```

```python
import functools

import jax
import jax.numpy as jnp
from jax import lax
from jax.experimental import pallas as pl
from jax.experimental.pallas import tpu as pltpu

F32 = jnp.float32
BF16 = jnp.bfloat16

D_MODEL = 1024
D_FF = 2816
N_META = 16
HG_WIDTH = 512
HG_HEADS = 4
HG_DIM = 128
SSM_WIDTH = 512
SSM_HEADS = 8
SSM_HEAD_DIM = 64
SSM_GROUPS = 2
SSM_STATE = 128
CONV_WIDTH = 4
CONV_DIM = SSM_WIDTH + 2 * SSM_GROUPS * SSM_STATE
EPS = 1e-6

LANES = 128
SUBLANES = 8
VMEM_LIMIT_BYTES = 56 * 1024 * 1024

COL_Q = 0
COL_F = HG_WIDTH
COL_I = 2 * HG_WIDTH
COL_G = 3 * HG_WIDTH
COL_Z = 4 * HG_WIDTH
COL_XBC = COL_Z + SSM_WIDTH
COL_DT = COL_XBC + CONV_DIM
D_IN_PAD = COL_DT + LANES

FF_TILE = 256
HG_CHUNK = 64
SSM_CHUNK = 128
PAD_ROWS = SUBLANES


def _dot(a, b):
    return jnp.dot(a, b, preferred_element_type=F32)


def _dot_nt(a, b):
    return lax.dot_general(a, b, (((1,), (1,)), ((), ())), preferred_element_type=F32)


def _dot_tn(a, b):
    return lax.dot_general(a, b, (((0,), (0,)), ((), ())), preferred_element_type=F32)


def _rms(x, w):
    return x * lax.rsqrt(jnp.mean(x * x, axis=-1, keepdims=True) + EPS) * w


def _silu(x):
    return x * jax.nn.sigmoid(x)


def _swiglu(xn, wg_ref, wu_ref, wd_ref):
    acc = jnp.zeros((xn.shape[0], D_MODEL), F32)
    for j in range(D_FF // FF_TILE):
        cols = slice(j * FF_TILE, (j + 1) * FF_TILE)
        g = _dot(xn, wg_ref[:, cols])
        u = _dot(xn, wu_ref[:, cols])
        acc = acc + _dot((_silu(g) * u).astype(BF16), wd_ref[cols, :])
    return acc


def _cumsum_rows(tri, a):
    a1 = a.astype(BF16)
    r1 = a - a1.astype(F32)
    a2 = r1.astype(BF16)
    a3 = (r1 - a2.astype(F32)).astype(BF16)
    return _dot(tri, a1) + _dot(tri, a2) + _dot(tri, a3)


def _lower_tri(n):
    row = lax.broadcasted_iota(jnp.int32, (n, n), 0)
    col = lax.broadcasted_iota(jnp.int32, (n, n), 1)
    return row >= col


def _forget_lower_bound(lbl):
    l0, l1 = lbl[0:1], lbl[1:2]
    m = jnp.maximum(l0, l1)
    e0, e1 = jnp.exp(l0 - m), jnp.exp(l1 - m)
    return e0 / (e0 + e1)


def _ffn_in_kernel(x_ref, n1_ref, wg_ref, wu_ref, wd_ref, nm_ref, win_ref, h1_ref, u_ref):
    x = x_ref[...]
    xn = _rms(x, n1_ref[...]).astype(BF16)
    h1 = x + 0.5 * _swiglu(xn, wg_ref, wu_ref, wd_ref)
    h1_ref[...] = h1
    hn = _rms(h1, nm_ref[...]).astype(BF16)
    u_ref[...] = _dot(hn, win_ref[...])


def _resident(shape):
    nd = len(shape)
    return pl.BlockSpec(shape, lambda *_: (0,) * nd, pipeline_mode=pl.Buffered(1))


def _ffn_in(x, n1, wg, wu, wd, nm, win, tm):
    n = x.shape[0]
    return pl.pallas_call(
        _ffn_in_kernel,
        grid=(n // tm,),
        in_specs=[pl.BlockSpec((tm, D_MODEL), lambda i: (i, 0)),
                  _resident(n1.shape), _resident(wg.shape), _resident(wu.shape), _resident(wd.shape),
                  _resident(nm.shape), _resident(win.shape)],
        out_specs=[pl.BlockSpec((tm, D_MODEL), lambda i: (i, 0)),
                   pl.BlockSpec((tm, D_IN_PAD), lambda i: (i, 0))],
        out_shape=[jax.ShapeDtypeStruct((n, D_MODEL), F32),
                   jax.ShapeDtypeStruct((n, D_IN_PAD), F32)],
        compiler_params=pltpu.CompilerParams(dimension_semantics=("arbitrary",),
                                             vmem_limit_bytes=VMEM_LIMIT_BYTES),
        name="ffn_in",
    )(x, n1, wg, wu, wd, nm, win)


def _ffn_out_kernel(h1_ref, mix_ref, wo_ref, n2_ref, wg_ref, wu_ref, wd_ref, nf_ref, y_ref):
    h2 = h1_ref[...] + _dot(mix_ref[...].astype(BF16), wo_ref[...])
    hn = _rms(h2, n2_ref[...]).astype(BF16)
    h3 = h2 + 0.5 * _swiglu(hn, wg_ref, wu_ref, wd_ref)
    y_ref[...] = _rms(h3, nf_ref[...])


def _ffn_out(h1, h1_block0, mix, wo, n2, wg, wu, wd, nf, tm):
    n = mix.shape[0]
    return pl.pallas_call(
        _ffn_out_kernel,
        grid=(n // tm,),
        in_specs=[pl.BlockSpec((tm, D_MODEL), lambda i: (i + h1_block0, 0)),
                  pl.BlockSpec((tm, D_MODEL), lambda i: (i, 0)),
                  _resident(wo.shape), _resident(n2.shape), _resident(wg.shape), _resident(wu.shape),
                  _resident(wd.shape), _resident(nf.shape)],
        out_specs=pl.BlockSpec((tm, D_MODEL), lambda i: (i, 0)),
        out_shape=jax.ShapeDtypeStruct((n, D_MODEL), F32),
        compiler_params=pltpu.CompilerParams(dimension_semantics=("arbitrary",),
                                             vmem_limit_bytes=VMEM_LIMIT_BYTES),
        name="ffn_out",
    )(h1, mix, wo, n2, wg, wu, wd, nf)


def _hgrn_chunk(r0, n_pad, u_ref, lb, hgn, mix_ref, st_scr):
    c = HG_CHUNK
    rows = slice(r0, r0 + c)
    fz = u_ref[rows, COL_F:COL_F + HG_WIDTH]
    logf = jnp.log(lb + (1.0 - lb) * jax.nn.sigmoid(fz))
    kk = (1.0 - lb) * jax.nn.sigmoid(-fz)
    if n_pad > r0:
        valid = lax.broadcasted_iota(jnp.int32, (c, HG_WIDTH), 0) >= (n_pad - r0)
        logf = jnp.where(valid, logf, 0.0)
        kk = jnp.where(valid, kk, 0.0)
    q = _silu(u_ref[rows, COL_Q:COL_Q + HG_WIDTH])
    v = u_ref[rows, COL_I:COL_I + HG_WIDTH].astype(BF16)
    gate = _silu(u_ref[rows, COL_G:COL_G + HG_WIDTH])

    causal = _lower_tri(c)
    b = _cumsum_rows(causal.astype(BF16), logf)
    b_last = b[c - 1:c, :]
    qt = (q * jnp.exp(b)).astype(BF16)
    kt = (kk * jnp.exp(-b)).astype(BF16)
    kh = (kk * jnp.exp(b_last - b)).astype(BF16)
    decay = jnp.exp(b_last)

    for h in range(HG_HEADS):
        sl = slice(h * HG_DIM, (h + 1) * HG_DIM)
        st = st_scr[h]
        scores = jnp.where(causal, _dot_nt(qt[:, sl], kt[:, sl]), 0.0).astype(BF16)
        o = _dot(scores, v[:, sl]) + _dot_nt(qt[:, sl], st.astype(BF16))
        st_scr[h] = st * decay[:, sl] + _dot_tn(v[:, sl], kh[:, sl])
        o = o * lax.rsqrt(jnp.mean(o * o, axis=-1, keepdims=True) + EPS)
        mix_ref[rows, sl] = (o * hgn[:, sl] * gate[:, sl]).astype(mix_ref.dtype)


def _ssd_chunk(r0, n_pad, u_ref, cw, cb, dtb, a_neg, dsk, ssn, mix_ref, hs_scr, xpad, ybuf):
    c = SSM_CHUNK
    rows = slice(r0, r0 + c)
    conv = cb
    for j in range(CONV_WIDTH):
        off = PAD_ROWS - (CONV_WIDTH - 1) + j + r0
        conv = conv + cw[j:j + 1, :] * xpad[off:off + c, :]
    act = _silu(conv)
    xs = act[:, 0:SSM_WIDTH]
    bm = act[:, SSM_WIDTH:SSM_WIDTH + SSM_GROUPS * SSM_STATE].astype(BF16)
    cm = act[:, SSM_WIDTH + SSM_GROUPS * SSM_STATE:].astype(BF16)

    dt = jax.nn.softplus(u_ref[rows, COL_DT:COL_DT + LANES] + dtb)
    if n_pad > r0:
        dt = jnp.where(lax.broadcasted_iota(jnp.int32, (c, LANES), 0) >= (n_pad - r0), dt, 0.0)
    causal = _lower_tri(c)
    cum = _cumsum_rows(causal.astype(BF16), dt * a_neg)
    last = cum[c - 1:c, :]
    cum_t = cum.T
    dt_t = dt.T
    e_cum = jnp.exp(cum)
    w_in = dt * jnp.exp(last - cum)
    e_last = jnp.exp(last)

    for g in range(SSM_GROUPS):
        bg = bm[:, g * SSM_STATE:(g + 1) * SSM_STATE]
        cg = cm[:, g * SSM_STATE:(g + 1) * SSM_STATE]
        cbt = _dot_nt(cg, bg)
        for rr in range(SSM_HEADS // SSM_GROUPS):
            r = g * (SSM_HEADS // SSM_GROUPS) + rr
            hsl = slice(r * SSM_HEAD_DIM, (r + 1) * SSM_HEAD_DIM)
            seg = jnp.exp(jnp.where(causal, cum[:, r:r + 1] - cum_t[r:r + 1, :], -jnp.inf))
            m = (cbt * seg * dt_t[r:r + 1, :]).astype(BF16)
            xr = xs[:, hsl]
            hr = hs_scr[r]
            y = _dot(m, xr.astype(BF16)) + _dot_nt(cg, hr.astype(BF16)) * e_cum[:, r:r + 1]
            xw = (xr * w_in[:, r:r + 1]).astype(BF16)
            hs_scr[r] = hr * e_last[:, r:r + 1] + _dot_tn(xw, bg)
            ybuf[rows, hsl] = y + dsk[:, hsl] * xr

    yz = ybuf[rows, :] * _silu(u_ref[rows, COL_Z:COL_Z + SSM_WIDTH])
    gw = SSM_WIDTH // SSM_GROUPS
    for g in range(SSM_GROUPS):
        sl = slice(g * gw, (g + 1) * gw)
        seg = yz[:, sl]
        seg = seg * lax.rsqrt(jnp.mean(seg * seg, axis=-1, keepdims=True) + EPS) * ssn[:, sl]
        mix_ref[rows, HG_WIDTH + g * gw:HG_WIDTH + (g + 1) * gw] = seg.astype(mix_ref.dtype)


def _mixer_kernel(n_pad, tb, u_ref, lbl_ref, hgn_ref, cw_ref, cb_ref, dtb_ref, alog_ref, dsk_ref, ssn_ref,
                  s0_ref, h0_ref, c0_ref, mix_ref, s_out_ref, h_out_ref, c_out_ref,
                  st_scr, hs_scr, xpad, ybuf):
    t = pl.program_id(1)

    @pl.when(t == 0)
    def _load_state():
        for h in range(HG_HEADS):
            st_scr[h] = s0_ref[0, h].T
        hs_scr[...] = h0_ref[0]
        xpad[0:PAD_ROWS, :] = c0_ref[0]

    xbc = u_ref[:, COL_XBC:COL_XBC + CONV_DIM]
    if n_pad:
        xbc = jnp.where(lax.broadcasted_iota(jnp.int32, (tb, CONV_DIM), 0) >= n_pad, xbc, 0.0)
    xpad[PAD_ROWS:PAD_ROWS + tb, :] = xbc

    lb = _forget_lower_bound(lbl_ref[...])
    hgn = hgn_ref[...]
    for r0 in range(0, tb, HG_CHUNK):
        _hgrn_chunk(r0, n_pad, u_ref, lb, hgn, mix_ref, st_scr)

    a_neg = -jnp.exp(alog_ref[...])
    for r0 in range(0, tb, SSM_CHUNK):
        _ssd_chunk(r0, n_pad, u_ref, cw_ref[...], cb_ref[...], dtb_ref[...], a_neg, dsk_ref[...], ssn_ref[...],
                   mix_ref, hs_scr, xpad, ybuf)

    xpad[0:PAD_ROWS, :] = xpad[tb:tb + PAD_ROWS, :]

    @pl.when(t == pl.num_programs(1) - 1)
    def _store_state():
        for h in range(HG_HEADS):
            s_out_ref[0, h] = st_scr[h].T
        h_out_ref[0] = hs_scr[...]
        c_out_ref[0] = xpad[0:PAD_ROWS, :]


def _mixer(u, u_block0, nb, nt, tb, n_pad, params, s0, h0, c0):
    lbl, hgn, cw, cb, dtb, alog, dsk, ssn = params
    small = [_resident(p.shape) for p in params]
    shared = lambda shape: pl.BlockSpec((1,) + shape[1:], lambda b, t: (0,) * len(shape))
    per_seq = lambda shape: pl.BlockSpec((1,) + shape, lambda b, t: (b,) + (0,) * len(shape))
    s_shape = (HG_HEADS, HG_DIM, HG_DIM)
    h_shape = (SSM_HEADS, SSM_HEAD_DIM, SSM_STATE)
    c_shape = (PAD_ROWS, CONV_DIM)
    return pl.pallas_call(
        functools.partial(_mixer_kernel, n_pad, tb),
        grid=(nb, nt),
        in_specs=[pl.BlockSpec((tb, D_IN_PAD), lambda b, t: (u_block0 + b * nt + t, 0))] + small
        + [shared(s0.shape), shared(h0.shape), shared(c0.shape)],
        out_specs=[pl.BlockSpec((tb, D_MODEL), lambda b, t: (b * nt + t, 0)),
                   per_seq(s_shape), per_seq(h_shape), per_seq(c_shape)],
        out_shape=[jax.ShapeDtypeStruct((nb * nt * tb, D_MODEL), BF16),
                   jax.ShapeDtypeStruct((nb,) + s_shape, F32),
                   jax.ShapeDtypeStruct((nb,) + h_shape, F32),
                   jax.ShapeDtypeStruct((nb,) + c_shape, F32)],
        scratch_shapes=[pltpu.VMEM(s_shape, F32), pltpu.VMEM(h_shape, F32),
                        pltpu.VMEM((PAD_ROWS + tb, CONV_DIM), F32), pltpu.VMEM((tb, SSM_WIDTH), F32)],
        compiler_params=pltpu.CompilerParams(dimension_semantics=("arbitrary", "arbitrary"),
                                             vmem_limit_bytes=VMEM_LIMIT_BYTES),
        name="mixer",
    )(u, *params, s0, h0, c0)


def _column_tile(row):
    return jnp.broadcast_to(row, (LANES, LANES)).T


def _sample_kernel(nbs, u_ref, lbl_ref, hgn_ref, cw_ref, cb_ref, dtb_ref, alog_ref, dsk_ref, ssn_ref,
                   sh_ref, ss_ref, sc_ref, mix_ref, sh_out_ref, ss_out_ref, sc_out_ref, obuf, ybuf):
    lb = _forget_lower_bound(lbl_ref[...])
    fz = u_ref[:, COL_F:COL_F + HG_WIDTH]
    f = lb + (1.0 - lb) * jax.nn.sigmoid(fz)
    kk = (1.0 - lb) * jax.nn.sigmoid(-fz)
    q = _silu(u_ref[:, COL_Q:COL_Q + HG_WIDTH])
    v = u_ref[:, COL_I:COL_I + HG_WIDTH]

    xbc = u_ref[:, COL_XBC:COL_XBC + CONV_DIM]
    cw = cw_ref[...]
    conv = cb_ref[...] + cw[CONV_WIDTH - 1:CONV_WIDTH, :] * xbc
    for j in range(CONV_WIDTH - 1):
        conv = conv + cw[j:j + 1, :] * sc_ref[j]
    for j in range(CONV_WIDTH - 2):
        sc_out_ref[j] = sc_ref[j + 1]
    sc_out_ref[CONV_WIDTH - 2] = xbc
    act = _silu(conv)
    xs = act[:, 0:SSM_WIDTH]
    bm = act[:, SSM_WIDTH:SSM_WIDTH + SSM_GROUPS * SSM_STATE]
    cm = act[:, SSM_WIDTH + SSM_GROUPS * SSM_STATE:]
    dt = jax.nn.softplus(u_ref[:, COL_DT:COL_DT + LANES] + dtb_ref[...])
    d_a = jnp.exp(dt * (-jnp.exp(alog_ref[...])))

    top_half = lax.broadcasted_iota(jnp.int32, (LANES, LANES), 0) < SSM_HEAD_DIM
    heads_per_group = SSM_HEADS // SSM_GROUPS
    for j in range(nbs):
        row = slice(j, j + 1)
        for h in range(HG_HEADS):
            sl = slice(h * HG_DIM, (h + 1) * HG_DIM)
            s_new = sh_ref[j, h] * _column_tile(f[row, sl]) + _column_tile(kk[row, sl]) * v[row, sl]
            sh_out_ref[j, h] = s_new
            obuf[row, sl] = jnp.sum(s_new * _column_tile(q[row, sl]), axis=0, keepdims=True)
        for rp in range(SSM_HEADS // 2):
            r0, r1 = 2 * rp, 2 * rp + 1
            g = r0 // heads_per_group
            sl = slice(rp * LANES, (rp + 1) * LANES)
            gsl = slice(g * SSM_STATE, (g + 1) * SSM_STATE)
            h2 = jnp.concatenate([ss_ref[j, r0], ss_ref[j, r1]], axis=0)
            da2 = jnp.where(top_half, d_a[row, r0:r0 + 1], d_a[row, r1:r1 + 1])
            dt2 = jnp.where(top_half, dt[row, r0:r0 + 1], dt[row, r1:r1 + 1])
            h_new = h2 * da2 + (dt2 * _column_tile(xs[row, sl])) * bm[row, gsl]
            ss_out_ref[j, r0] = h_new[0:SSM_HEAD_DIM]
            ss_out_ref[j, r1] = h_new[SSM_HEAD_DIM:]
            ybuf[row, sl] = jnp.sum((h_new * cm[row, gsl]).T, axis=0, keepdims=True)

    o = obuf[...]
    gate = _silu(u_ref[:, COL_G:COL_G + HG_WIDTH])
    hgn = hgn_ref[...]
    for h in range(HG_HEADS):
        sl = slice(h * HG_DIM, (h + 1) * HG_DIM)
        oh = o[:, sl]
        oh = oh * lax.rsqrt(jnp.mean(oh * oh, axis=-1, keepdims=True) + EPS)
        mix_ref[:, sl] = oh * hgn[:, sl] * gate[:, sl]
    yz = (ybuf[...] + dsk_ref[...] * xs) * _silu(u_ref[:, COL_Z:COL_Z + SSM_WIDTH])
    ssn = ssn_ref[...]
    gw = SSM_WIDTH // SSM_GROUPS
    for g in range(SSM_GROUPS):
        sl = slice(g * gw, (g + 1) * gw)
        seg = yz[:, sl]
        seg = seg * lax.rsqrt(jnp.mean(seg * seg, axis=-1, keepdims=True) + EPS) * ssn[:, sl]
        mix_ref[:, HG_WIDTH + g * gw:HG_WIDTH + (g + 1) * gw] = seg


def _sample_mixer(u, u_block0, n, nbs, params, sh, ss, sc_t):
    small = [_resident(p.shape) for p in params]
    return pl.pallas_call(
        functools.partial(_sample_kernel, nbs),
        grid=(n // nbs,),
        in_specs=[pl.BlockSpec((nbs, D_IN_PAD), lambda i: (u_block0 + i, 0))] + small
        + [pl.BlockSpec((nbs,) + sh.shape[1:], lambda i: (i, 0, 0, 0)),
           pl.BlockSpec((nbs,) + ss.shape[1:], lambda i: (i, 0, 0, 0)),
           pl.BlockSpec((CONV_WIDTH - 1, nbs, CONV_DIM), lambda i: (0, i, 0))],
        out_specs=[pl.BlockSpec((nbs, D_MODEL), lambda i: (i, 0)),
                   pl.BlockSpec((nbs,) + sh.shape[1:], lambda i: (i, 0, 0, 0)),
                   pl.BlockSpec((nbs,) + ss.shape[1:], lambda i: (i, 0, 0, 0)),
                   pl.BlockSpec((CONV_WIDTH - 1, nbs, CONV_DIM), lambda i: (0, i, 0))],
        out_shape=[jax.ShapeDtypeStruct((n, D_MODEL), F32),
                   jax.ShapeDtypeStruct(sh.shape, F32),
                   jax.ShapeDtypeStruct(ss.shape, F32),
                   jax.ShapeDtypeStruct(sc_t.shape, F32)],
        scratch_shapes=[pltpu.VMEM((nbs, HG_WIDTH), F32), pltpu.VMEM((nbs, SSM_WIDTH), F32)],
        compiler_params=pltpu.CompilerParams(dimension_semantics=("arbitrary",),
                                             vmem_limit_bytes=VMEM_LIMIT_BYTES),
        name="sample_mixer",
    )(u, *params, sh, ss, sc_t)


TM_PROMPT = 256
TM_SMALL = 128
TB_PROMPT = 256
SAMPLES_PER_STEP = 8


def _pad_lanes(row, value=0.0):
    return jnp.pad(row, ((0, 0), (0, LANES - row.shape[1])), constant_values=value)


def kernel(x_prompt, x_sample, state_hgrn, state_ssm, state_conv, meta_tokens, lb_logits, norm_ffn1, w_ffn1_gate, w_ffn1_up, w_ffn1_down, norm_mix, w_in, hg_norm, conv_w, conv_b, dt_bias, a_log, d_skip, ssm_norm, w_out, norm_ffn2, w_ffn2_gate, w_ffn2_up, w_ffn2_down, norm_final):
    bp, seq_p, _ = x_prompt.shape
    n_s = x_sample.shape[0]
    assert x_sample.shape[1] == 1 and n_s == TM_SMALL and seq_p % TB_PROMPT == 0
    layer = 0

    n1, nm, n2 = norm_ffn1[layer][None], norm_mix[layer][None], norm_ffn2[layer][None]
    nf = norm_final[None]
    wg1, wu1, wd1 = (w[layer].astype(BF16) for w in (w_ffn1_gate, w_ffn1_up, w_ffn1_down))
    wg2, wu2, wd2 = (w[layer].astype(BF16) for w in (w_ffn2_gate, w_ffn2_up, w_ffn2_down))
    win = jnp.pad(w_in[layer], ((0, 0), (0, D_IN_PAD - w_in.shape[2]))).astype(BF16)
    wo = w_out[layer].astype(BF16)
    mixer_params = (lb_logits, hg_norm[layer][None], conv_w[layer], conv_b[layer][None],
                    _pad_lanes(dt_bias[layer][None]), _pad_lanes(a_log[layer][None]),
                    jnp.repeat(d_skip[layer], SSM_HEAD_DIM)[None], ssm_norm[layer][None])

    n_pad = TM_SMALL - N_META
    x_small = jnp.concatenate([jnp.zeros((n_pad, D_MODEL), F32), meta_tokens, x_sample[:, 0]], axis=0)
    h1_small, u_small = _ffn_in(x_small, n1, wg1, wu1, wd1, nm, win, TM_SMALL)

    zeros_s = jnp.zeros((1, HG_HEADS, HG_DIM, HG_DIM), F32)
    zeros_h = jnp.zeros((1, SSM_HEADS, SSM_HEAD_DIM, SSM_STATE), F32)
    zeros_c = jnp.zeros((1, PAD_ROWS, CONV_DIM), F32)
    _, s_meta, h_meta, c_meta = _mixer(u_small, 0, 1, 1, TM_SMALL, n_pad, mixer_params,
                                       zeros_s, zeros_h, zeros_c)

    sc_t = jnp.swapaxes(state_conv[layer], 0, 1)
    mix_s, hgrn_s, ssm_s, conv_s_t = _sample_mixer(
        u_small, TM_SMALL // SAMPLES_PER_STEP, n_s, SAMPLES_PER_STEP, mixer_params,
        state_hgrn[layer], state_ssm[layer], sc_t)
    y_s = _ffn_out(h1_small, 1, mix_s, wo, n2, wg2, wu2, wd2, nf, TM_SMALL)

    xp = x_prompt.reshape(bp * seq_p, D_MODEL)
    h1_p, u_p = _ffn_in(xp, n1, wg1, wu1, wd1, nm, win, TM_PROMPT)
    mix_p, hgrn_p, ssm_p, conv_p = _mixer(u_p, 0, bp, seq_p // TB_PROMPT, TB_PROMPT, 0, mixer_params,
                                          s_meta, h_meta, c_meta)
    y_p = _ffn_out(h1_p, 0, mix_p, wo, n2, wg2, wu2, wd2, nf, TM_PROMPT)

    keep = slice(PAD_ROWS - (CONV_WIDTH - 1), PAD_ROWS)
    return (y_p.reshape(bp, seq_p, D_MODEL),
            y_s.reshape(n_s, 1, D_MODEL),
            hgrn_p[None], ssm_p[None], conv_p[:, keep][None],
            hgrn_s[None], ssm_s[None], jnp.swapaxes(conv_s_t, 0, 1)[None])
```

```python
import functools

import jax
import jax.numpy as jnp
from jax import lax
from jax.experimental import pallas as pl
from jax.experimental.pallas import tpu as pltpu

F32 = jnp.float32
BF16 = jnp.bfloat16

D_MODEL = 1024
D_FF = 2816
N_META = 16
HG_WIDTH = 512
HG_HEADS = 4
HG_DIM = 128
SSM_WIDTH = 512
SSM_HEADS = 8
SSM_HEAD_DIM = 64
SSM_GROUPS = 2
SSM_STATE = 128
CONV_WIDTH = 4
CONV_DIM = SSM_WIDTH + 2 * SSM_GROUPS * SSM_STATE
EPS = 1e-6

LANES = 128
SUBLANES = 8
VMEM_LIMIT_BYTES = 56 * 1024 * 1024

COL_Q = 0
COL_F = HG_WIDTH
COL_I = 2 * HG_WIDTH
COL_G = 3 * HG_WIDTH
COL_Z = 4 * HG_WIDTH
COL_XBC = COL_Z + SSM_WIDTH
COL_DT = COL_XBC + CONV_DIM
D_IN_PAD = COL_DT + LANES

FF_TILE = 256
HG_CHUNK = 64
SSM_CHUNK = 128
PAD_ROWS = SUBLANES


def _dot(a, b):
    return jnp.dot(a, b, preferred_element_type=F32)


def _dot_nt(a, b):
    return lax.dot_general(a, b, (((1,), (1,)), ((), ())), preferred_element_type=F32)


def _dot_tn(a, b):
    return lax.dot_general(a, b, (((0,), (0,)), ((), ())), preferred_element_type=F32)


def _rms(x, w):
    return x * lax.rsqrt(jnp.mean(x * x, axis=-1, keepdims=True) + EPS) * w


def _silu(x):
    return x * jax.nn.sigmoid(x)


def _swiglu(xn, wg_ref, wu_ref, wd_ref):
    acc = jnp.zeros((xn.shape[0], D_MODEL), F32)
    for j in range(D_FF // FF_TILE):
        cols = slice(j * FF_TILE, (j + 1) * FF_TILE)
        g = _dot(xn, wg_ref[:, cols])
        u = _dot(xn, wu_ref[:, cols])
        acc = acc + _dot((_silu(g) * u).astype(BF16), wd_ref[cols, :])
    return acc


def _cumsum_rows(tri, a):
    a1 = a.astype(BF16)
    r1 = a - a1.astype(F32)
    a2 = r1.astype(BF16)
    a3 = (r1 - a2.astype(F32)).astype(BF16)
    return _dot(tri, a1) + _dot(tri, a2) + _dot(tri, a3)


def _lower_tri(n):
    row = lax.broadcasted_iota(jnp.int32, (n, n), 0)
    col = lax.broadcasted_iota(jnp.int32, (n, n), 1)
    return row >= col


def _forget_lower_bound(lbl):
    l0, l1 = lbl[0:1], lbl[1:2]
    m = jnp.maximum(l0, l1)
    e0, e1 = jnp.exp(l0 - m), jnp.exp(l1 - m)
    return e0 / (e0 + e1)


def _ffn_in_kernel(x_ref, n1_ref, wg_ref, wu_ref, wd_ref, nm_ref, win_ref, h1_ref, u_ref):
    x = x_ref[...]
    xn = _rms(x, n1_ref[...]).astype(BF16)
    h1 = x + 0.5 * _swiglu(xn, wg_ref, wu_ref, wd_ref)
    h1_ref[...] = h1
    hn = _rms(h1, nm_ref[...]).astype(BF16)
    u_ref[...] = _dot(hn, win_ref[...])


def _resident(shape):
    nd = len(shape)
    return pl.BlockSpec(shape, lambda *_: (0,) * nd, pipeline_mode=pl.Buffered(1))


def _ffn_in(x, n1, wg, wu, wd, nm, win, tm):
    n = x.shape[0]
    return pl.pallas_call(
        _ffn_in_kernel,
        grid=(n // tm,),
        in_specs=[pl.BlockSpec((tm, D_MODEL), lambda i: (i, 0)),
                  _resident(n1.shape), _resident(wg.shape), _resident(wu.shape), _resident(wd.shape),
                  _resident(nm.shape), _resident(win.shape)],
        out_specs=[pl.BlockSpec((tm, D_MODEL), lambda i: (i, 0)),
                   pl.BlockSpec((tm, D_IN_PAD), lambda i: (i, 0))],
        out_shape=[jax.ShapeDtypeStruct((n, D_MODEL), F32),
                   jax.ShapeDtypeStruct((n, D_IN_PAD), F32)],
        compiler_params=pltpu.CompilerParams(dimension_semantics=("arbitrary",),
                                             vmem_limit_bytes=VMEM_LIMIT_BYTES),
        name="ffn_in",
    )(x, n1, wg, wu, wd, nm, win)


def _ffn_out_kernel(h1_ref, mix_ref, wo_ref, n2_ref, wg_ref, wu_ref, wd_ref, nf_ref, y_ref):
    h2 = h1_ref[...] + _dot(mix_ref[...].astype(BF16), wo_ref[...])
    hn = _rms(h2, n2_ref[...]).astype(BF16)
    h3 = h2 + 0.5 * _swiglu(hn, wg_ref, wu_ref, wd_ref)
    y_ref[...] = _rms(h3, nf_ref[...])


def _ffn_out(h1, h1_block0, mix, wo, n2, wg, wu, wd, nf, tm):
    n = mix.shape[0]
    return pl.pallas_call(
        _ffn_out_kernel,
        grid=(n // tm,),
        in_specs=[pl.BlockSpec((tm, D_MODEL), lambda i: (i + h1_block0, 0)),
                  pl.BlockSpec((tm, D_MODEL), lambda i: (i, 0)),
                  _resident(wo.shape), _resident(n2.shape), _resident(wg.shape), _resident(wu.shape),
                  _resident(wd.shape), _resident(nf.shape)],
        out_specs=pl.BlockSpec((tm, D_MODEL), lambda i: (i, 0)),
        out_shape=jax.ShapeDtypeStruct((n, D_MODEL), F32),
        compiler_params=pltpu.CompilerParams(dimension_semantics=("arbitrary",),
                                             vmem_limit_bytes=VMEM_LIMIT_BYTES),
        name="ffn_out",
    )(h1, mix, wo, n2, wg, wu, wd, nf)


def _hgrn_chunk(r0, n_pad, u_ref, lb, hgn, mix_ref, st_scr):
    c = HG_CHUNK
    rows = slice(r0, r0 + c)
    fz = u_ref[rows, COL_F:COL_F + HG_WIDTH]
    logf = jnp.log(lb + (1.0 - lb) * jax.nn.sigmoid(fz))
    kk = (1.0 - lb) * jax.nn.sigmoid(-fz)
    if n_pad > r0:
        valid = lax.broadcasted_iota(jnp.int32, (c, HG_WIDTH), 0) >= (n_pad - r0)
        logf = jnp.where(valid, logf, 0.0)
        kk = jnp.where(valid, kk, 0.0)
    q = _silu(u_ref[rows, COL_Q:COL_Q + HG_WIDTH])
    v = u_ref[rows, COL_I:COL_I + HG_WIDTH].astype(BF16)
    gate = _silu(u_ref[rows, COL_G:COL_G + HG_WIDTH])

    causal = _lower_tri(c)
    b = _cumsum_rows(causal.astype(BF16), logf)
    b_last = b[c - 1:c, :]
    qt = (q * jnp.exp(b)).astype(BF16)
    kt = (kk * jnp.exp(-b)).astype(BF16)
    kh = (kk * jnp.exp(b_last - b)).astype(BF16)
    decay = jnp.exp(b_last)

    for h in range(HG_HEADS):
        sl = slice(h * HG_DIM, (h + 1) * HG_DIM)
        st = st_scr[h]
        scores = jnp.where(causal, _dot_nt(qt[:, sl], kt[:, sl]), 0.0).astype(BF16)
        o = _dot(scores, v[:, sl]) + _dot_nt(qt[:, sl], st.astype(BF16))
        st_scr[h] = st * decay[:, sl] + _dot_tn(v[:, sl], kh[:, sl])
        o = o * lax.rsqrt(jnp.mean(o * o, axis=-1, keepdims=True) + EPS)
        mix_ref[rows, sl] = (o * hgn[:, sl] * gate[:, sl]).astype(mix_ref.dtype)


def _ssd_chunk(r0, n_pad, u_ref, cw, cb, dtb, a_neg, dsk, ssn, mix_ref, hs_scr, xpad, ybuf):
    c = SSM_CHUNK
    rows = slice(r0, r0 + c)
    conv = cb
    for j in range(CONV_WIDTH):
        off = PAD_ROWS - (CONV_WIDTH - 1) + j + r0
        conv = conv + cw[j:j + 1, :] * xpad[off:off + c, :]
    act = _silu(conv)
    xs = act[:, 0:SSM_WIDTH]
    bm = act[:, SSM_WIDTH:SSM_WIDTH + SSM_GROUPS * SSM_STATE].astype(BF16)
    cm = act[:, SSM_WIDTH + SSM_GROUPS * SSM_STATE:].astype(BF16)

    dt = jax.nn.softplus(u_ref[rows, COL_DT:COL_DT + LANES] + dtb)
    if n_pad > r0:
        dt = jnp.where(lax.broadcasted_iota(jnp.int32, (c, LANES), 0) >= (n_pad - r0), dt, 0.0)
    causal = _lower_tri(c)
    cum = _cumsum_rows(causal.astype(BF16), dt * a_neg)
    last = cum[c - 1:c, :]
    cum_t = cum.T
    dt_t = dt.T
    e_cum = jnp.exp(cum)
    w_in = dt * jnp.exp(last - cum)
    e_last = jnp.exp(last)
    yield

    for g in range(SSM_GROUPS):
        bg = bm[:, g * SSM_STATE:(g + 1) * SSM_STATE]
        cg = cm[:, g * SSM_STATE:(g + 1) * SSM_STATE]
        cbt = _dot_nt(cg, bg)
        for rr in range(SSM_HEADS // SSM_GROUPS):
            r = g * (SSM_HEADS // SSM_GROUPS) + rr
            hsl = slice(r * SSM_HEAD_DIM, (r + 1) * SSM_HEAD_DIM)
            seg = jnp.exp(jnp.where(causal, cum[:, r:r + 1] - cum_t[r:r + 1, :], -jnp.inf))
            m = (cbt * seg * dt_t[r:r + 1, :]).astype(BF16)
            xr = xs[:, hsl]
            hr = hs_scr[r]
            y = _dot(m, xr.astype(BF16)) + _dot_nt(cg, hr.astype(BF16)) * e_cum[:, r:r + 1]
            xw = (xr * w_in[:, r:r + 1]).astype(BF16)
            hs_scr[r] = hr * e_last[:, r:r + 1] + _dot_tn(xw, bg)
            ybuf[rows, hsl] = y + dsk[:, hsl] * xr
        yield

    yz = ybuf[rows, :] * _silu(u_ref[rows, COL_Z:COL_Z + SSM_WIDTH])
    gw = SSM_WIDTH // SSM_GROUPS
    for g in range(SSM_GROUPS):
        sl = slice(g * gw, (g + 1) * gw)
        seg = yz[:, sl]
        seg = seg * lax.rsqrt(jnp.mean(seg * seg, axis=-1, keepdims=True) + EPS) * ssn[:, sl]
        mix_ref[rows, HG_WIDTH + g * gw:HG_WIDTH + (g + 1) * gw] = seg.astype(mix_ref.dtype)


def _mixer_steps(n_pad, tb, u_ref, param_refs, mix_ref, st_scr, hs_scr, xpad, ybuf):
    lbl_ref, hgn_ref, cw_ref, cb_ref, dtb_ref, alog_ref, dsk_ref, ssn_ref = param_refs
    xbc = u_ref[:, COL_XBC:COL_XBC + CONV_DIM]
    if n_pad:
        xbc = jnp.where(lax.broadcasted_iota(jnp.int32, (tb, CONV_DIM), 0) >= n_pad, xbc, 0.0)
    xpad[PAD_ROWS:PAD_ROWS + tb, :] = xbc

    lb = _forget_lower_bound(lbl_ref[...])
    hgn = hgn_ref[...]
    for r0 in range(0, tb, HG_CHUNK):
        _hgrn_chunk(r0, n_pad, u_ref, lb, hgn, mix_ref, st_scr)
        yield

    a_neg = -jnp.exp(alog_ref[...])
    for r0 in range(0, tb, SSM_CHUNK):
        yield from _ssd_chunk(r0, n_pad, u_ref, cw_ref[...], cb_ref[...], dtb_ref[...], a_neg, dsk_ref[...],
                              ssn_ref[...], mix_ref, hs_scr, xpad, ybuf)
        yield

    xpad[0:PAD_ROWS, :] = xpad[tb:tb + PAD_ROWS, :]


def _interleave(first, second):
    live = [first, second]
    while live:
        for gen in list(live):
            try:
                next(gen)
            except StopIteration:
                live.remove(gen)


def _mixer_block(*args):
    for _ in _mixer_steps(*args):
        pass


def _load_state(s0_ref, h0_ref, c0_ref, st_scr, hs_scr, xpad):
    for h in range(HG_HEADS):
        st_scr[h] = s0_ref[0, h].T
    hs_scr[...] = h0_ref[0]
    xpad[0:PAD_ROWS, :] = c0_ref[0]


def _store_state(s_out_ref, h_out_ref, c_out_ref, st_scr, hs_scr, xpad):
    for h in range(HG_HEADS):
        s_out_ref[0, h] = st_scr[h].T
    h_out_ref[0] = hs_scr[...]
    c_out_ref[0] = xpad[0:PAD_ROWS, :]


N_MIXER_PARAMS = 8


def _mixer_kernel(n_pad, tb, u_ref, *refs):
    param_refs = refs[:N_MIXER_PARAMS]
    s0_ref, h0_ref, c0_ref, mix_ref, s_out_ref, h_out_ref, c_out_ref, st_scr, hs_scr, xpad, ybuf = (
        refs[N_MIXER_PARAMS:])
    t = pl.program_id(1)

    @pl.when(t == 0)
    def _():
        _load_state(s0_ref, h0_ref, c0_ref, st_scr, hs_scr, xpad)

    _mixer_block(n_pad, tb, u_ref, param_refs, mix_ref, st_scr, hs_scr, xpad, ybuf)

    @pl.when(t == pl.num_programs(1) - 1)
    def _():
        _store_state(s_out_ref, h_out_ref, c_out_ref, st_scr, hs_scr, xpad)


def _mixer(u, u_block0, nb, nt, tb, n_pad, params, s0, h0, c0):
    lbl, hgn, cw, cb, dtb, alog, dsk, ssn = params
    small = [_resident(p.shape) for p in params]
    shared = lambda shape: pl.BlockSpec((1,) + shape[1:], lambda b, t: (0,) * len(shape))
    per_seq = lambda shape: pl.BlockSpec((1,) + shape, lambda b, t: (b,) + (0,) * len(shape))
    s_shape = (HG_HEADS, HG_DIM, HG_DIM)
    h_shape = (SSM_HEADS, SSM_HEAD_DIM, SSM_STATE)
    c_shape = (PAD_ROWS, CONV_DIM)
    return pl.pallas_call(
        functools.partial(_mixer_kernel, n_pad, tb),
        grid=(nb, nt),
        in_specs=[pl.BlockSpec((tb, D_IN_PAD), lambda b, t: (u_block0 + b * nt + t, 0))] + small
        + [shared(s0.shape), shared(h0.shape), shared(c0.shape)],
        out_specs=[pl.BlockSpec((tb, D_MODEL), lambda b, t: (b * nt + t, 0)),
                   per_seq(s_shape), per_seq(h_shape), per_seq(c_shape)],
        out_shape=[jax.ShapeDtypeStruct((nb * nt * tb, D_MODEL), BF16),
                   jax.ShapeDtypeStruct((nb,) + s_shape, F32),
                   jax.ShapeDtypeStruct((nb,) + h_shape, F32),
                   jax.ShapeDtypeStruct((nb,) + c_shape, F32)],
        scratch_shapes=[pltpu.VMEM(s_shape, F32), pltpu.VMEM(h_shape, F32),
                        pltpu.VMEM((PAD_ROWS + tb, CONV_DIM), F32), pltpu.VMEM((tb, SSM_WIDTH), F32)],
        compiler_params=pltpu.CompilerParams(dimension_semantics=("arbitrary", "arbitrary"),
                                             vmem_limit_bytes=VMEM_LIMIT_BYTES),
        name="mixer",
    )(u, *params, s0, h0, c0)


def _ffn_in_mixer_kernel(tm, nt, x_ref, n1_ref, wg_ref, wu_ref, wd_ref, nm_ref, win_ref, *refs):
    param_refs = refs[:N_MIXER_PARAMS]
    (s0_ref, h0_ref, c0_ref, h1_ref, mix_ref, s_out_ref, h_out_ref, c_out_ref,
     u_even, u_odd, st_scr, hs_scr, xpad, ybuf) = refs[N_MIXER_PARAMS:]
    s = pl.program_id(0)
    scanned = jnp.maximum(s - 1, 0)

    @pl.when(s == 0)
    def _():
        u_odd[...] = jnp.zeros(u_odd.shape, F32)

    @pl.when(scanned % nt == 0)
    def _():
        _load_state(s0_ref, h0_ref, c0_ref, st_scr, hs_scr, xpad)

    def project_steps(u_write):
        x = x_ref[...]
        xn = _rms(x, n1_ref[...]).astype(BF16)
        yield
        acc = jnp.zeros((tm, D_MODEL), F32)
        for j in range(D_FF // FF_TILE):
            cols = slice(j * FF_TILE, (j + 1) * FF_TILE)
            g = _dot(xn, wg_ref[:, cols])
            u = _dot(xn, wu_ref[:, cols])
            acc = acc + _dot((_silu(g) * u).astype(BF16), wd_ref[cols, :])
            yield
        h1 = x + 0.5 * acc
        h1_ref[...] = h1
        hn = _rms(h1, nm_ref[...]).astype(BF16)
        for c0 in range(0, D_IN_PAD, D_MODEL):
            cols = slice(c0, min(c0 + D_MODEL, D_IN_PAD))
            yield
            u_write[:, cols] = _dot(hn, win_ref[:, cols])

    def project_and_scan(u_write, u_read):
        _interleave(project_steps(u_write),
                    _mixer_steps(0, tm, u_read, param_refs, mix_ref, st_scr, hs_scr, xpad, ybuf))

    pl.when(s % 2 == 0)(lambda: project_and_scan(u_even, u_odd))
    pl.when(s % 2 == 1)(lambda: project_and_scan(u_odd, u_even))

    @pl.when((s >= 1) & (scanned % nt == nt - 1))
    def _():
        _store_state(s_out_ref, h_out_ref, c_out_ref, st_scr, hs_scr, xpad)


def _ffn_in_mixer(x, nb, nt, tm, dense_params, params, s0, h0, c0):
    n_blocks = nb * nt
    last = n_blocks - 1
    dense = [_resident(p.shape) for p in dense_params]
    small = [_resident(p.shape) for p in params]
    shared = lambda shape: pl.BlockSpec((1,) + shape[1:], lambda s: (0,) * len(shape))
    per_seq = lambda shape: pl.BlockSpec((1,) + shape, lambda s: (jnp.maximum(s - 1, 0) // nt,) + (0,) * len(shape))
    s_shape = (HG_HEADS, HG_DIM, HG_DIM)
    h_shape = (SSM_HEADS, SSM_HEAD_DIM, SSM_STATE)
    c_shape = (PAD_ROWS, CONV_DIM)
    return pl.pallas_call(
        functools.partial(_ffn_in_mixer_kernel, tm, nt),
        grid=(n_blocks + 1,),
        in_specs=[pl.BlockSpec((tm, D_MODEL), lambda s: (jnp.minimum(s, last), 0))] + dense + small
        + [shared(s0.shape), shared(h0.shape), shared(c0.shape)],
        out_specs=[pl.BlockSpec((tm, D_MODEL), lambda s: (jnp.minimum(s, last), 0)),
                   pl.BlockSpec((tm, D_MODEL), lambda s: (jnp.maximum(s - 1, 0), 0)),
                   per_seq(s_shape), per_seq(h_shape), per_seq(c_shape)],
        out_shape=[jax.ShapeDtypeStruct((n_blocks * tm, D_MODEL), F32),
                   jax.ShapeDtypeStruct((n_blocks * tm, D_MODEL), BF16),
                   jax.ShapeDtypeStruct((nb,) + s_shape, F32),
                   jax.ShapeDtypeStruct((nb,) + h_shape, F32),
                   jax.ShapeDtypeStruct((nb,) + c_shape, F32)],
        scratch_shapes=[pltpu.VMEM((tm, D_IN_PAD), F32), pltpu.VMEM((tm, D_IN_PAD), F32),
                        pltpu.VMEM(s_shape, F32), pltpu.VMEM(h_shape, F32),
                        pltpu.VMEM((PAD_ROWS + tm, CONV_DIM), F32), pltpu.VMEM((tm, SSM_WIDTH), F32)],
        compiler_params=pltpu.CompilerParams(dimension_semantics=("arbitrary",),
                                             vmem_limit_bytes=VMEM_LIMIT_BYTES),
        name="ffn_in_mixer",
    )(x, *dense_params, *params, s0, h0, c0)


def _column_tile(row):
    return jnp.broadcast_to(row, (LANES, LANES)).T


def _sample_kernel(nbs, u_ref, lbl_ref, hgn_ref, cw_ref, cb_ref, dtb_ref, alog_ref, dsk_ref, ssn_ref,
                   sh_ref, ss_ref, sc_ref, mix_ref, sh_out_ref, ss_out_ref, sc_out_ref, obuf, ybuf):
    lb = _forget_lower_bound(lbl_ref[...])
    fz = u_ref[:, COL_F:COL_F + HG_WIDTH]
    f = lb + (1.0 - lb) * jax.nn.sigmoid(fz)
    kk = (1.0 - lb) * jax.nn.sigmoid(-fz)
    q = _silu(u_ref[:, COL_Q:COL_Q + HG_WIDTH])
    v = u_ref[:, COL_I:COL_I + HG_WIDTH]

    xbc = u_ref[:, COL_XBC:COL_XBC + CONV_DIM]
    cw = cw_ref[...]
    conv = cb_ref[...] + cw[CONV_WIDTH - 1:CONV_WIDTH, :] * xbc
    for j in range(CONV_WIDTH - 1):
        conv = conv + cw[j:j + 1, :] * sc_ref[j]
    for j in range(CONV_WIDTH - 2):
        sc_out_ref[j] = sc_ref[j + 1]
    sc_out_ref[CONV_WIDTH - 2] = xbc
    act = _silu(conv)
    xs = act[:, 0:SSM_WIDTH]
    bm = act[:, SSM_WIDTH:SSM_WIDTH + SSM_GROUPS * SSM_STATE]
    cm = act[:, SSM_WIDTH + SSM_GROUPS * SSM_STATE:]
    dt = jax.nn.softplus(u_ref[:, COL_DT:COL_DT + LANES] + dtb_ref[...])
    d_a = jnp.exp(dt * (-jnp.exp(alog_ref[...])))

    top_half = lax.broadcasted_iota(jnp.int32, (LANES, LANES), 0) < SSM_HEAD_DIM
    heads_per_group = SSM_HEADS // SSM_GROUPS
    for j in range(nbs):
        row = slice(j, j + 1)
        for h in range(HG_HEADS):
            sl = slice(h * HG_DIM, (h + 1) * HG_DIM)
            s_new = sh_ref[j, h] * _column_tile(f[row, sl]) + _column_tile(kk[row, sl]) * v[row, sl]
            sh_out_ref[j, h] = s_new
            obuf[row, sl] = jnp.sum(s_new * _column_tile(q[row, sl]), axis=0, keepdims=True)
        for rp in range(SSM_HEADS // 2):
            r0, r1 = 2 * rp, 2 * rp + 1
            g = r0 // heads_per_group
            sl = slice(rp * LANES, (rp + 1) * LANES)
            gsl = slice(g * SSM_STATE, (g + 1) * SSM_STATE)
            h2 = jnp.concatenate([ss_ref[j, r0], ss_ref[j, r1]], axis=0)
            da2 = jnp.where(top_half, d_a[row, r0:r0 + 1], d_a[row, r1:r1 + 1])
            dt2 = jnp.where(top_half, dt[row, r0:r0 + 1], dt[row, r1:r1 + 1])
            h_new = h2 * da2 + (dt2 * _column_tile(xs[row, sl])) * bm[row, gsl]
            ss_out_ref[j, r0] = h_new[0:SSM_HEAD_DIM]
            ss_out_ref[j, r1] = h_new[SSM_HEAD_DIM:]
            ybuf[row, sl] = jnp.sum((h_new * cm[row, gsl]).T, axis=0, keepdims=True)

    o = obuf[...]
    gate = _silu(u_ref[:, COL_G:COL_G + HG_WIDTH])
    hgn = hgn_ref[...]
    for h in range(HG_HEADS):
        sl = slice(h * HG_DIM, (h + 1) * HG_DIM)
        oh = o[:, sl]
        oh = oh * lax.rsqrt(jnp.mean(oh * oh, axis=-1, keepdims=True) + EPS)
        mix_ref[:, sl] = oh * hgn[:, sl] * gate[:, sl]
    yz = (ybuf[...] + dsk_ref[...] * xs) * _silu(u_ref[:, COL_Z:COL_Z + SSM_WIDTH])
    ssn = ssn_ref[...]
    gw = SSM_WIDTH // SSM_GROUPS
    for g in range(SSM_GROUPS):
        sl = slice(g * gw, (g + 1) * gw)
        seg = yz[:, sl]
        seg = seg * lax.rsqrt(jnp.mean(seg * seg, axis=-1, keepdims=True) + EPS) * ssn[:, sl]
        mix_ref[:, HG_WIDTH + g * gw:HG_WIDTH + (g + 1) * gw] = seg


def _sample_mixer(u, u_block0, n, nbs, params, sh, ss, sc_t):
    small = [_resident(p.shape) for p in params]
    return pl.pallas_call(
        functools.partial(_sample_kernel, nbs),
        grid=(n // nbs,),
        in_specs=[pl.BlockSpec((nbs, D_IN_PAD), lambda i: (u_block0 + i, 0))] + small
        + [pl.BlockSpec((nbs,) + sh.shape[1:], lambda i: (i, 0, 0, 0)),
           pl.BlockSpec((nbs,) + ss.shape[1:], lambda i: (i, 0, 0, 0)),
           pl.BlockSpec((CONV_WIDTH - 1, nbs, CONV_DIM), lambda i: (0, i, 0))],
        out_specs=[pl.BlockSpec((nbs, D_MODEL), lambda i: (i, 0)),
                   pl.BlockSpec((nbs,) + sh.shape[1:], lambda i: (i, 0, 0, 0)),
                   pl.BlockSpec((nbs,) + ss.shape[1:], lambda i: (i, 0, 0, 0)),
                   pl.BlockSpec((CONV_WIDTH - 1, nbs, CONV_DIM), lambda i: (0, i, 0))],
        out_shape=[jax.ShapeDtypeStruct((n, D_MODEL), F32),
                   jax.ShapeDtypeStruct(sh.shape, F32),
                   jax.ShapeDtypeStruct(ss.shape, F32),
                   jax.ShapeDtypeStruct(sc_t.shape, F32)],
        scratch_shapes=[pltpu.VMEM((nbs, HG_WIDTH), F32), pltpu.VMEM((nbs, SSM_WIDTH), F32)],
        compiler_params=pltpu.CompilerParams(dimension_semantics=("arbitrary",),
                                             vmem_limit_bytes=VMEM_LIMIT_BYTES),
        name="sample_mixer",
    )(u, *params, sh, ss, sc_t)


TM_PROMPT = 256
TM_SMALL = 128
SAMPLES_PER_STEP = 8


def _pad_lanes(row, value=0.0):
    return jnp.pad(row, ((0, 0), (0, LANES - row.shape[1])), constant_values=value)


def kernel(x_prompt, x_sample, state_hgrn, state_ssm, state_conv, meta_tokens, lb_logits, norm_ffn1, w_ffn1_gate, w_ffn1_up, w_ffn1_down, norm_mix, w_in, hg_norm, conv_w, conv_b, dt_bias, a_log, d_skip, ssm_norm, w_out, norm_ffn2, w_ffn2_gate, w_ffn2_up, w_ffn2_down, norm_final):
    bp, seq_p, _ = x_prompt.shape
    n_s = x_sample.shape[0]
    assert x_sample.shape[1] == 1 and n_s == TM_SMALL and seq_p % TM_PROMPT == 0
    layer = 0

    n1, nm, n2 = norm_ffn1[layer][None], norm_mix[layer][None], norm_ffn2[layer][None]
    nf = norm_final[None]
    wg1, wu1, wd1 = (w[layer].astype(BF16) for w in (w_ffn1_gate, w_ffn1_up, w_ffn1_down))
    wg2, wu2, wd2 = (w[layer].astype(BF16) for w in (w_ffn2_gate, w_ffn2_up, w_ffn2_down))
    win = jnp.pad(w_in[layer], ((0, 0), (0, D_IN_PAD - w_in.shape[2]))).astype(BF16)
    wo = w_out[layer].astype(BF16)
    mixer_params = (lb_logits, hg_norm[layer][None], conv_w[layer], conv_b[layer][None],
                    _pad_lanes(dt_bias[layer][None]), _pad_lanes(a_log[layer][None]),
                    jnp.repeat(d_skip[layer], SSM_HEAD_DIM)[None], ssm_norm[layer][None])

    n_pad = TM_SMALL - N_META
    x_small = jnp.concatenate([jnp.zeros((n_pad, D_MODEL), F32), meta_tokens, x_sample[:, 0]], axis=0)
    h1_small, u_small = _ffn_in(x_small, n1, wg1, wu1, wd1, nm, win, TM_SMALL)

    zeros_s = jnp.zeros((1, HG_HEADS, HG_DIM, HG_DIM), F32)
    zeros_h = jnp.zeros((1, SSM_HEADS, SSM_HEAD_DIM, SSM_STATE), F32)
    zeros_c = jnp.zeros((1, PAD_ROWS, CONV_DIM), F32)
    _, s_meta, h_meta, c_meta = _mixer(u_small, 0, 1, 1, TM_SMALL, n_pad, mixer_params,
                                       zeros_s, zeros_h, zeros_c)

    sc_t = jnp.swapaxes(state_conv[layer], 0, 1)
    mix_s, hgrn_s, ssm_s, conv_s_t = _sample_mixer(
        u_small, TM_SMALL // SAMPLES_PER_STEP, n_s, SAMPLES_PER_STEP, mixer_params,
        state_hgrn[layer], state_ssm[layer], sc_t)
    y_s = _ffn_out(h1_small, 1, mix_s, wo, n2, wg2, wu2, wd2, nf, TM_SMALL)

    xp = x_prompt.reshape(bp * seq_p, D_MODEL)
    h1_p, mix_p, hgrn_p, ssm_p, conv_p = _ffn_in_mixer(
        xp, bp, seq_p // TM_PROMPT, TM_PROMPT, (n1, wg1, wu1, wd1, nm, win), mixer_params,
        s_meta, h_meta, c_meta)
    y_p = _ffn_out(h1_p, 0, mix_p, wo, n2, wg2, wu2, wd2, nf, TM_PROMPT)

    keep = slice(PAD_ROWS - (CONV_WIDTH - 1), PAD_ROWS)
    return (y_p.reshape(bp, seq_p, D_MODEL),
            y_s.reshape(n_s, 1, D_MODEL),
            hgrn_p[None], ssm_p[None], conv_p[:, keep][None],
            hgrn_s[None], ssm_s[None], jnp.swapaxes(conv_s_t, 0, 1)[None])
```

```python
import functools

import jax
import jax.numpy as jnp
from jax import lax
from jax.experimental import pallas as pl
from jax.experimental.pallas import tpu as pltpu

F32 = jnp.float32
BF16 = jnp.bfloat16

D_MODEL = 1024
D_FF = 2816
N_META = 16
HG_WIDTH = 512
HG_HEADS = 4
HG_DIM = 128
SSM_WIDTH = 512
SSM_HEADS = 8
SSM_HEAD_DIM = 64
SSM_GROUPS = 2
SSM_STATE = 128
CONV_WIDTH = 4
CONV_DIM = SSM_WIDTH + 2 * SSM_GROUPS * SSM_STATE
EPS = 1e-6

LANES = 128
SUBLANES = 8
VMEM_LIMIT_BYTES = 56 * 1024 * 1024

COL_Q = 0
COL_F = HG_WIDTH
COL_I = 2 * HG_WIDTH
COL_G = 3 * HG_WIDTH
COL_Z = 4 * HG_WIDTH
COL_XBC = COL_Z + SSM_WIDTH
COL_DT = COL_XBC + CONV_DIM
D_IN_PAD = COL_DT + LANES

FF_TILE = 256
HG_CHUNK = 64
SSM_CHUNK = 128
PAD_ROWS = SUBLANES


def _dot(a, b):
    return jnp.dot(a, b, preferred_element_type=F32)


def _dot_nt(a, b):
    return lax.dot_general(a, b, (((1,), (1,)), ((), ())), preferred_element_type=F32)


def _dot_tn(a, b):
    return lax.dot_general(a, b, (((0,), (0,)), ((), ())), preferred_element_type=F32)


def _rms(x, w):
    return x * lax.rsqrt(jnp.mean(x * x, axis=-1, keepdims=True) + EPS) * w


def _silu(x):
    return x * jax.nn.sigmoid(x)


def _swiglu(xn, wg_ref, wu_ref, wd_ref):
    acc = jnp.zeros((xn.shape[0], D_MODEL), F32)
    for j in range(D_FF // FF_TILE):
        cols = slice(j * FF_TILE, (j + 1) * FF_TILE)
        g = _dot(xn, wg_ref[:, cols])
        u = _dot(xn, wu_ref[:, cols])
        acc = acc + _dot((_silu(g) * u).astype(BF16), wd_ref[cols, :])
    return acc


def _cumsum_rows(tri, a):
    a1 = a.astype(BF16)
    r1 = a - a1.astype(F32)
    a2 = r1.astype(BF16)
    a3 = (r1 - a2.astype(F32)).astype(BF16)
    return _dot(tri, a1) + _dot(tri, a2) + _dot(tri, a3)


def _lower_tri(n):
    row = lax.broadcasted_iota(jnp.int32, (n, n), 0)
    col = lax.broadcasted_iota(jnp.int32, (n, n), 1)
    return row >= col


def _forget_lower_bound(lbl):
    l0, l1 = lbl[0:1], lbl[1:2]
    m = jnp.maximum(l0, l1)
    e0, e1 = jnp.exp(l0 - m), jnp.exp(l1 - m)
    return e0 / (e0 + e1)


def _ffn_in_kernel(x_ref, n1_ref, wg_ref, wu_ref, wd_ref, nm_ref, win_ref, h1_ref, u_ref):
    x = x_ref[...]
    xn = _rms(x, n1_ref[...]).astype(BF16)
    h1 = x + 0.5 * _swiglu(xn, wg_ref, wu_ref, wd_ref)
    h1_ref[...] = h1
    hn = _rms(h1, nm_ref[...]).astype(BF16)
    u_ref[...] = _dot(hn, win_ref[...])


def _resident(shape):
    nd = len(shape)
    return pl.BlockSpec(shape, lambda *_: (0,) * nd, pipeline_mode=pl.Buffered(1))


def _ffn_in(x, n1, wg, wu, wd, nm, win, tm):
    n = x.shape[0]
    return pl.pallas_call(
        _ffn_in_kernel,
        grid=(n // tm,),
        in_specs=[pl.BlockSpec((tm, D_MODEL), lambda i: (i, 0)),
                  _resident(n1.shape), _resident(wg.shape), _resident(wu.shape), _resident(wd.shape),
                  _resident(nm.shape), _resident(win.shape)],
        out_specs=[pl.BlockSpec((tm, D_MODEL), lambda i: (i, 0)),
                   pl.BlockSpec((tm, D_IN_PAD), lambda i: (i, 0))],
        out_shape=[jax.ShapeDtypeStruct((n, D_MODEL), F32),
                   jax.ShapeDtypeStruct((n, D_IN_PAD), F32)],
        compiler_params=pltpu.CompilerParams(dimension_semantics=("arbitrary",),
                                             vmem_limit_bytes=VMEM_LIMIT_BYTES),
        name="ffn_in",
    )(x, n1, wg, wu, wd, nm, win)


def _ffn_out_kernel(h1_ref, mix_ref, wo_ref, n2_ref, wg_ref, wu_ref, wd_ref, nf_ref, y_ref):
    h2 = h1_ref[...] + _dot(mix_ref[...].astype(BF16), wo_ref[...])
    hn = _rms(h2, n2_ref[...]).astype(BF16)
    h3 = h2 + 0.5 * _swiglu(hn, wg_ref, wu_ref, wd_ref)
    y_ref[...] = _rms(h3, nf_ref[...])


def _ffn_out(h1, h1_block0, mix, wo, n2, wg, wu, wd, nf, tm):
    n = mix.shape[0]
    return pl.pallas_call(
        _ffn_out_kernel,
        grid=(n // tm,),
        in_specs=[pl.BlockSpec((tm, D_MODEL), lambda i: (i + h1_block0, 0)),
                  pl.BlockSpec((tm, D_MODEL), lambda i: (i, 0)),
                  _resident(wo.shape), _resident(n2.shape), _resident(wg.shape), _resident(wu.shape),
                  _resident(wd.shape), _resident(nf.shape)],
        out_specs=pl.BlockSpec((tm, D_MODEL), lambda i: (i, 0)),
        out_shape=jax.ShapeDtypeStruct((n, D_MODEL), F32),
        compiler_params=pltpu.CompilerParams(dimension_semantics=("arbitrary",),
                                             vmem_limit_bytes=VMEM_LIMIT_BYTES),
        name="ffn_out",
    )(h1, mix, wo, n2, wg, wu, wd, nf)


def _hgrn_chunk(r0, n_pad, u_ref, lb, hgn, mix_ref, st_scr):
    c = HG_CHUNK
    rows = slice(r0, r0 + c)
    fz = u_ref[rows, COL_F:COL_F + HG_WIDTH]
    logf = jnp.log(lb + (1.0 - lb) * jax.nn.sigmoid(fz))
    kk = (1.0 - lb) * jax.nn.sigmoid(-fz)
    if n_pad > r0:
        valid = lax.broadcasted_iota(jnp.int32, (c, HG_WIDTH), 0) >= (n_pad - r0)
        logf = jnp.where(valid, logf, 0.0)
        kk = jnp.where(valid, kk, 0.0)
    q = _silu(u_ref[rows, COL_Q:COL_Q + HG_WIDTH])
    v = u_ref[rows, COL_I:COL_I + HG_WIDTH].astype(BF16)
    gate = _silu(u_ref[rows, COL_G:COL_G + HG_WIDTH])

    causal = _lower_tri(c)
    b = _cumsum_rows(causal.astype(BF16), logf)
    b_last = b[c - 1:c, :]
    qt = (q * jnp.exp(b)).astype(BF16)
    kt = (kk * jnp.exp(-b)).astype(BF16)
    kh = (kk * jnp.exp(b_last - b)).astype(BF16)
    decay = jnp.exp(b_last)

    for h in range(HG_HEADS):
        sl = slice(h * HG_DIM, (h + 1) * HG_DIM)
        st = st_scr[h]
        scores = jnp.where(causal, _dot_nt(qt[:, sl], kt[:, sl]), 0.0).astype(BF16)
        o = _dot(scores, v[:, sl]) + _dot_nt(qt[:, sl], st.astype(BF16))
        st_scr[h] = st * decay[:, sl] + _dot_tn(v[:, sl], kh[:, sl])
        o = o * lax.rsqrt(jnp.mean(o * o, axis=-1, keepdims=True) + EPS)
        mix_ref[rows, sl] = (o * hgn[:, sl] * gate[:, sl]).astype(mix_ref.dtype)


def _ssd_chunk(r0, n_pad, u_ref, cw, cb, dtb, a_neg, dsk, ssn, mix_ref, hs_scr, xpad, ybuf):
    c = SSM_CHUNK
    rows = slice(r0, r0 + c)
    conv = cb
    for j in range(CONV_WIDTH):
        off = PAD_ROWS - (CONV_WIDTH - 1) + j + r0
        conv = conv + cw[j:j + 1, :] * xpad[off:off + c, :]
    act = _silu(conv)
    xs = act[:, 0:SSM_WIDTH]
    bm = act[:, SSM_WIDTH:SSM_WIDTH + SSM_GROUPS * SSM_STATE].astype(BF16)
    cm = act[:, SSM_WIDTH + SSM_GROUPS * SSM_STATE:].astype(BF16)

    dt = jax.nn.softplus(u_ref[rows, COL_DT:COL_DT + LANES] + dtb)
    if n_pad > r0:
        dt = jnp.where(lax.broadcasted_iota(jnp.int32, (c, LANES), 0) >= (n_pad - r0), dt, 0.0)
    causal = _lower_tri(c)
    cum = _cumsum_rows(causal.astype(BF16), dt * a_neg)
    last = cum[c - 1:c, :]
    cum_t = cum.T
    dt_t = dt.T
    e_cum = jnp.exp(cum)
    w_in = dt * jnp.exp(last - cum)
    e_last = jnp.exp(last)
    yield

    for g in range(SSM_GROUPS):
        bg = bm[:, g * SSM_STATE:(g + 1) * SSM_STATE]
        cg = cm[:, g * SSM_STATE:(g + 1) * SSM_STATE]
        cbt = _dot_nt(cg, bg)
        for rr in range(SSM_HEADS // SSM_GROUPS):
            r = g * (SSM_HEADS // SSM_GROUPS) + rr
            hsl = slice(r * SSM_HEAD_DIM, (r + 1) * SSM_HEAD_DIM)
            seg = jnp.exp(jnp.where(causal, cum[:, r:r + 1] - cum_t[r:r + 1, :], -jnp.inf))
            m = (cbt * seg * dt_t[r:r + 1, :]).astype(BF16)
            xr = xs[:, hsl]
            hr = hs_scr[r]
            y = _dot(m, xr.astype(BF16)) + _dot_nt(cg, hr.astype(BF16)) * e_cum[:, r:r + 1]
            xw = (xr * w_in[:, r:r + 1]).astype(BF16)
            hs_scr[r] = hr * e_last[:, r:r + 1] + _dot_tn(xw, bg)
            ybuf[rows, hsl] = y + dsk[:, hsl] * xr
        yield

    yz = ybuf[rows, :] * _silu(u_ref[rows, COL_Z:COL_Z + SSM_WIDTH])
    gw = SSM_WIDTH // SSM_GROUPS
    for g in range(SSM_GROUPS):
        sl = slice(g * gw, (g + 1) * gw)
        seg = yz[:, sl]
        seg = seg * lax.rsqrt(jnp.mean(seg * seg, axis=-1, keepdims=True) + EPS) * ssn[:, sl]
        mix_ref[rows, HG_WIDTH + g * gw:HG_WIDTH + (g + 1) * gw] = seg.astype(mix_ref.dtype)


def _mixer_steps(n_pad, tb, u_ref, param_refs, mix_ref, st_scr, hs_scr, xpad, ybuf):
    lbl_ref, hgn_ref, cw_ref, cb_ref, dtb_ref, alog_ref, dsk_ref, ssn_ref = param_refs
    xbc = u_ref[:, COL_XBC:COL_XBC + CONV_DIM]
    if n_pad:
        xbc = jnp.where(lax.broadcasted_iota(jnp.int32, (tb, CONV_DIM), 0) >= n_pad, xbc, 0.0)
    xpad[PAD_ROWS:PAD_ROWS + tb, :] = xbc

    lb = _forget_lower_bound(lbl_ref[...])
    hgn = hgn_ref[...]
    for r0 in range(0, tb, HG_CHUNK):
        _hgrn_chunk(r0, n_pad, u_ref, lb, hgn, mix_ref, st_scr)
        yield

    a_neg = -jnp.exp(alog_ref[...])
    for r0 in range(0, tb, SSM_CHUNK):
        yield from _ssd_chunk(r0, n_pad, u_ref, cw_ref[...], cb_ref[...], dtb_ref[...], a_neg, dsk_ref[...],
                              ssn_ref[...], mix_ref, hs_scr, xpad, ybuf)
        yield

    xpad[0:PAD_ROWS, :] = xpad[tb:tb + PAD_ROWS, :]


def _interleave(first, second):
    live = [first, second]
    while live:
        for gen in list(live):
            try:
                next(gen)
            except StopIteration:
                live.remove(gen)


def _mixer_block(*args):
    for _ in _mixer_steps(*args):
        pass


def _load_state(s0_ref, h0_ref, c0_ref, st_scr, hs_scr, xpad):
    for h in range(HG_HEADS):
        st_scr[h] = s0_ref[0, h].T
    hs_scr[...] = h0_ref[0]
    xpad[0:PAD_ROWS, :] = c0_ref[0]


def _store_state(s_out_ref, h_out_ref, c_out_ref, st_scr, hs_scr, xpad):
    for h in range(HG_HEADS):
        s_out_ref[0, h] = st_scr[h].T
    h_out_ref[0] = hs_scr[...]
    c_out_ref[0] = xpad[0:PAD_ROWS, :]


N_MIXER_PARAMS = 8


def _mixer_kernel(n_pad, tb, u_ref, *refs):
    param_refs = refs[:N_MIXER_PARAMS]
    s0_ref, h0_ref, c0_ref, mix_ref, s_out_ref, h_out_ref, c_out_ref, st_scr, hs_scr, xpad, ybuf = (
        refs[N_MIXER_PARAMS:])
    t = pl.program_id(1)

    @pl.when(t == 0)
    def _():
        _load_state(s0_ref, h0_ref, c0_ref, st_scr, hs_scr, xpad)

    _mixer_block(n_pad, tb, u_ref, param_refs, mix_ref, st_scr, hs_scr, xpad, ybuf)

    @pl.when(t == pl.num_programs(1) - 1)
    def _():
        _store_state(s_out_ref, h_out_ref, c_out_ref, st_scr, hs_scr, xpad)


def _mixer(u, u_block0, nb, nt, tb, n_pad, params, s0, h0, c0):
    lbl, hgn, cw, cb, dtb, alog, dsk, ssn = params
    small = [_resident(p.shape) for p in params]
    shared = lambda shape: pl.BlockSpec((1,) + shape[1:], lambda b, t: (0,) * len(shape))
    per_seq = lambda shape: pl.BlockSpec((1,) + shape, lambda b, t: (b,) + (0,) * len(shape))
    s_shape = (HG_HEADS, HG_DIM, HG_DIM)
    h_shape = (SSM_HEADS, SSM_HEAD_DIM, SSM_STATE)
    c_shape = (PAD_ROWS, CONV_DIM)
    return pl.pallas_call(
        functools.partial(_mixer_kernel, n_pad, tb),
        grid=(nb, nt),
        in_specs=[pl.BlockSpec((tb, D_IN_PAD), lambda b, t: (u_block0 + b * nt + t, 0))] + small
        + [shared(s0.shape), shared(h0.shape), shared(c0.shape)],
        out_specs=[pl.BlockSpec((tb, D_MODEL), lambda b, t: (b * nt + t, 0)),
                   per_seq(s_shape), per_seq(h_shape), per_seq(c_shape)],
        out_shape=[jax.ShapeDtypeStruct((nb * nt * tb, D_MODEL), BF16),
                   jax.ShapeDtypeStruct((nb,) + s_shape, F32),
                   jax.ShapeDtypeStruct((nb,) + h_shape, F32),
                   jax.ShapeDtypeStruct((nb,) + c_shape, F32)],
        scratch_shapes=[pltpu.VMEM(s_shape, F32), pltpu.VMEM(h_shape, F32),
                        pltpu.VMEM((PAD_ROWS + tb, CONV_DIM), F32), pltpu.VMEM((tb, SSM_WIDTH), F32)],
        compiler_params=pltpu.CompilerParams(dimension_semantics=("arbitrary", "arbitrary"),
                                             vmem_limit_bytes=VMEM_LIMIT_BYTES),
        name="mixer",
    )(u, *params, s0, h0, c0)


def _ffn_in_mixer_kernel(tm, nt, x_ref, n1_ref, wg_ref, wu_ref, wd_ref, nm_ref, win_ref, *refs):
    param_refs = refs[:N_MIXER_PARAMS]
    (s0_ref, h0_ref, c0_ref, h1_ref, mix_ref, s_out_ref, h_out_ref, c_out_ref,
     u_even, u_odd, st_scr, hs_scr, xpad, ybuf) = refs[N_MIXER_PARAMS:]
    s = pl.program_id(0)
    scanned = jnp.maximum(s - 1, 0)

    @pl.when(s == 0)
    def _():
        u_odd[...] = jnp.zeros(u_odd.shape, F32)

    @pl.when(scanned % nt == 0)
    def _():
        _load_state(s0_ref, h0_ref, c0_ref, st_scr, hs_scr, xpad)

    def project_steps(u_write):
        x = x_ref[...]
        xn = _rms(x, n1_ref[...]).astype(BF16)
        yield
        acc = jnp.zeros((tm, D_MODEL), F32)
        for j in range(D_FF // FF_TILE):
            cols = slice(j * FF_TILE, (j + 1) * FF_TILE)
            g = _dot(xn, wg_ref[:, cols])
            u = _dot(xn, wu_ref[:, cols])
            acc = acc + _dot((_silu(g) * u).astype(BF16), wd_ref[cols, :])
            yield
        h1 = x + 0.5 * acc
        h1_ref[...] = h1
        hn = _rms(h1, nm_ref[...]).astype(BF16)
        for c0 in range(0, D_IN_PAD, D_MODEL):
            cols = slice(c0, min(c0 + D_MODEL, D_IN_PAD))
            yield
            u_write[:, cols] = _dot(hn, win_ref[:, cols])

    def project_and_scan(u_write, u_read):
        _interleave(project_steps(u_write),
                    _mixer_steps(0, tm, u_read, param_refs, mix_ref, st_scr, hs_scr, xpad, ybuf))

    pl.when(s % 2 == 0)(lambda: project_and_scan(u_even, u_odd))
    pl.when(s % 2 == 1)(lambda: project_and_scan(u_odd, u_even))

    @pl.when((s >= 1) & (scanned % nt == nt - 1))
    def _():
        _store_state(s_out_ref, h_out_ref, c_out_ref, st_scr, hs_scr, xpad)


def _ffn_in_mixer(x, nb, nt, tm, dense_params, params, s0, h0, c0):
    n_blocks = nb * nt
    last = n_blocks - 1
    dense = [_resident(p.shape) for p in dense_params]
    small = [_resident(p.shape) for p in params]
    shared = lambda shape: pl.BlockSpec((1,) + shape[1:], lambda s: (0,) * len(shape))
    per_seq = lambda shape: pl.BlockSpec((1,) + shape, lambda s: (jnp.maximum(s - 1, 0) // nt,) + (0,) * len(shape))
    s_shape = (HG_HEADS, HG_DIM, HG_DIM)
    h_shape = (SSM_HEADS, SSM_HEAD_DIM, SSM_STATE)
    c_shape = (PAD_ROWS, CONV_DIM)
    return pl.pallas_call(
        functools.partial(_ffn_in_mixer_kernel, tm, nt),
        grid=(n_blocks + 1,),
        in_specs=[pl.BlockSpec((tm, D_MODEL), lambda s: (jnp.minimum(s, last), 0))] + dense + small
        + [shared(s0.shape), shared(h0.shape), shared(c0.shape)],
        out_specs=[pl.BlockSpec((tm, D_MODEL), lambda s: (jnp.minimum(s, last), 0)),
                   pl.BlockSpec((tm, D_MODEL), lambda s: (jnp.maximum(s - 1, 0), 0)),
                   per_seq(s_shape), per_seq(h_shape), per_seq(c_shape)],
        out_shape=[jax.ShapeDtypeStruct((n_blocks * tm, D_MODEL), F32),
                   jax.ShapeDtypeStruct((n_blocks * tm, D_MODEL), BF16),
                   jax.ShapeDtypeStruct((nb,) + s_shape, F32),
                   jax.ShapeDtypeStruct((nb,) + h_shape, F32),
                   jax.ShapeDtypeStruct((nb,) + c_shape, F32)],
        scratch_shapes=[pltpu.VMEM((tm, D_IN_PAD), F32), pltpu.VMEM((tm, D_IN_PAD), F32),
                        pltpu.VMEM(s_shape, F32), pltpu.VMEM(h_shape, F32),
                        pltpu.VMEM((PAD_ROWS + tm, CONV_DIM), F32), pltpu.VMEM((tm, SSM_WIDTH), F32)],
        compiler_params=pltpu.CompilerParams(dimension_semantics=("arbitrary",),
                                             vmem_limit_bytes=VMEM_LIMIT_BYTES),
        name="ffn_in_mixer",
    )(x, *dense_params, *params, s0, h0, c0)


def _column_tile(row):
    return jnp.broadcast_to(row, (LANES, LANES)).T


def _sample_kernel(nbs, u_ref, lbl_ref, hgn_ref, cw_ref, cb_ref, dtb_ref, alog_ref, dsk_ref, ssn_ref,
                   sh_ref, ss_ref, sc_ref, mix_ref, sh_out_ref, ss_out_ref, sc_out_ref, obuf, ybuf):
    lb = _forget_lower_bound(lbl_ref[...])
    fz = u_ref[:, COL_F:COL_F + HG_WIDTH]
    f = lb + (1.0 - lb) * jax.nn.sigmoid(fz)
    kk = (1.0 - lb) * jax.nn.sigmoid(-fz)
    q = _silu(u_ref[:, COL_Q:COL_Q + HG_WIDTH])
    v = u_ref[:, COL_I:COL_I + HG_WIDTH]

    xbc = u_ref[:, COL_XBC:COL_XBC + CONV_DIM]
    cw = cw_ref[...]
    conv = cb_ref[...] + cw[CONV_WIDTH - 1:CONV_WIDTH, :] * xbc
    for j in range(CONV_WIDTH - 1):
        conv = conv + cw[j:j + 1, :] * sc_ref[j]
    for j in range(CONV_WIDTH - 2):
        sc_out_ref[j] = sc_ref[j + 1]
    sc_out_ref[CONV_WIDTH - 2] = xbc
    act = _silu(conv)
    xs = act[:, 0:SSM_WIDTH]
    bm = act[:, SSM_WIDTH:SSM_WIDTH + SSM_GROUPS * SSM_STATE]
    cm = act[:, SSM_WIDTH + SSM_GROUPS * SSM_STATE:]
    dt = jax.nn.softplus(u_ref[:, COL_DT:COL_DT + LANES] + dtb_ref[...])
    d_a = jnp.exp(dt * (-jnp.exp(alog_ref[...])))

    top_half = lax.broadcasted_iota(jnp.int32, (LANES, LANES), 0) < SSM_HEAD_DIM
    heads_per_group = SSM_HEADS // SSM_GROUPS
    for j in range(nbs):
        row = slice(j, j + 1)
        for h in range(HG_HEADS):
            sl = slice(h * HG_DIM, (h + 1) * HG_DIM)
            s_new = sh_ref[j, h] * _column_tile(f[row, sl]) + _column_tile(kk[row, sl]) * v[row, sl]
            sh_out_ref[j, h] = s_new
            obuf[row, sl] = jnp.sum(s_new * _column_tile(q[row, sl]), axis=0, keepdims=True)
        for rp in range(SSM_HEADS // 2):
            r0, r1 = 2 * rp, 2 * rp + 1
            g = r0 // heads_per_group
            sl = slice(rp * LANES, (rp + 1) * LANES)
            gsl = slice(g * SSM_STATE, (g + 1) * SSM_STATE)
            h2 = jnp.concatenate([ss_ref[j, r0], ss_ref[j, r1]], axis=0)
            da2 = jnp.where(top_half, d_a[row, r0:r0 + 1], d_a[row, r1:r1 + 1])
            dt2 = jnp.where(top_half, dt[row, r0:r0 + 1], dt[row, r1:r1 + 1])
            h_new = h2 * da2 + (dt2 * _column_tile(xs[row, sl])) * bm[row, gsl]
            ss_out_ref[j, r0] = h_new[0:SSM_HEAD_DIM]
            ss_out_ref[j, r1] = h_new[SSM_HEAD_DIM:]
            ybuf[row, sl] = jnp.sum((h_new * cm[row, gsl]).T, axis=0, keepdims=True)

    o = obuf[...]
    gate = _silu(u_ref[:, COL_G:COL_G + HG_WIDTH])
    hgn = hgn_ref[...]
    for h in range(HG_HEADS):
        sl = slice(h * HG_DIM, (h + 1) * HG_DIM)
        oh = o[:, sl]
        oh = oh * lax.rsqrt(jnp.mean(oh * oh, axis=-1, keepdims=True) + EPS)
        mix_ref[:, sl] = oh * hgn[:, sl] * gate[:, sl]
    yz = (ybuf[...] + dsk_ref[...] * xs) * _silu(u_ref[:, COL_Z:COL_Z + SSM_WIDTH])
    ssn = ssn_ref[...]
    gw = SSM_WIDTH // SSM_GROUPS
    for g in range(SSM_GROUPS):
        sl = slice(g * gw, (g + 1) * gw)
        seg = yz[:, sl]
        seg = seg * lax.rsqrt(jnp.mean(seg * seg, axis=-1, keepdims=True) + EPS) * ssn[:, sl]
        mix_ref[:, HG_WIDTH + g * gw:HG_WIDTH + (g + 1) * gw] = seg


def _sample_mixer(u, u_block0, n, nbs, params, sh, ss, sc_t):
    small = [_resident(p.shape) for p in params]
    return pl.pallas_call(
        functools.partial(_sample_kernel, nbs),
        grid=(n // nbs,),
        in_specs=[pl.BlockSpec((nbs, D_IN_PAD), lambda i: (u_block0 + i, 0))] + small
        + [pl.BlockSpec((nbs,) + sh.shape[1:], lambda i: (i, 0, 0, 0)),
           pl.BlockSpec((nbs,) + ss.shape[1:], lambda i: (i, 0, 0, 0)),
           pl.BlockSpec((CONV_WIDTH - 1, nbs, CONV_DIM), lambda i: (0, i, 0))],
        out_specs=[pl.BlockSpec((nbs, D_MODEL), lambda i: (i, 0)),
                   pl.BlockSpec((nbs,) + sh.shape[1:], lambda i: (i, 0, 0, 0)),
                   pl.BlockSpec((nbs,) + ss.shape[1:], lambda i: (i, 0, 0, 0)),
                   pl.BlockSpec((CONV_WIDTH - 1, nbs, CONV_DIM), lambda i: (0, i, 0))],
        out_shape=[jax.ShapeDtypeStruct((n, D_MODEL), F32),
                   jax.ShapeDtypeStruct(sh.shape, F32),
                   jax.ShapeDtypeStruct(ss.shape, F32),
                   jax.ShapeDtypeStruct(sc_t.shape, F32)],
        scratch_shapes=[pltpu.VMEM((nbs, HG_WIDTH), F32), pltpu.VMEM((nbs, SSM_WIDTH), F32)],
        compiler_params=pltpu.CompilerParams(dimension_semantics=("arbitrary",),
                                             vmem_limit_bytes=VMEM_LIMIT_BYTES),
        name="sample_mixer",
    )(u, *params, sh, ss, sc_t)


TM_PROMPT = 256
TM_OUT = 512
TM_SMALL = 128
SAMPLES_PER_STEP = 8


def _pad_lanes(row, value=0.0):
    return jnp.pad(row, ((0, 0), (0, LANES - row.shape[1])), constant_values=value)


def kernel(x_prompt, x_sample, state_hgrn, state_ssm, state_conv, meta_tokens, lb_logits, norm_ffn1, w_ffn1_gate, w_ffn1_up, w_ffn1_down, norm_mix, w_in, hg_norm, conv_w, conv_b, dt_bias, a_log, d_skip, ssm_norm, w_out, norm_ffn2, w_ffn2_gate, w_ffn2_up, w_ffn2_down, norm_final):
    bp, seq_p, _ = x_prompt.shape
    n_s = x_sample.shape[0]
    assert x_sample.shape[1] == 1 and n_s == TM_SMALL and seq_p % TM_PROMPT == 0
    layer = 0

    n1, nm, n2 = norm_ffn1[layer][None], norm_mix[layer][None], norm_ffn2[layer][None]
    nf = norm_final[None]
    wg1, wu1, wd1 = (w[layer].astype(BF16) for w in (w_ffn1_gate, w_ffn1_up, w_ffn1_down))
    wg2, wu2, wd2 = (w[layer].astype(BF16) for w in (w_ffn2_gate, w_ffn2_up, w_ffn2_down))
    win = jnp.pad(w_in[layer], ((0, 0), (0, D_IN_PAD - w_in.shape[2]))).astype(BF16)
    wo = w_out[layer].astype(BF16)
    mixer_params = (lb_logits, hg_norm[layer][None], conv_w[layer], conv_b[layer][None],
                    _pad_lanes(dt_bias[layer][None]), _pad_lanes(a_log[layer][None]),
                    jnp.repeat(d_skip[layer], SSM_HEAD_DIM)[None], ssm_norm[layer][None])

    n_pad = TM_SMALL - N_META
    x_small = jnp.concatenate([jnp.zeros((n_pad, D_MODEL), F32), meta_tokens, x_sample[:, 0]], axis=0)
    h1_small, u_small = _ffn_in(x_small, n1, wg1, wu1, wd1, nm, win, TM_SMALL)

    zeros_s = jnp.zeros((1, HG_HEADS, HG_DIM, HG_DIM), F32)
    zeros_h = jnp.zeros((1, SSM_HEADS, SSM_HEAD_DIM, SSM_STATE), F32)
    zeros_c = jnp.zeros((1, PAD_ROWS, CONV_DIM), F32)
    _, s_meta, h_meta, c_meta = _mixer(u_small, 0, 1, 1, TM_SMALL, n_pad, mixer_params,
                                       zeros_s, zeros_h, zeros_c)

    sc_t = jnp.swapaxes(state_conv[layer], 0, 1)
    mix_s, hgrn_s, ssm_s, conv_s_t = _sample_mixer(
        u_small, TM_SMALL // SAMPLES_PER_STEP, n_s, SAMPLES_PER_STEP, mixer_params,
        state_hgrn[layer], state_ssm[layer], sc_t)
    y_s = _ffn_out(h1_small, 1, mix_s, wo, n2, wg2, wu2, wd2, nf, TM_SMALL)

    xp = x_prompt.reshape(bp * seq_p, D_MODEL)
    h1_p, mix_p, hgrn_p, ssm_p, conv_p = _ffn_in_mixer(
        xp, bp, seq_p // TM_PROMPT, TM_PROMPT, (n1, wg1, wu1, wd1, nm, win), mixer_params,
        s_meta, h_meta, c_meta)
    y_p = _ffn_out(h1_p, 0, mix_p, wo, n2, wg2, wu2, wd2, nf, TM_OUT)

    keep = slice(PAD_ROWS - (CONV_WIDTH - 1), PAD_ROWS)
    return (y_p.reshape(bp, seq_p, D_MODEL),
            y_s.reshape(n_s, 1, D_MODEL),
            hgrn_p[None], ssm_p[None], conv_p[:, keep][None],
            hgrn_s[None], ssm_s[None], jnp.swapaxes(conv_s_t, 0, 1)[None])
```

```python
import functools

import jax
import jax.numpy as jnp
from jax import lax
from jax.experimental import pallas as pl
from jax.experimental.pallas import tpu as pltpu

F32 = jnp.float32
BF16 = jnp.bfloat16

D_MODEL = 1024
D_FF = 2816
N_META = 16
HG_WIDTH = 512
HG_HEADS = 4
HG_DIM = 128
SSM_WIDTH = 512
SSM_HEADS = 8
SSM_HEAD_DIM = 64
SSM_GROUPS = 2
SSM_STATE = 128
CONV_WIDTH = 4
CONV_DIM = SSM_WIDTH + 2 * SSM_GROUPS * SSM_STATE
EPS = 1e-6

LANES = 128
SUBLANES = 8
VMEM_LIMIT_BYTES = 56 * 1024 * 1024

COL_Q = 0
COL_F = HG_WIDTH
COL_I = 2 * HG_WIDTH
COL_G = 3 * HG_WIDTH
COL_Z = 4 * HG_WIDTH
COL_XBC = COL_Z + SSM_WIDTH
COL_DT = COL_XBC + CONV_DIM
D_IN_PAD = COL_DT + LANES

FF_TILE = 256
HG_CHUNK = 64
SSM_CHUNK = 128
PAD_ROWS = SUBLANES
HG_PAIRS = HG_HEADS // 2
HEADS_PER_GROUP = SSM_HEADS // SSM_GROUPS
GROUP_WIDTH = SSM_WIDTH // SSM_GROUPS
assert 2 * SSM_HEAD_DIM == LANES and 2 * HG_DIM == FF_TILE


def _dot(a, b):
    return jnp.dot(a, b, preferred_element_type=F32)


def _dot_nt(a, b):
    return lax.dot_general(a, b, (((1,), (1,)), ((), ())), preferred_element_type=F32)


def _dot_tn(a, b):
    return lax.dot_general(a, b, (((0,), (0,)), ((), ())), preferred_element_type=F32)


def _rms(x, w):
    return x * lax.rsqrt(jnp.mean(x * x, axis=-1, keepdims=True) + EPS) * w


def _silu(x):
    return x * jax.nn.sigmoid(x)


def _swiglu(xn, wg_ref, wu_ref, wd_ref):
    acc = jnp.zeros((xn.shape[0], D_MODEL), F32)
    for j in range(D_FF // FF_TILE):
        cols = slice(j * FF_TILE, (j + 1) * FF_TILE)
        g = _dot(xn, wg_ref[:, cols])
        u = _dot(xn, wu_ref[:, cols])
        acc = acc + _dot((_silu(g) * u).astype(BF16), wd_ref[cols, :])
    return acc


def _cumsum_rows(tri, a):
    a1 = a.astype(BF16)
    r1 = a - a1.astype(F32)
    a2 = r1.astype(BF16)
    a3 = (r1 - a2.astype(F32)).astype(BF16)
    return _dot(tri, a1) + _dot(tri, a2) + _dot(tri, a3)


def _lower_tri(n):
    row = lax.broadcasted_iota(jnp.int32, (n, n), 0)
    col = lax.broadcasted_iota(jnp.int32, (n, n), 1)
    return row >= col


def _chunked_tri(n, chunk):
    assert chunk & (chunk - 1) == 0
    row = lax.broadcasted_iota(jnp.int32, (n, n), 0)
    col = lax.broadcasted_iota(jnp.int32, (n, n), 1)
    same_chunk = (row ^ col) < chunk
    return ((row >= col) & same_chunk).astype(BF16)


def _tiled_lower_tri(chunk, reps):
    assert chunk & (chunk - 1) == 0
    row = lax.broadcasted_iota(jnp.int32, (chunk, reps * chunk), 0)
    col = lax.broadcasted_iota(jnp.int32, (chunk, reps * chunk), 1)
    return row >= (col & (chunk - 1))


def _block_diag(blocks):
    n = len(blocks)
    r, c = blocks[0].shape
    rows = []
    for i, blk in enumerate(blocks):
        parts = []
        if i:
            parts.append(jnp.zeros((r, c * i), blk.dtype))
        parts.append(blk)
        if i < n - 1:
            parts.append(jnp.zeros((r, c * (n - 1 - i)), blk.dtype))
        rows.append(jnp.concatenate(parts, axis=1))
    return jnp.concatenate(rows, axis=0)


def _forget_lower_bound(lbl):
    l0, l1 = lbl[0:1], lbl[1:2]
    m = jnp.maximum(l0, l1)
    e0, e1 = jnp.exp(l0 - m), jnp.exp(l1 - m)
    return e0 / (e0 + e1)


def _resident(shape):
    nd = len(shape)
    return pl.BlockSpec(shape, lambda *_: (0,) * nd, pipeline_mode=pl.Buffered(1))


def _compiler_params(n_grid_axes):
    return pltpu.CompilerParams(dimension_semantics=("arbitrary",) * n_grid_axes,
                                vmem_limit_bytes=VMEM_LIMIT_BYTES)


def _ffn1_kernel(x_ref, n1_ref, wg_ref, wu_ref, wd_ref, h1_ref):
    x = x_ref[...]
    xn = _rms(x, n1_ref[...]).astype(BF16)
    h1_ref[...] = x + 0.5 * _swiglu(xn, wg_ref, wu_ref, wd_ref)


def _ffn1(x, n1, wg, wu, wd, tm):
    n = x.shape[0]
    return pl.pallas_call(
        _ffn1_kernel,
        grid=(n // tm,),
        in_specs=[pl.BlockSpec((tm, D_MODEL), lambda i: (i, 0)),
                  _resident(n1.shape), _resident(wg.shape), _resident(wu.shape), _resident(wd.shape)],
        out_specs=pl.BlockSpec((tm, D_MODEL), lambda i: (i, 0)),
        out_shape=jax.ShapeDtypeStruct((n, D_MODEL), F32),
        compiler_params=_compiler_params(1),
        name="ffn1",
    )(x, n1, wg, wu, wd)


def _ffn_in_kernel(x_ref, n1_ref, wg_ref, wu_ref, wd_ref, nm_ref, win_ref, h1_ref, u_ref):
    x = x_ref[...]
    xn = _rms(x, n1_ref[...]).astype(BF16)
    h1 = x + 0.5 * _swiglu(xn, wg_ref, wu_ref, wd_ref)
    h1_ref[...] = h1
    hn = _rms(h1, nm_ref[...]).astype(BF16)
    u_ref[...] = _dot(hn, win_ref[...])


def _ffn_in(x, n1, wg, wu, wd, nm, win, tm):
    n = x.shape[0]
    return pl.pallas_call(
        _ffn_in_kernel,
        grid=(n // tm,),
        in_specs=[pl.BlockSpec((tm, D_MODEL), lambda i: (i, 0)),
                  _resident(n1.shape), _resident(wg.shape), _resident(wu.shape), _resident(wd.shape),
                  _resident(nm.shape), _resident(win.shape)],
        out_specs=[pl.BlockSpec((tm, D_MODEL), lambda i: (i, 0)),
                   pl.BlockSpec((tm, D_IN_PAD), lambda i: (i, 0))],
        out_shape=[jax.ShapeDtypeStruct((n, D_MODEL), F32),
                   jax.ShapeDtypeStruct((n, D_IN_PAD), F32)],
        compiler_params=_compiler_params(1),
        name="ffn_in",
    )(x, n1, wg, wu, wd, nm, win)


def _ffn_out_kernel(h1_ref, mix_ref, wo_ref, n2_ref, wg_ref, wu_ref, wd_ref, nf_ref, y_ref):
    h2 = h1_ref[...] + _dot(mix_ref[...].astype(BF16), wo_ref[...])
    hn = _rms(h2, n2_ref[...]).astype(BF16)
    h3 = h2 + 0.5 * _swiglu(hn, wg_ref, wu_ref, wd_ref)
    y_ref[...] = _rms(h3, nf_ref[...])


def _ffn_out(h1, h1_block0, mix, wo, n2, wg, wu, wd, nf, tm):
    n = mix.shape[0]
    return pl.pallas_call(
        _ffn_out_kernel,
        grid=(n // tm,),
        in_specs=[pl.BlockSpec((tm, D_MODEL), lambda i: (i + h1_block0, 0)),
                  pl.BlockSpec((tm, D_MODEL), lambda i: (i, 0)),
                  _resident(wo.shape), _resident(n2.shape), _resident(wg.shape), _resident(wu.shape),
                  _resident(wd.shape), _resident(nf.shape)],
        out_specs=pl.BlockSpec((tm, D_MODEL), lambda i: (i, 0)),
        out_shape=jax.ShapeDtypeStruct((n, D_MODEL), F32),
        compiler_params=_compiler_params(1),
        name="ffn_out",
    )(h1, mix, wo, n2, wg, wu, wd, nf)


def _head_slices(a, width):
    return [a[:, i:i + width] for i in range(0, a.shape[1], width)]


def _hgrn_block(n_pad, tb, u_ref, lb, hgn, mix_ref, std_scr):
    c = HG_CHUNK
    fz = u_ref[:, COL_F:COL_F + HG_WIDTH]
    logf = jnp.log(lb + (1.0 - lb) * jax.nn.sigmoid(fz))
    kk = (1.0 - lb) * jax.nn.sigmoid(-fz)
    if n_pad:
        valid = lax.broadcasted_iota(jnp.int32, (tb, HG_WIDTH), 0) >= n_pad
        logf = jnp.where(valid, logf, 0.0)
        kk = jnp.where(valid, kk, 0.0)
    q = _silu(u_ref[:, COL_Q:COL_Q + HG_WIDTH])
    v = u_ref[:, COL_I:COL_I + HG_WIDTH].astype(BF16)
    gate = _silu(u_ref[:, COL_G:COL_G + HG_WIDTH])

    b = _cumsum_rows(_chunked_tri(tb, c), logf)
    qt = (q * jnp.exp(b)).astype(BF16)
    kt = (kk * jnp.exp(-b)).astype(BF16)
    causal = _tiled_lower_tri(c, HG_HEADS)

    for r0 in range(0, tb, c):
        rows = slice(r0, r0 + c)
        b_c = b[rows]
        b_last = b_c[c - 1:c, :]
        kh_c = (kk[rows] * jnp.exp(b_last - b_c)).astype(BF16)
        decay = jnp.exp(b_last)
        qt_c = qt[rows]
        kd = _block_diag(_head_slices(kt[rows], HG_DIM))
        vd = _block_diag(_head_slices(v[rows], HG_DIM))
        scores = jnp.where(causal, _dot_nt(qt_c, kd), 0.0).astype(BF16)
        o = _dot(scores, vd)
        o_prev = []
        for p in range(HG_PAIRS):
            lanes = slice(p * 2 * HG_DIM, (p + 1) * 2 * HG_DIM)
            std = std_scr[p]
            o_prev.append(_dot_nt(qt_c[:, lanes], std.astype(BF16)))
            khd = _block_diag(_head_slices(kh_c[:, lanes], HG_DIM))
            std_scr[p] = std * decay[:, lanes] + _dot_tn(vd[p * 2 * c:(p + 1) * 2 * c, lanes], khd)
        o = o + jnp.concatenate(o_prev, axis=1)
        for h in range(HG_HEADS):
            sl = slice(h * HG_DIM, (h + 1) * HG_DIM)
            oh = o[:, sl]
            oh = oh * lax.rsqrt(jnp.mean(oh * oh, axis=-1, keepdims=True) + EPS)
            mix_ref[rows, sl] = (oh * hgn[:, sl] * gate[rows, sl]).astype(mix_ref.dtype)


def _pair_columns(a, r0, r1, first_half):
    shape = (a.shape[0], LANES)
    return jnp.where(first_half, jnp.broadcast_to(a[:, r0:r0 + 1], shape), jnp.broadcast_to(a[:, r1:r1 + 1], shape))


def _group_columns(a, g, first_half):
    r = g * HEADS_PER_GROUP
    return jnp.concatenate([_pair_columns(a, r + i, r + i + 1, first_half)
                            for i in range(0, HEADS_PER_GROUP, 2)], axis=1)


def _ssd_block(n_pad, tb, u_ref, cw, cb, dtb, a_neg, dsk, ssn, mix_ref, hg_scr, xpad):
    c = SSM_CHUNK
    conv = cb
    for j in range(CONV_WIDTH):
        off = PAD_ROWS - (CONV_WIDTH - 1) + j
        conv = conv + cw[j:j + 1, :] * xpad[off:off + tb, :]
    act = _silu(conv)
    xs = act[:, 0:SSM_WIDTH]
    bm = act[:, SSM_WIDTH:SSM_WIDTH + SSM_GROUPS * SSM_STATE].astype(BF16)
    cm = act[:, SSM_WIDTH + SSM_GROUPS * SSM_STATE:].astype(BF16)
    z_gate = _silu(u_ref[:, COL_Z:COL_Z + SSM_WIDTH])

    dt = jax.nn.softplus(u_ref[:, COL_DT:COL_DT + LANES] + dtb)
    if n_pad:
        dt = jnp.where(lax.broadcasted_iota(jnp.int32, (tb, LANES), 0) >= n_pad, dt, 0.0)
    cum = _cumsum_rows(_chunked_tri(tb, c), dt * a_neg)
    causal = _lower_tri(c)
    first_half = lax.broadcasted_iota(jnp.int32, (1, LANES), 1) < SSM_HEAD_DIM
    zeros = jnp.zeros((c, LANES), BF16)

    for r0 in range(0, tb, c):
        rows = slice(r0, r0 + c)
        cum_c = cum[rows]
        dt_c = dt[rows]
        last = cum_c[c - 1:c, :]
        cum_t = cum_c.T
        dt_t = dt_c.T
        e_cum = jnp.exp(cum_c)
        w_in = dt_c * jnp.exp(last - cum_c)
        e_last = jnp.exp(last)
        for g in range(SSM_GROUPS):
            glanes = slice(g * GROUP_WIDTH, (g + 1) * GROUP_WIDTH)
            bg = bm[rows, g * SSM_STATE:(g + 1) * SSM_STATE]
            cg = cm[rows, g * SSM_STATE:(g + 1) * SSM_STATE]
            xg = xs[rows, glanes]
            cbt = _dot_nt(cg, bg)
            m_heads = []
            for rr in range(HEADS_PER_GROUP):
                r = g * HEADS_PER_GROUP + rr
                seg = jnp.exp(jnp.where(causal, cum_c[:, r:r + 1] - cum_t[r:r + 1, :], -jnp.inf))
                m_heads.append((cbt * seg * dt_t[r:r + 1, :]).astype(BF16))
            xb = xg.astype(BF16)
            xd_rows = []
            for rr in range(HEADS_PER_GROUP):
                tile = xb[:, (rr // 2) * LANES:(rr // 2 + 1) * LANES]
                tile = jnp.where(first_half if rr % 2 == 0 else ~first_half, tile, jnp.zeros_like(tile))
                xd_rows.append(jnp.concatenate([tile, zeros] if rr < 2 else [zeros, tile], axis=1))
            xd = jnp.concatenate(xd_rows, axis=0)
            hg = hg_scr[g]
            y = (_dot(jnp.concatenate(m_heads, axis=1), xd)
                 + _dot(cg, hg.astype(BF16)) * _group_columns(e_cum, g, first_half))
            xw = (xg * _group_columns(w_in, g, first_half)).astype(BF16)
            hg_scr[g] = hg * _group_columns(e_last, g, first_half) + _dot_tn(bg, xw)
            yz = (y + dsk[:, glanes] * xg) * z_gate[rows, glanes]
            yz = yz * lax.rsqrt(jnp.mean(yz * yz, axis=-1, keepdims=True) + EPS) * ssn[:, glanes]
            mix_ref[rows, HG_WIDTH + g * GROUP_WIDTH:HG_WIDTH + (g + 1) * GROUP_WIDTH] = yz.astype(mix_ref.dtype)


def _mixer_block(n_pad, tb, u_ref, param_refs, mix_ref, std_scr, hg_scr, xpad):
    lbl_ref, hgn_ref, cw_ref, cb_ref, dtb_ref, alog_ref, dsk_ref, ssn_ref = param_refs
    xbc = u_ref[:, COL_XBC:COL_XBC + CONV_DIM]
    if n_pad:
        xbc = jnp.where(lax.broadcasted_iota(jnp.int32, (tb, CONV_DIM), 0) >= n_pad, xbc, 0.0)
    xpad[PAD_ROWS:PAD_ROWS + tb, :] = xbc
    _hgrn_block(n_pad, tb, u_ref, _forget_lower_bound(lbl_ref[...]), hgn_ref[...], mix_ref, std_scr)
    _ssd_block(n_pad, tb, u_ref, cw_ref[...], cb_ref[...], dtb_ref[...], -jnp.exp(alog_ref[...]), dsk_ref[...],
               ssn_ref[...], mix_ref, hg_scr, xpad)
    xpad[0:PAD_ROWS, :] = xpad[tb:tb + PAD_ROWS, :]


def _load_state(s0_ref, h0_ref, c0_ref, std_scr, hg_scr, xpad):
    for p in range(HG_PAIRS):
        std_scr[p] = _block_diag([s0_ref[0, 2 * p].T, s0_ref[0, 2 * p + 1].T])
    for r in range(0, SSM_HEADS, 2):
        g, lane0 = r // HEADS_PER_GROUP, (r % HEADS_PER_GROUP) * SSM_HEAD_DIM
        hg_scr[g, :, lane0:lane0 + LANES] = jnp.concatenate([h0_ref[0, r], h0_ref[0, r + 1]], axis=0).T
    xpad[0:PAD_ROWS, :] = c0_ref[0]


def _store_state(s_out_ref, h_out_ref, c_out_ref, std_scr, hg_scr, xpad):
    for h in range(HG_HEADS):
        d0 = (h % 2) * HG_DIM
        s_out_ref[0, h] = std_scr[h // 2, d0:d0 + HG_DIM, d0:d0 + HG_DIM].T
    for r in range(0, SSM_HEADS, 2):
        g, lane0 = r // HEADS_PER_GROUP, (r % HEADS_PER_GROUP) * SSM_HEAD_DIM
        pair = hg_scr[g, :, lane0:lane0 + LANES].T
        h_out_ref[0, r] = pair[0:SSM_HEAD_DIM]
        h_out_ref[0, r + 1] = pair[SSM_HEAD_DIM:]
    c_out_ref[0] = xpad[0:PAD_ROWS, :]


N_MIXER_PARAMS = 8
STD_SHAPE = (HG_PAIRS, 2 * HG_DIM, 2 * HG_DIM)
HG_SHAPE = (SSM_GROUPS, SSM_STATE, GROUP_WIDTH)
S_SHAPE = (HG_HEADS, HG_DIM, HG_DIM)
H_SHAPE = (SSM_HEADS, SSM_HEAD_DIM, SSM_STATE)
C_SHAPE = (PAD_ROWS, CONV_DIM)


def _scan_kernel(n_pad, tb, project, *refs):
    if project:
        h1_ref, nm_ref, win_ref = refs[:3]
        refs = refs[3:]
    else:
        u_ref = refs[0]
        refs = refs[1:]
    param_refs = refs[:N_MIXER_PARAMS]
    s0_ref, h0_ref, c0_ref, mix_ref, s_out_ref, h_out_ref, c_out_ref = refs[N_MIXER_PARAMS:N_MIXER_PARAMS + 7]
    scratch = refs[N_MIXER_PARAMS + 7:]
    if project:
        u_ref, std_scr, hg_scr, xpad = scratch
    else:
        std_scr, hg_scr, xpad = scratch
    t = pl.program_id(1)

    @pl.when(t == 0)
    def _():
        _load_state(s0_ref, h0_ref, c0_ref, std_scr, hg_scr, xpad)

    if project:
        hn = _rms(h1_ref[...], nm_ref[...]).astype(BF16)
        u_ref[...] = _dot(hn, win_ref[...])
    _mixer_block(n_pad, tb, u_ref, param_refs, mix_ref, std_scr, hg_scr, xpad)

    @pl.when(t == pl.num_programs(1) - 1)
    def _():
        _store_state(s_out_ref, h_out_ref, c_out_ref, std_scr, hg_scr, xpad)


def _scan(src, src_block0, nb, nt, tb, n_pad, proj, params, s0, h0, c0):
    lead = tuple(proj) + tuple(params) if proj else tuple(params)
    shared = lambda shape: pl.BlockSpec((1,) + shape[1:], lambda b, t: (0,) * len(shape))
    per_seq = lambda shape: pl.BlockSpec((1,) + shape, lambda b, t: (b,) + (0,) * len(shape))
    scratch = [pltpu.VMEM(STD_SHAPE, F32), pltpu.VMEM(HG_SHAPE, F32), pltpu.VMEM((PAD_ROWS + tb, CONV_DIM), F32)]
    if proj:
        scratch = [pltpu.VMEM((tb, D_IN_PAD), F32)] + scratch
    return pl.pallas_call(
        functools.partial(_scan_kernel, n_pad, tb, bool(proj)),
        grid=(nb, nt),
        in_specs=[pl.BlockSpec((tb, src.shape[1]), lambda b, t: (src_block0 + b * nt + t, 0))]
        + [_resident(p.shape) for p in lead]
        + [shared(s0.shape), shared(h0.shape), shared(c0.shape)],
        out_specs=[pl.BlockSpec((tb, D_MODEL), lambda b, t: (b * nt + t, 0)),
                   per_seq(S_SHAPE), per_seq(H_SHAPE), per_seq(C_SHAPE)],
        out_shape=[jax.ShapeDtypeStruct((nb * nt * tb, D_MODEL), BF16),
                   jax.ShapeDtypeStruct((nb,) + S_SHAPE, F32),
                   jax.ShapeDtypeStruct((nb,) + H_SHAPE, F32),
                   jax.ShapeDtypeStruct((nb,) + C_SHAPE, F32)],
        scratch_shapes=scratch,
        compiler_params=_compiler_params(2),
        name="proj_mixer" if proj else "mixer",
    )(src, *lead, s0, h0, c0)


def _column_tile(row):
    return jnp.broadcast_to(row, (LANES, LANES)).T


def _sample_kernel(nbs, u_ref, lbl_ref, hgn_ref, cw_ref, cb_ref, dtb_ref, alog_ref, dsk_ref, ssn_ref,
                   sh_ref, ss_ref, sc_ref, mix_ref, sh_out_ref, ss_out_ref, sc_out_ref, obuf, ybuf):
    lb = _forget_lower_bound(lbl_ref[...])
    fz = u_ref[:, COL_F:COL_F + HG_WIDTH]
    f = lb + (1.0 - lb) * jax.nn.sigmoid(fz)
    kk = (1.0 - lb) * jax.nn.sigmoid(-fz)
    q = _silu(u_ref[:, COL_Q:COL_Q + HG_WIDTH])
    v = u_ref[:, COL_I:COL_I + HG_WIDTH]

    xbc = u_ref[:, COL_XBC:COL_XBC + CONV_DIM]
    cw = cw_ref[...]
    conv = cb_ref[...] + cw[CONV_WIDTH - 1:CONV_WIDTH, :] * xbc
    for j in range(CONV_WIDTH - 1):
        conv = conv + cw[j:j + 1, :] * sc_ref[j]
    for j in range(CONV_WIDTH - 2):
        sc_out_ref[j] = sc_ref[j + 1]
    sc_out_ref[CONV_WIDTH - 2] = xbc
    act = _silu(conv)
    xs = act[:, 0:SSM_WIDTH]
    bm = act[:, SSM_WIDTH:SSM_WIDTH + SSM_GROUPS * SSM_STATE]
    cm = act[:, SSM_WIDTH + SSM_GROUPS * SSM_STATE:]
    dt = jax.nn.softplus(u_ref[:, COL_DT:COL_DT + LANES] + dtb_ref[...])
    d_a = jnp.exp(dt * (-jnp.exp(alog_ref[...])))

    top_half = lax.broadcasted_iota(jnp.int32, (LANES, LANES), 0) < SSM_HEAD_DIM
    for j in range(nbs):
        row = slice(j, j + 1)
        for h in range(HG_HEADS):
            sl = slice(h * HG_DIM, (h + 1) * HG_DIM)
            s_new = sh_ref[j, h] * _column_tile(f[row, sl]) + _column_tile(kk[row, sl]) * v[row, sl]
            sh_out_ref[j, h] = s_new
            obuf[row, sl] = jnp.sum(s_new * _column_tile(q[row, sl]), axis=0, keepdims=True)
        for rp in range(SSM_HEADS // 2):
            r0, r1 = 2 * rp, 2 * rp + 1
            g = r0 // HEADS_PER_GROUP
            sl = slice(rp * LANES, (rp + 1) * LANES)
            gsl = slice(g * SSM_STATE, (g + 1) * SSM_STATE)
            h2 = jnp.concatenate([ss_ref[j, r0], ss_ref[j, r1]], axis=0)
            da2 = jnp.where(top_half, d_a[row, r0:r0 + 1], d_a[row, r1:r1 + 1])
            dt2 = jnp.where(top_half, dt[row, r0:r0 + 1], dt[row, r1:r1 + 1])
            h_new = h2 * da2 + (dt2 * _column_tile(xs[row, sl])) * bm[row, gsl]
            ss_out_ref[j, r0] = h_new[0:SSM_HEAD_DIM]
            ss_out_ref[j, r1] = h_new[SSM_HEAD_DIM:]
            ybuf[row, sl] = jnp.sum((h_new * cm[row, gsl]).T, axis=0, keepdims=True)

    o = obuf[...]
    gate = _silu(u_ref[:, COL_G:COL_G + HG_WIDTH])
    hgn = hgn_ref[...]
    for h in range(HG_HEADS):
        sl = slice(h * HG_DIM, (h + 1) * HG_DIM)
        oh = o[:, sl]
        oh = oh * lax.rsqrt(jnp.mean(oh * oh, axis=-1, keepdims=True) + EPS)
        mix_ref[:, sl] = oh * hgn[:, sl] * gate[:, sl]
    yz = (ybuf[...] + dsk_ref[...] * xs) * _silu(u_ref[:, COL_Z:COL_Z + SSM_WIDTH])
    ssn = ssn_ref[...]
    for g in range(SSM_GROUPS):
        sl = slice(g * GROUP_WIDTH, (g + 1) * GROUP_WIDTH)
        seg = yz[:, sl]
        seg = seg * lax.rsqrt(jnp.mean(seg * seg, axis=-1, keepdims=True) + EPS) * ssn[:, sl]
        mix_ref[:, HG_WIDTH + g * GROUP_WIDTH:HG_WIDTH + (g + 1) * GROUP_WIDTH] = seg


def _sample_mixer(u, u_block0, n, nbs, params, sh, ss, sc_t):
    small = [_resident(p.shape) for p in params]
    return pl.pallas_call(
        functools.partial(_sample_kernel, nbs),
        grid=(n // nbs,),
        in_specs=[pl.BlockSpec((nbs, D_IN_PAD), lambda i: (u_block0 + i, 0))] + small
        + [pl.BlockSpec((nbs,) + sh.shape[1:], lambda i: (i, 0, 0, 0)),
           pl.BlockSpec((nbs,) + ss.shape[1:], lambda i: (i, 0, 0, 0)),
           pl.BlockSpec((CONV_WIDTH - 1, nbs, CONV_DIM), lambda i: (0, i, 0))],
        out_specs=[pl.BlockSpec((nbs, D_MODEL), lambda i: (i, 0)),
                   pl.BlockSpec((nbs,) + sh.shape[1:], lambda i: (i, 0, 0, 0)),
                   pl.BlockSpec((nbs,) + ss.shape[1:], lambda i: (i, 0, 0, 0)),
                   pl.BlockSpec((CONV_WIDTH - 1, nbs, CONV_DIM), lambda i: (0, i, 0))],
        out_shape=[jax.ShapeDtypeStruct((n, D_MODEL), F32),
                   jax.ShapeDtypeStruct(sh.shape, F32),
                   jax.ShapeDtypeStruct(ss.shape, F32),
                   jax.ShapeDtypeStruct(sc_t.shape, F32)],
        scratch_shapes=[pltpu.VMEM((nbs, HG_WIDTH), F32), pltpu.VMEM((nbs, SSM_WIDTH), F32)],
        compiler_params=_compiler_params(1),
        name="sample_mixer",
    )(u, *params, sh, ss, sc_t)


TM_DENSE = 512
TM_SCAN = 256
TM_SMALL = 128
SAMPLES_PER_STEP = 8


def _pad_lanes(row, value=0.0):
    return jnp.pad(row, ((0, 0), (0, LANES - row.shape[1])), constant_values=value)


def kernel(x_prompt, x_sample, state_hgrn, state_ssm, state_conv, meta_tokens, lb_logits, norm_ffn1, w_ffn1_gate, w_ffn1_up, w_ffn1_down, norm_mix, w_in, hg_norm, conv_w, conv_b, dt_bias, a_log, d_skip, ssm_norm, w_out, norm_ffn2, w_ffn2_gate, w_ffn2_up, w_ffn2_down, norm_final):
    bp, seq_p, _ = x_prompt.shape
    n_s = x_sample.shape[0]
    assert x_sample.shape[1] == 1 and n_s == TM_SMALL and seq_p % TM_SCAN == 0 and (bp * seq_p) % TM_DENSE == 0
    layer = 0

    n1, nm, n2 = norm_ffn1[layer][None], norm_mix[layer][None], norm_ffn2[layer][None]
    nf = norm_final[None]
    wg1, wu1, wd1 = (w[layer].astype(BF16) for w in (w_ffn1_gate, w_ffn1_up, w_ffn1_down))
    wg2, wu2, wd2 = (w[layer].astype(BF16) for w in (w_ffn2_gate, w_ffn2_up, w_ffn2_down))
    win = jnp.pad(w_in[layer], ((0, 0), (0, D_IN_PAD - w_in.shape[2]))).astype(BF16)
    wo = w_out[layer].astype(BF16)
    mixer_params = (lb_logits, hg_norm[layer][None], conv_w[layer], conv_b[layer][None],
                    _pad_lanes(dt_bias[layer][None]), _pad_lanes(a_log[layer][None]),
                    jnp.repeat(d_skip[layer], SSM_HEAD_DIM)[None], ssm_norm[layer][None])

    n_pad = TM_SMALL - N_META
    x_small = jnp.concatenate([jnp.zeros((n_pad, D_MODEL), F32), meta_tokens, x_sample[:, 0]], axis=0)
    h1_small, u_small = _ffn_in(x_small, n1, wg1, wu1, wd1, nm, win, TM_SMALL)

    zeros_s = jnp.zeros((1,) + S_SHAPE, F32)
    zeros_h = jnp.zeros((1,) + H_SHAPE, F32)
    zeros_c = jnp.zeros((1,) + C_SHAPE, F32)
    _, s_meta, h_meta, c_meta = _scan(u_small, 0, 1, 1, TM_SMALL, n_pad, None, mixer_params,
                                      zeros_s, zeros_h, zeros_c)

    sc_t = jnp.swapaxes(state_conv[layer], 0, 1)
    mix_s, hgrn_s, ssm_s, conv_s_t = _sample_mixer(
        u_small, TM_SMALL // SAMPLES_PER_STEP, n_s, SAMPLES_PER_STEP, mixer_params,
        state_hgrn[layer], state_ssm[layer], sc_t)
    y_s = _ffn_out(h1_small, 1, mix_s, wo, n2, wg2, wu2, wd2, nf, TM_SMALL)

    xp = x_prompt.reshape(bp * seq_p, D_MODEL)
    h1_p = _ffn1(xp, n1, wg1, wu1, wd1, TM_DENSE)
    mix_p, hgrn_p, ssm_p, conv_p = _scan(h1_p, 0, bp, seq_p // TM_SCAN, TM_SCAN, 0, (nm, win), mixer_params,
                                         s_meta, h_meta, c_meta)
    y_p = _ffn_out(h1_p, 0, mix_p, wo, n2, wg2, wu2, wd2, nf, TM_DENSE)

    keep = slice(PAD_ROWS - (CONV_WIDTH - 1), PAD_ROWS)
    return (y_p.reshape(bp, seq_p, D_MODEL),
            y_s.reshape(n_s, 1, D_MODEL),
            hgrn_p[None], ssm_p[None], conv_p[:, keep][None],
            hgrn_s[None], ssm_s[None], jnp.swapaxes(conv_s_t, 0, 1)[None])
```

```python
import functools

import jax
import jax.numpy as jnp
from jax import lax
from jax.experimental import pallas as pl
from jax.experimental.pallas import tpu as pltpu

F32 = jnp.float32
BF16 = jnp.bfloat16

D_MODEL = 1024
D_FF = 2816
N_META = 16
HG_WIDTH = 512
HG_HEADS = 4
HG_DIM = 128
SSM_WIDTH = 512
SSM_HEADS = 8
SSM_HEAD_DIM = 64
SSM_GROUPS = 2
SSM_STATE = 128
CONV_WIDTH = 4
CONV_DIM = SSM_WIDTH + 2 * SSM_GROUPS * SSM_STATE
EPS = 1e-6

LANES = 128
SUBLANES = 8
VMEM_LIMIT_BYTES = 56 * 1024 * 1024

COL_Q = 0
COL_F = HG_WIDTH
COL_I = 2 * HG_WIDTH
COL_G = 3 * HG_WIDTH
COL_Z = 4 * HG_WIDTH
COL_XBC = COL_Z + SSM_WIDTH
COL_DT = COL_XBC + CONV_DIM
D_IN_PAD = COL_DT + LANES

FF_TILE = 256
HG_CHUNK = 64
SSM_CHUNK = 128
PAD_ROWS = SUBLANES
HG_PAIRS = HG_HEADS // 2
HEADS_PER_GROUP = SSM_HEADS // SSM_GROUPS
GROUP_WIDTH = SSM_WIDTH // SSM_GROUPS
assert 2 * SSM_HEAD_DIM == LANES and 2 * HG_DIM == FF_TILE


def _dot(a, b):
    return jnp.dot(a, b, preferred_element_type=F32)


def _dot_nt(a, b):
    return lax.dot_general(a, b, (((1,), (1,)), ((), ())), preferred_element_type=F32)


def _dot_tn(a, b):
    return lax.dot_general(a, b, (((0,), (0,)), ((), ())), preferred_element_type=F32)


def _rms(x, w):
    return x * lax.rsqrt(jnp.mean(x * x, axis=-1, keepdims=True) + EPS) * w


def _silu(x):
    return x * jax.nn.sigmoid(x)


def _swiglu(xn, wg_ref, wu_ref, wd_ref):
    acc = jnp.zeros((xn.shape[0], D_MODEL), F32)
    for j in range(D_FF // FF_TILE):
        cols = slice(j * FF_TILE, (j + 1) * FF_TILE)
        g = _dot(xn, wg_ref[:, cols])
        u = _dot(xn, wu_ref[:, cols])
        acc = acc + _dot((_silu(g) * u).astype(BF16), wd_ref[cols, :])
    return acc


def _cumsum_rows(tri, a):
    a1 = a.astype(BF16)
    r1 = a - a1.astype(F32)
    a2 = r1.astype(BF16)
    a3 = (r1 - a2.astype(F32)).astype(BF16)
    return _dot(tri, a1) + _dot(tri, a2) + _dot(tri, a3)


def _lower_tri(n):
    row = lax.broadcasted_iota(jnp.int32, (n, n), 0)
    col = lax.broadcasted_iota(jnp.int32, (n, n), 1)
    return row >= col


def _chunked_tri(n, chunk):
    assert chunk & (chunk - 1) == 0
    row = lax.broadcasted_iota(jnp.int32, (n, n), 0)
    col = lax.broadcasted_iota(jnp.int32, (n, n), 1)
    same_chunk = (row ^ col) < chunk
    return ((row >= col) & same_chunk).astype(BF16)


def _tiled_lower_tri(chunk, reps):
    assert chunk & (chunk - 1) == 0
    row = lax.broadcasted_iota(jnp.int32, (chunk, reps * chunk), 0)
    col = lax.broadcasted_iota(jnp.int32, (chunk, reps * chunk), 1)
    return row >= (col & (chunk - 1))


def _block_diag(blocks):
    n = len(blocks)
    r, c = blocks[0].shape
    rows = []
    for i, blk in enumerate(blocks):
        parts = []
        if i:
            parts.append(jnp.zeros((r, c * i), blk.dtype))
        parts.append(blk)
        if i < n - 1:
            parts.append(jnp.zeros((r, c * (n - 1 - i)), blk.dtype))
        rows.append(jnp.concatenate(parts, axis=1))
    return jnp.concatenate(rows, axis=0)


def _forget_lower_bound(lbl):
    l0, l1 = lbl[0:1], lbl[1:2]
    m = jnp.maximum(l0, l1)
    e0, e1 = jnp.exp(l0 - m), jnp.exp(l1 - m)
    return e0 / (e0 + e1)


def _resident(shape):
    nd = len(shape)
    return pl.BlockSpec(shape, lambda *_: (0,) * nd, pipeline_mode=pl.Buffered(1))


def _compiler_params(n_grid_axes):
    return pltpu.CompilerParams(dimension_semantics=("arbitrary",) * n_grid_axes,
                                vmem_limit_bytes=VMEM_LIMIT_BYTES)


def _ffn1_kernel(x_ref, n1_ref, wg_ref, wu_ref, wd_ref, h1_ref):
    x = x_ref[...]
    xn = _rms(x, n1_ref[...]).astype(BF16)
    h1_ref[...] = x + 0.5 * _swiglu(xn, wg_ref, wu_ref, wd_ref)


def _ffn1(x, n1, wg, wu, wd, tm):
    n = x.shape[0]
    return pl.pallas_call(
        _ffn1_kernel,
        grid=(n // tm,),
        in_specs=[pl.BlockSpec((tm, D_MODEL), lambda i: (i, 0)),
                  _resident(n1.shape), _resident(wg.shape), _resident(wu.shape), _resident(wd.shape)],
        out_specs=pl.BlockSpec((tm, D_MODEL), lambda i: (i, 0)),
        out_shape=jax.ShapeDtypeStruct((n, D_MODEL), F32),
        compiler_params=_compiler_params(1),
        name="ffn1",
    )(x, n1, wg, wu, wd)


def _ffn_tile_step(xn, wg_ref, wu_ref, wd_ref, wg_out, wu_out, wd_out, acc_scr):
    wg, wu, wd = wg_ref[...].astype(BF16), wu_ref[...].astype(BF16), wd_ref[...].astype(BF16)
    wg_out[...] = wg
    wu_out[...] = wu
    wd_out[...] = wd
    acc_scr[...] += _dot((_silu(_dot(xn, wg)) * _dot(xn, wu)).astype(BF16), wd)


def _ffn1_small_kernel(x_ref, n1_ref, wg_ref, wu_ref, wd_ref, h1_ref, wg_out, wu_out, wd_out, xn_scr, acc_scr):
    j = pl.program_id(0)

    @pl.when(j == 0)
    def _():
        xn_scr[...] = _rms(x_ref[...], n1_ref[...]).astype(BF16)
        acc_scr[...] = jnp.zeros(acc_scr.shape, F32)

    _ffn_tile_step(xn_scr[...], wg_ref, wu_ref, wd_ref, wg_out, wu_out, wd_out, acc_scr)

    @pl.when(j == pl.num_programs(0) - 1)
    def _():
        h1_ref[...] = x_ref[...] + 0.5 * acc_scr[...]


def _ffn2_small_kernel(h1_ref, mix_ref, wo_ref, n2_ref, wg_ref, wu_ref, wd_ref, nf_ref,
                       y_ref, wg_out, wu_out, wd_out, h2_scr, hn_scr, acc_scr):
    j = pl.program_id(0)

    @pl.when(j == 0)
    def _():
        h2 = h1_ref[...] + _dot(mix_ref[...].astype(BF16), wo_ref[...])
        h2_scr[...] = h2
        hn_scr[...] = _rms(h2, n2_ref[...]).astype(BF16)
        acc_scr[...] = jnp.zeros(acc_scr.shape, F32)

    _ffn_tile_step(hn_scr[...], wg_ref, wu_ref, wd_ref, wg_out, wu_out, wd_out, acc_scr)

    @pl.when(j == pl.num_programs(0) - 1)
    def _():
        y_ref[...] = _rms(h2_scr[...] + 0.5 * acc_scr[...], nf_ref[...])


def _weight_tile_specs():
    cols = pl.BlockSpec((D_MODEL, FF_TILE), lambda j: (0, j))
    rows = pl.BlockSpec((FF_TILE, D_MODEL), lambda j: (j, 0))
    shapes = [jax.ShapeDtypeStruct((D_MODEL, D_FF), BF16), jax.ShapeDtypeStruct((D_MODEL, D_FF), BF16),
              jax.ShapeDtypeStruct((D_FF, D_MODEL), BF16)]
    return [cols, cols, rows], shapes


def _ffn1_small(x, n1, wg, wu, wd):
    n = x.shape[0]
    wspecs, wshapes = _weight_tile_specs()
    return pl.pallas_call(
        _ffn1_small_kernel,
        grid=(D_FF // FF_TILE,),
        in_specs=[_resident(x.shape), _resident(n1.shape)] + wspecs,
        out_specs=[pl.BlockSpec((n, D_MODEL), lambda j: (0, 0))] + wspecs,
        out_shape=[jax.ShapeDtypeStruct((n, D_MODEL), F32)] + wshapes,
        scratch_shapes=[pltpu.VMEM((n, D_MODEL), BF16), pltpu.VMEM((n, D_MODEL), F32)],
        compiler_params=_compiler_params(1),
        name="ffn1_small",
    )(x, n1, wg, wu, wd)


def _ffn2_small(h1, h1_block, mix, wo, n2, wg, wu, wd, nf):
    n = mix.shape[0]
    wspecs, wshapes = _weight_tile_specs()
    return pl.pallas_call(
        _ffn2_small_kernel,
        grid=(D_FF // FF_TILE,),
        in_specs=[pl.BlockSpec((n, D_MODEL), lambda j: (h1_block, 0), pipeline_mode=pl.Buffered(1)),
                  _resident(mix.shape), _resident(wo.shape), _resident(n2.shape)] + wspecs
        + [_resident(nf.shape)],
        out_specs=[pl.BlockSpec((n, D_MODEL), lambda j: (0, 0))] + wspecs,
        out_shape=[jax.ShapeDtypeStruct((n, D_MODEL), F32)] + wshapes,
        scratch_shapes=[pltpu.VMEM((n, D_MODEL), F32), pltpu.VMEM((n, D_MODEL), BF16),
                        pltpu.VMEM((n, D_MODEL), F32)],
        compiler_params=_compiler_params(1),
        name="ffn2_small",
    )(h1, mix, wo, n2, wg, wu, wd, nf)


def _proj_kernel(h1_ref, nm_ref, win_ref, u_ref):
    u_ref[...] = _dot(_rms(h1_ref[...], nm_ref[...]).astype(BF16), win_ref[...])


def _proj_small(h1, nm, win):
    n = h1.shape[0]
    return pl.pallas_call(
        _proj_kernel,
        grid=(1,),
        in_specs=[_resident(h1.shape), _resident(nm.shape), _resident(win.shape)],
        out_specs=pl.BlockSpec((n, D_IN_PAD), lambda i: (0, 0)),
        out_shape=jax.ShapeDtypeStruct((n, D_IN_PAD), F32),
        compiler_params=_compiler_params(1),
        name="proj_small",
    )(h1, nm, win)


def _ffn_out_kernel(h1_ref, mix_ref, wo_ref, n2_ref, wg_ref, wu_ref, wd_ref, nf_ref, y_ref):
    h2 = h1_ref[...] + _dot(mix_ref[...].astype(BF16), wo_ref[...])
    hn = _rms(h2, n2_ref[...]).astype(BF16)
    h3 = h2 + 0.5 * _swiglu(hn, wg_ref, wu_ref, wd_ref)
    y_ref[...] = _rms(h3, nf_ref[...])


def _ffn_out(h1, h1_block0, mix, wo, n2, wg, wu, wd, nf, tm):
    n = mix.shape[0]
    return pl.pallas_call(
        _ffn_out_kernel,
        grid=(n // tm,),
        in_specs=[pl.BlockSpec((tm, D_MODEL), lambda i: (i + h1_block0, 0)),
                  pl.BlockSpec((tm, D_MODEL), lambda i: (i, 0)),
                  _resident(wo.shape), _resident(n2.shape), _resident(wg.shape), _resident(wu.shape),
                  _resident(wd.shape), _resident(nf.shape)],
        out_specs=pl.BlockSpec((tm, D_MODEL), lambda i: (i, 0)),
        out_shape=jax.ShapeDtypeStruct((n, D_MODEL), F32),
        compiler_params=_compiler_params(1),
        name="ffn_out",
    )(h1, mix, wo, n2, wg, wu, wd, nf)


def _head_slices(a, width):
    return [a[:, i:i + width] for i in range(0, a.shape[1], width)]


def _hgrn_block(n_pad, tb, u_ref, lb, hgn, mix_ref, st_scr):
    c = HG_CHUNK
    fz = u_ref[:, COL_F:COL_F + HG_WIDTH]
    logf = jnp.log(lb + (1.0 - lb) * jax.nn.sigmoid(fz))
    kk = (1.0 - lb) * jax.nn.sigmoid(-fz)
    if n_pad:
        valid = lax.broadcasted_iota(jnp.int32, (tb, HG_WIDTH), 0) >= n_pad
        logf = jnp.where(valid, logf, 0.0)
        kk = jnp.where(valid, kk, 0.0)
    q = _silu(u_ref[:, COL_Q:COL_Q + HG_WIDTH])
    v = u_ref[:, COL_I:COL_I + HG_WIDTH].astype(BF16)
    gate = _silu(u_ref[:, COL_G:COL_G + HG_WIDTH])

    b = _cumsum_rows(_chunked_tri(tb, c), logf)
    qt = (q * jnp.exp(b)).astype(BF16)
    kt = (kk * jnp.exp(-b)).astype(BF16)
    causal = _tiled_lower_tri(c, HG_HEADS)

    for r0 in range(0, tb, c):
        rows = slice(r0, r0 + c)
        b_c = b[rows]
        b_last = b_c[c - 1:c, :]
        kh_c = (kk[rows] * jnp.exp(b_last - b_c)).astype(BF16)
        decay = jnp.exp(b_last)
        qt_c = qt[rows]
        kd = _block_diag(_head_slices(kt[rows], HG_DIM))
        vd = _block_diag(_head_slices(v[rows], HG_DIM))
        scores = jnp.where(causal, _dot_nt(qt_c, kd), 0.0).astype(BF16)
        o = _dot(scores, vd)
        o_prev = []
        for p in range(HG_PAIRS):
            lanes = slice(p * 2 * HG_DIM, (p + 1) * 2 * HG_DIM)
            heads = (2 * p, 2 * p + 1)
            st = [st_scr[h] for h in heads]
            o_prev.append(_dot_nt(qt_c[:, lanes], _block_diag([s.astype(BF16) for s in st])))
            upd = _dot_tn(vd[p * 2 * c:(p + 1) * 2 * c, lanes], jnp.concatenate([kh_c[:, lanes]] * 2, axis=0))
            for i, h in enumerate(heads):
                blk = slice(i * HG_DIM, (i + 1) * HG_DIM)
                st_scr[h] = st[i] * decay[:, h * HG_DIM:(h + 1) * HG_DIM] + upd[blk, blk]
        o = o + jnp.concatenate(o_prev, axis=1)
        for h in range(HG_HEADS):
            sl = slice(h * HG_DIM, (h + 1) * HG_DIM)
            oh = o[:, sl]
            oh = oh * lax.rsqrt(jnp.mean(oh * oh, axis=-1, keepdims=True) + EPS)
            mix_ref[rows, sl] = (oh * hgn[:, sl] * gate[rows, sl]).astype(mix_ref.dtype)


def _pair_columns(a, r0, r1, first_half):
    shape = (a.shape[0], LANES)
    return jnp.where(first_half, jnp.broadcast_to(a[:, r0:r0 + 1], shape), jnp.broadcast_to(a[:, r1:r1 + 1], shape))


def _group_columns(a, g, first_half):
    r = g * HEADS_PER_GROUP
    return jnp.concatenate([_pair_columns(a, r + i, r + i + 1, first_half)
                            for i in range(0, HEADS_PER_GROUP, 2)], axis=1)


def _ssd_block(n_pad, tb, u_ref, cw, cb, dtb, a_neg, dsk, ssn, mix_ref, hg_scr, xpad):
    c = SSM_CHUNK
    conv = cb
    for j in range(CONV_WIDTH):
        off = PAD_ROWS - (CONV_WIDTH - 1) + j
        conv = conv + cw[j:j + 1, :] * xpad[off:off + tb, :]
    act = _silu(conv)
    xs = act[:, 0:SSM_WIDTH]
    bm = act[:, SSM_WIDTH:SSM_WIDTH + SSM_GROUPS * SSM_STATE].astype(BF16)
    cm = act[:, SSM_WIDTH + SSM_GROUPS * SSM_STATE:].astype(BF16)
    z_gate = _silu(u_ref[:, COL_Z:COL_Z + SSM_WIDTH])

    dt = jax.nn.softplus(u_ref[:, COL_DT:COL_DT + LANES] + dtb)
    if n_pad:
        dt = jnp.where(lax.broadcasted_iota(jnp.int32, (tb, LANES), 0) >= n_pad, dt, 0.0)
    cum = _cumsum_rows(_chunked_tri(tb, c), dt * a_neg)
    causal = _lower_tri(c)
    first_half = lax.broadcasted_iota(jnp.int32, (1, LANES), 1) < SSM_HEAD_DIM
    zeros = jnp.zeros((c, LANES), BF16)

    for r0 in range(0, tb, c):
        rows = slice(r0, r0 + c)
        cum_c = cum[rows]
        dt_c = dt[rows]
        last = cum_c[c - 1:c, :]
        cum_t = cum_c.T
        dt_t = dt_c.T
        e_cum = jnp.exp(cum_c)
        w_in = dt_c * jnp.exp(last - cum_c)
        e_last = jnp.exp(last)
        for g in range(SSM_GROUPS):
            glanes = slice(g * GROUP_WIDTH, (g + 1) * GROUP_WIDTH)
            bg = bm[rows, g * SSM_STATE:(g + 1) * SSM_STATE]
            cg = cm[rows, g * SSM_STATE:(g + 1) * SSM_STATE]
            xg = xs[rows, glanes]
            cbt = _dot_nt(cg, bg)
            m_heads = []
            for rr in range(HEADS_PER_GROUP):
                r = g * HEADS_PER_GROUP + rr
                seg = jnp.exp(jnp.where(causal, cum_c[:, r:r + 1] - cum_t[r:r + 1, :], -jnp.inf))
                m_heads.append((cbt * seg * dt_t[r:r + 1, :]).astype(BF16))
            xb = xg.astype(BF16)
            xd_rows = []
            for rr in range(HEADS_PER_GROUP):
                tile = xb[:, (rr // 2) * LANES:(rr // 2 + 1) * LANES]
                tile = jnp.where(first_half if rr % 2 == 0 else ~first_half, tile, jnp.zeros_like(tile))
                xd_rows.append(jnp.concatenate([tile, zeros] if rr < 2 else [zeros, tile], axis=1))
            xd = jnp.concatenate(xd_rows, axis=0)
            hg = hg_scr[g]
            y = (_dot(jnp.concatenate(m_heads, axis=1), xd)
                 + _dot(cg, hg.astype(BF16)) * _group_columns(e_cum, g, first_half))
            xw = (xg * _group_columns(w_in, g, first_half)).astype(BF16)
            hg_scr[g] = hg * _group_columns(e_last, g, first_half) + _dot_tn(bg, xw)
            yz = (y + dsk[:, glanes] * xg) * z_gate[rows, glanes]
            yz = yz * lax.rsqrt(jnp.mean(yz * yz, axis=-1, keepdims=True) + EPS) * ssn[:, glanes]
            mix_ref[rows, HG_WIDTH + g * GROUP_WIDTH:HG_WIDTH + (g + 1) * GROUP_WIDTH] = yz.astype(mix_ref.dtype)


def _mixer_block(n_pad, tb, u_ref, param_refs, mix_ref, st_scr, hg_scr, xpad):
    lbl_ref, hgn_ref, cw_ref, cb_ref, dtb_ref, alog_ref, dsk_ref, ssn_ref = param_refs
    xbc = u_ref[:, COL_XBC:COL_XBC + CONV_DIM]
    if n_pad:
        xbc = jnp.where(lax.broadcasted_iota(jnp.int32, (tb, CONV_DIM), 0) >= n_pad, xbc, 0.0)
    xpad[PAD_ROWS:PAD_ROWS + tb, :] = xbc
    _hgrn_block(n_pad, tb, u_ref, _forget_lower_bound(lbl_ref[...]), hgn_ref[...], mix_ref, st_scr)
    _ssd_block(n_pad, tb, u_ref, cw_ref[...], cb_ref[...], dtb_ref[...], -jnp.exp(alog_ref[...]), dsk_ref[...],
               ssn_ref[...], mix_ref, hg_scr, xpad)
    xpad[0:PAD_ROWS, :] = xpad[tb:tb + PAD_ROWS, :]


def _load_state(s0_ref, h0_ref, c0_ref, st_scr, hg_scr, xpad):
    for h in range(HG_HEADS):
        st_scr[h] = s0_ref[0, h].T
    for r in range(0, SSM_HEADS, 2):
        g, lane0 = r // HEADS_PER_GROUP, (r % HEADS_PER_GROUP) * SSM_HEAD_DIM
        hg_scr[g, :, lane0:lane0 + LANES] = jnp.concatenate([h0_ref[0, r], h0_ref[0, r + 1]], axis=0).T
    xpad[0:PAD_ROWS, :] = c0_ref[0]


def _store_state(s_out_ref, h_out_ref, c_out_ref, st_scr, hg_scr, xpad):
    for h in range(HG_HEADS):
        s_out_ref[0, h] = st_scr[h].T
    for r in range(0, SSM_HEADS, 2):
        g, lane0 = r // HEADS_PER_GROUP, (r % HEADS_PER_GROUP) * SSM_HEAD_DIM
        pair = hg_scr[g, :, lane0:lane0 + LANES].T
        h_out_ref[0, r] = pair[0:SSM_HEAD_DIM]
        h_out_ref[0, r + 1] = pair[SSM_HEAD_DIM:]
    c_out_ref[0] = xpad[0:PAD_ROWS, :]


N_MIXER_PARAMS = 8
HG_SHAPE = (SSM_GROUPS, SSM_STATE, GROUP_WIDTH)
S_SHAPE = (HG_HEADS, HG_DIM, HG_DIM)
H_SHAPE = (SSM_HEADS, SSM_HEAD_DIM, SSM_STATE)
C_SHAPE = (PAD_ROWS, CONV_DIM)


def _scan_kernel(n_pad, tb, project, *refs):
    if project:
        h1_ref, nm_ref, win_ref = refs[:3]
        refs = refs[3:]
    else:
        u_ref = refs[0]
        refs = refs[1:]
    param_refs = refs[:N_MIXER_PARAMS]
    s0_ref, h0_ref, c0_ref, mix_ref, s_out_ref, h_out_ref, c_out_ref = refs[N_MIXER_PARAMS:N_MIXER_PARAMS + 7]
    scratch = refs[N_MIXER_PARAMS + 7:]
    if project:
        u_ref, st_scr, hg_scr, xpad = scratch
    else:
        st_scr, hg_scr, xpad = scratch
    t = pl.program_id(1)

    @pl.when(t == 0)
    def _():
        _load_state(s0_ref, h0_ref, c0_ref, st_scr, hg_scr, xpad)

    if project:
        hn = _rms(h1_ref[...], nm_ref[...]).astype(BF16)
        u_ref[...] = _dot(hn, win_ref[...])
    _mixer_block(n_pad, tb, u_ref, param_refs, mix_ref, st_scr, hg_scr, xpad)

    @pl.when(t == pl.num_programs(1) - 1)
    def _():
        _store_state(s_out_ref, h_out_ref, c_out_ref, st_scr, hg_scr, xpad)


def _scan(src, src_block0, nb, nt, tb, n_pad, proj, params, s0, h0, c0):
    lead = tuple(proj) + tuple(params) if proj else tuple(params)
    shared = lambda shape: pl.BlockSpec((1,) + shape[1:], lambda b, t: (0,) * len(shape))
    per_seq = lambda shape: pl.BlockSpec((1,) + shape, lambda b, t: (b,) + (0,) * len(shape))
    scratch = [pltpu.VMEM(S_SHAPE, F32), pltpu.VMEM(HG_SHAPE, F32), pltpu.VMEM((PAD_ROWS + tb, CONV_DIM), F32)]
    if proj:
        scratch = [pltpu.VMEM((tb, D_IN_PAD), F32)] + scratch
    return pl.pallas_call(
        functools.partial(_scan_kernel, n_pad, tb, bool(proj)),
        grid=(nb, nt),
        in_specs=[pl.BlockSpec((tb, src.shape[1]), lambda b, t: (src_block0 + b * nt + t, 0))]
        + [_resident(p.shape) for p in lead]
        + [shared(s0.shape), shared(h0.shape), shared(c0.shape)],
        out_specs=[pl.BlockSpec((tb, D_MODEL), lambda b, t: (b * nt + t, 0)),
                   per_seq(S_SHAPE), per_seq(H_SHAPE), per_seq(C_SHAPE)],
        out_shape=[jax.ShapeDtypeStruct((nb * nt * tb, D_MODEL), BF16),
                   jax.ShapeDtypeStruct((nb,) + S_SHAPE, F32),
                   jax.ShapeDtypeStruct((nb,) + H_SHAPE, F32),
                   jax.ShapeDtypeStruct((nb,) + C_SHAPE, F32)],
        scratch_shapes=scratch,
        compiler_params=_compiler_params(2),
        name="proj_mixer" if proj else "mixer",
    )(src, *lead, s0, h0, c0)


def _column_tile(row):
    return jnp.broadcast_to(row, (LANES, LANES)).T


def _sample_kernel(nbs, u_ref, lbl_ref, hgn_ref, cw_ref, cb_ref, dtb_ref, alog_ref, dsk_ref, ssn_ref,
                   sh_ref, ss_ref, sc_ref, mix_ref, sh_out_ref, ss_out_ref, sc_out_ref, obuf, ybuf):
    lb = _forget_lower_bound(lbl_ref[...])
    fz = u_ref[:, COL_F:COL_F + HG_WIDTH]
    f = lb + (1.0 - lb) * jax.nn.sigmoid(fz)
    kk = (1.0 - lb) * jax.nn.sigmoid(-fz)
    q = _silu(u_ref[:, COL_Q:COL_Q + HG_WIDTH])
    v = u_ref[:, COL_I:COL_I + HG_WIDTH]

    xbc = u_ref[:, COL_XBC:COL_XBC + CONV_DIM]
    cw = cw_ref[...]
    conv = cb_ref[...] + cw[CONV_WIDTH - 1:CONV_WIDTH, :] * xbc
    for j in range(CONV_WIDTH - 1):
        conv = conv + cw[j:j + 1, :] * sc_ref[j]
    for j in range(CONV_WIDTH - 2):
        sc_out_ref[j] = sc_ref[j + 1]
    sc_out_ref[CONV_WIDTH - 2] = xbc
    act = _silu(conv)
    xs = act[:, 0:SSM_WIDTH]
    bm = act[:, SSM_WIDTH:SSM_WIDTH + SSM_GROUPS * SSM_STATE]
    cm = act[:, SSM_WIDTH + SSM_GROUPS * SSM_STATE:]
    dt = jax.nn.softplus(u_ref[:, COL_DT:COL_DT + LANES] + dtb_ref[...])
    d_a = jnp.exp(dt * (-jnp.exp(alog_ref[...])))

    top_half = lax.broadcasted_iota(jnp.int32, (LANES, LANES), 0) < SSM_HEAD_DIM
    for j in range(nbs):
        row = slice(j, j + 1)
        for h in range(HG_HEADS):
            sl = slice(h * HG_DIM, (h + 1) * HG_DIM)
            s_new = sh_ref[j, h] * _column_tile(f[row, sl]) + _column_tile(kk[row, sl]) * v[row, sl]
            sh_out_ref[j, h] = s_new
            obuf[row, sl] = jnp.sum(s_new * _column_tile(q[row, sl]), axis=0, keepdims=True)
        for rp in range(SSM_HEADS // 2):
            r0, r1 = 2 * rp, 2 * rp + 1
            g = r0 // HEADS_PER_GROUP
            sl = slice(rp * LANES, (rp + 1) * LANES)
            gsl = slice(g * SSM_STATE, (g + 1) * SSM_STATE)
            h2 = jnp.concatenate([ss_ref[j, r0], ss_ref[j, r1]], axis=0)
            da2 = jnp.where(top_half, d_a[row, r0:r0 + 1], d_a[row, r1:r1 + 1])
            dt2 = jnp.where(top_half, dt[row, r0:r0 + 1], dt[row, r1:r1 + 1])
            h_new = h2 * da2 + (dt2 * _column_tile(xs[row, sl])) * bm[row, gsl]
            ss_out_ref[j, r0] = h_new[0:SSM_HEAD_DIM]
            ss_out_ref[j, r1] = h_new[SSM_HEAD_DIM:]
            ybuf[row, sl] = jnp.sum((h_new * cm[row, gsl]).T, axis=0, keepdims=True)

    o = obuf[...]
    gate = _silu(u_ref[:, COL_G:COL_G + HG_WIDTH])
    hgn = hgn_ref[...]
    for h in range(HG_HEADS):
        sl = slice(h * HG_DIM, (h + 1) * HG_DIM)
        oh = o[:, sl]
        oh = oh * lax.rsqrt(jnp.mean(oh * oh, axis=-1, keepdims=True) + EPS)
        mix_ref[:, sl] = oh * hgn[:, sl] * gate[:, sl]
    yz = (ybuf[...] + dsk_ref[...] * xs) * _silu(u_ref[:, COL_Z:COL_Z + SSM_WIDTH])
    ssn = ssn_ref[...]
    for g in range(SSM_GROUPS):
        sl = slice(g * GROUP_WIDTH, (g + 1) * GROUP_WIDTH)
        seg = yz[:, sl]
        seg = seg * lax.rsqrt(jnp.mean(seg * seg, axis=-1, keepdims=True) + EPS) * ssn[:, sl]
        mix_ref[:, HG_WIDTH + g * GROUP_WIDTH:HG_WIDTH + (g + 1) * GROUP_WIDTH] = seg


def _sample_mixer(u, u_block0, n, nbs, params, sh, ss, sc_t):
    small = [_resident(p.shape) for p in params]
    return pl.pallas_call(
        functools.partial(_sample_kernel, nbs),
        grid=(n // nbs,),
        in_specs=[pl.BlockSpec((nbs, D_IN_PAD), lambda i: (u_block0 + i, 0))] + small
        + [pl.BlockSpec((nbs,) + sh.shape[1:], lambda i: (i, 0, 0, 0)),
           pl.BlockSpec((nbs,) + ss.shape[1:], lambda i: (i, 0, 0, 0)),
           pl.BlockSpec((CONV_WIDTH - 1, nbs, CONV_DIM), lambda i: (0, i, 0))],
        out_specs=[pl.BlockSpec((nbs, D_MODEL), lambda i: (i, 0)),
                   pl.BlockSpec((nbs,) + sh.shape[1:], lambda i: (i, 0, 0, 0)),
                   pl.BlockSpec((nbs,) + ss.shape[1:], lambda i: (i, 0, 0, 0)),
                   pl.BlockSpec((CONV_WIDTH - 1, nbs, CONV_DIM), lambda i: (0, i, 0))],
        out_shape=[jax.ShapeDtypeStruct((n, D_MODEL), F32),
                   jax.ShapeDtypeStruct(sh.shape, F32),
                   jax.ShapeDtypeStruct(ss.shape, F32),
                   jax.ShapeDtypeStruct(sc_t.shape, F32)],
        scratch_shapes=[pltpu.VMEM((nbs, HG_WIDTH), F32), pltpu.VMEM((nbs, SSM_WIDTH), F32)],
        compiler_params=_compiler_params(1),
        name="sample_mixer",
    )(u, *params, sh, ss, sc_t)


TM_DENSE = 512
TM_SCAN = 256
TM_SMALL = 128
SAMPLES_PER_STEP = 8


def _pad_lanes(row, value=0.0):
    return jnp.pad(row, ((0, 0), (0, LANES - row.shape[1])), constant_values=value)


def kernel(x_prompt, x_sample, state_hgrn, state_ssm, state_conv, meta_tokens, lb_logits, norm_ffn1, w_ffn1_gate, w_ffn1_up, w_ffn1_down, norm_mix, w_in, hg_norm, conv_w, conv_b, dt_bias, a_log, d_skip, ssm_norm, w_out, norm_ffn2, w_ffn2_gate, w_ffn2_up, w_ffn2_down, norm_final):
    bp, seq_p, _ = x_prompt.shape
    n_s = x_sample.shape[0]
    assert x_sample.shape[1] == 1 and n_s == TM_SMALL and seq_p % TM_SCAN == 0 and (bp * seq_p) % TM_DENSE == 0
    layer = 0

    n1, nm, n2 = norm_ffn1[layer][None], norm_mix[layer][None], norm_ffn2[layer][None]
    nf = norm_final[None]
    win =jnp.pad(w_in[layer], ((0, 0), (0, D_IN_PAD - w_in.shape[2]))).astype(BF16)
    wo = w_out[layer].astype(BF16)
    mixer_params = (lb_logits, hg_norm[layer][None], conv_w[layer], conv_b[layer][None],
                    _pad_lanes(dt_bias[layer][None]), _pad_lanes(a_log[layer][None]),
                    jnp.repeat(d_skip[layer], SSM_HEAD_DIM)[None], ssm_norm[layer][None])

    n_pad = TM_SMALL - N_META
    x_small = jnp.concatenate([jnp.zeros((n_pad, D_MODEL), F32), meta_tokens, x_sample[:, 0]], axis=0)
    h1_small, wg1, wu1, wd1 = _ffn1_small(x_small, n1, w_ffn1_gate[layer], w_ffn1_up[layer], w_ffn1_down[layer])
    u_small = _proj_small(h1_small, nm, win)

    zeros_s = jnp.zeros((1,) + S_SHAPE, F32)
    zeros_h = jnp.zeros((1,) + H_SHAPE, F32)
    zeros_c = jnp.zeros((1,) + C_SHAPE, F32)
    _, s_meta, h_meta, c_meta = _scan(u_small, 0, 1, 1, TM_SMALL, n_pad, None, mixer_params,
                                      zeros_s, zeros_h, zeros_c)

    sc_t = jnp.swapaxes(state_conv[layer], 0, 1)
    mix_s, hgrn_s, ssm_s, conv_s_t = _sample_mixer(
        u_small, TM_SMALL // SAMPLES_PER_STEP, n_s, SAMPLES_PER_STEP, mixer_params,
        state_hgrn[layer], state_ssm[layer], sc_t)
    y_s, wg2, wu2, wd2 = _ffn2_small(h1_small, 1, mix_s, wo, n2, w_ffn2_gate[layer], w_ffn2_up[layer],
                                     w_ffn2_down[layer], nf)

    xp = x_prompt.reshape(bp * seq_p, D_MODEL)
    h1_p = _ffn1(xp, n1, wg1, wu1, wd1, TM_DENSE)
    mix_p, hgrn_p, ssm_p, conv_p = _scan(h1_p, 0, bp, seq_p // TM_SCAN, TM_SCAN, 0, (nm, win), mixer_params,
                                         s_meta, h_meta, c_meta)
    y_p = _ffn_out(h1_p, 0, mix_p, wo, n2, wg2, wu2, wd2, nf, TM_DENSE)

    keep = slice(PAD_ROWS - (CONV_WIDTH - 1), PAD_ROWS)
    return (y_p.reshape(bp, seq_p, D_MODEL),
            y_s.reshape(n_s, 1, D_MODEL),
            hgrn_p[None], ssm_p[None], conv_p[:, keep][None],
            hgrn_s[None], ssm_s[None], jnp.swapaxes(conv_s_t, 0, 1)[None])
```

```python
import functools

import jax
import jax.numpy as jnp
from jax import lax
from jax.experimental import pallas as pl
from jax.experimental.pallas import tpu as pltpu

F32 = jnp.float32
BF16 = jnp.bfloat16

D_MODEL = 1024
D_FF = 2816
N_META = 16
HG_WIDTH = 512
HG_HEADS = 4
HG_DIM = 128
SSM_WIDTH = 512
SSM_HEADS = 8
SSM_HEAD_DIM = 64
SSM_GROUPS = 2
SSM_STATE = 128
CONV_WIDTH = 4
CONV_DIM = SSM_WIDTH + 2 * SSM_GROUPS * SSM_STATE
EPS = 1e-6

LANES = 128
SUBLANES = 8
VMEM_LIMIT_BYTES = 56 * 1024 * 1024

COL_Q = 0
COL_F = HG_WIDTH
COL_I = 2 * HG_WIDTH
COL_G = 3 * HG_WIDTH
COL_Z = 4 * HG_WIDTH
COL_XBC = COL_Z + SSM_WIDTH
COL_DT = COL_XBC + CONV_DIM
D_IN_PAD = COL_DT + LANES

FF_TILE = 256
HG_CHUNK = 64
SSM_CHUNK = 128
PAD_ROWS = SUBLANES
HG_PAIRS = HG_HEADS // 2
HEADS_PER_GROUP = SSM_HEADS // SSM_GROUPS
GROUP_WIDTH = SSM_WIDTH // SSM_GROUPS
assert 2 * SSM_HEAD_DIM == LANES and 2 * HG_DIM == FF_TILE


def _dot(a, b):
    return jnp.dot(a, b, preferred_element_type=F32)


def _dot_nt(a, b):
    return lax.dot_general(a, b, (((1,), (1,)), ((), ())), preferred_element_type=F32)


def _dot_tn(a, b):
    return lax.dot_general(a, b, (((0,), (0,)), ((), ())), preferred_element_type=F32)


def _rms(x, w):
    return x * lax.rsqrt(jnp.mean(x * x, axis=-1, keepdims=True) + EPS) * w


def _silu(x):
    return x * jax.nn.sigmoid(x)


def _swiglu(xn, wg_ref, wu_ref, wd_ref):
    acc = jnp.zeros((xn.shape[0], D_MODEL), F32)
    for j in range(D_FF // FF_TILE):
        cols = slice(j * FF_TILE, (j + 1) * FF_TILE)
        g = _dot(xn, wg_ref[:, cols])
        u = _dot(xn, wu_ref[:, cols])
        acc = acc + _dot((_silu(g) * u).astype(BF16), wd_ref[cols, :])
    return acc


def _cumsum_rows(tri, a):
    a1 = a.astype(BF16)
    r1 = a - a1.astype(F32)
    a2 = r1.astype(BF16)
    a3 = (r1 - a2.astype(F32)).astype(BF16)
    return _dot(tri, a1) + _dot(tri, a2) + _dot(tri, a3)


def _lower_tri(n):
    row = lax.broadcasted_iota(jnp.int32, (n, n), 0)
    col = lax.broadcasted_iota(jnp.int32, (n, n), 1)
    return row >= col


def _chunked_tri(n, chunk):
    assert chunk & (chunk - 1) == 0
    row = lax.broadcasted_iota(jnp.int32, (n, n), 0)
    col = lax.broadcasted_iota(jnp.int32, (n, n), 1)
    same_chunk = (row ^ col) < chunk
    return ((row >= col) & same_chunk).astype(BF16)


def _tiled_lower_tri(chunk, reps):
    assert chunk & (chunk - 1) == 0
    row = lax.broadcasted_iota(jnp.int32, (chunk, reps * chunk), 0)
    col = lax.broadcasted_iota(jnp.int32, (chunk, reps * chunk), 1)
    return row >= (col & (chunk - 1))


def _block_diag(blocks):
    n = len(blocks)
    r, c = blocks[0].shape
    rows = []
    for i, blk in enumerate(blocks):
        parts = []
        if i:
            parts.append(jnp.zeros((r, c * i), blk.dtype))
        parts.append(blk)
        if i < n - 1:
            parts.append(jnp.zeros((r, c * (n - 1 - i)), blk.dtype))
        rows.append(jnp.concatenate(parts, axis=1))
    return jnp.concatenate(rows, axis=0)


def _forget_lower_bound(lbl):
    l0, l1 = lbl[0:1], lbl[1:2]
    m = jnp.maximum(l0, l1)
    e0, e1 = jnp.exp(l0 - m), jnp.exp(l1 - m)
    return e0 / (e0 + e1)


def _resident(shape):
    nd = len(shape)
    return pl.BlockSpec(shape, lambda *_: (0,) * nd, pipeline_mode=pl.Buffered(1))


def _compiler_params(n_grid_axes, flags=None):
    return pltpu.CompilerParams(dimension_semantics=("arbitrary",) * n_grid_axes,
                                vmem_limit_bytes=VMEM_LIMIT_BYTES, flags=flags)


def _ffn1_kernel(x_ref, n1_ref, wg_ref, wu_ref, wd_ref, h1_ref):
    x = x_ref[...]
    xn = _rms(x, n1_ref[...]).astype(BF16)
    h1_ref[...] = x + 0.5 * _swiglu(xn, wg_ref, wu_ref, wd_ref)


def _ffn1(x, n1, wg, wu, wd, tm):
    n = x.shape[0]
    return pl.pallas_call(
        _ffn1_kernel,
        grid=(n // tm,),
        in_specs=[pl.BlockSpec((tm, D_MODEL), lambda i: (i, 0)),
                  _resident(n1.shape), _resident(wg.shape), _resident(wu.shape), _resident(wd.shape)],
        out_specs=pl.BlockSpec((tm, D_MODEL), lambda i: (i, 0)),
        out_shape=jax.ShapeDtypeStruct((n, D_MODEL), F32),
        compiler_params=_compiler_params(1),
        name="ffn1",
    )(x, n1, wg, wu, wd)


def _ffn_tile_step(xn, wg_ref, wu_ref, wd_ref, wg_out, wu_out, wd_out, acc_scr):
    wg, wu, wd = wg_ref[...].astype(BF16), wu_ref[...].astype(BF16), wd_ref[...].astype(BF16)
    wg_out[...] = wg
    wu_out[...] = wu
    wd_out[...] = wd
    acc_scr[...] += _dot((_silu(_dot(xn, wg)) * _dot(xn, wu)).astype(BF16), wd)


def _ffn1_small_kernel(x_ref, n1_ref, wg_ref, wu_ref, wd_ref, h1_ref, wg_out, wu_out, wd_out, xn_scr, acc_scr):
    j = pl.program_id(0)

    @pl.when(j == 0)
    def _():
        xn_scr[...] = _rms(x_ref[...], n1_ref[...]).astype(BF16)
        acc_scr[...] = jnp.zeros(acc_scr.shape, F32)

    _ffn_tile_step(xn_scr[...], wg_ref, wu_ref, wd_ref, wg_out, wu_out, wd_out, acc_scr)

    @pl.when(j == pl.num_programs(0) - 1)
    def _():
        h1_ref[...] = x_ref[...] + 0.5 * acc_scr[...]


def _ffn2_small_kernel(h1_ref, mix_ref, wo_ref, n2_ref, wg_ref, wu_ref, wd_ref, nf_ref,
                       y_ref, wg_out, wu_out, wd_out, h2_scr, hn_scr, acc_scr):
    j = pl.program_id(0)

    @pl.when(j == 0)
    def _():
        h2 = h1_ref[...] + _dot(mix_ref[...].astype(BF16), wo_ref[...])
        h2_scr[...] = h2
        hn_scr[...] = _rms(h2, n2_ref[...]).astype(BF16)
        acc_scr[...] = jnp.zeros(acc_scr.shape, F32)

    _ffn_tile_step(hn_scr[...], wg_ref, wu_ref, wd_ref, wg_out, wu_out, wd_out, acc_scr)

    @pl.when(j == pl.num_programs(0) - 1)
    def _():
        y_ref[...] = _rms(h2_scr[...] + 0.5 * acc_scr[...], nf_ref[...])


def _weight_tile_specs():
    cols = pl.BlockSpec((D_MODEL, FF_TILE), lambda j: (0, j))
    rows = pl.BlockSpec((FF_TILE, D_MODEL), lambda j: (j, 0))
    shapes = [jax.ShapeDtypeStruct((D_MODEL, D_FF), BF16), jax.ShapeDtypeStruct((D_MODEL, D_FF), BF16),
              jax.ShapeDtypeStruct((D_FF, D_MODEL), BF16)]
    return [cols, cols, rows], shapes


def _ffn1_small(x, n1, wg, wu, wd):
    n = x.shape[0]
    wspecs, wshapes = _weight_tile_specs()
    return pl.pallas_call(
        _ffn1_small_kernel,
        grid=(D_FF // FF_TILE,),
        in_specs=[_resident(x.shape), _resident(n1.shape)] + wspecs,
        out_specs=[pl.BlockSpec((n, D_MODEL), lambda j: (0, 0))] + wspecs,
        out_shape=[jax.ShapeDtypeStruct((n, D_MODEL), F32)] + wshapes,
        scratch_shapes=[pltpu.VMEM((n, D_MODEL), BF16), pltpu.VMEM((n, D_MODEL), F32)],
        compiler_params=_compiler_params(1),
        name="ffn1_small",
    )(x, n1, wg, wu, wd)


def _ffn2_small(h1, h1_block, mix, wo, n2, wg, wu, wd, nf):
    n = mix.shape[0]
    wspecs, wshapes = _weight_tile_specs()
    return pl.pallas_call(
        _ffn2_small_kernel,
        grid=(D_FF // FF_TILE,),
        in_specs=[pl.BlockSpec((n, D_MODEL), lambda j: (h1_block, 0), pipeline_mode=pl.Buffered(1)),
                  _resident(mix.shape), _resident(wo.shape), _resident(n2.shape)] + wspecs
        + [_resident(nf.shape)],
        out_specs=[pl.BlockSpec((n, D_MODEL), lambda j: (0, 0))] + wspecs,
        out_shape=[jax.ShapeDtypeStruct((n, D_MODEL), F32)] + wshapes,
        scratch_shapes=[pltpu.VMEM((n, D_MODEL), F32), pltpu.VMEM((n, D_MODEL), BF16),
                        pltpu.VMEM((n, D_MODEL), F32)],
        compiler_params=_compiler_params(1),
        name="ffn2_small",
    )(h1, mix, wo, n2, wg, wu, wd, nf)


def _proj_kernel(h1_ref, nm_ref, win_ref, u_ref):
    u_ref[...] = _dot(_rms(h1_ref[...], nm_ref[...]).astype(BF16), win_ref[...])


def _proj_small(h1, nm, win):
    n = h1.shape[0]
    return pl.pallas_call(
        _proj_kernel,
        grid=(1,),
        in_specs=[_resident(h1.shape), _resident(nm.shape), _resident(win.shape)],
        out_specs=pl.BlockSpec((n, D_IN_PAD), lambda i: (0, 0)),
        out_shape=jax.ShapeDtypeStruct((n, D_IN_PAD), F32),
        compiler_params=_compiler_params(1),
        name="proj_small",
    )(h1, nm, win)


def _ffn_out_kernel(h1_ref, mix_ref, wo_ref, n2_ref, wg_ref, wu_ref, wd_ref, nf_ref, y_ref):
    h2 = h1_ref[...] + _dot(mix_ref[...].astype(BF16), wo_ref[...])
    hn = _rms(h2, n2_ref[...]).astype(BF16)
    h3 = h2 + 0.5 * _swiglu(hn, wg_ref, wu_ref, wd_ref)
    y_ref[...] = _rms(h3, nf_ref[...])


def _ffn_out(h1, h1_block0, mix, wo, n2, wg, wu, wd, nf, tm):
    n = mix.shape[0]
    return pl.pallas_call(
        _ffn_out_kernel,
        grid=(n // tm,),
        in_specs=[pl.BlockSpec((tm, D_MODEL), lambda i: (i + h1_block0, 0)),
                  pl.BlockSpec((tm, D_MODEL), lambda i: (i, 0)),
                  _resident(wo.shape), _resident(n2.shape), _resident(wg.shape), _resident(wu.shape),
                  _resident(wd.shape), _resident(nf.shape)],
        out_specs=pl.BlockSpec((tm, D_MODEL), lambda i: (i, 0)),
        out_shape=jax.ShapeDtypeStruct((n, D_MODEL), F32),
        compiler_params=_compiler_params(1),
        name="ffn_out",
    )(h1, mix, wo, n2, wg, wu, wd, nf)


def _head_slices(a, width):
    return [a[:, i:i + width] for i in range(0, a.shape[1], width)]


def _hgrn_block(n_pad, tb, u_ref, lb, hgn, tri, mix_ref, st_scr):
    c = HG_CHUNK
    fz = u_ref[:, COL_F:COL_F + HG_WIDTH]
    logf = jnp.log(lb + (1.0 - lb) * jax.nn.sigmoid(fz))
    kk = (1.0 - lb) * jax.nn.sigmoid(-fz)
    if n_pad:
        valid = lax.broadcasted_iota(jnp.int32, (tb, HG_WIDTH), 0) >= n_pad
        logf = jnp.where(valid, logf, 0.0)
        kk = jnp.where(valid, kk, 0.0)
    q = _silu(u_ref[:, COL_Q:COL_Q + HG_WIDTH])
    v = u_ref[:, COL_I:COL_I + HG_WIDTH].astype(BF16)
    gate = _silu(u_ref[:, COL_G:COL_G + HG_WIDTH])

    b = _cumsum_rows(tri, logf)
    qt = (q * jnp.exp(b)).astype(BF16)
    kt = (kk * jnp.exp(-b)).astype(BF16)
    causal = _tiled_lower_tri(c, HG_HEADS)

    for r0 in range(0, tb, c):
        yield
        rows = slice(r0, r0 + c)
        b_c = b[rows]
        b_last = b_c[c - 1:c, :]
        kh_c = (kk[rows] * jnp.exp(b_last - b_c)).astype(BF16)
        decay = jnp.exp(b_last)
        qt_c = qt[rows]
        kd = _block_diag(_head_slices(kt[rows], HG_DIM))
        vd = _block_diag(_head_slices(v[rows], HG_DIM))
        scores = jnp.where(causal, _dot_nt(qt_c, kd), 0.0).astype(BF16)
        o = _dot(scores, vd)
        o_prev = []
        for p in range(HG_PAIRS):
            lanes = slice(p * 2 * HG_DIM, (p + 1) * 2 * HG_DIM)
            heads = (2 * p, 2 * p + 1)
            st = [st_scr[h] for h in heads]
            o_prev.append(_dot_nt(qt_c[:, lanes], _block_diag([s.astype(BF16) for s in st])))
            upd = _dot_tn(vd[p * 2 * c:(p + 1) * 2 * c, lanes], jnp.concatenate([kh_c[:, lanes]] * 2, axis=0))
            for i, h in enumerate(heads):
                blk = slice(i * HG_DIM, (i + 1) * HG_DIM)
                st_scr[h] = st[i] * decay[:, h * HG_DIM:(h + 1) * HG_DIM] + upd[blk, blk]
        o = o + jnp.concatenate(o_prev, axis=1)
        for h in range(HG_HEADS):
            sl = slice(h * HG_DIM, (h + 1) * HG_DIM)
            oh = o[:, sl]
            oh = oh * lax.rsqrt(jnp.mean(oh * oh, axis=-1, keepdims=True) + EPS)
            mix_ref[rows, sl] = (oh * hgn[:, sl] * gate[rows, sl]).astype(mix_ref.dtype)


def _pair_columns(a, r0, r1, first_half):
    shape = (a.shape[0], LANES)
    return jnp.where(first_half, jnp.broadcast_to(a[:, r0:r0 + 1], shape), jnp.broadcast_to(a[:, r1:r1 + 1], shape))


def _group_columns(a, g, first_half):
    r = g * HEADS_PER_GROUP
    return jnp.concatenate([_pair_columns(a, r + i, r + i + 1, first_half)
                            for i in range(0, HEADS_PER_GROUP, 2)], axis=1)


def _ssd_block(n_pad, tb, u_ref, col0, cw, cb, dtb, a_neg, dsk, ssn, tri, mix_ref, hg_scr, xpad):
    c = SSM_CHUNK
    conv = cb
    for j in range(CONV_WIDTH):
        off = PAD_ROWS - (CONV_WIDTH - 1) + j
        conv = conv + cw[j:j + 1, :] * xpad[off:off + tb, :]
    act = _silu(conv)
    xs = act[:, 0:SSM_WIDTH]
    bm = act[:, SSM_WIDTH:SSM_WIDTH + SSM_GROUPS * SSM_STATE].astype(BF16)
    cm = act[:, SSM_WIDTH + SSM_GROUPS * SSM_STATE:].astype(BF16)
    z_gate = _silu(u_ref[:, col0:col0 + SSM_WIDTH])

    col_dt = col0 + COL_DT - COL_Z
    dt = jax.nn.softplus(u_ref[:, col_dt:col_dt + LANES] + dtb)
    if n_pad:
        dt = jnp.where(lax.broadcasted_iota(jnp.int32, (tb, LANES), 0) >= n_pad, dt, 0.0)
    cum = _cumsum_rows(tri, dt * a_neg)
    causal = _lower_tri(c)
    first_half = lax.broadcasted_iota(jnp.int32, (1, LANES), 1) < SSM_HEAD_DIM
    zeros = jnp.zeros((c, LANES), BF16)

    for r0 in range(0, tb, c):
        rows = slice(r0, r0 + c)
        cum_c = cum[rows]
        dt_c = dt[rows]
        last = cum_c[c - 1:c, :]
        cum_t = cum_c.T
        dt_t = dt_c.T
        e_cum = jnp.exp(cum_c)
        w_in = dt_c * jnp.exp(last - cum_c)
        e_last = jnp.exp(last)
        for g in range(SSM_GROUPS):
            yield
            glanes = slice(g * GROUP_WIDTH, (g + 1) * GROUP_WIDTH)
            bg = bm[rows, g * SSM_STATE:(g + 1) * SSM_STATE]
            cg = cm[rows, g * SSM_STATE:(g + 1) * SSM_STATE]
            xg = xs[rows, glanes]
            cbt = _dot_nt(cg, bg)
            m_heads = []
            for rr in range(HEADS_PER_GROUP):
                r = g * HEADS_PER_GROUP + rr
                seg = jnp.exp(jnp.where(causal, cum_c[:, r:r + 1] - cum_t[r:r + 1, :], -jnp.inf))
                m_heads.append((cbt * seg * dt_t[r:r + 1, :]).astype(BF16))
            xb = xg.astype(BF16)
            xd_rows = []
            for rr in range(HEADS_PER_GROUP):
                tile = xb[:, (rr // 2) * LANES:(rr // 2 + 1) * LANES]
                tile = jnp.where(first_half if rr % 2 == 0 else ~first_half, tile, jnp.zeros_like(tile))
                xd_rows.append(jnp.concatenate([tile, zeros] if rr < 2 else [zeros, tile], axis=1))
            xd = jnp.concatenate(xd_rows, axis=0)
            hg = hg_scr[g]
            y = (_dot(jnp.concatenate(m_heads, axis=1), xd)
                 + _dot(cg, hg.astype(BF16)) * _group_columns(e_cum, g, first_half))
            xw = (xg * _group_columns(w_in, g, first_half)).astype(BF16)
            hg_scr[g] = hg * _group_columns(e_last, g, first_half) + _dot_tn(bg, xw)
            yz = (y + dsk[:, glanes] * xg) * z_gate[rows, glanes]
            yz = yz * lax.rsqrt(jnp.mean(yz * yz, axis=-1, keepdims=True) + EPS) * ssn[:, glanes]
            mix_ref[rows, HG_WIDTH + g * GROUP_WIDTH:HG_WIDTH + (g + 1) * GROUP_WIDTH] = yz.astype(mix_ref.dtype)


def _interleave(main, side, side_steps):
    for n in side_steps:
        if next(main, StopIteration) is StopIteration:
            break
        for _ in range(n):
            next(side, None)
    for _ in main:
        pass
    for _ in side:
        pass


def _mixer_block(*args):
    for _ in _mixer_steps(*args):
        pass


def _mixer_steps(n_pad, tb, uh_ref, us_ref, us_col0, param_refs, mix_ref, st_scr, hg_scr, xpad):
    lbl_ref, hgn_ref, cw_ref, cb_ref, dtb_ref, alog_ref, dsk_ref, ssn_ref, tri_hg_ref, tri_ssm_ref = param_refs
    yield from _hgrn_block(n_pad, tb, uh_ref, _forget_lower_bound(lbl_ref[...]), hgn_ref[...], tri_hg_ref[...],
                           mix_ref, st_scr)
    yield
    col_xbc = us_col0 + COL_XBC - COL_Z
    xbc = us_ref[:, col_xbc:col_xbc + CONV_DIM]
    if n_pad:
        xbc = jnp.where(lax.broadcasted_iota(jnp.int32, (tb, CONV_DIM), 0) >= n_pad, xbc, 0.0)
    xpad[PAD_ROWS:PAD_ROWS + tb, :] = xbc
    yield from _ssd_block(n_pad, tb, us_ref, us_col0, cw_ref[...], cb_ref[...], dtb_ref[...],
                          -jnp.exp(alog_ref[...]), dsk_ref[...], ssn_ref[...], tri_ssm_ref[...],
                          mix_ref, hg_scr, xpad)
    xpad[0:PAD_ROWS, :] = xpad[tb:tb + PAD_ROWS, :]


def _load_state(s0_ref, h0_ref, c0_ref, st_scr, hg_scr, xpad):
    for h in range(HG_HEADS):
        st_scr[h] = s0_ref[0, h].T
    for r in range(0, SSM_HEADS, 2):
        g, lane0 = r // HEADS_PER_GROUP, (r % HEADS_PER_GROUP) * SSM_HEAD_DIM
        hg_scr[g, :, lane0:lane0 + LANES] = jnp.concatenate([h0_ref[0, r], h0_ref[0, r + 1]], axis=0).T
    xpad[0:PAD_ROWS, :] = c0_ref[0]


def _store_state(s_out_ref, h_out_ref, c_out_ref, st_scr, hg_scr, xpad):
    for h in range(HG_HEADS):
        s_out_ref[0, h] = st_scr[h].T
    for r in range(0, SSM_HEADS, 2):
        g, lane0 = r // HEADS_PER_GROUP, (r % HEADS_PER_GROUP) * SSM_HEAD_DIM
        pair = hg_scr[g, :, lane0:lane0 + LANES].T
        h_out_ref[0, r] = pair[0:SSM_HEAD_DIM]
        h_out_ref[0, r + 1] = pair[SSM_HEAD_DIM:]
    c_out_ref[0] = xpad[0:PAD_ROWS, :]


N_MIXER_PARAMS = 10


def _scan_params(params, tb):
    return tuple(params) + (_chunked_tri(tb, HG_CHUNK), _chunked_tri(tb, SSM_CHUNK))


PROJ_TILE = 512
PROJ_STEPS_AFTER_SCAN_STEP = (0, 0, 0, 0, 3, 2, 2, 1, 1)
HG_SHAPE = (SSM_GROUPS, SSM_STATE, GROUP_WIDTH)
S_SHAPE = (HG_HEADS, HG_DIM, HG_DIM)
H_SHAPE = (SSM_HEADS, SSM_HEAD_DIM, SSM_STATE)
C_SHAPE = (PAD_ROWS, CONV_DIM)


def _scan_kernel(n_pad, tb, project, *refs):
    if project:
        h1_ref, nm_ref, win_ref = refs[:3]
        refs = refs[3:]
    else:
        u_ref = refs[0]
        refs = refs[1:]
    param_refs = refs[:N_MIXER_PARAMS]
    s0_ref, h0_ref, c0_ref, mix_ref, s_out_ref, h_out_ref, c_out_ref = refs[N_MIXER_PARAMS:N_MIXER_PARAMS + 7]
    scratch = refs[N_MIXER_PARAMS + 7:]
    if project:
        uh_scr, us_scr, st_scr, hg_scr, xpad = scratch
    else:
        st_scr, hg_scr, xpad = scratch
    t = pl.program_id(1)

    @pl.when(t == 0)
    def _():
        _load_state(s0_ref, h0_ref, c0_ref, st_scr, hg_scr, xpad)

    if project:
        hn = _rms(h1_ref[...], nm_ref[...]).astype(BF16)
        uh_scr[...] = _dot(hn, win_ref[:, 0:COL_Z])
        us_scr[...] = _dot(hn, win_ref[:, COL_Z:])
        _mixer_block(n_pad, tb, uh_scr, us_scr, 0, param_refs, mix_ref, st_scr, hg_scr, xpad)
    else:
        _mixer_block(n_pad, tb, u_ref, u_ref, COL_Z, param_refs, mix_ref, st_scr, hg_scr, xpad)

    @pl.when(t == pl.num_programs(1) - 1)
    def _():
        _store_state(s_out_ref, h_out_ref, c_out_ref, st_scr, hg_scr, xpad)


def _scan_pipelined_kernel(tb, nt, h1_ref, nm_ref, win_ref, *refs):
    param_refs = refs[:N_MIXER_PARAMS]
    s0_ref, h0_ref, c0_ref, mix_ref, s_out_ref, h_out_ref, c_out_ref = refs[N_MIXER_PARAMS:N_MIXER_PARAMS + 7]
    uh_even, us_even, uh_odd, us_odd, st_scr, hg_scr, xpad = refs[N_MIXER_PARAMS + 7:]
    s = pl.program_id(0)
    scanned = jnp.maximum(s - 1, 0)

    @pl.when(s == 0)
    def _():
        uh_odd[...] = jnp.zeros(uh_odd.shape, F32)
        us_odd[...] = jnp.zeros(us_odd.shape, F32)

    @pl.when(scanned % nt == 0)
    def _():
        _load_state(s0_ref, h0_ref, c0_ref, st_scr, hg_scr, xpad)

    def project_steps(uh_w, us_w):
        hn = _rms(h1_ref[...], nm_ref[...]).astype(BF16)
        for c0 in range(0, D_IN_PAD, PROJ_TILE):
            c1 = min(c0 + PROJ_TILE, D_IN_PAD)
            yield
            if c0 < COL_Z:
                uh_w[:, c0:c1] = _dot(hn, win_ref[:, c0:c1])
            else:
                us_w[:, c0 - COL_Z:c1 - COL_Z] = _dot(hn, win_ref[:, c0:c1])

    def body(uh_w, us_w, uh_r, us_r):
        _interleave(_mixer_steps(0, tb, uh_r, us_r, 0, param_refs, mix_ref, st_scr, hg_scr, xpad),
                    project_steps(uh_w, us_w), PROJ_STEPS_AFTER_SCAN_STEP)

    pl.when(s % 2 == 0)(lambda: body(uh_even, us_even, uh_odd, us_odd))
    pl.when(s % 2 == 1)(lambda: body(uh_odd, us_odd, uh_even, us_even))

    @pl.when((s >= 1) & (scanned % nt == nt - 1))
    def _():
        _store_state(s_out_ref, h_out_ref, c_out_ref, st_scr, hg_scr, xpad)


def _scan_pipelined(h1, nb, nt, tb, proj, params, s0, h0, c0):
    n_blocks = nb * nt
    last = n_blocks - 1
    lead = tuple(proj) + _scan_params(params, tb)
    shared = lambda shape: pl.BlockSpec((1,) + shape[1:], lambda s: (0,) * len(shape))
    per_seq = lambda shape: pl.BlockSpec((1,) + shape, lambda s: (jnp.maximum(s - 1, 0) // nt,) + (0,) * len(shape))
    u_bufs = [pltpu.VMEM((tb, COL_Z), F32), pltpu.VMEM((tb, D_IN_PAD - COL_Z), F32)]
    return pl.pallas_call(
        functools.partial(_scan_pipelined_kernel, tb, nt),
        grid=(n_blocks + 1,),
        in_specs=[pl.BlockSpec((tb, D_MODEL), lambda s: (jnp.minimum(s, last), 0))]
        + [_resident(p.shape) for p in lead]
        + [shared(s0.shape), shared(h0.shape), shared(c0.shape)],
        out_specs=[pl.BlockSpec((tb, D_MODEL), lambda s: (jnp.maximum(s - 1, 0), 0)),
                   per_seq(S_SHAPE), per_seq(H_SHAPE), per_seq(C_SHAPE)],
        out_shape=[jax.ShapeDtypeStruct((n_blocks * tb, D_MODEL), BF16),
                   jax.ShapeDtypeStruct((nb,) + S_SHAPE, F32),
                   jax.ShapeDtypeStruct((nb,) + H_SHAPE, F32),
                   jax.ShapeDtypeStruct((nb,) + C_SHAPE, F32)],
        scratch_shapes=u_bufs + u_bufs + [pltpu.VMEM(S_SHAPE, F32), pltpu.VMEM(HG_SHAPE, F32),
                                          pltpu.VMEM((PAD_ROWS + tb, CONV_DIM), F32)],
        compiler_params=_compiler_params(1),
        name="proj_mixer",
    )(h1, *lead, s0, h0, c0)


def _scan(src, src_block0, nb, nt, tb, n_pad, proj, params, s0, h0, c0):
    lead = (tuple(proj) if proj else ()) + _scan_params(params, tb)
    shared = lambda shape: pl.BlockSpec((1,) + shape[1:], lambda b, t: (0,) * len(shape))
    per_seq = lambda shape: pl.BlockSpec((1,) + shape, lambda b, t: (b,) + (0,) * len(shape))
    scratch = [pltpu.VMEM(S_SHAPE, F32), pltpu.VMEM(HG_SHAPE, F32), pltpu.VMEM((PAD_ROWS + tb, CONV_DIM), F32)]
    if proj:
        scratch = [pltpu.VMEM((tb, COL_Z), F32), pltpu.VMEM((tb, D_IN_PAD - COL_Z), F32)] + scratch
    return pl.pallas_call(
        functools.partial(_scan_kernel, n_pad, tb, bool(proj)),
        grid=(nb, nt),
        in_specs=[pl.BlockSpec((tb, src.shape[1]), lambda b, t: (src_block0 + b * nt + t, 0))]
        + [_resident(p.shape) for p in lead]
        + [shared(s0.shape), shared(h0.shape), shared(c0.shape)],
        out_specs=[pl.BlockSpec((tb, D_MODEL), lambda b, t: (b * nt + t, 0)),
                   per_seq(S_SHAPE), per_seq(H_SHAPE), per_seq(C_SHAPE)],
        out_shape=[jax.ShapeDtypeStruct((nb * nt * tb, D_MODEL), BF16),
                   jax.ShapeDtypeStruct((nb,) + S_SHAPE, F32),
                   jax.ShapeDtypeStruct((nb,) + H_SHAPE, F32),
                   jax.ShapeDtypeStruct((nb,) + C_SHAPE, F32)],
        scratch_shapes=scratch,
        compiler_params=_compiler_params(2),
        name="proj_mixer" if proj else "mixer",
    )(src, *lead, s0, h0, c0)


def _column_tile(row):
    return jnp.broadcast_to(row, (LANES, LANES)).T


def _sample_kernel(nbs, u_ref, lbl_ref, hgn_ref, cw_ref, cb_ref, dtb_ref, alog_ref, dsk_ref, ssn_ref,
                   sh_ref, ss_ref, sc_ref, mix_ref, sh_out_ref, ss_out_ref, sc_out_ref, obuf, ybuf):
    lb = _forget_lower_bound(lbl_ref[...])
    fz = u_ref[:, COL_F:COL_F + HG_WIDTH]
    f = lb + (1.0 - lb) * jax.nn.sigmoid(fz)
    kk = (1.0 - lb) * jax.nn.sigmoid(-fz)
    q = _silu(u_ref[:, COL_Q:COL_Q + HG_WIDTH])
    v = u_ref[:, COL_I:COL_I + HG_WIDTH]

    xbc = u_ref[:, COL_XBC:COL_XBC + CONV_DIM]
    cw = cw_ref[...]
    conv = cb_ref[...] + cw[CONV_WIDTH - 1:CONV_WIDTH, :] * xbc
    for j in range(CONV_WIDTH - 1):
        conv = conv + cw[j:j + 1, :] * sc_ref[j]
    for j in range(CONV_WIDTH - 2):
        sc_out_ref[j] = sc_ref[j + 1]
    sc_out_ref[CONV_WIDTH - 2] = xbc
    act = _silu(conv)
    xs = act[:, 0:SSM_WIDTH]
    bm = act[:, SSM_WIDTH:SSM_WIDTH + SSM_GROUPS * SSM_STATE]
    cm = act[:, SSM_WIDTH + SSM_GROUPS * SSM_STATE:]
    dt = jax.nn.softplus(u_ref[:, COL_DT:COL_DT + LANES] + dtb_ref[...])
    d_a = jnp.exp(dt * (-jnp.exp(alog_ref[...])))

    top_half = lax.broadcasted_iota(jnp.int32, (LANES, LANES), 0) < SSM_HEAD_DIM
    for j in range(nbs):
        row = slice(j, j + 1)
        for h in range(HG_HEADS):
            sl = slice(h * HG_DIM, (h + 1) * HG_DIM)
            s_new = sh_ref[j, h] * _column_tile(f[row, sl]) + _column_tile(kk[row, sl]) * v[row, sl]
            sh_out_ref[j, h] = s_new
            obuf[row, sl] = jnp.sum(s_new * _column_tile(q[row, sl]), axis=0, keepdims=True)
        for rp in range(SSM_HEADS // 2):
            r0, r1 = 2 * rp, 2 * rp + 1
            g = r0 // HEADS_PER_GROUP
            sl = slice(rp * LANES, (rp + 1) * LANES)
            gsl = slice(g * SSM_STATE, (g + 1) * SSM_STATE)
            h2 = jnp.concatenate([ss_ref[j, r0], ss_ref[j, r1]], axis=0)
            da2 = jnp.where(top_half, d_a[row, r0:r0 + 1], d_a[row, r1:r1 + 1])
            dt2 = jnp.where(top_half, dt[row, r0:r0 + 1], dt[row, r1:r1 + 1])
            h_new = h2 * da2 + (dt2 * _column_tile(xs[row, sl])) * bm[row, gsl]
            ss_out_ref[j, r0] = h_new[0:SSM_HEAD_DIM]
            ss_out_ref[j, r1] = h_new[SSM_HEAD_DIM:]
            ybuf[row, sl] = jnp.sum((h_new * cm[row, gsl]).T, axis=0, keepdims=True)

    o = obuf[...]
    gate = _silu(u_ref[:, COL_G:COL_G + HG_WIDTH])
    hgn = hgn_ref[...]
    for h in range(HG_HEADS):
        sl = slice(h * HG_DIM, (h + 1) * HG_DIM)
        oh = o[:, sl]
        oh = oh * lax.rsqrt(jnp.mean(oh * oh, axis=-1, keepdims=True) + EPS)
        mix_ref[:, sl] = oh * hgn[:, sl] * gate[:, sl]
    yz = (ybuf[...] + dsk_ref[...] * xs) * _silu(u_ref[:, COL_Z:COL_Z + SSM_WIDTH])
    ssn = ssn_ref[...]
    for g in range(SSM_GROUPS):
        sl = slice(g * GROUP_WIDTH, (g + 1) * GROUP_WIDTH)
        seg = yz[:, sl]
        seg = seg * lax.rsqrt(jnp.mean(seg * seg, axis=-1, keepdims=True) + EPS) * ssn[:, sl]
        mix_ref[:, HG_WIDTH + g * GROUP_WIDTH:HG_WIDTH + (g + 1) * GROUP_WIDTH] = seg


def _sample_mixer(u, u_block0, n, nbs, params, sh, ss, sc_t):
    small = [_resident(p.shape) for p in params]
    return pl.pallas_call(
        functools.partial(_sample_kernel, nbs),
        grid=(n // nbs,),
        in_specs=[pl.BlockSpec((nbs, D_IN_PAD), lambda i: (u_block0 + i, 0))] + small
        + [pl.BlockSpec((nbs,) + sh.shape[1:], lambda i: (i, 0, 0, 0)),
           pl.BlockSpec((nbs,) + ss.shape[1:], lambda i: (i, 0, 0, 0)),
           pl.BlockSpec((CONV_WIDTH - 1, nbs, CONV_DIM), lambda i: (0, i, 0))],
        out_specs=[pl.BlockSpec((nbs, D_MODEL), lambda i: (i, 0)),
                   pl.BlockSpec((nbs,) + sh.shape[1:], lambda i: (i, 0, 0, 0)),
                   pl.BlockSpec((nbs,) + ss.shape[1:], lambda i: (i, 0, 0, 0)),
                   pl.BlockSpec((CONV_WIDTH - 1, nbs, CONV_DIM), lambda i: (0, i, 0))],
        out_shape=[jax.ShapeDtypeStruct((n, D_MODEL), F32),
                   jax.ShapeDtypeStruct(sh.shape, F32),
                   jax.ShapeDtypeStruct(ss.shape, F32),
                   jax.ShapeDtypeStruct(sc_t.shape, F32)],
        scratch_shapes=[pltpu.VMEM((nbs, HG_WIDTH), F32), pltpu.VMEM((nbs, SSM_WIDTH), F32)],
        compiler_params=_compiler_params(1),
        name="sample_mixer",
    )(u, *params, sh, ss, sc_t)


TM_DENSE = 512
TM_SCAN = 256
TM_SMALL = 128
SAMPLES_PER_STEP = 8


def _pad_lanes(row, value=0.0):
    return jnp.pad(row, ((0, 0), (0, LANES - row.shape[1])), constant_values=value)


def kernel(x_prompt, x_sample, state_hgrn, state_ssm, state_conv, meta_tokens, lb_logits, norm_ffn1, w_ffn1_gate, w_ffn1_up, w_ffn1_down, norm_mix, w_in, hg_norm, conv_w, conv_b, dt_bias, a_log, d_skip, ssm_norm, w_out, norm_ffn2, w_ffn2_gate, w_ffn2_up, w_ffn2_down, norm_final):
    bp, seq_p, _ = x_prompt.shape
    n_s = x_sample.shape[0]
    assert x_sample.shape[1] == 1 and n_s == TM_SMALL and seq_p % TM_SCAN == 0 and (bp * seq_p) % TM_DENSE == 0
    layer = 0

    n1, nm, n2 = norm_ffn1[layer][None], norm_mix[layer][None], norm_ffn2[layer][None]
    nf = norm_final[None]
    win =jnp.pad(w_in[layer], ((0, 0), (0, D_IN_PAD - w_in.shape[2]))).astype(BF16)
    wo = w_out[layer].astype(BF16)
    mixer_params = (lb_logits, hg_norm[layer][None], conv_w[layer], conv_b[layer][None],
                    _pad_lanes(dt_bias[layer][None]), _pad_lanes(a_log[layer][None]),
                    jnp.repeat(d_skip[layer], SSM_HEAD_DIM)[None], ssm_norm[layer][None])

    n_pad = TM_SMALL - N_META
    x_small = jnp.concatenate([jnp.zeros((n_pad, D_MODEL), F32), meta_tokens, x_sample[:, 0]], axis=0)
    h1_small, wg1, wu1, wd1 = _ffn1_small(x_small, n1, w_ffn1_gate[layer], w_ffn1_up[layer], w_ffn1_down[layer])
    u_small = _proj_small(h1_small, nm, win)

    zeros_s = jnp.zeros((1,) + S_SHAPE, F32)
    zeros_h = jnp.zeros((1,) + H_SHAPE, F32)
    zeros_c = jnp.zeros((1,) + C_SHAPE, F32)
    _, s_meta, h_meta, c_meta = _scan(u_small, 0, 1, 1, TM_SMALL, n_pad, None, mixer_params,
                                      zeros_s, zeros_h, zeros_c)

    sc_t = jnp.swapaxes(state_conv[layer], 0, 1)
    mix_s, hgrn_s, ssm_s, conv_s_t = _sample_mixer(
        u_small, TM_SMALL // SAMPLES_PER_STEP, n_s, SAMPLES_PER_STEP, mixer_params,
        state_hgrn[layer], state_ssm[layer], sc_t)
    y_s, wg2, wu2, wd2 = _ffn2_small(h1_small, 1, mix_s, wo, n2, w_ffn2_gate[layer], w_ffn2_up[layer],
                                     w_ffn2_down[layer], nf)

    xp = x_prompt.reshape(bp * seq_p, D_MODEL)
    h1_p = _ffn1(xp, n1, wg1, wu1, wd1, TM_DENSE)
    mix_p, hgrn_p, ssm_p, conv_p = _scan_pipelined(h1_p, bp, seq_p // TM_SCAN, TM_SCAN, (nm, win), mixer_params,
                                                   s_meta, h_meta, c_meta)
    y_p = _ffn_out(h1_p, 0, mix_p, wo, n2, wg2, wu2, wd2, nf, TM_DENSE)

    keep = slice(PAD_ROWS - (CONV_WIDTH - 1), PAD_ROWS)
    return (y_p.reshape(bp, seq_p, D_MODEL),
            y_s.reshape(n_s, 1, D_MODEL),
            hgrn_p[None], ssm_p[None], conv_p[:, keep][None],
            hgrn_s[None], ssm_s[None], jnp.swapaxes(conv_s_t, 0, 1)[None])
```

```python
import functools

import jax
import jax.numpy as jnp
from jax import lax
from jax.experimental import pallas as pl
from jax.experimental.pallas import tpu as pltpu

F32 = jnp.float32
BF16 = jnp.bfloat16

D_MODEL = 1024
D_FF = 2816
N_META = 16
HG_WIDTH = 512
HG_HEADS = 4
HG_DIM = 128
SSM_WIDTH = 512
SSM_HEADS = 8
SSM_HEAD_DIM = 64
SSM_GROUPS = 2
SSM_STATE = 128
CONV_WIDTH = 4
CONV_DIM = SSM_WIDTH + 2 * SSM_GROUPS * SSM_STATE
EPS = 1e-6

LANES = 128
SUBLANES = 8
VMEM_LIMIT_BYTES = 56 * 1024 * 1024

COL_Q = 0
COL_F = HG_WIDTH
COL_I = 2 * HG_WIDTH
COL_G = 3 * HG_WIDTH
COL_Z = 4 * HG_WIDTH
COL_XBC = COL_Z + SSM_WIDTH
COL_DT = COL_XBC + CONV_DIM
D_IN_PAD = COL_DT + LANES

FF_TILE = 256
HG_CHUNK = 64
HG_MAX_CHUNK_LOG_DECAY = 80.0
SSM_CHUNK = 128
PAD_ROWS = SUBLANES
HG_PAIRS = HG_HEADS // 2
HEADS_PER_GROUP = SSM_HEADS // SSM_GROUPS
GROUP_WIDTH = SSM_WIDTH // SSM_GROUPS
assert 2 * SSM_HEAD_DIM == LANES and 2 * HG_DIM == FF_TILE


def _dot(a, b):
    return jnp.dot(a, b, preferred_element_type=F32)


def _dot_nt(a, b):
    return lax.dot_general(a, b, (((1,), (1,)), ((), ())), preferred_element_type=F32)


def _dot_tn(a, b):
    return lax.dot_general(a, b, (((0,), (0,)), ((), ())), preferred_element_type=F32)


def _rms(x, w):
    return x * lax.rsqrt(jnp.mean(x * x, axis=-1, keepdims=True) + EPS) * w


def _silu(x):
    return x * jax.nn.sigmoid(x)


def _swiglu(xn, wg_ref, wu_ref, wd_ref):
    acc = jnp.zeros((xn.shape[0], D_MODEL), F32)
    for j in range(D_FF // FF_TILE):
        cols = slice(j * FF_TILE, (j + 1) * FF_TILE)
        g = _dot(xn, wg_ref[:, cols])
        u = _dot(xn, wu_ref[:, cols])
        acc = acc + _dot((_silu(g) * u).astype(BF16), wd_ref[cols, :])
    return acc


def _cumsum_rows(tri, a):
    a1 = a.astype(BF16)
    r1 = a - a1.astype(F32)
    a2 = r1.astype(BF16)
    a3 = (r1 - a2.astype(F32)).astype(BF16)
    return _dot(tri, a1) + _dot(tri, a2) + _dot(tri, a3)


def _lower_tri(n):
    row = lax.broadcasted_iota(jnp.int32, (n, n), 0)
    col = lax.broadcasted_iota(jnp.int32, (n, n), 1)
    return row >= col


def _chunked_tri(n, chunk):
    assert chunk & (chunk - 1) == 0
    row = lax.broadcasted_iota(jnp.int32, (n, n), 0)
    col = lax.broadcasted_iota(jnp.int32, (n, n), 1)
    same_chunk = (row ^ col) < chunk
    return ((row >= col) & same_chunk).astype(BF16)


def _tiled_lower_tri(chunk, reps):
    assert chunk & (chunk - 1) == 0
    row = lax.broadcasted_iota(jnp.int32, (chunk, reps * chunk), 0)
    col = lax.broadcasted_iota(jnp.int32, (chunk, reps * chunk), 1)
    return row >= (col & (chunk - 1))


def _block_diag(blocks):
    n = len(blocks)
    r, c = blocks[0].shape
    rows = []
    for i, blk in enumerate(blocks):
        parts = []
        if i:
            parts.append(jnp.zeros((r, c * i), blk.dtype))
        parts.append(blk)
        if i < n - 1:
            parts.append(jnp.zeros((r, c * (n - 1 - i)), blk.dtype))
        rows.append(jnp.concatenate(parts, axis=1))
    return jnp.concatenate(rows, axis=0)


def _forget_lower_bound(lbl):
    l0, l1 = lbl[0:1], lbl[1:2]
    m = jnp.maximum(l0, l1)
    e0, e1 = jnp.exp(l0 - m), jnp.exp(l1 - m)
    return e0 / (e0 + e1)


def _resident(shape):
    nd = len(shape)
    return pl.BlockSpec(shape, lambda *_: (0,) * nd, pipeline_mode=pl.Buffered(1))


def _compiler_params(n_grid_axes, flags=None):
    return pltpu.CompilerParams(dimension_semantics=("arbitrary",) * n_grid_axes,
                                vmem_limit_bytes=VMEM_LIMIT_BYTES, flags=flags)


def _ffn1_kernel(x_ref, n1_ref, wg_ref, wu_ref, wd_ref, h1_ref):
    x = x_ref[...]
    xn = _rms(x, n1_ref[...]).astype(BF16)
    h1_ref[...] = x + 0.5 * _swiglu(xn, wg_ref, wu_ref, wd_ref)


def _ffn1(x, n1, wg, wu, wd, tm):
    n = x.shape[0]
    return pl.pallas_call(
        _ffn1_kernel,
        grid=(n // tm,),
        in_specs=[pl.BlockSpec((tm, D_MODEL), lambda i: (i, 0)),
                  _resident(n1.shape), _resident(wg.shape), _resident(wu.shape), _resident(wd.shape)],
        out_specs=pl.BlockSpec((tm, D_MODEL), lambda i: (i, 0)),
        out_shape=jax.ShapeDtypeStruct((n, D_MODEL), F32),
        compiler_params=_compiler_params(1),
        name="ffn1",
    )(x, n1, wg, wu, wd)


def _ffn_tile_step(xn, wg_ref, wu_ref, wd_ref, wg_out, wu_out, wd_out, acc_scr):
    wg, wu, wd = wg_ref[...].astype(BF16), wu_ref[...].astype(BF16), wd_ref[...].astype(BF16)
    wg_out[...] = wg
    wu_out[...] = wu
    wd_out[...] = wd
    acc_scr[...] += _dot((_silu(_dot(xn, wg)) * _dot(xn, wu)).astype(BF16), wd)


def _ffn1_small_kernel(x_ref, n1_ref, wg_ref, wu_ref, wd_ref, h1_ref, wg_out, wu_out, wd_out, xn_scr, acc_scr):
    j = pl.program_id(0)

    @pl.when(j == 0)
    def _():
        xn_scr[...] = _rms(x_ref[...], n1_ref[...]).astype(BF16)
        acc_scr[...] = jnp.zeros(acc_scr.shape, F32)

    _ffn_tile_step(xn_scr[...], wg_ref, wu_ref, wd_ref, wg_out, wu_out, wd_out, acc_scr)

    @pl.when(j == pl.num_programs(0) - 1)
    def _():
        h1_ref[...] = x_ref[...] + 0.5 * acc_scr[...]


def _ffn2_small_kernel(h1_ref, mix_ref, wo_ref, n2_ref, wg_ref, wu_ref, wd_ref, nf_ref,
                       y_ref, wg_out, wu_out, wd_out, h2_scr, hn_scr, acc_scr):
    j = pl.program_id(0)

    @pl.when(j == 0)
    def _():
        h2 = h1_ref[...] + _dot(mix_ref[...].astype(BF16), wo_ref[...])
        h2_scr[...] = h2
        hn_scr[...] = _rms(h2, n2_ref[...]).astype(BF16)
        acc_scr[...] = jnp.zeros(acc_scr.shape, F32)

    _ffn_tile_step(hn_scr[...], wg_ref, wu_ref, wd_ref, wg_out, wu_out, wd_out, acc_scr)

    @pl.when(j == pl.num_programs(0) - 1)
    def _():
        y_ref[...] = _rms(h2_scr[...] + 0.5 * acc_scr[...], nf_ref[...])


def _weight_tile_specs():
    cols = pl.BlockSpec((D_MODEL, FF_TILE), lambda j: (0, j))
    rows = pl.BlockSpec((FF_TILE, D_MODEL), lambda j: (j, 0))
    shapes = [jax.ShapeDtypeStruct((D_MODEL, D_FF), BF16), jax.ShapeDtypeStruct((D_MODEL, D_FF), BF16),
              jax.ShapeDtypeStruct((D_FF, D_MODEL), BF16)]
    return [cols, cols, rows], shapes


def _ffn1_small(x, n1, wg, wu, wd):
    n = x.shape[0]
    wspecs, wshapes = _weight_tile_specs()
    return pl.pallas_call(
        _ffn1_small_kernel,
        grid=(D_FF // FF_TILE,),
        in_specs=[_resident(x.shape), _resident(n1.shape)] + wspecs,
        out_specs=[pl.BlockSpec((n, D_MODEL), lambda j: (0, 0))] + wspecs,
        out_shape=[jax.ShapeDtypeStruct((n, D_MODEL), F32)] + wshapes,
        scratch_shapes=[pltpu.VMEM((n, D_MODEL), BF16), pltpu.VMEM((n, D_MODEL), F32)],
        compiler_params=_compiler_params(1),
        name="ffn1_small",
    )(x, n1, wg, wu, wd)


def _ffn2_small(h1, h1_block, mix, wo, n2, wg, wu, wd, nf):
    n = mix.shape[0]
    wspecs, wshapes = _weight_tile_specs()
    return pl.pallas_call(
        _ffn2_small_kernel,
        grid=(D_FF // FF_TILE,),
        in_specs=[pl.BlockSpec((n, D_MODEL), lambda j: (h1_block, 0), pipeline_mode=pl.Buffered(1)),
                  _resident(mix.shape), _resident(wo.shape), _resident(n2.shape)] + wspecs
        + [_resident(nf.shape)],
        out_specs=[pl.BlockSpec((n, D_MODEL), lambda j: (0, 0))] + wspecs,
        out_shape=[jax.ShapeDtypeStruct((n, D_MODEL), F32)] + wshapes,
        scratch_shapes=[pltpu.VMEM((n, D_MODEL), F32), pltpu.VMEM((n, D_MODEL), BF16),
                        pltpu.VMEM((n, D_MODEL), F32)],
        compiler_params=_compiler_params(1),
        name="ffn2_small",
    )(h1, mix, wo, n2, wg, wu, wd, nf)


def _proj_kernel(h1_ref, nm_ref, win_ref, u_ref):
    u_ref[...] = _dot(_rms(h1_ref[...], nm_ref[...]).astype(BF16), win_ref[...])


def _proj_small(h1, nm, win):
    n = h1.shape[0]
    return pl.pallas_call(
        _proj_kernel,
        grid=(1,),
        in_specs=[_resident(h1.shape), _resident(nm.shape), _resident(win.shape)],
        out_specs=pl.BlockSpec((n, D_IN_PAD), lambda i: (0, 0)),
        out_shape=jax.ShapeDtypeStruct((n, D_IN_PAD), F32),
        compiler_params=_compiler_params(1),
        name="proj_small",
    )(h1, nm, win)


def _ffn_out_kernel(h1_ref, mix_ref, wo_ref, n2_ref, wg_ref, wu_ref, wd_ref, nf_ref, y_ref):
    h2 = h1_ref[...] + _dot(mix_ref[...].astype(BF16), wo_ref[...])
    hn = _rms(h2, n2_ref[...]).astype(BF16)
    h3 = h2 + 0.5 * _swiglu(hn, wg_ref, wu_ref, wd_ref)
    y_ref[...] = _rms(h3, nf_ref[...])


def _ffn_out(h1, h1_block0, mix, wo, n2, wg, wu, wd, nf, tm):
    n = mix.shape[0]
    return pl.pallas_call(
        _ffn_out_kernel,
        grid=(n // tm,),
        in_specs=[pl.BlockSpec((tm, D_MODEL), lambda i: (i + h1_block0, 0)),
                  pl.BlockSpec((tm, D_MODEL), lambda i: (i, 0)),
                  _resident(wo.shape), _resident(n2.shape), _resident(wg.shape), _resident(wu.shape),
                  _resident(wd.shape), _resident(nf.shape)],
        out_specs=pl.BlockSpec((tm, D_MODEL), lambda i: (i, 0)),
        out_shape=jax.ShapeDtypeStruct((n, D_MODEL), F32),
        compiler_params=_compiler_params(1),
        name="ffn_out",
    )(h1, mix, wo, n2, wg, wu, wd, nf)


def _head_slices(a, width):
    return [a[:, i:i + width] for i in range(0, a.shape[1], width)]


def _column_tile(row):
    return jnp.broadcast_to(row, (LANES, LANES)).T


def _store_hgrn_out(rows, o, gate, hgn, mix_ref):
    for h in range(HG_HEADS):
        sl = slice(h * HG_DIM, (h + 1) * HG_DIM)
        oh = o[:, sl]
        oh = oh * lax.rsqrt(jnp.mean(oh * oh, axis=-1, keepdims=True) + EPS)
        mix_ref[rows, sl] = (oh * hgn[:, sl] * gate[:, sl]).astype(mix_ref.dtype)


def _hgrn_block(n_pad, tb, u_ref, lb, hgn, tri, mix_ref, st_scr, o_scr):
    fz = u_ref[:, COL_F:COL_F + HG_WIDTH]
    logf = jnp.log(lb + (1.0 - lb) * jax.nn.sigmoid(fz))
    if n_pad:
        logf = jnp.where(lax.broadcasted_iota(jnp.int32, (tb, HG_WIDTH), 0) >= n_pad, logf, 0.0)
    b = _cumsum_rows(tri, logf)
    chunk_ends = jnp.concatenate([b[r:r + 1] for r in range(HG_CHUNK - 1, tb, HG_CHUNK)], axis=0)
    chunked_is_safe = jnp.max(-chunk_ends) < HG_MAX_CHUNK_LOG_DECAY

    @pl.when(chunked_is_safe)
    def _():
        _hgrn_chunked(n_pad, tb, u_ref, lb, hgn, b, mix_ref, st_scr)

    @pl.when(jnp.logical_not(chunked_is_safe))
    def _():
        _hgrn_per_token(n_pad, tb, u_ref, lb, hgn, mix_ref, st_scr, o_scr)


def _hgrn_per_token(n_pad, tb, u_ref, lb, hgn, mix_ref, st_scr, o_scr):
    for h in range(HG_HEADS):
        st_scr[h] = st_scr[h].T

    def sublane_group(i, carry):
        rows = pl.ds(pl.multiple_of(i * SUBLANES, SUBLANES), SUBLANES)
        fz = u_ref[rows, COL_F:COL_F + HG_WIDTH]
        f = lb + (1.0 - lb) * jax.nn.sigmoid(fz)
        kk = (1.0 - lb) * jax.nn.sigmoid(-fz)
        if n_pad:
            valid = i * SUBLANES + lax.broadcasted_iota(jnp.int32, (SUBLANES, HG_WIDTH), 0) >= n_pad
            f = jnp.where(valid, f, 1.0)
            kk = jnp.where(valid, kk, 0.0)
        q = _silu(u_ref[rows, COL_Q:COL_Q + HG_WIDTH])
        v = u_ref[rows, COL_I:COL_I + HG_WIDTH]
        for h in range(HG_HEADS):
            sl = slice(h * HG_DIM, (h + 1) * HG_DIM)
            s = st_scr[h]
            o_rows = []
            for j in range(SUBLANES):
                r = slice(j, j + 1)
                s = s * _column_tile(f[r, sl]) + _column_tile(kk[r, sl]) * v[r, sl]
                o_rows.append(jnp.sum(s * _column_tile(q[r, sl]), axis=0, keepdims=True))
            st_scr[h] = s
            o_scr[rows, sl] = jnp.concatenate(o_rows, axis=0)
        return carry

    lax.fori_loop(0, tb // SUBLANES, sublane_group, 0)
    for h in range(HG_HEADS):
        st_scr[h] = st_scr[h].T
    _store_hgrn_out(slice(0, tb), o_scr[...], _silu(u_ref[:, COL_G:COL_G + HG_WIDTH]), hgn, mix_ref)


def _hgrn_chunked(n_pad, tb, u_ref, lb, hgn, b, mix_ref, st_scr):
    c = HG_CHUNK
    kk = (1.0 - lb) * jax.nn.sigmoid(-u_ref[:, COL_F:COL_F + HG_WIDTH])
    if n_pad:
        kk = jnp.where(lax.broadcasted_iota(jnp.int32, (tb, HG_WIDTH), 0) >= n_pad, kk, 0.0)
    q = _silu(u_ref[:, COL_Q:COL_Q + HG_WIDTH])
    v = u_ref[:, COL_I:COL_I + HG_WIDTH].astype(BF16)
    gate = _silu(u_ref[:, COL_G:COL_G + HG_WIDTH])
    qt = (q * jnp.exp(b)).astype(BF16)
    kt = (kk * jnp.exp(-b)).astype(BF16)
    causal = _tiled_lower_tri(c, HG_HEADS)

    for r0 in range(0, tb, c):
        rows = slice(r0, r0 + c)
        b_c = b[rows]
        b_last = b_c[c - 1:c, :]
        kh_c = (kk[rows] * jnp.exp(b_last - b_c)).astype(BF16)
        decay = jnp.exp(b_last)
        qt_c = qt[rows]
        kd = _block_diag(_head_slices(kt[rows], HG_DIM))
        vd = _block_diag(_head_slices(v[rows], HG_DIM))
        scores = jnp.where(causal, _dot_nt(qt_c, kd), 0.0).astype(BF16)
        o = _dot(scores, vd)
        o_prev = []
        for p in range(HG_PAIRS):
            lanes = slice(p * 2 * HG_DIM, (p + 1) * 2 * HG_DIM)
            heads = (2 * p, 2 * p + 1)
            st = [st_scr[h] for h in heads]
            o_prev.append(_dot_nt(qt_c[:, lanes], _block_diag([s.astype(BF16) for s in st])))
            upd = _dot_tn(vd[p * 2 * c:(p + 1) * 2 * c, lanes], jnp.concatenate([kh_c[:, lanes]] * 2, axis=0))
            for i, h in enumerate(heads):
                blk = slice(i * HG_DIM, (i + 1) * HG_DIM)
                st_scr[h] = st[i] * decay[:, h * HG_DIM:(h + 1) * HG_DIM] + upd[blk, blk]
        _store_hgrn_out(rows, o + jnp.concatenate(o_prev, axis=1), gate[rows], hgn, mix_ref)


def _pair_columns(a, r0, r1, first_half):
    shape = (a.shape[0], LANES)
    return jnp.where(first_half, jnp.broadcast_to(a[:, r0:r0 + 1], shape), jnp.broadcast_to(a[:, r1:r1 + 1], shape))


def _group_columns(a, g, first_half):
    r = g * HEADS_PER_GROUP
    return jnp.concatenate([_pair_columns(a, r + i, r + i + 1, first_half)
                            for i in range(0, HEADS_PER_GROUP, 2)], axis=1)


def _ssd_block(n_pad, tb, u_ref, col0, cw, cb, dtb, a_neg, dsk, ssn, tri, mix_ref, hg_scr, xpad):
    c = SSM_CHUNK
    conv = cb
    for j in range(CONV_WIDTH):
        off = PAD_ROWS - (CONV_WIDTH - 1) + j
        conv = conv + cw[j:j + 1, :] * xpad[off:off + tb, :]
    act = _silu(conv)
    xs = act[:, 0:SSM_WIDTH]
    bm = act[:, SSM_WIDTH:SSM_WIDTH + SSM_GROUPS * SSM_STATE].astype(BF16)
    cm = act[:, SSM_WIDTH + SSM_GROUPS * SSM_STATE:].astype(BF16)
    z_gate = _silu(u_ref[:, col0:col0 + SSM_WIDTH])

    col_dt = col0 + COL_DT - COL_Z
    dt = jax.nn.softplus(u_ref[:, col_dt:col_dt + LANES] + dtb)
    if n_pad:
        dt = jnp.where(lax.broadcasted_iota(jnp.int32, (tb, LANES), 0) >= n_pad, dt, 0.0)
    cum = _cumsum_rows(tri, dt * a_neg)
    causal = _lower_tri(c)
    first_half = lax.broadcasted_iota(jnp.int32, (1, LANES), 1) < SSM_HEAD_DIM
    zeros = jnp.zeros((c, LANES), BF16)

    for r0 in range(0, tb, c):
        rows = slice(r0, r0 + c)
        cum_c = cum[rows]
        dt_c = dt[rows]
        last = cum_c[c - 1:c, :]
        cum_t = cum_c.T
        dt_t = dt_c.T
        e_cum = jnp.exp(cum_c)
        w_in = dt_c * jnp.exp(last - cum_c)
        e_last = jnp.exp(last)
        for g in range(SSM_GROUPS):
            yield
            glanes = slice(g * GROUP_WIDTH, (g + 1) * GROUP_WIDTH)
            bg = bm[rows, g * SSM_STATE:(g + 1) * SSM_STATE]
            cg = cm[rows, g * SSM_STATE:(g + 1) * SSM_STATE]
            xg = xs[rows, glanes]
            cbt = _dot_nt(cg, bg)
            m_heads = []
            for rr in range(HEADS_PER_GROUP):
                r = g * HEADS_PER_GROUP + rr
                seg = jnp.exp(jnp.where(causal, cum_c[:, r:r + 1] - cum_t[r:r + 1, :], -jnp.inf))
                m_heads.append((cbt * seg * dt_t[r:r + 1, :]).astype(BF16))
            xb = xg.astype(BF16)
            xd_rows = []
            for rr in range(HEADS_PER_GROUP):
                tile = xb[:, (rr // 2) * LANES:(rr // 2 + 1) * LANES]
                tile = jnp.where(first_half if rr % 2 == 0 else ~first_half, tile, jnp.zeros_like(tile))
                xd_rows.append(jnp.concatenate([tile, zeros] if rr < 2 else [zeros, tile], axis=1))
            xd = jnp.concatenate(xd_rows, axis=0)
            hg = hg_scr[g]
            y = (_dot(jnp.concatenate(m_heads, axis=1), xd)
                 + _dot(cg, hg.astype(BF16)) * _group_columns(e_cum, g, first_half))
            xw = (xg * _group_columns(w_in, g, first_half)).astype(BF16)
            hg_scr[g] = hg * _group_columns(e_last, g, first_half) + _dot_tn(bg, xw)
            yz = (y + dsk[:, glanes] * xg) * z_gate[rows, glanes]
            yz = yz * lax.rsqrt(jnp.mean(yz * yz, axis=-1, keepdims=True) + EPS) * ssn[:, glanes]
            mix_ref[rows, HG_WIDTH + g * GROUP_WIDTH:HG_WIDTH + (g + 1) * GROUP_WIDTH] = yz.astype(mix_ref.dtype)


def _interleave(main, side, side_steps):
    for n in side_steps:
        if next(main, StopIteration) is StopIteration:
            break
        for _ in range(n):
            next(side, None)
    for _ in main:
        pass
    for _ in side:
        pass


def _mixer_block(*args):
    for _ in _mixer_steps(*args):
        pass


def _mixer_steps(n_pad, tb, uh_ref, us_ref, us_col0, param_refs, mix_ref, st_scr, hg_scr, xpad, o_scr):
    lbl_ref, hgn_ref, cw_ref, cb_ref, dtb_ref, alog_ref, dsk_ref, ssn_ref, tri_hg_ref, tri_ssm_ref = param_refs
    _hgrn_block(n_pad, tb, uh_ref, _forget_lower_bound(lbl_ref[...]), hgn_ref[...], tri_hg_ref[...],
                mix_ref, st_scr, o_scr)
    yield
    col_xbc = us_col0 + COL_XBC - COL_Z
    xbc = us_ref[:, col_xbc:col_xbc + CONV_DIM]
    if n_pad:
        xbc = jnp.where(lax.broadcasted_iota(jnp.int32, (tb, CONV_DIM), 0) >= n_pad, xbc, 0.0)
    xpad[PAD_ROWS:PAD_ROWS + tb, :] = xbc
    yield from _ssd_block(n_pad, tb, us_ref, us_col0, cw_ref[...], cb_ref[...], dtb_ref[...],
                          -jnp.exp(alog_ref[...]), dsk_ref[...], ssn_ref[...], tri_ssm_ref[...],
                          mix_ref, hg_scr, xpad)
    xpad[0:PAD_ROWS, :] = xpad[tb:tb + PAD_ROWS, :]


def _load_state(s0_ref, h0_ref, c0_ref, st_scr, hg_scr, xpad):
    for h in range(HG_HEADS):
        st_scr[h] = s0_ref[0, h].T
    for r in range(0, SSM_HEADS, 2):
        g, lane0 = r // HEADS_PER_GROUP, (r % HEADS_PER_GROUP) * SSM_HEAD_DIM
        hg_scr[g, :, lane0:lane0 + LANES] = jnp.concatenate([h0_ref[0, r], h0_ref[0, r + 1]], axis=0).T
    xpad[0:PAD_ROWS, :] = c0_ref[0]


def _store_state(s_out_ref, h_out_ref, c_out_ref, st_scr, hg_scr, xpad):
    for h in range(HG_HEADS):
        s_out_ref[0, h] = st_scr[h].T
    for r in range(0, SSM_HEADS, 2):
        g, lane0 = r // HEADS_PER_GROUP, (r % HEADS_PER_GROUP) * SSM_HEAD_DIM
        pair = hg_scr[g, :, lane0:lane0 + LANES].T
        h_out_ref[0, r] = pair[0:SSM_HEAD_DIM]
        h_out_ref[0, r + 1] = pair[SSM_HEAD_DIM:]
    c_out_ref[0] = xpad[0:PAD_ROWS, :]


N_MIXER_PARAMS = 10


def _scan_params(params, tb):
    return tuple(params) + (_chunked_tri(tb, HG_CHUNK), _chunked_tri(tb, SSM_CHUNK))


PROJ_TILE = 512
PROJ_STEPS_AFTER_SCAN_STEP = (3, 2, 2, 1, 1)
HG_SHAPE = (SSM_GROUPS, SSM_STATE, GROUP_WIDTH)
S_SHAPE = (HG_HEADS, HG_DIM, HG_DIM)
H_SHAPE = (SSM_HEADS, SSM_HEAD_DIM, SSM_STATE)
C_SHAPE = (PAD_ROWS, CONV_DIM)


def _scan_kernel(n_pad, tb, project, *refs):
    if project:
        h1_ref, nm_ref, win_ref = refs[:3]
        refs = refs[3:]
    else:
        u_ref = refs[0]
        refs = refs[1:]
    param_refs = refs[:N_MIXER_PARAMS]
    s0_ref, h0_ref, c0_ref, mix_ref, s_out_ref, h_out_ref, c_out_ref = refs[N_MIXER_PARAMS:N_MIXER_PARAMS + 7]
    scratch = refs[N_MIXER_PARAMS + 7:]
    if project:
        uh_scr, us_scr, st_scr, hg_scr, xpad, o_scr = scratch
    else:
        st_scr, hg_scr, xpad, o_scr = scratch
    t = pl.program_id(1)

    @pl.when(t == 0)
    def _():
        _load_state(s0_ref, h0_ref, c0_ref, st_scr, hg_scr, xpad)

    if project:
        hn = _rms(h1_ref[...], nm_ref[...]).astype(BF16)
        uh_scr[...] = _dot(hn, win_ref[:, 0:COL_Z])
        us_scr[...] = _dot(hn, win_ref[:, COL_Z:])
        _mixer_block(n_pad, tb, uh_scr, us_scr, 0, param_refs, mix_ref, st_scr, hg_scr, xpad, o_scr)
    else:
        _mixer_block(n_pad, tb, u_ref, u_ref, COL_Z, param_refs, mix_ref, st_scr, hg_scr, xpad, o_scr)

    @pl.when(t == pl.num_programs(1) - 1)
    def _():
        _store_state(s_out_ref, h_out_ref, c_out_ref, st_scr, hg_scr, xpad)


def _scan_pipelined_kernel(tb, nt, h1_ref, nm_ref, win_ref, *refs):
    param_refs = refs[:N_MIXER_PARAMS]
    s0_ref, h0_ref, c0_ref, mix_ref, s_out_ref, h_out_ref, c_out_ref = refs[N_MIXER_PARAMS:N_MIXER_PARAMS + 7]
    uh_even, us_even, uh_odd, us_odd, st_scr, hg_scr, xpad, o_scr = refs[N_MIXER_PARAMS + 7:]
    s = pl.program_id(0)
    scanned = jnp.maximum(s - 1, 0)

    @pl.when(s == 0)
    def _():
        uh_odd[...] = jnp.zeros(uh_odd.shape, F32)
        us_odd[...] = jnp.zeros(us_odd.shape, F32)

    @pl.when(scanned % nt == 0)
    def _():
        _load_state(s0_ref, h0_ref, c0_ref, st_scr, hg_scr, xpad)

    def project_steps(uh_w, us_w):
        hn = _rms(h1_ref[...], nm_ref[...]).astype(BF16)
        for c0 in range(0, D_IN_PAD, PROJ_TILE):
            c1 = min(c0 + PROJ_TILE, D_IN_PAD)
            yield
            if c0 < COL_Z:
                uh_w[:, c0:c1] = _dot(hn, win_ref[:, c0:c1])
            else:
                us_w[:, c0 - COL_Z:c1 - COL_Z] = _dot(hn, win_ref[:, c0:c1])

    def body(uh_w, us_w, uh_r, us_r):
        _interleave(_mixer_steps(0, tb, uh_r, us_r, 0, param_refs, mix_ref, st_scr, hg_scr, xpad, o_scr),
                    project_steps(uh_w, us_w), PROJ_STEPS_AFTER_SCAN_STEP)

    pl.when(s % 2 == 0)(lambda: body(uh_even, us_even, uh_odd, us_odd))
    pl.when(s % 2 == 1)(lambda: body(uh_odd, us_odd, uh_even, us_even))

    @pl.when((s >= 1) & (scanned % nt == nt - 1))
    def _():
        _store_state(s_out_ref, h_out_ref, c_out_ref, st_scr, hg_scr, xpad)


def _scan_pipelined(h1, nb, nt, tb, proj, params, s0, h0, c0):
    n_blocks = nb * nt
    last = n_blocks - 1
    lead = tuple(proj) + _scan_params(params, tb)
    shared = lambda shape: pl.BlockSpec((1,) + shape[1:], lambda s: (0,) * len(shape))
    per_seq = lambda shape: pl.BlockSpec((1,) + shape, lambda s: (jnp.maximum(s - 1, 0) // nt,) + (0,) * len(shape))
    u_bufs = [pltpu.VMEM((tb, COL_Z), F32), pltpu.VMEM((tb, D_IN_PAD - COL_Z), F32)]
    return pl.pallas_call(
        functools.partial(_scan_pipelined_kernel, tb, nt),
        grid=(n_blocks + 1,),
        in_specs=[pl.BlockSpec((tb, D_MODEL), lambda s: (jnp.minimum(s, last), 0))]
        + [_resident(p.shape) for p in lead]
        + [shared(s0.shape), shared(h0.shape), shared(c0.shape)],
        out_specs=[pl.BlockSpec((tb, D_MODEL), lambda s: (jnp.maximum(s - 1, 0), 0)),
                   per_seq(S_SHAPE), per_seq(H_SHAPE), per_seq(C_SHAPE)],
        out_shape=[jax.ShapeDtypeStruct((n_blocks * tb, D_MODEL), BF16),
                   jax.ShapeDtypeStruct((nb,) + S_SHAPE, F32),
                   jax.ShapeDtypeStruct((nb,) + H_SHAPE, F32),
                   jax.ShapeDtypeStruct((nb,) + C_SHAPE, F32)],
        scratch_shapes=u_bufs + u_bufs + [pltpu.VMEM(S_SHAPE, F32), pltpu.VMEM(HG_SHAPE, F32),
                                          pltpu.VMEM((PAD_ROWS + tb, CONV_DIM), F32),
                                          pltpu.VMEM((tb, HG_WIDTH), F32)],
        compiler_params=_compiler_params(1),
        name="proj_mixer",
    )(h1, *lead, s0, h0, c0)


def _scan(src, src_block0, nb, nt, tb, n_pad, proj, params, s0, h0, c0):
    lead = (tuple(proj) if proj else ()) + _scan_params(params, tb)
    shared = lambda shape: pl.BlockSpec((1,) + shape[1:], lambda b, t: (0,) * len(shape))
    per_seq = lambda shape: pl.BlockSpec((1,) + shape, lambda b, t: (b,) + (0,) * len(shape))
    scratch = [pltpu.VMEM(S_SHAPE, F32), pltpu.VMEM(HG_SHAPE, F32), pltpu.VMEM((PAD_ROWS + tb, CONV_DIM), F32),
               pltpu.VMEM((tb, HG_WIDTH), F32)]
    if proj:
        scratch = [pltpu.VMEM((tb, COL_Z), F32), pltpu.VMEM((tb, D_IN_PAD - COL_Z), F32)] + scratch
    return pl.pallas_call(
        functools.partial(_scan_kernel, n_pad, tb, bool(proj)),
        grid=(nb, nt),
        in_specs=[pl.BlockSpec((tb, src.shape[1]), lambda b, t: (src_block0 + b * nt + t, 0))]
        + [_resident(p.shape) for p in lead]
        + [shared(s0.shape), shared(h0.shape), shared(c0.shape)],
        out_specs=[pl.BlockSpec((tb, D_MODEL), lambda b, t: (b * nt + t, 0)),
                   per_seq(S_SHAPE), per_seq(H_SHAPE), per_seq(C_SHAPE)],
        out_shape=[jax.ShapeDtypeStruct((nb * nt * tb, D_MODEL), BF16),
                   jax.ShapeDtypeStruct((nb,) + S_SHAPE, F32),
                   jax.ShapeDtypeStruct((nb,) + H_SHAPE, F32),
                   jax.ShapeDtypeStruct((nb,) + C_SHAPE, F32)],
        scratch_shapes=scratch,
        compiler_params=_compiler_params(2),
        name="proj_mixer" if proj else "mixer",
    )(src, *lead, s0, h0, c0)


def _sample_kernel(nbs, u_ref, lbl_ref, hgn_ref, cw_ref, cb_ref, dtb_ref, alog_ref, dsk_ref, ssn_ref,
                   sh_ref, ss_ref, sc_ref, mix_ref, sh_out_ref, ss_out_ref, sc_out_ref, obuf, ybuf):
    lb = _forget_lower_bound(lbl_ref[...])
    fz = u_ref[:, COL_F:COL_F + HG_WIDTH]
    f = lb + (1.0 - lb) * jax.nn.sigmoid(fz)
    kk = (1.0 - lb) * jax.nn.sigmoid(-fz)
    q = _silu(u_ref[:, COL_Q:COL_Q + HG_WIDTH])
    v = u_ref[:, COL_I:COL_I + HG_WIDTH]

    xbc = u_ref[:, COL_XBC:COL_XBC + CONV_DIM]
    cw = cw_ref[...]
    conv = cb_ref[...] + cw[CONV_WIDTH - 1:CONV_WIDTH, :] * xbc
    for j in range(CONV_WIDTH - 1):
        conv = conv + cw[j:j + 1, :] * sc_ref[j]
    for j in range(CONV_WIDTH - 2):
        sc_out_ref[j] = sc_ref[j + 1]
    sc_out_ref[CONV_WIDTH - 2] = xbc
    act = _silu(conv)
    xs = act[:, 0:SSM_WIDTH]
    bm = act[:, SSM_WIDTH:SSM_WIDTH + SSM_GROUPS * SSM_STATE]
    cm = act[:, SSM_WIDTH + SSM_GROUPS * SSM_STATE:]
    dt = jax.nn.softplus(u_ref[:, COL_DT:COL_DT + LANES] + dtb_ref[...])
    d_a = jnp.exp(dt * (-jnp.exp(alog_ref[...])))

    top_half = lax.broadcasted_iota(jnp.int32, (LANES, LANES), 0) < SSM_HEAD_DIM
    for j in range(nbs):
        row = slice(j, j + 1)
        for h in range(HG_HEADS):
            sl = slice(h * HG_DIM, (h + 1) * HG_DIM)
            s_new = sh_ref[j, h] * _column_tile(f[row, sl]) + _column_tile(kk[row, sl]) * v[row, sl]
            sh_out_ref[j, h] = s_new
            obuf[row, sl] = jnp.sum(s_new * _column_tile(q[row, sl]), axis=0, keepdims=True)
        for rp in range(SSM_HEADS // 2):
            r0, r1 = 2 * rp, 2 * rp + 1
            g = r0 // HEADS_PER_GROUP
            sl = slice(rp * LANES, (rp + 1) * LANES)
            gsl = slice(g * SSM_STATE, (g + 1) * SSM_STATE)
            h2 = jnp.concatenate([ss_ref[j, r0], ss_ref[j, r1]], axis=0)
            da2 = jnp.where(top_half, d_a[row, r0:r0 + 1], d_a[row, r1:r1 + 1])
            dt2 = jnp.where(top_half, dt[row, r0:r0 + 1], dt[row, r1:r1 + 1])
            h_new = h2 * da2 + (dt2 * _column_tile(xs[row, sl])) * bm[row, gsl]
            ss_out_ref[j, r0] = h_new[0:SSM_HEAD_DIM]
            ss_out_ref[j, r1] = h_new[SSM_HEAD_DIM:]
            ybuf[row, sl] = jnp.sum((h_new * cm[row, gsl]).T, axis=0, keepdims=True)

    o = obuf[...]
    gate = _silu(u_ref[:, COL_G:COL_G + HG_WIDTH])
    hgn = hgn_ref[...]
    for h in range(HG_HEADS):
        sl = slice(h * HG_DIM, (h + 1) * HG_DIM)
        oh = o[:, sl]
        oh = oh * lax.rsqrt(jnp.mean(oh * oh, axis=-1, keepdims=True) + EPS)
        mix_ref[:, sl] = oh * hgn[:, sl] * gate[:, sl]
    yz = (ybuf[...] + dsk_ref[...] * xs) * _silu(u_ref[:, COL_Z:COL_Z + SSM_WIDTH])
    ssn = ssn_ref[...]
    for g in range(SSM_GROUPS):
        sl = slice(g * GROUP_WIDTH, (g + 1) * GROUP_WIDTH)
        seg = yz[:, sl]
        seg = seg * lax.rsqrt(jnp.mean(seg * seg, axis=-1, keepdims=True) + EPS) * ssn[:, sl]
        mix_ref[:, HG_WIDTH + g * GROUP_WIDTH:HG_WIDTH + (g + 1) * GROUP_WIDTH] = seg


def _sample_mixer(u, u_block0, n, nbs, params, sh, ss, sc_t):
    small = [_resident(p.shape) for p in params]
    return pl.pallas_call(
        functools.partial(_sample_kernel, nbs),
        grid=(n // nbs,),
        in_specs=[pl.BlockSpec((nbs, D_IN_PAD), lambda i: (u_block0 + i, 0))] + small
        + [pl.BlockSpec((nbs,) + sh.shape[1:], lambda i: (i, 0, 0, 0)),
           pl.BlockSpec((nbs,) + ss.shape[1:], lambda i: (i, 0, 0, 0)),
           pl.BlockSpec((CONV_WIDTH - 1, nbs, CONV_DIM), lambda i: (0, i, 0))],
        out_specs=[pl.BlockSpec((nbs, D_MODEL), lambda i: (i, 0)),
                   pl.BlockSpec((nbs,) + sh.shape[1:], lambda i: (i, 0, 0, 0)),
                   pl.BlockSpec((nbs,) + ss.shape[1:], lambda i: (i, 0, 0, 0)),
                   pl.BlockSpec((CONV_WIDTH - 1, nbs, CONV_DIM), lambda i: (0, i, 0))],
        out_shape=[jax.ShapeDtypeStruct((n, D_MODEL), F32),
                   jax.ShapeDtypeStruct(sh.shape, F32),
                   jax.ShapeDtypeStruct(ss.shape, F32),
                   jax.ShapeDtypeStruct(sc_t.shape, F32)],
        scratch_shapes=[pltpu.VMEM((nbs, HG_WIDTH), F32), pltpu.VMEM((nbs, SSM_WIDTH), F32)],
        compiler_params=_compiler_params(1),
        name="sample_mixer",
    )(u, *params, sh, ss, sc_t)


TM_DENSE = 512
TM_SCAN = 256
TM_SMALL = 128
SAMPLES_PER_STEP = 8


def _pad_lanes(row, value=0.0):
    return jnp.pad(row, ((0, 0), (0, LANES - row.shape[1])), constant_values=value)


def kernel(x_prompt, x_sample, state_hgrn, state_ssm, state_conv, meta_tokens, lb_logits, norm_ffn1, w_ffn1_gate, w_ffn1_up, w_ffn1_down, norm_mix, w_in, hg_norm, conv_w, conv_b, dt_bias, a_log, d_skip, ssm_norm, w_out, norm_ffn2, w_ffn2_gate, w_ffn2_up, w_ffn2_down, norm_final):
    bp, seq_p, _ = x_prompt.shape
    n_s = x_sample.shape[0]
    assert x_sample.shape[1] == 1 and n_s == TM_SMALL and seq_p % TM_SCAN == 0 and (bp * seq_p) % TM_DENSE == 0
    layer = 0

    n1, nm, n2 = norm_ffn1[layer][None], norm_mix[layer][None], norm_ffn2[layer][None]
    nf = norm_final[None]
    win =jnp.pad(w_in[layer], ((0, 0), (0, D_IN_PAD - w_in.shape[2]))).astype(BF16)
    wo = w_out[layer].astype(BF16)
    mixer_params = (lb_logits, hg_norm[layer][None], conv_w[layer], conv_b[layer][None],
                    _pad_lanes(dt_bias[layer][None]), _pad_lanes(a_log[layer][None]),
                    jnp.repeat(d_skip[layer], SSM_HEAD_DIM)[None], ssm_norm[layer][None])

    n_pad = TM_SMALL - N_META
    x_small = jnp.concatenate([jnp.zeros((n_pad, D_MODEL), F32), meta_tokens, x_sample[:, 0]], axis=0)
    h1_small, wg1, wu1, wd1 = _ffn1_small(x_small, n1, w_ffn1_gate[layer], w_ffn1_up[layer], w_ffn1_down[layer])
    u_small = _proj_small(h1_small, nm, win)

    zeros_s = jnp.zeros((1,) + S_SHAPE, F32)
    zeros_h = jnp.zeros((1,) + H_SHAPE, F32)
    zeros_c = jnp.zeros((1,) + C_SHAPE, F32)
    _, s_meta, h_meta, c_meta = _scan(u_small, 0, 1, 1, TM_SMALL, n_pad, None, mixer_params,
                                      zeros_s, zeros_h, zeros_c)

    sc_t = jnp.swapaxes(state_conv[layer], 0, 1)
    mix_s, hgrn_s, ssm_s, conv_s_t = _sample_mixer(
        u_small, TM_SMALL // SAMPLES_PER_STEP, n_s, SAMPLES_PER_STEP, mixer_params,
        state_hgrn[layer], state_ssm[layer], sc_t)
    y_s, wg2, wu2, wd2 = _ffn2_small(h1_small, 1, mix_s, wo, n2, w_ffn2_gate[layer], w_ffn2_up[layer],
                                     w_ffn2_down[layer], nf)

    xp = x_prompt.reshape(bp * seq_p, D_MODEL)
    h1_p = _ffn1(xp, n1, wg1, wu1, wd1, TM_DENSE)
    mix_p, hgrn_p, ssm_p, conv_p = _scan_pipelined(h1_p, bp, seq_p // TM_SCAN, TM_SCAN, (nm, win), mixer_params,
                                                   s_meta, h_meta, c_meta)
    y_p = _ffn_out(h1_p, 0, mix_p, wo, n2, wg2, wu2, wd2, nf, TM_DENSE)

    keep = slice(PAD_ROWS - (CONV_WIDTH - 1), PAD_ROWS)
    return (y_p.reshape(bp, seq_p, D_MODEL),
            y_s.reshape(n_s, 1, D_MODEL),
            hgrn_p[None], ssm_p[None], conv_p[:, keep][None],
            hgrn_s[None], ssm_s[None], jnp.swapaxes(conv_s_t, 0, 1)[None])
```

```python
import functools

import jax
import jax.numpy as jnp
from jax import lax
from jax.experimental import pallas as pl
from jax.experimental.pallas import tpu as pltpu

F32 = jnp.float32
BF16 = jnp.bfloat16

D_MODEL = 1024
D_FF = 2816
N_META = 16
HG_WIDTH = 512
HG_HEADS = 4
HG_DIM = 128
SSM_WIDTH = 512
SSM_HEADS = 8
SSM_HEAD_DIM = 64
SSM_GROUPS = 2
SSM_STATE = 128
CONV_WIDTH = 4
CONV_DIM = SSM_WIDTH + 2 * SSM_GROUPS * SSM_STATE
EPS = 1e-6

LANES = 128
SUBLANES = 8
VMEM_LIMIT_BYTES = 56 * 1024 * 1024

COL_Q = 0
COL_F = HG_WIDTH
COL_I = 2 * HG_WIDTH
COL_G = 3 * HG_WIDTH
COL_Z = 4 * HG_WIDTH
COL_XBC = COL_Z + SSM_WIDTH
COL_DT = COL_XBC + CONV_DIM
D_IN_PAD = COL_DT + LANES

FF_TILE = 256
HG_CHUNK = 64
HG_MAX_CHUNK_LOG_DECAY = 80.0
SSM_CHUNK = 128
PAD_ROWS = SUBLANES
HG_PAIRS = HG_HEADS // 2
HEADS_PER_GROUP = SSM_HEADS // SSM_GROUPS
GROUP_WIDTH = SSM_WIDTH // SSM_GROUPS
assert 2 * SSM_HEAD_DIM == LANES and 2 * HG_DIM == FF_TILE


def _dot(a, b):
    return jnp.dot(a, b, preferred_element_type=F32)


def _dot_nt(a, b):
    return lax.dot_general(a, b, (((1,), (1,)), ((), ())), preferred_element_type=F32)


def _dot_tn(a, b):
    return lax.dot_general(a, b, (((0,), (0,)), ((), ())), preferred_element_type=F32)


def _rms(x, w):
    return x * lax.rsqrt(jnp.mean(x * x, axis=-1, keepdims=True) + EPS) * w


def _silu(x):
    return x * jax.nn.sigmoid(x)


def _swiglu(xn, wg_ref, wu_ref, wd_ref):
    acc = jnp.zeros((xn.shape[0], D_MODEL), F32)
    for j in range(D_FF // FF_TILE):
        cols = slice(j * FF_TILE, (j + 1) * FF_TILE)
        g = _dot(xn, wg_ref[:, cols])
        u = _dot(xn, wu_ref[:, cols])
        acc = acc + _dot((_silu(g) * u).astype(BF16), wd_ref[cols, :])
    return acc


def _cumsum_rows(tri, a):
    a1 = a.astype(BF16)
    r1 = a - a1.astype(F32)
    a2 = r1.astype(BF16)
    a3 = (r1 - a2.astype(F32)).astype(BF16)
    return _dot(tri, a1) + _dot(tri, a2) + _dot(tri, a3)


def _lower_tri(n):
    row = lax.broadcasted_iota(jnp.int32, (n, n), 0)
    col = lax.broadcasted_iota(jnp.int32, (n, n), 1)
    return row >= col


def _chunked_tri(n, chunk):
    assert chunk & (chunk - 1) == 0
    row = lax.broadcasted_iota(jnp.int32, (n, n), 0)
    col = lax.broadcasted_iota(jnp.int32, (n, n), 1)
    same_chunk = (row ^ col) < chunk
    return ((row >= col) & same_chunk).astype(BF16)


def _tiled_lower_tri(chunk, reps):
    assert chunk & (chunk - 1) == 0
    row = lax.broadcasted_iota(jnp.int32, (chunk, reps * chunk), 0)
    col = lax.broadcasted_iota(jnp.int32, (chunk, reps * chunk), 1)
    return row >= (col & (chunk - 1))


def _block_diag(blocks):
    n = len(blocks)
    r, c = blocks[0].shape
    rows = []
    for i, blk in enumerate(blocks):
        parts = []
        if i:
            parts.append(jnp.zeros((r, c * i), blk.dtype))
        parts.append(blk)
        if i < n - 1:
            parts.append(jnp.zeros((r, c * (n - 1 - i)), blk.dtype))
        rows.append(jnp.concatenate(parts, axis=1))
    return jnp.concatenate(rows, axis=0)


def _forget_lower_bound(lbl):
    l0, l1 = lbl[0:1], lbl[1:2]
    m = jnp.maximum(l0, l1)
    e0, e1 = jnp.exp(l0 - m), jnp.exp(l1 - m)
    return e0 / (e0 + e1)


def _resident(shape):
    nd = len(shape)
    return pl.BlockSpec(shape, lambda *_: (0,) * nd, pipeline_mode=pl.Buffered(1))


def _compiler_params(n_grid_axes, flags=None):
    return pltpu.CompilerParams(dimension_semantics=("arbitrary",) * n_grid_axes,
                                vmem_limit_bytes=VMEM_LIMIT_BYTES, flags=flags)


def _ffn1_kernel(x_ref, n1_ref, wg_ref, wu_ref, wd_ref, h1_ref):
    x = x_ref[...]
    xn = _rms(x, n1_ref[...]).astype(BF16)
    h1_ref[...] = x + 0.5 * _swiglu(xn, wg_ref, wu_ref, wd_ref)


def _ffn1(x, n1, wg, wu, wd, tm):
    n = x.shape[0]
    return pl.pallas_call(
        _ffn1_kernel,
        grid=(n // tm,),
        in_specs=[pl.BlockSpec((tm, D_MODEL), lambda i: (i, 0)),
                  _resident(n1.shape), _resident(wg.shape), _resident(wu.shape), _resident(wd.shape)],
        out_specs=pl.BlockSpec((tm, D_MODEL), lambda i: (i, 0)),
        out_shape=jax.ShapeDtypeStruct((n, D_MODEL), F32),
        compiler_params=_compiler_params(1),
        name="ffn1",
    )(x, n1, wg, wu, wd)


def _ffn_tile_step(xn, wg_ref, wu_ref, wd_ref, wg_out, wu_out, wd_out, acc_scr):
    wg, wu, wd = wg_ref[...].astype(BF16), wu_ref[...].astype(BF16), wd_ref[...].astype(BF16)
    wg_out[...] = wg
    wu_out[...] = wu
    wd_out[...] = wd
    acc_scr[...] += _dot((_silu(_dot(xn, wg)) * _dot(xn, wu)).astype(BF16), wd)


def _ffn1_small_kernel(x_ref, n1_ref, wg_ref, wu_ref, wd_ref, h1_ref, wg_out, wu_out, wd_out, xn_scr, acc_scr):
    j = pl.program_id(0)

    @pl.when(j == 0)
    def _():
        xn_scr[...] = _rms(x_ref[...], n1_ref[...]).astype(BF16)
        acc_scr[...] = jnp.zeros(acc_scr.shape, F32)

    _ffn_tile_step(xn_scr[...], wg_ref, wu_ref, wd_ref, wg_out, wu_out, wd_out, acc_scr)

    @pl.when(j == pl.num_programs(0) - 1)
    def _():
        h1_ref[...] = x_ref[...] + 0.5 * acc_scr[...]


def _ffn2_small_kernel(h1_ref, mix_ref, wo_ref, n2_ref, wg_ref, wu_ref, wd_ref, nf_ref,
                       y_ref, wg_out, wu_out, wd_out, h2_scr, hn_scr, acc_scr):
    j = pl.program_id(0)

    @pl.when(j == 0)
    def _():
        h2 = h1_ref[...] + _dot(mix_ref[...].astype(BF16), wo_ref[...])
        h2_scr[...] = h2
        hn_scr[...] = _rms(h2, n2_ref[...]).astype(BF16)
        acc_scr[...] = jnp.zeros(acc_scr.shape, F32)

    _ffn_tile_step(hn_scr[...], wg_ref, wu_ref, wd_ref, wg_out, wu_out, wd_out, acc_scr)

    @pl.when(j == pl.num_programs(0) - 1)
    def _():
        y_ref[...] = _rms(h2_scr[...] + 0.5 * acc_scr[...], nf_ref[...])


def _weight_tile_specs():
    cols = pl.BlockSpec((D_MODEL, FF_TILE), lambda j: (0, j))
    rows = pl.BlockSpec((FF_TILE, D_MODEL), lambda j: (j, 0))
    shapes = [jax.ShapeDtypeStruct((D_MODEL, D_FF), BF16), jax.ShapeDtypeStruct((D_MODEL, D_FF), BF16),
              jax.ShapeDtypeStruct((D_FF, D_MODEL), BF16)]
    return [cols, cols, rows], shapes


def _ffn1_small(x, n1, wg, wu, wd):
    n = x.shape[0]
    wspecs, wshapes = _weight_tile_specs()
    return pl.pallas_call(
        _ffn1_small_kernel,
        grid=(D_FF // FF_TILE,),
        in_specs=[_resident(x.shape), _resident(n1.shape)] + wspecs,
        out_specs=[pl.BlockSpec((n, D_MODEL), lambda j: (0, 0))] + wspecs,
        out_shape=[jax.ShapeDtypeStruct((n, D_MODEL), F32)] + wshapes,
        scratch_shapes=[pltpu.VMEM((n, D_MODEL), BF16), pltpu.VMEM((n, D_MODEL), F32)],
        compiler_params=_compiler_params(1),
        name="ffn1_small",
    )(x, n1, wg, wu, wd)


def _ffn2_small(h1, h1_block, mix, wo, n2, wg, wu, wd, nf):
    n = mix.shape[0]
    wspecs, wshapes = _weight_tile_specs()
    return pl.pallas_call(
        _ffn2_small_kernel,
        grid=(D_FF // FF_TILE,),
        in_specs=[pl.BlockSpec((n, D_MODEL), lambda j: (h1_block, 0), pipeline_mode=pl.Buffered(1)),
                  _resident(mix.shape), _resident(wo.shape), _resident(n2.shape)] + wspecs
        + [_resident(nf.shape)],
        out_specs=[pl.BlockSpec((n, D_MODEL), lambda j: (0, 0))] + wspecs,
        out_shape=[jax.ShapeDtypeStruct((n, D_MODEL), F32)] + wshapes,
        scratch_shapes=[pltpu.VMEM((n, D_MODEL), F32), pltpu.VMEM((n, D_MODEL), BF16),
                        pltpu.VMEM((n, D_MODEL), F32)],
        compiler_params=_compiler_params(1),
        name="ffn2_small",
    )(h1, mix, wo, n2, wg, wu, wd, nf)


def _proj_kernel(h1_ref, nm_ref, win_ref, u_ref):
    u_ref[...] = _dot(_rms(h1_ref[...], nm_ref[...]).astype(BF16), win_ref[...])


def _proj_small(h1, nm, win):
    n = h1.shape[0]
    return pl.pallas_call(
        _proj_kernel,
        grid=(1,),
        in_specs=[_resident(h1.shape), _resident(nm.shape), _resident(win.shape)],
        out_specs=pl.BlockSpec((n, D_IN_PAD), lambda i: (0, 0)),
        out_shape=jax.ShapeDtypeStruct((n, D_IN_PAD), F32),
        compiler_params=_compiler_params(1),
        name="proj_small",
    )(h1, nm, win)


def _ffn_out_kernel(h1_ref, mix_ref, wo_ref, n2_ref, wg_ref, wu_ref, wd_ref, nf_ref, y_ref):
    h2 = h1_ref[...] + _dot(mix_ref[...].astype(BF16), wo_ref[...])
    hn = _rms(h2, n2_ref[...]).astype(BF16)
    h3 = h2 + 0.5 * _swiglu(hn, wg_ref, wu_ref, wd_ref)
    y_ref[...] = _rms(h3, nf_ref[...])


def _ffn_out(h1, h1_block0, mix, wo, n2, wg, wu, wd, nf, tm):
    n = mix.shape[0]
    return pl.pallas_call(
        _ffn_out_kernel,
        grid=(n // tm,),
        in_specs=[pl.BlockSpec((tm, D_MODEL), lambda i: (i + h1_block0, 0)),
                  pl.BlockSpec((tm, D_MODEL), lambda i: (i, 0)),
                  _resident(wo.shape), _resident(n2.shape), _resident(wg.shape), _resident(wu.shape),
                  _resident(wd.shape), _resident(nf.shape)],
        out_specs=pl.BlockSpec((tm, D_MODEL), lambda i: (i, 0)),
        out_shape=jax.ShapeDtypeStruct((n, D_MODEL), F32),
        compiler_params=_compiler_params(1),
        name="ffn_out",
    )(h1, mix, wo, n2, wg, wu, wd, nf)


def _head_slices(a, width):
    return [a[:, i:i + width] for i in range(0, a.shape[1], width)]


def _column_tile(row):
    return jnp.broadcast_to(row, (LANES, LANES)).T


def _store_hgrn_out(rows, o, gate, hgn, mix_ref):
    for h in range(HG_HEADS):
        sl = slice(h * HG_DIM, (h + 1) * HG_DIM)
        oh = o[:, sl]
        oh = oh * lax.rsqrt(jnp.mean(oh * oh, axis=-1, keepdims=True) + EPS)
        mix_ref[rows, sl] = (oh * hgn[:, sl] * gate[:, sl]).astype(mix_ref.dtype)


def _hgrn_log_decay(n_pad, tb, fz, lb, tri):
    logf = jnp.log(lb + (1.0 - lb) * jax.nn.sigmoid(fz))
    if n_pad:
        logf = jnp.where(lax.broadcasted_iota(jnp.int32, (tb, HG_WIDTH), 0) >= n_pad, logf, 0.0)
    b = _cumsum_rows(tri, logf)
    chunk_ends = jnp.concatenate([b[r:r + 1] for r in range(HG_CHUNK - 1, tb, HG_CHUNK)], axis=0)
    return b, jnp.max(-chunk_ends)


def _hgrn_block(n_pad, tb, u_ref, lb, hgn, b, worst_log_decay, mix_ref, st_scr, o_scr):
    chunked_is_safe = worst_log_decay < HG_MAX_CHUNK_LOG_DECAY

    @pl.when(chunked_is_safe)
    def _():
        _hgrn_chunked(n_pad, tb, u_ref, lb, hgn, b, mix_ref, st_scr)

    @pl.when(jnp.logical_not(chunked_is_safe))
    def _():
        _hgrn_per_token(n_pad, tb, u_ref, lb, hgn, mix_ref, st_scr, o_scr)


def _hgrn_per_token(n_pad, tb, u_ref, lb, hgn, mix_ref, st_scr, o_scr):
    for h in range(HG_HEADS):
        st_scr[h] = st_scr[h].T

    def sublane_group(i, carry):
        rows = pl.ds(pl.multiple_of(i * SUBLANES, SUBLANES), SUBLANES)
        fz = u_ref[rows, COL_F:COL_F + HG_WIDTH]
        f = lb + (1.0 - lb) * jax.nn.sigmoid(fz)
        kk = (1.0 - lb) * jax.nn.sigmoid(-fz)
        if n_pad:
            valid = i * SUBLANES + lax.broadcasted_iota(jnp.int32, (SUBLANES, HG_WIDTH), 0) >= n_pad
            f = jnp.where(valid, f, 1.0)
            kk = jnp.where(valid, kk, 0.0)
        q = _silu(u_ref[rows, COL_Q:COL_Q + HG_WIDTH])
        v = u_ref[rows, COL_I:COL_I + HG_WIDTH]
        for h in range(HG_HEADS):
            sl = slice(h * HG_DIM, (h + 1) * HG_DIM)
            s = st_scr[h]
            o_rows = []
            for j in range(SUBLANES):
                r = slice(j, j + 1)
                s = s * _column_tile(f[r, sl]) + _column_tile(kk[r, sl]) * v[r, sl]
                o_rows.append(jnp.sum(s * _column_tile(q[r, sl]), axis=0, keepdims=True))
            st_scr[h] = s
            o_scr[rows, sl] = jnp.concatenate(o_rows, axis=0)
        return carry

    lax.fori_loop(0, tb // SUBLANES, sublane_group, 0)
    for h in range(HG_HEADS):
        st_scr[h] = st_scr[h].T
    _store_hgrn_out(slice(0, tb), o_scr[...], _silu(u_ref[:, COL_G:COL_G + HG_WIDTH]), hgn, mix_ref)


def _hgrn_chunked(n_pad, tb, u_ref, lb, hgn, b, mix_ref, st_scr):
    c = HG_CHUNK
    kk = (1.0 - lb) * jax.nn.sigmoid(-u_ref[:, COL_F:COL_F + HG_WIDTH])
    if n_pad:
        kk = jnp.where(lax.broadcasted_iota(jnp.int32, (tb, HG_WIDTH), 0) >= n_pad, kk, 0.0)
    q = _silu(u_ref[:, COL_Q:COL_Q + HG_WIDTH])
    v = u_ref[:, COL_I:COL_I + HG_WIDTH].astype(BF16)
    gate = _silu(u_ref[:, COL_G:COL_G + HG_WIDTH])
    qt = (q * jnp.exp(b)).astype(BF16)
    kt = (kk * jnp.exp(-b)).astype(BF16)
    causal = _tiled_lower_tri(c, HG_HEADS)

    for r0 in range(0, tb, c):
        rows = slice(r0, r0 + c)
        b_c = b[rows]
        b_last = b_c[c - 1:c, :]
        kh_c = (kk[rows] * jnp.exp(b_last - b_c)).astype(BF16)
        decay = jnp.exp(b_last)
        qt_c = qt[rows]
        kd = _block_diag(_head_slices(kt[rows], HG_DIM))
        vd = _block_diag(_head_slices(v[rows], HG_DIM))
        scores = jnp.where(causal, _dot_nt(qt_c, kd), 0.0).astype(BF16)
        o = _dot(scores, vd)
        o_prev = []
        for p in range(HG_PAIRS):
            lanes = slice(p * 2 * HG_DIM, (p + 1) * 2 * HG_DIM)
            heads = (2 * p, 2 * p + 1)
            st = [st_scr[h] for h in heads]
            o_prev.append(_dot_nt(qt_c[:, lanes], _block_diag([s.astype(BF16) for s in st])))
            upd = _dot_tn(vd[p * 2 * c:(p + 1) * 2 * c, lanes], jnp.concatenate([kh_c[:, lanes]] * 2, axis=0))
            for i, h in enumerate(heads):
                blk = slice(i * HG_DIM, (i + 1) * HG_DIM)
                st_scr[h] = st[i] * decay[:, h * HG_DIM:(h + 1) * HG_DIM] + upd[blk, blk]
        _store_hgrn_out(rows, o + jnp.concatenate(o_prev, axis=1), gate[rows], hgn, mix_ref)


def _pair_columns(a, r0, r1, first_half):
    shape = (a.shape[0], LANES)
    return jnp.where(first_half, jnp.broadcast_to(a[:, r0:r0 + 1], shape), jnp.broadcast_to(a[:, r1:r1 + 1], shape))


def _group_columns(a, g, first_half):
    r = g * HEADS_PER_GROUP
    return jnp.concatenate([_pair_columns(a, r + i, r + i + 1, first_half)
                            for i in range(0, HEADS_PER_GROUP, 2)], axis=1)


def _ssd_block(n_pad, tb, u_ref, col0, cw, cb, dtb, a_neg, dsk, ssn, tri, mix_ref, hg_scr, xpad):
    c = SSM_CHUNK
    conv = cb
    for j in range(CONV_WIDTH):
        off = PAD_ROWS - (CONV_WIDTH - 1) + j
        conv = conv + cw[j:j + 1, :] * xpad[off:off + tb, :]
    act = _silu(conv)
    xs = act[:, 0:SSM_WIDTH]
    bm = act[:, SSM_WIDTH:SSM_WIDTH + SSM_GROUPS * SSM_STATE].astype(BF16)
    cm = act[:, SSM_WIDTH + SSM_GROUPS * SSM_STATE:].astype(BF16)
    z_gate = _silu(u_ref[:, col0:col0 + SSM_WIDTH])

    col_dt = col0 + COL_DT - COL_Z
    dt = jax.nn.softplus(u_ref[:, col_dt:col_dt + LANES] + dtb)
    if n_pad:
        dt = jnp.where(lax.broadcasted_iota(jnp.int32, (tb, LANES), 0) >= n_pad, dt, 0.0)
    cum = _cumsum_rows(tri, dt * a_neg)
    causal = _lower_tri(c)
    first_half = lax.broadcasted_iota(jnp.int32, (1, LANES), 1) < SSM_HEAD_DIM
    zeros = jnp.zeros((c, LANES), BF16)

    for r0 in range(0, tb, c):
        rows = slice(r0, r0 + c)
        cum_c = cum[rows]
        dt_c = dt[rows]
        last = cum_c[c - 1:c, :]
        cum_t = cum_c.T
        dt_t = dt_c.T
        e_cum = jnp.exp(cum_c)
        w_in = dt_c * jnp.exp(last - cum_c)
        e_last = jnp.exp(last)
        for g in range(SSM_GROUPS):
            yield
            glanes = slice(g * GROUP_WIDTH, (g + 1) * GROUP_WIDTH)
            bg = bm[rows, g * SSM_STATE:(g + 1) * SSM_STATE]
            cg = cm[rows, g * SSM_STATE:(g + 1) * SSM_STATE]
            xg = xs[rows, glanes]
            cbt = _dot_nt(cg, bg)
            m_heads = []
            for rr in range(HEADS_PER_GROUP):
                r = g * HEADS_PER_GROUP + rr
                seg = jnp.exp(jnp.where(causal, cum_c[:, r:r + 1] - cum_t[r:r + 1, :], -jnp.inf))
                m_heads.append((cbt * seg * dt_t[r:r + 1, :]).astype(BF16))
            xb = xg.astype(BF16)
            xd_rows = []
            for rr in range(HEADS_PER_GROUP):
                tile = xb[:, (rr // 2) * LANES:(rr // 2 + 1) * LANES]
                tile = jnp.where(first_half if rr % 2 == 0 else ~first_half, tile, jnp.zeros_like(tile))
                xd_rows.append(jnp.concatenate([tile, zeros] if rr < 2 else [zeros, tile], axis=1))
            xd = jnp.concatenate(xd_rows, axis=0)
            hg = hg_scr[g]
            y = (_dot(jnp.concatenate(m_heads, axis=1), xd)
                 + _dot(cg, hg.astype(BF16)) * _group_columns(e_cum, g, first_half))
            xw = (xg * _group_columns(w_in, g, first_half)).astype(BF16)
            hg_scr[g] = hg * _group_columns(e_last, g, first_half) + _dot_tn(bg, xw)
            yz = (y + dsk[:, glanes] * xg) * z_gate[rows, glanes]
            yz = yz * lax.rsqrt(jnp.mean(yz * yz, axis=-1, keepdims=True) + EPS) * ssn[:, glanes]
            mix_ref[rows, HG_WIDTH + g * GROUP_WIDTH:HG_WIDTH + (g + 1) * GROUP_WIDTH] = yz.astype(mix_ref.dtype)


def _interleave(main, side, side_steps):
    for n in side_steps:
        if next(main, StopIteration) is StopIteration:
            break
        for _ in range(n):
            next(side, None)
    for _ in main:
        pass
    for _ in side:
        pass


def _mixer_block(*args):
    for _ in _mixer_steps(*args):
        pass


def _mixer_steps(n_pad, tb, uh_ref, us_ref, us_col0, param_refs, mix_ref, st_scr, hg_scr, xpad, o_scr,
                 log_decay=None):
    lbl_ref, hgn_ref, cw_ref, cb_ref, dtb_ref, alog_ref, dsk_ref, ssn_ref, tri_hg_ref, tri_ssm_ref = param_refs
    lb = _forget_lower_bound(lbl_ref[...])
    if log_decay is None:
        b, worst = _hgrn_log_decay(n_pad, tb, uh_ref[:, COL_F:COL_F + HG_WIDTH], lb, tri_hg_ref[...])
    else:
        b, worst = log_decay[0][...], log_decay[1][0]
    _hgrn_block(n_pad, tb, uh_ref, lb, hgn_ref[...], b, worst, mix_ref, st_scr, o_scr)
    yield
    col_xbc = us_col0 + COL_XBC - COL_Z
    xbc = us_ref[:, col_xbc:col_xbc + CONV_DIM]
    if n_pad:
        xbc = jnp.where(lax.broadcasted_iota(jnp.int32, (tb, CONV_DIM), 0) >= n_pad, xbc, 0.0)
    xpad[PAD_ROWS:PAD_ROWS + tb, :] = xbc
    yield from _ssd_block(n_pad, tb, us_ref, us_col0, cw_ref[...], cb_ref[...], dtb_ref[...],
                          -jnp.exp(alog_ref[...]), dsk_ref[...], ssn_ref[...], tri_ssm_ref[...],
                          mix_ref, hg_scr, xpad)
    xpad[0:PAD_ROWS, :] = xpad[tb:tb + PAD_ROWS, :]


def _load_state(s0_ref, h0_ref, c0_ref, st_scr, hg_scr, xpad):
    for h in range(HG_HEADS):
        st_scr[h] = s0_ref[0, h].T
    for r in range(0, SSM_HEADS, 2):
        g, lane0 = r // HEADS_PER_GROUP, (r % HEADS_PER_GROUP) * SSM_HEAD_DIM
        hg_scr[g, :, lane0:lane0 + LANES] = jnp.concatenate([h0_ref[0, r], h0_ref[0, r + 1]], axis=0).T
    xpad[0:PAD_ROWS, :] = c0_ref[0]


def _store_state(s_out_ref, h_out_ref, c_out_ref, st_scr, hg_scr, xpad):
    for h in range(HG_HEADS):
        s_out_ref[0, h] = st_scr[h].T
    for r in range(0, SSM_HEADS, 2):
        g, lane0 = r // HEADS_PER_GROUP, (r % HEADS_PER_GROUP) * SSM_HEAD_DIM
        pair = hg_scr[g, :, lane0:lane0 + LANES].T
        h_out_ref[0, r] = pair[0:SSM_HEAD_DIM]
        h_out_ref[0, r + 1] = pair[SSM_HEAD_DIM:]
    c_out_ref[0] = xpad[0:PAD_ROWS, :]


N_MIXER_PARAMS = 10


def _scan_params(params, tb):
    return tuple(params) + (_chunked_tri(tb, HG_CHUNK), _chunked_tri(tb, SSM_CHUNK))


PROJ_TILE = 512
PROJ_STEPS_AFTER_SCAN_STEP = (3, 2, 2, 1, 1)
HG_SHAPE = (SSM_GROUPS, SSM_STATE, GROUP_WIDTH)
S_SHAPE = (HG_HEADS, HG_DIM, HG_DIM)
H_SHAPE = (SSM_HEADS, SSM_HEAD_DIM, SSM_STATE)
C_SHAPE = (PAD_ROWS, CONV_DIM)


def _scan_kernel(n_pad, tb, project, *refs):
    if project:
        h1_ref, nm_ref, win_ref = refs[:3]
        refs = refs[3:]
    else:
        u_ref = refs[0]
        refs = refs[1:]
    param_refs = refs[:N_MIXER_PARAMS]
    s0_ref, h0_ref, c0_ref, mix_ref, s_out_ref, h_out_ref, c_out_ref = refs[N_MIXER_PARAMS:N_MIXER_PARAMS + 7]
    scratch = refs[N_MIXER_PARAMS + 7:]
    if project:
        uh_scr, us_scr, st_scr, hg_scr, xpad, o_scr = scratch
    else:
        st_scr, hg_scr, xpad, o_scr = scratch
    t = pl.program_id(1)

    @pl.when(t == 0)
    def _():
        _load_state(s0_ref, h0_ref, c0_ref, st_scr, hg_scr, xpad)

    if project:
        hn = _rms(h1_ref[...], nm_ref[...]).astype(BF16)
        uh_scr[...] = _dot(hn, win_ref[:, 0:COL_Z])
        us_scr[...] = _dot(hn, win_ref[:, COL_Z:])
        _mixer_block(n_pad, tb, uh_scr, us_scr, 0, param_refs, mix_ref, st_scr, hg_scr, xpad, o_scr)
    else:
        _mixer_block(n_pad, tb, u_ref, u_ref, COL_Z, param_refs, mix_ref, st_scr, hg_scr, xpad, o_scr)

    @pl.when(t == pl.num_programs(1) - 1)
    def _():
        _store_state(s_out_ref, h_out_ref, c_out_ref, st_scr, hg_scr, xpad)


def _scan_pipelined_kernel(tb, nt, h1_ref, nm_ref, win_ref, *refs):
    param_refs = refs[:N_MIXER_PARAMS]
    s0_ref, h0_ref, c0_ref, mix_ref, s_out_ref, h_out_ref, c_out_ref = refs[N_MIXER_PARAMS:N_MIXER_PARAMS + 7]
    even, odd = refs[N_MIXER_PARAMS + 7:N_MIXER_PARAMS + 11], refs[N_MIXER_PARAMS + 11:N_MIXER_PARAMS + 15]
    st_scr, hg_scr, xpad, o_scr = refs[N_MIXER_PARAMS + 15:]
    s = pl.program_id(0)
    scanned = jnp.maximum(s - 1, 0)

    @pl.when(s == 0)
    def _():
        uh, us, b, worst = odd
        uh[...] = jnp.zeros(uh.shape, F32)
        us[...] = jnp.zeros(us.shape, F32)
        b[...] = jnp.zeros(b.shape, F32)
        worst[0] = 0.0

    @pl.when(scanned % nt == 0)
    def _():
        _load_state(s0_ref, h0_ref, c0_ref, st_scr, hg_scr, xpad)

    def project_steps(uh_w, us_w, b_w, worst_w):
        hn = _rms(h1_ref[...], nm_ref[...]).astype(BF16)
        for c0 in range(0, D_IN_PAD, PROJ_TILE):
            c1 = min(c0 + PROJ_TILE, D_IN_PAD)
            yield
            tile = _dot(hn, win_ref[:, c0:c1])
            if c0 < COL_Z:
                uh_w[:, c0:c1] = tile
            else:
                us_w[:, c0 - COL_Z:c1 - COL_Z] = tile
            if c0 == COL_F:
                assert c1 == COL_F + HG_WIDTH
                lb = _forget_lower_bound(param_refs[0][...])
                b, worst = _hgrn_log_decay(0, tb, tile, lb, param_refs[N_MIXER_PARAMS - 2][...])
                b_w[...] = b
                worst_w[0] = worst

    def body(write, read):
        uh_r, us_r, b_r, worst_r = read
        _interleave(_mixer_steps(0, tb, uh_r, us_r, 0, param_refs, mix_ref, st_scr, hg_scr, xpad, o_scr,
                                 (b_r, worst_r)),
                    project_steps(*write), PROJ_STEPS_AFTER_SCAN_STEP)

    pl.when(s % 2 == 0)(lambda: body(even, odd))
    pl.when(s % 2 == 1)(lambda: body(odd, even))

    @pl.when((s >= 1) & (scanned % nt == nt - 1))
    def _():
        _store_state(s_out_ref, h_out_ref, c_out_ref, st_scr, hg_scr, xpad)


def _scan_pipelined(h1, nb, nt, tb, proj, params, s0, h0, c0):
    n_blocks = nb * nt
    last = n_blocks - 1
    lead = tuple(proj) + _scan_params(params, tb)
    shared = lambda shape: pl.BlockSpec((1,) + shape[1:], lambda s: (0,) * len(shape))
    per_seq = lambda shape: pl.BlockSpec((1,) + shape, lambda s: (jnp.maximum(s - 1, 0) // nt,) + (0,) * len(shape))
    u_bufs = [pltpu.VMEM((tb, COL_Z), F32), pltpu.VMEM((tb, D_IN_PAD - COL_Z), F32),
              pltpu.VMEM((tb, HG_WIDTH), F32), pltpu.SMEM((1,), F32)]
    return pl.pallas_call(
        functools.partial(_scan_pipelined_kernel, tb, nt),
        grid=(n_blocks + 1,),
        in_specs=[pl.BlockSpec((tb, D_MODEL), lambda s: (jnp.minimum(s, last), 0))]
        + [_resident(p.shape) for p in lead]
        + [shared(s0.shape), shared(h0.shape), shared(c0.shape)],
        out_specs=[pl.BlockSpec((tb, D_MODEL), lambda s: (jnp.maximum(s - 1, 0), 0)),
                   per_seq(S_SHAPE), per_seq(H_SHAPE), per_seq(C_SHAPE)],
        out_shape=[jax.ShapeDtypeStruct((n_blocks * tb, D_MODEL), BF16),
                   jax.ShapeDtypeStruct((nb,) + S_SHAPE, F32),
                   jax.ShapeDtypeStruct((nb,) + H_SHAPE, F32),
                   jax.ShapeDtypeStruct((nb,) + C_SHAPE, F32)],
        scratch_shapes=u_bufs + u_bufs + [pltpu.VMEM(S_SHAPE, F32), pltpu.VMEM(HG_SHAPE, F32),
                                          pltpu.VMEM((PAD_ROWS + tb, CONV_DIM), F32),
                                          pltpu.VMEM((tb, HG_WIDTH), F32)],
        compiler_params=_compiler_params(1),
        name="proj_mixer",
    )(h1, *lead, s0, h0, c0)


def _scan(src, src_block0, nb, nt, tb, n_pad, proj, params, s0, h0, c0):
    lead = (tuple(proj) if proj else ()) + _scan_params(params, tb)
    shared = lambda shape: pl.BlockSpec((1,) + shape[1:], lambda b, t: (0,) * len(shape))
    per_seq = lambda shape: pl.BlockSpec((1,) + shape, lambda b, t: (b,) + (0,) * len(shape))
    scratch = [pltpu.VMEM(S_SHAPE, F32), pltpu.VMEM(HG_SHAPE, F32), pltpu.VMEM((PAD_ROWS + tb, CONV_DIM), F32),
               pltpu.VMEM((tb, HG_WIDTH), F32)]
    if proj:
        scratch = [pltpu.VMEM((tb, COL_Z), F32), pltpu.VMEM((tb, D_IN_PAD - COL_Z), F32)] + scratch
    return pl.pallas_call(
        functools.partial(_scan_kernel, n_pad, tb, bool(proj)),
        grid=(nb, nt),
        in_specs=[pl.BlockSpec((tb, src.shape[1]), lambda b, t: (src_block0 + b * nt + t, 0))]
        + [_resident(p.shape) for p in lead]
        + [shared(s0.shape), shared(h0.shape), shared(c0.shape)],
        out_specs=[pl.BlockSpec((tb, D_MODEL), lambda b, t: (b * nt + t, 0)),
                   per_seq(S_SHAPE), per_seq(H_SHAPE), per_seq(C_SHAPE)],
        out_shape=[jax.ShapeDtypeStruct((nb * nt * tb, D_MODEL), BF16),
                   jax.ShapeDtypeStruct((nb,) + S_SHAPE, F32),
                   jax.ShapeDtypeStruct((nb,) + H_SHAPE, F32),
                   jax.ShapeDtypeStruct((nb,) + C_SHAPE, F32)],
        scratch_shapes=scratch,
        compiler_params=_compiler_params(2),
        name="proj_mixer" if proj else "mixer",
    )(src, *lead, s0, h0, c0)


def _sample_kernel(nbs, u_ref, lbl_ref, hgn_ref, cw_ref, cb_ref, dtb_ref, alog_ref, dsk_ref, ssn_ref,
                   sh_ref, ss_ref, sc_ref, mix_ref, sh_out_ref, ss_out_ref, sc_out_ref, obuf, ybuf):
    lb = _forget_lower_bound(lbl_ref[...])
    fz = u_ref[:, COL_F:COL_F + HG_WIDTH]
    f = lb + (1.0 - lb) * jax.nn.sigmoid(fz)
    kk = (1.0 - lb) * jax.nn.sigmoid(-fz)
    q = _silu(u_ref[:, COL_Q:COL_Q + HG_WIDTH])
    v = u_ref[:, COL_I:COL_I + HG_WIDTH]

    xbc = u_ref[:, COL_XBC:COL_XBC + CONV_DIM]
    cw = cw_ref[...]
    conv = cb_ref[...] + cw[CONV_WIDTH - 1:CONV_WIDTH, :] * xbc
    for j in range(CONV_WIDTH - 1):
        conv = conv + cw[j:j + 1, :] * sc_ref[j]
    for j in range(CONV_WIDTH - 2):
        sc_out_ref[j] = sc_ref[j + 1]
    sc_out_ref[CONV_WIDTH - 2] = xbc
    act = _silu(conv)
    xs = act[:, 0:SSM_WIDTH]
    bm = act[:, SSM_WIDTH:SSM_WIDTH + SSM_GROUPS * SSM_STATE]
    cm = act[:, SSM_WIDTH + SSM_GROUPS * SSM_STATE:]
    dt = jax.nn.softplus(u_ref[:, COL_DT:COL_DT + LANES] + dtb_ref[...])
    d_a = jnp.exp(dt * (-jnp.exp(alog_ref[...])))

    top_half = lax.broadcasted_iota(jnp.int32, (LANES, LANES), 0) < SSM_HEAD_DIM
    for j in range(nbs):
        row = slice(j, j + 1)
        for h in range(HG_HEADS):
            sl = slice(h * HG_DIM, (h + 1) * HG_DIM)
            s_new = sh_ref[j, h] * _column_tile(f[row, sl]) + _column_tile(kk[row, sl]) * v[row, sl]
            sh_out_ref[j, h] = s_new
            obuf[row, sl] = jnp.sum(s_new * _column_tile(q[row, sl]), axis=0, keepdims=True)
        for rp in range(SSM_HEADS // 2):
            r0, r1 = 2 * rp, 2 * rp + 1
            g = r0 // HEADS_PER_GROUP
            sl = slice(rp * LANES, (rp + 1) * LANES)
            gsl = slice(g * SSM_STATE, (g + 1) * SSM_STATE)
            h2 = jnp.concatenate([ss_ref[j, r0], ss_ref[j, r1]], axis=0)
            da2 = jnp.where(top_half, d_a[row, r0:r0 + 1], d_a[row, r1:r1 + 1])
            dt2 = jnp.where(top_half, dt[row, r0:r0 + 1], dt[row, r1:r1 + 1])
            h_new = h2 * da2 + (dt2 * _column_tile(xs[row, sl])) * bm[row, gsl]
            ss_out_ref[j, r0] = h_new[0:SSM_HEAD_DIM]
            ss_out_ref[j, r1] = h_new[SSM_HEAD_DIM:]
            ybuf[row, sl] = jnp.sum((h_new * cm[row, gsl]).T, axis=0, keepdims=True)

    o = obuf[...]
    gate = _silu(u_ref[:, COL_G:COL_G + HG_WIDTH])
    hgn = hgn_ref[...]
    for h in range(HG_HEADS):
        sl = slice(h * HG_DIM, (h + 1) * HG_DIM)
        oh = o[:, sl]
        oh = oh * lax.rsqrt(jnp.mean(oh * oh, axis=-1, keepdims=True) + EPS)
        mix_ref[:, sl] = oh * hgn[:, sl] * gate[:, sl]
    yz = (ybuf[...] + dsk_ref[...] * xs) * _silu(u_ref[:, COL_Z:COL_Z + SSM_WIDTH])
    ssn = ssn_ref[...]
    for g in range(SSM_GROUPS):
        sl = slice(g * GROUP_WIDTH, (g + 1) * GROUP_WIDTH)
        seg = yz[:, sl]
        seg = seg * lax.rsqrt(jnp.mean(seg * seg, axis=-1, keepdims=True) + EPS) * ssn[:, sl]
        mix_ref[:, HG_WIDTH + g * GROUP_WIDTH:HG_WIDTH + (g + 1) * GROUP_WIDTH] = seg


def _sample_mixer(u, u_block0, n, nbs, params, sh, ss, sc_t):
    small = [_resident(p.shape) for p in params]
    return pl.pallas_call(
        functools.partial(_sample_kernel, nbs),
        grid=(n // nbs,),
        in_specs=[pl.BlockSpec((nbs, D_IN_PAD), lambda i: (u_block0 + i, 0))] + small
        + [pl.BlockSpec((nbs,) + sh.shape[1:], lambda i: (i, 0, 0, 0)),
           pl.BlockSpec((nbs,) + ss.shape[1:], lambda i: (i, 0, 0, 0)),
           pl.BlockSpec((CONV_WIDTH - 1, nbs, CONV_DIM), lambda i: (0, i, 0))],
        out_specs=[pl.BlockSpec((nbs, D_MODEL), lambda i: (i, 0)),
                   pl.BlockSpec((nbs,) + sh.shape[1:], lambda i: (i, 0, 0, 0)),
                   pl.BlockSpec((nbs,) + ss.shape[1:], lambda i: (i, 0, 0, 0)),
                   pl.BlockSpec((CONV_WIDTH - 1, nbs, CONV_DIM), lambda i: (0, i, 0))],
        out_shape=[jax.ShapeDtypeStruct((n, D_MODEL), F32),
                   jax.ShapeDtypeStruct(sh.shape, F32),
                   jax.ShapeDtypeStruct(ss.shape, F32),
                   jax.ShapeDtypeStruct(sc_t.shape, F32)],
        scratch_shapes=[pltpu.VMEM((nbs, HG_WIDTH), F32), pltpu.VMEM((nbs, SSM_WIDTH), F32)],
        compiler_params=_compiler_params(1),
        name="sample_mixer",
    )(u, *params, sh, ss, sc_t)


TM_DENSE = 512
TM_SCAN = 256
TM_SMALL = 128
SAMPLES_PER_STEP = 8


def _pad_lanes(row, value=0.0):
    return jnp.pad(row, ((0, 0), (0, LANES - row.shape[1])), constant_values=value)


def kernel(x_prompt, x_sample, state_hgrn, state_ssm, state_conv, meta_tokens, lb_logits, norm_ffn1, w_ffn1_gate, w_ffn1_up, w_ffn1_down, norm_mix, w_in, hg_norm, conv_w, conv_b, dt_bias, a_log, d_skip, ssm_norm, w_out, norm_ffn2, w_ffn2_gate, w_ffn2_up, w_ffn2_down, norm_final):
    bp, seq_p, _ = x_prompt.shape
    n_s = x_sample.shape[0]
    assert x_sample.shape[1] == 1 and n_s == TM_SMALL and seq_p % TM_SCAN == 0 and (bp * seq_p) % TM_DENSE == 0
    layer = 0

    n1, nm, n2 = norm_ffn1[layer][None], norm_mix[layer][None], norm_ffn2[layer][None]
    nf = norm_final[None]
    win =jnp.pad(w_in[layer], ((0, 0), (0, D_IN_PAD - w_in.shape[2]))).astype(BF16)
    wo = w_out[layer].astype(BF16)
    mixer_params = (lb_logits, hg_norm[layer][None], conv_w[layer], conv_b[layer][None],
                    _pad_lanes(dt_bias[layer][None]), _pad_lanes(a_log[layer][None]),
                    jnp.repeat(d_skip[layer], SSM_HEAD_DIM)[None], ssm_norm[layer][None])

    n_pad = TM_SMALL - N_META
    x_small = jnp.concatenate([jnp.zeros((n_pad, D_MODEL), F32), meta_tokens, x_sample[:, 0]], axis=0)
    h1_small, wg1, wu1, wd1 = _ffn1_small(x_small, n1, w_ffn1_gate[layer], w_ffn1_up[layer], w_ffn1_down[layer])
    u_small = _proj_small(h1_small, nm, win)

    zeros_s = jnp.zeros((1,) + S_SHAPE, F32)
    zeros_h = jnp.zeros((1,) + H_SHAPE, F32)
    zeros_c = jnp.zeros((1,) + C_SHAPE, F32)
    _, s_meta, h_meta, c_meta = _scan(u_small, 0, 1, 1, TM_SMALL, n_pad, None, mixer_params,
                                      zeros_s, zeros_h, zeros_c)

    sc_t = jnp.swapaxes(state_conv[layer], 0, 1)
    mix_s, hgrn_s, ssm_s, conv_s_t = _sample_mixer(
        u_small, TM_SMALL // SAMPLES_PER_STEP, n_s, SAMPLES_PER_STEP, mixer_params,
        state_hgrn[layer], state_ssm[layer], sc_t)
    y_s, wg2, wu2, wd2 = _ffn2_small(h1_small, 1, mix_s, wo, n2, w_ffn2_gate[layer], w_ffn2_up[layer],
                                     w_ffn2_down[layer], nf)

    xp = x_prompt.reshape(bp * seq_p, D_MODEL)
    h1_p = _ffn1(xp, n1, wg1, wu1, wd1, TM_DENSE)
    mix_p, hgrn_p, ssm_p, conv_p = _scan_pipelined(h1_p, bp, seq_p // TM_SCAN, TM_SCAN, (nm, win), mixer_params,
                                                   s_meta, h_meta, c_meta)
    y_p = _ffn_out(h1_p, 0, mix_p, wo, n2, wg2, wu2, wd2, nf, TM_DENSE)

    keep = slice(PAD_ROWS - (CONV_WIDTH - 1), PAD_ROWS)
    return (y_p.reshape(bp, seq_p, D_MODEL),
            y_s.reshape(n_s, 1, D_MODEL),
            hgrn_p[None], ssm_p[None], conv_p[:, keep][None],
            hgrn_s[None], ssm_s[None], jnp.swapaxes(conv_s_t, 0, 1)[None])
```

```python
import functools

import jax
import jax.numpy as jnp
from jax import lax
from jax.experimental import pallas as pl
from jax.experimental.pallas import tpu as pltpu

F32 = jnp.float32
BF16 = jnp.bfloat16

D_MODEL = 1024
D_FF = 2816
N_META = 16
HG_WIDTH = 512
HG_HEADS = 4
HG_DIM = 128
SSM_WIDTH = 512
SSM_HEADS = 8
SSM_HEAD_DIM = 64
SSM_GROUPS = 2
SSM_STATE = 128
CONV_WIDTH = 4
CONV_DIM = SSM_WIDTH + 2 * SSM_GROUPS * SSM_STATE
EPS = 1e-6

LANES = 128
SUBLANES = 8
VMEM_LIMIT_BYTES = 56 * 1024 * 1024

COL_Q = 0
COL_F = HG_WIDTH
COL_I = 2 * HG_WIDTH
COL_G = 3 * HG_WIDTH
COL_Z = 4 * HG_WIDTH
COL_XBC = COL_Z + SSM_WIDTH
COL_DT = COL_XBC + CONV_DIM
D_IN_PAD = COL_DT + LANES

FF_TILE = 256
HG_CHUNK = 64
HG_MAX_CHUNK_LOG_DECAY = 80.0
SSM_CHUNK = 128
PAD_ROWS = SUBLANES
HG_PAIRS = HG_HEADS // 2
HEADS_PER_GROUP = SSM_HEADS // SSM_GROUPS
GROUP_WIDTH = SSM_WIDTH // SSM_GROUPS
assert 2 * SSM_HEAD_DIM == LANES and 2 * HG_DIM == FF_TILE


def _dot(a, b):
    return jnp.dot(a, b, preferred_element_type=F32)


def _dot_nt(a, b):
    return lax.dot_general(a, b, (((1,), (1,)), ((), ())), preferred_element_type=F32)


def _dot_tn(a, b):
    return lax.dot_general(a, b, (((0,), (0,)), ((), ())), preferred_element_type=F32)


def _rms(x, w):
    return x * lax.rsqrt(jnp.mean(x * x, axis=-1, keepdims=True) + EPS) * w


def _silu(x):
    return x * jax.nn.sigmoid(x)


def _swiglu(xn, wg_ref, wu_ref, wd_ref):
    acc = jnp.zeros((xn.shape[0], D_MODEL), F32)
    for j in range(D_FF // FF_TILE):
        cols = slice(j * FF_TILE, (j + 1) * FF_TILE)
        g = _dot(xn, wg_ref[:, cols])
        u = _dot(xn, wu_ref[:, cols])
        acc = acc + _dot((_silu(g) * u).astype(BF16), wd_ref[cols, :])
    return acc


def _cumsum_rows(tri, a):
    n = tri.shape[0]
    if a.shape[0] > n:
        return jnp.concatenate([_cumsum_rows(tri, a[r:r + n]) for r in range(0, a.shape[0], n)], axis=0)
    a1 = a.astype(BF16)
    r1 = a - a1.astype(F32)
    a2 = r1.astype(BF16)
    a3 = (r1 - a2.astype(F32)).astype(BF16)
    return _dot(tri, a1) + _dot(tri, a2) + _dot(tri, a3)


def _lower_tri(n):
    row = lax.broadcasted_iota(jnp.int32, (n, n), 0)
    col = lax.broadcasted_iota(jnp.int32, (n, n), 1)
    return row >= col


def _chunked_tri(n, chunk):
    assert chunk & (chunk - 1) == 0
    row = lax.broadcasted_iota(jnp.int32, (n, n), 0)
    col = lax.broadcasted_iota(jnp.int32, (n, n), 1)
    same_chunk = (row ^ col) < chunk
    return ((row >= col) & same_chunk).astype(BF16)


def _tiled_lower_tri(chunk, reps):
    assert chunk & (chunk - 1) == 0
    row = lax.broadcasted_iota(jnp.int32, (chunk, reps * chunk), 0)
    col = lax.broadcasted_iota(jnp.int32, (chunk, reps * chunk), 1)
    return row >= (col & (chunk - 1))


def _block_diag(blocks):
    n = len(blocks)
    r, c = blocks[0].shape
    rows = []
    for i, blk in enumerate(blocks):
        parts = []
        if i:
            parts.append(jnp.zeros((r, c * i), blk.dtype))
        parts.append(blk)
        if i < n - 1:
            parts.append(jnp.zeros((r, c * (n - 1 - i)), blk.dtype))
        rows.append(jnp.concatenate(parts, axis=1))
    return jnp.concatenate(rows, axis=0)


def _forget_lower_bound(lbl):
    l0, l1 = lbl[0:1], lbl[1:2]
    m = jnp.maximum(l0, l1)
    e0, e1 = jnp.exp(l0 - m), jnp.exp(l1 - m)
    return e0 / (e0 + e1)


def _resident(shape):
    nd = len(shape)
    return pl.BlockSpec(shape, lambda *_: (0,) * nd, pipeline_mode=pl.Buffered(1))


def _compiler_params(n_grid_axes, flags=None):
    return pltpu.CompilerParams(dimension_semantics=("arbitrary",) * n_grid_axes,
                                vmem_limit_bytes=VMEM_LIMIT_BYTES, flags=flags)


def _ffn1_kernel(x_ref, n1_ref, wg_ref, wu_ref, wd_ref, h1_ref):
    x = x_ref[...]
    xn = _rms(x, n1_ref[...]).astype(BF16)
    h1_ref[...] = x + 0.5 * _swiglu(xn, wg_ref, wu_ref, wd_ref)


def _ffn1(x, n1, wg, wu, wd, tm):
    n = x.shape[0]
    return pl.pallas_call(
        _ffn1_kernel,
        grid=(n // tm,),
        in_specs=[pl.BlockSpec((tm, D_MODEL), lambda i: (i, 0)),
                  _resident(n1.shape), _resident(wg.shape), _resident(wu.shape), _resident(wd.shape)],
        out_specs=pl.BlockSpec((tm, D_MODEL), lambda i: (i, 0)),
        out_shape=jax.ShapeDtypeStruct((n, D_MODEL), F32),
        compiler_params=_compiler_params(1),
        name="ffn1",
    )(x, n1, wg, wu, wd)


def _ffn_tile_step(xn, wg_ref, wu_ref, wd_ref, wg_out, wu_out, wd_out, acc_scr):
    wg, wu, wd = wg_ref[...].astype(BF16), wu_ref[...].astype(BF16), wd_ref[...].astype(BF16)
    wg_out[...] = wg
    wu_out[...] = wu
    wd_out[...] = wd
    acc_scr[...] += _dot((_silu(_dot(xn, wg)) * _dot(xn, wu)).astype(BF16), wd)


def _ffn1_small_kernel(x_ref, n1_ref, wg_ref, wu_ref, wd_ref, h1_ref, wg_out, wu_out, wd_out, xn_scr, acc_scr):
    j = pl.program_id(0)

    @pl.when(j == 0)
    def _():
        xn_scr[...] = _rms(x_ref[...], n1_ref[...]).astype(BF16)
        acc_scr[...] = jnp.zeros(acc_scr.shape, F32)

    _ffn_tile_step(xn_scr[...], wg_ref, wu_ref, wd_ref, wg_out, wu_out, wd_out, acc_scr)

    @pl.when(j == pl.num_programs(0) - 1)
    def _():
        h1_ref[...] = x_ref[...] + 0.5 * acc_scr[...]


def _ffn2_small_kernel(h1_ref, mix_ref, wo_ref, n2_ref, wg_ref, wu_ref, wd_ref, nf_ref,
                       y_ref, wg_out, wu_out, wd_out, h2_scr, hn_scr, acc_scr):
    j = pl.program_id(0)

    @pl.when(j == 0)
    def _():
        h2 = h1_ref[...] + _dot(mix_ref[...].astype(BF16), wo_ref[...])
        h2_scr[...] = h2
        hn_scr[...] = _rms(h2, n2_ref[...]).astype(BF16)
        acc_scr[...] = jnp.zeros(acc_scr.shape, F32)

    _ffn_tile_step(hn_scr[...], wg_ref, wu_ref, wd_ref, wg_out, wu_out, wd_out, acc_scr)

    @pl.when(j == pl.num_programs(0) - 1)
    def _():
        y_ref[...] = _rms(h2_scr[...] + 0.5 * acc_scr[...], nf_ref[...])


def _weight_tile_specs():
    cols = pl.BlockSpec((D_MODEL, FF_TILE), lambda j: (0, j))
    rows = pl.BlockSpec((FF_TILE, D_MODEL), lambda j: (j, 0))
    shapes = [jax.ShapeDtypeStruct((D_MODEL, D_FF), BF16), jax.ShapeDtypeStruct((D_MODEL, D_FF), BF16),
              jax.ShapeDtypeStruct((D_FF, D_MODEL), BF16)]
    return [cols, cols, rows], shapes


def _ffn1_small(x, n1, wg, wu, wd):
    n = x.shape[0]
    wspecs, wshapes = _weight_tile_specs()
    return pl.pallas_call(
        _ffn1_small_kernel,
        grid=(D_FF // FF_TILE,),
        in_specs=[_resident(x.shape), _resident(n1.shape)] + wspecs,
        out_specs=[pl.BlockSpec((n, D_MODEL), lambda j: (0, 0))] + wspecs,
        out_shape=[jax.ShapeDtypeStruct((n, D_MODEL), F32)] + wshapes,
        scratch_shapes=[pltpu.VMEM((n, D_MODEL), BF16), pltpu.VMEM((n, D_MODEL), F32)],
        compiler_params=_compiler_params(1),
        name="ffn1_small",
    )(x, n1, wg, wu, wd)


def _ffn2_small(h1, h1_block, mix, wo, n2, wg, wu, wd, nf):
    n = mix.shape[0]
    wspecs, wshapes = _weight_tile_specs()
    return pl.pallas_call(
        _ffn2_small_kernel,
        grid=(D_FF // FF_TILE,),
        in_specs=[pl.BlockSpec((n, D_MODEL), lambda j: (h1_block, 0), pipeline_mode=pl.Buffered(1)),
                  _resident(mix.shape), _resident(wo.shape), _resident(n2.shape)] + wspecs
        + [_resident(nf.shape)],
        out_specs=[pl.BlockSpec((n, D_MODEL), lambda j: (0, 0))] + wspecs,
        out_shape=[jax.ShapeDtypeStruct((n, D_MODEL), F32)] + wshapes,
        scratch_shapes=[pltpu.VMEM((n, D_MODEL), F32), pltpu.VMEM((n, D_MODEL), BF16),
                        pltpu.VMEM((n, D_MODEL), F32)],
        compiler_params=_compiler_params(1),
        name="ffn2_small",
    )(h1, mix, wo, n2, wg, wu, wd, nf)


def _proj_kernel(h1_ref, nm_ref, win_ref, u_ref):
    u_ref[...] = _dot(_rms(h1_ref[...], nm_ref[...]).astype(BF16), win_ref[...])


def _proj_small(h1, nm, win):
    n = h1.shape[0]
    return pl.pallas_call(
        _proj_kernel,
        grid=(1,),
        in_specs=[_resident(h1.shape), _resident(nm.shape), _resident(win.shape)],
        out_specs=pl.BlockSpec((n, D_IN_PAD), lambda i: (0, 0)),
        out_shape=jax.ShapeDtypeStruct((n, D_IN_PAD), F32),
        compiler_params=_compiler_params(1),
        name="proj_small",
    )(h1, nm, win)


def _ffn_out_kernel(h1_ref, mix_ref, wo_ref, n2_ref, wg_ref, wu_ref, wd_ref, nf_ref, y_ref):
    h2 = h1_ref[...] + _dot(mix_ref[...].astype(BF16), wo_ref[...])
    hn = _rms(h2, n2_ref[...]).astype(BF16)
    h3 = h2 + 0.5 * _swiglu(hn, wg_ref, wu_ref, wd_ref)
    y_ref[...] = _rms(h3, nf_ref[...])


def _ffn_out(h1, h1_block0, mix, wo, n2, wg, wu, wd, nf, tm):
    n = mix.shape[0]
    return pl.pallas_call(
        _ffn_out_kernel,
        grid=(n // tm,),
        in_specs=[pl.BlockSpec((tm, D_MODEL), lambda i: (i + h1_block0, 0)),
                  pl.BlockSpec((tm, D_MODEL), lambda i: (i, 0)),
                  _resident(wo.shape), _resident(n2.shape), _resident(wg.shape), _resident(wu.shape),
                  _resident(wd.shape), _resident(nf.shape)],
        out_specs=pl.BlockSpec((tm, D_MODEL), lambda i: (i, 0)),
        out_shape=jax.ShapeDtypeStruct((n, D_MODEL), F32),
        compiler_params=_compiler_params(1),
        name="ffn_out",
    )(h1, mix, wo, n2, wg, wu, wd, nf)


def _head_slices(a, width):
    return [a[:, i:i + width] for i in range(0, a.shape[1], width)]


def _column_tile(row):
    return jnp.broadcast_to(row, (LANES, LANES)).T


def _store_hgrn_out(rows, o, gate, hgn, mix_ref):
    for h in range(HG_HEADS):
        sl = slice(h * HG_DIM, (h + 1) * HG_DIM)
        oh = o[:, sl]
        oh = oh * lax.rsqrt(jnp.mean(oh * oh, axis=-1, keepdims=True) + EPS)
        mix_ref[rows, sl] = (oh * hgn[:, sl] * gate[:, sl]).astype(mix_ref.dtype)


def _hgrn_log_decay(n_pad, tb, fz, lb, tri):
    logf = jnp.log(lb + (1.0 - lb) * jax.nn.sigmoid(fz))
    if n_pad:
        logf = jnp.where(lax.broadcasted_iota(jnp.int32, (tb, HG_WIDTH), 0) >= n_pad, logf, 0.0)
    b = _cumsum_rows(tri, logf)
    chunk_ends = jnp.concatenate([b[r:r + 1] for r in range(HG_CHUNK - 1, tb, HG_CHUNK)], axis=0)
    return b, jnp.max(-chunk_ends)


def _hgrn_block(n_pad, tb, u_ref, lb, hgn, b, worst_log_decay, mix_ref, st_scr, o_scr):
    chunked_is_safe = worst_log_decay < HG_MAX_CHUNK_LOG_DECAY

    @pl.when(chunked_is_safe)
    def _():
        _hgrn_chunked(n_pad, tb, u_ref, lb, hgn, b, mix_ref, st_scr)

    @pl.when(jnp.logical_not(chunked_is_safe))
    def _():
        _hgrn_per_token(n_pad, tb, u_ref, lb, hgn, mix_ref, st_scr, o_scr)


def _hgrn_per_token(n_pad, tb, u_ref, lb, hgn, mix_ref, st_scr, o_scr):
    for h in range(HG_HEADS):
        st_scr[h] = st_scr[h].T

    def sublane_group(i, carry):
        rows = pl.ds(pl.multiple_of(i * SUBLANES, SUBLANES), SUBLANES)
        fz = u_ref[rows, COL_F:COL_F + HG_WIDTH]
        f = lb + (1.0 - lb) * jax.nn.sigmoid(fz)
        kk = (1.0 - lb) * jax.nn.sigmoid(-fz)
        if n_pad:
            valid = i * SUBLANES + lax.broadcasted_iota(jnp.int32, (SUBLANES, HG_WIDTH), 0) >= n_pad
            f = jnp.where(valid, f, 1.0)
            kk = jnp.where(valid, kk, 0.0)
        q = _silu(u_ref[rows, COL_Q:COL_Q + HG_WIDTH])
        v = u_ref[rows, COL_I:COL_I + HG_WIDTH]
        for h in range(HG_HEADS):
            sl = slice(h * HG_DIM, (h + 1) * HG_DIM)
            s = st_scr[h]
            o_rows = []
            for j in range(SUBLANES):
                r = slice(j, j + 1)
                s = s * _column_tile(f[r, sl]) + _column_tile(kk[r, sl]) * v[r, sl]
                o_rows.append(jnp.sum(s * _column_tile(q[r, sl]), axis=0, keepdims=True))
            st_scr[h] = s
            o_scr[rows, sl] = jnp.concatenate(o_rows, axis=0)
        return carry

    lax.fori_loop(0, tb // SUBLANES, sublane_group, 0)
    for h in range(HG_HEADS):
        st_scr[h] = st_scr[h].T
    _store_hgrn_out(slice(0, tb), o_scr[...], _silu(u_ref[:, COL_G:COL_G + HG_WIDTH]), hgn, mix_ref)


def _hgrn_chunked(n_pad, tb, u_ref, lb, hgn, b, mix_ref, st_scr):
    c = HG_CHUNK
    kk = (1.0 - lb) * jax.nn.sigmoid(-u_ref[:, COL_F:COL_F + HG_WIDTH])
    if n_pad:
        kk = jnp.where(lax.broadcasted_iota(jnp.int32, (tb, HG_WIDTH), 0) >= n_pad, kk, 0.0)
    q = _silu(u_ref[:, COL_Q:COL_Q + HG_WIDTH])
    v = u_ref[:, COL_I:COL_I + HG_WIDTH].astype(BF16)
    gate = _silu(u_ref[:, COL_G:COL_G + HG_WIDTH])
    qt = (q * jnp.exp(b)).astype(BF16)
    kt = (kk * jnp.exp(-b)).astype(BF16)
    causal = _tiled_lower_tri(c, HG_HEADS)

    for r0 in range(0, tb, c):
        rows = slice(r0, r0 + c)
        b_c = b[rows]
        b_last = b_c[c - 1:c, :]
        kh_c = (kk[rows] * jnp.exp(b_last - b_c)).astype(BF16)
        decay = jnp.exp(b_last)
        qt_c = qt[rows]
        kd = _block_diag(_head_slices(kt[rows], HG_DIM))
        vd = _block_diag(_head_slices(v[rows], HG_DIM))
        scores = jnp.where(causal, _dot_nt(qt_c, kd), 0.0).astype(BF16)
        o = _dot(scores, vd)
        o_prev = []
        for p in range(HG_PAIRS):
            lanes = slice(p * 2 * HG_DIM, (p + 1) * 2 * HG_DIM)
            heads = (2 * p, 2 * p + 1)
            st = [st_scr[h] for h in heads]
            o_prev.append(_dot_nt(qt_c[:, lanes], _block_diag([s.astype(BF16) for s in st])))
            upd = _dot_tn(vd[p * 2 * c:(p + 1) * 2 * c, lanes], jnp.concatenate([kh_c[:, lanes]] * 2, axis=0))
            for i, h in enumerate(heads):
                blk = slice(i * HG_DIM, (i + 1) * HG_DIM)
                st_scr[h] = st[i] * decay[:, h * HG_DIM:(h + 1) * HG_DIM] + upd[blk, blk]
        _store_hgrn_out(rows, o + jnp.concatenate(o_prev, axis=1), gate[rows], hgn, mix_ref)


def _pair_columns(a, r0, r1, first_half):
    shape = (a.shape[0], LANES)
    return jnp.where(first_half, jnp.broadcast_to(a[:, r0:r0 + 1], shape), jnp.broadcast_to(a[:, r1:r1 + 1], shape))


def _group_columns(a, g, first_half):
    r = g * HEADS_PER_GROUP
    return jnp.concatenate([_pair_columns(a, r + i, r + i + 1, first_half)
                            for i in range(0, HEADS_PER_GROUP, 2)], axis=1)


def _ssd_block(n_pad, tb, u_ref, col0, cw, cb, dtb, a_neg, dsk, ssn, tri, mix_ref, hg_scr, xpad):
    c = SSM_CHUNK
    conv = cb
    for j in range(CONV_WIDTH):
        off = PAD_ROWS - (CONV_WIDTH - 1) + j
        conv = conv + cw[j:j + 1, :] * xpad[off:off + tb, :]
    act = _silu(conv)
    xs = act[:, 0:SSM_WIDTH]
    bm = act[:, SSM_WIDTH:SSM_WIDTH + SSM_GROUPS * SSM_STATE].astype(BF16)
    cm = act[:, SSM_WIDTH + SSM_GROUPS * SSM_STATE:].astype(BF16)
    z_gate = _silu(u_ref[:, col0:col0 + SSM_WIDTH])

    col_dt = col0 + COL_DT - COL_Z
    dt = jax.nn.softplus(u_ref[:, col_dt:col_dt + LANES] + dtb)
    if n_pad:
        dt = jnp.where(lax.broadcasted_iota(jnp.int32, (tb, LANES), 0) >= n_pad, dt, 0.0)
    cum = _cumsum_rows(tri, dt * a_neg)
    causal = _lower_tri(c)
    first_half = lax.broadcasted_iota(jnp.int32, (1, LANES), 1) < SSM_HEAD_DIM
    zeros = jnp.zeros((c, LANES), BF16)

    for r0 in range(0, tb, c):
        rows = slice(r0, r0 + c)
        cum_c = cum[rows]
        dt_c = dt[rows]
        last = cum_c[c - 1:c, :]
        cum_t = cum_c.T
        dt_t = dt_c.T
        e_cum = jnp.exp(cum_c)
        w_in = dt_c * jnp.exp(last - cum_c)
        e_last = jnp.exp(last)
        for g in range(SSM_GROUPS):
            yield
            glanes = slice(g * GROUP_WIDTH, (g + 1) * GROUP_WIDTH)
            bg = bm[rows, g * SSM_STATE:(g + 1) * SSM_STATE]
            cg = cm[rows, g * SSM_STATE:(g + 1) * SSM_STATE]
            xg = xs[rows, glanes]
            cbt = _dot_nt(cg, bg)
            m_heads = []
            for rr in range(HEADS_PER_GROUP):
                r = g * HEADS_PER_GROUP + rr
                seg = jnp.exp(jnp.where(causal, cum_c[:, r:r + 1] - cum_t[r:r + 1, :], -jnp.inf))
                m_heads.append((cbt * seg * dt_t[r:r + 1, :]).astype(BF16))
            xb = xg.astype(BF16)
            xd_rows = []
            for rr in range(HEADS_PER_GROUP):
                tile = xb[:, (rr // 2) * LANES:(rr // 2 + 1) * LANES]
                tile = jnp.where(first_half if rr % 2 == 0 else ~first_half, tile, jnp.zeros_like(tile))
                xd_rows.append(jnp.concatenate([tile, zeros] if rr < 2 else [zeros, tile], axis=1))
            xd = jnp.concatenate(xd_rows, axis=0)
            hg = hg_scr[g]
            y = (_dot(jnp.concatenate(m_heads, axis=1), xd)
                 + _dot(cg, hg.astype(BF16)) * _group_columns(e_cum, g, first_half))
            xw = (xg * _group_columns(w_in, g, first_half)).astype(BF16)
            hg_scr[g] = hg * _group_columns(e_last, g, first_half) + _dot_tn(bg, xw)
            yz = (y + dsk[:, glanes] * xg) * z_gate[rows, glanes]
            yz = yz * lax.rsqrt(jnp.mean(yz * yz, axis=-1, keepdims=True) + EPS) * ssn[:, glanes]
            mix_ref[rows, HG_WIDTH + g * GROUP_WIDTH:HG_WIDTH + (g + 1) * GROUP_WIDTH] = yz.astype(mix_ref.dtype)


def _interleave(main, side, side_steps):
    for n in side_steps:
        if next(main, StopIteration) is StopIteration:
            break
        for _ in range(n):
            next(side, None)
    for _ in main:
        pass
    for _ in side:
        pass


def _mixer_block(*args):
    for _ in _mixer_steps(*args):
        pass


def _mixer_steps(n_pad, tb, uh_ref, us_ref, us_col0, param_refs, mix_ref, st_scr, hg_scr, xpad, o_scr,
                 log_decay=None):
    lbl_ref, hgn_ref, cw_ref, cb_ref, dtb_ref, alog_ref, dsk_ref, ssn_ref, tri_hg_ref, tri_ssm_ref = param_refs
    lb = _forget_lower_bound(lbl_ref[...])
    if log_decay is None:
        b, worst = _hgrn_log_decay(n_pad, tb, uh_ref[:, COL_F:COL_F + HG_WIDTH], lb, tri_hg_ref[...])
    else:
        b, worst = log_decay[0][...], log_decay[1][0]
    _hgrn_block(n_pad, tb, uh_ref, lb, hgn_ref[...], b, worst, mix_ref, st_scr, o_scr)
    yield
    col_xbc = us_col0 + COL_XBC - COL_Z
    xbc = us_ref[:, col_xbc:col_xbc + CONV_DIM]
    if n_pad:
        xbc = jnp.where(lax.broadcasted_iota(jnp.int32, (tb, CONV_DIM), 0) >= n_pad, xbc, 0.0)
    xpad[PAD_ROWS:PAD_ROWS + tb, :] = xbc
    yield from _ssd_block(n_pad, tb, us_ref, us_col0, cw_ref[...], cb_ref[...], dtb_ref[...],
                          -jnp.exp(alog_ref[...]), dsk_ref[...], ssn_ref[...], tri_ssm_ref[...],
                          mix_ref, hg_scr, xpad)
    xpad[0:PAD_ROWS, :] = xpad[tb:tb + PAD_ROWS, :]


def _load_state(s0_ref, h0_ref, c0_ref, st_scr, hg_scr, xpad):
    for h in range(HG_HEADS):
        st_scr[h] = s0_ref[0, h].T
    for r in range(0, SSM_HEADS, 2):
        g, lane0 = r // HEADS_PER_GROUP, (r % HEADS_PER_GROUP) * SSM_HEAD_DIM
        hg_scr[g, :, lane0:lane0 + LANES] = jnp.concatenate([h0_ref[0, r], h0_ref[0, r + 1]], axis=0).T
    xpad[0:PAD_ROWS, :] = c0_ref[0]


def _store_state(s_out_ref, h_out_ref, c_out_ref, st_scr, hg_scr, xpad):
    for h in range(HG_HEADS):
        s_out_ref[0, h] = st_scr[h].T
    for r in range(0, SSM_HEADS, 2):
        g, lane0 = r // HEADS_PER_GROUP, (r % HEADS_PER_GROUP) * SSM_HEAD_DIM
        pair = hg_scr[g, :, lane0:lane0 + LANES].T
        h_out_ref[0, r] = pair[0:SSM_HEAD_DIM]
        h_out_ref[0, r + 1] = pair[SSM_HEAD_DIM:]
    c_out_ref[0] = xpad[0:PAD_ROWS, :]


N_MIXER_PARAMS = 10


def _scan_params(params, tb):
    n = min(tb, FF_TILE)
    return tuple(params) + (_chunked_tri(n, HG_CHUNK), _chunked_tri(n, SSM_CHUNK))


PROJ_TILE = 512
PROJ_STEPS_AFTER_SCAN_STEP = (3, 2, 1, 1, 1, 1)
HG_SHAPE = (SSM_GROUPS, SSM_STATE, GROUP_WIDTH)
S_SHAPE = (HG_HEADS, HG_DIM, HG_DIM)
H_SHAPE = (SSM_HEADS, SSM_HEAD_DIM, SSM_STATE)
C_SHAPE = (PAD_ROWS, CONV_DIM)


def _scan_kernel(n_pad, tb, project, *refs):
    if project:
        h1_ref, nm_ref, win_ref = refs[:3]
        refs = refs[3:]
    else:
        u_ref = refs[0]
        refs = refs[1:]
    param_refs = refs[:N_MIXER_PARAMS]
    s0_ref, h0_ref, c0_ref, mix_ref, s_out_ref, h_out_ref, c_out_ref = refs[N_MIXER_PARAMS:N_MIXER_PARAMS + 7]
    scratch = refs[N_MIXER_PARAMS + 7:]
    if project:
        uh_scr, us_scr, st_scr, hg_scr, xpad, o_scr = scratch
    else:
        st_scr, hg_scr, xpad, o_scr = scratch
    t = pl.program_id(1)

    @pl.when(t == 0)
    def _():
        _load_state(s0_ref, h0_ref, c0_ref, st_scr, hg_scr, xpad)

    if project:
        hn = _rms(h1_ref[...], nm_ref[...]).astype(BF16)
        uh_scr[...] = _dot(hn, win_ref[:, 0:COL_Z])
        us_scr[...] = _dot(hn, win_ref[:, COL_Z:])
        _mixer_block(n_pad, tb, uh_scr, us_scr, 0, param_refs, mix_ref, st_scr, hg_scr, xpad, o_scr)
    else:
        _mixer_block(n_pad, tb, u_ref, u_ref, COL_Z, param_refs, mix_ref, st_scr, hg_scr, xpad, o_scr)

    @pl.when(t == pl.num_programs(1) - 1)
    def _():
        _store_state(s_out_ref, h_out_ref, c_out_ref, st_scr, hg_scr, xpad)


def _scan_pipelined_kernel(tb, nt, h1_ref, nm_ref, win_ref, *refs):
    param_refs = refs[:N_MIXER_PARAMS]
    s0_ref, h0_ref, c0_ref, mix_ref, s_out_ref, h_out_ref, c_out_ref = refs[N_MIXER_PARAMS:N_MIXER_PARAMS + 7]
    even, odd = refs[N_MIXER_PARAMS + 7:N_MIXER_PARAMS + 11], refs[N_MIXER_PARAMS + 11:N_MIXER_PARAMS + 15]
    st_scr, hg_scr, xpad, o_scr = refs[N_MIXER_PARAMS + 15:]
    s = pl.program_id(0)
    scanned = jnp.maximum(s - 1, 0)

    @pl.when(s == 0)
    def _():
        uh, us, b, worst = odd
        uh[...] = jnp.zeros(uh.shape, F32)
        us[...] = jnp.zeros(us.shape, F32)
        b[...] = jnp.zeros(b.shape, F32)
        worst[0] = 0.0

    @pl.when(scanned % nt == 0)
    def _():
        _load_state(s0_ref, h0_ref, c0_ref, st_scr, hg_scr, xpad)

    def project_steps(uh_w, us_w, b_w, worst_w):
        hn = _rms(h1_ref[...], nm_ref[...]).astype(BF16)
        for c0 in range(0, D_IN_PAD, PROJ_TILE):
            c1 = min(c0 + PROJ_TILE, D_IN_PAD)
            yield
            tile = _dot(hn, win_ref[:, c0:c1])
            if c0 < COL_Z:
                uh_w[:, c0:c1] = tile
            else:
                us_w[:, c0 - COL_Z:c1 - COL_Z] = tile
            if c0 == COL_F:
                assert c1 == COL_F + HG_WIDTH
                lb = _forget_lower_bound(param_refs[0][...])
                b, worst = _hgrn_log_decay(0, tb, tile, lb, param_refs[N_MIXER_PARAMS - 2][...])
                b_w[...] = b
                worst_w[0] = worst

    def body(write, read):
        uh_r, us_r, b_r, worst_r = read
        _interleave(_mixer_steps(0, tb, uh_r, us_r, 0, param_refs, mix_ref, st_scr, hg_scr, xpad, o_scr,
                                 (b_r, worst_r)),
                    project_steps(*write), PROJ_STEPS_AFTER_SCAN_STEP)

    pl.when(s % 2 == 0)(lambda: body(even, odd))
    pl.when(s % 2 == 1)(lambda: body(odd, even))

    @pl.when((s >= 1) & (scanned % nt == nt - 1))
    def _():
        _store_state(s_out_ref, h_out_ref, c_out_ref, st_scr, hg_scr, xpad)


def _scan_pipelined(h1, nb, nt, tb, proj, params, s0, h0, c0):
    n_blocks = nb * nt
    last = n_blocks - 1
    lead = tuple(proj) + _scan_params(params, tb)
    shared = lambda shape: pl.BlockSpec((1,) + shape[1:], lambda s: (0,) * len(shape))
    per_seq = lambda shape: pl.BlockSpec((1,) + shape, lambda s: (jnp.maximum(s - 1, 0) // nt,) + (0,) * len(shape))
    u_bufs = [pltpu.VMEM((tb, COL_Z), F32), pltpu.VMEM((tb, D_IN_PAD - COL_Z), F32),
              pltpu.VMEM((tb, HG_WIDTH), F32), pltpu.SMEM((1,), F32)]
    return pl.pallas_call(
        functools.partial(_scan_pipelined_kernel, tb, nt),
        grid=(n_blocks + 1,),
        in_specs=[pl.BlockSpec((tb, D_MODEL), lambda s: (jnp.minimum(s, last), 0))]
        + [_resident(p.shape) for p in lead]
        + [shared(s0.shape), shared(h0.shape), shared(c0.shape)],
        out_specs=[pl.BlockSpec((tb, D_MODEL), lambda s: (jnp.maximum(s - 1, 0), 0)),
                   per_seq(S_SHAPE), per_seq(H_SHAPE), per_seq(C_SHAPE)],
        out_shape=[jax.ShapeDtypeStruct((n_blocks * tb, D_MODEL), BF16),
                   jax.ShapeDtypeStruct((nb,) + S_SHAPE, F32),
                   jax.ShapeDtypeStruct((nb,) + H_SHAPE, F32),
                   jax.ShapeDtypeStruct((nb,) + C_SHAPE, F32)],
        scratch_shapes=u_bufs + u_bufs + [pltpu.VMEM(S_SHAPE, F32), pltpu.VMEM(HG_SHAPE, F32),
                                          pltpu.VMEM((PAD_ROWS + tb, CONV_DIM), F32),
                                          pltpu.VMEM((tb, HG_WIDTH), F32)],
        compiler_params=_compiler_params(1),
        name="proj_mixer",
    )(h1, *lead, s0, h0, c0)


def _scan(src, src_block0, nb, nt, tb, n_pad, proj, params, s0, h0, c0):
    lead = (tuple(proj) if proj else ()) + _scan_params(params, tb)
    shared = lambda shape: pl.BlockSpec((1,) + shape[1:], lambda b, t: (0,) * len(shape))
    per_seq = lambda shape: pl.BlockSpec((1,) + shape, lambda b, t: (b,) + (0,) * len(shape))
    scratch = [pltpu.VMEM(S_SHAPE, F32), pltpu.VMEM(HG_SHAPE, F32), pltpu.VMEM((PAD_ROWS + tb, CONV_DIM), F32),
               pltpu.VMEM((tb, HG_WIDTH), F32)]
    if proj:
        scratch = [pltpu.VMEM((tb, COL_Z), F32), pltpu.VMEM((tb, D_IN_PAD - COL_Z), F32)] + scratch
    return pl.pallas_call(
        functools.partial(_scan_kernel, n_pad, tb, bool(proj)),
        grid=(nb, nt),
        in_specs=[pl.BlockSpec((tb, src.shape[1]), lambda b, t: (src_block0 + b * nt + t, 0))]
        + [_resident(p.shape) for p in lead]
        + [shared(s0.shape), shared(h0.shape), shared(c0.shape)],
        out_specs=[pl.BlockSpec((tb, D_MODEL), lambda b, t: (b * nt + t, 0)),
                   per_seq(S_SHAPE), per_seq(H_SHAPE), per_seq(C_SHAPE)],
        out_shape=[jax.ShapeDtypeStruct((nb * nt * tb, D_MODEL), BF16),
                   jax.ShapeDtypeStruct((nb,) + S_SHAPE, F32),
                   jax.ShapeDtypeStruct((nb,) + H_SHAPE, F32),
                   jax.ShapeDtypeStruct((nb,) + C_SHAPE, F32)],
        scratch_shapes=scratch,
        compiler_params=_compiler_params(2),
        name="proj_mixer" if proj else "mixer",
    )(src, *lead, s0, h0, c0)


def _sample_kernel(nbs, u_ref, lbl_ref, hgn_ref, cw_ref, cb_ref, dtb_ref, alog_ref, dsk_ref, ssn_ref,
                   sh_ref, ss_ref, sc_ref, mix_ref, sh_out_ref, ss_out_ref, sc_out_ref, obuf, ybuf):
    lb = _forget_lower_bound(lbl_ref[...])
    fz = u_ref[:, COL_F:COL_F + HG_WIDTH]
    f = lb + (1.0 - lb) * jax.nn.sigmoid(fz)
    kk = (1.0 - lb) * jax.nn.sigmoid(-fz)
    q = _silu(u_ref[:, COL_Q:COL_Q + HG_WIDTH])
    v = u_ref[:, COL_I:COL_I + HG_WIDTH]

    xbc = u_ref[:, COL_XBC:COL_XBC + CONV_DIM]
    cw = cw_ref[...]
    conv = cb_ref[...] + cw[CONV_WIDTH - 1:CONV_WIDTH, :] * xbc
    for j in range(CONV_WIDTH - 1):
        conv = conv + cw[j:j + 1, :] * sc_ref[j]
    for j in range(CONV_WIDTH - 2):
        sc_out_ref[j] = sc_ref[j + 1]
    sc_out_ref[CONV_WIDTH - 2] = xbc
    act = _silu(conv)
    xs = act[:, 0:SSM_WIDTH]
    bm = act[:, SSM_WIDTH:SSM_WIDTH + SSM_GROUPS * SSM_STATE]
    cm = act[:, SSM_WIDTH + SSM_GROUPS * SSM_STATE:]
    dt = jax.nn.softplus(u_ref[:, COL_DT:COL_DT + LANES] + dtb_ref[...])
    d_a = jnp.exp(dt * (-jnp.exp(alog_ref[...])))

    top_half = lax.broadcasted_iota(jnp.int32, (LANES, LANES), 0) < SSM_HEAD_DIM
    for j in range(nbs):
        row = slice(j, j + 1)
        for h in range(HG_HEADS):
            sl = slice(h * HG_DIM, (h + 1) * HG_DIM)
            s_new = sh_ref[j, h] * _column_tile(f[row, sl]) + _column_tile(kk[row, sl]) * v[row, sl]
            sh_out_ref[j, h] = s_new
            obuf[row, sl] = jnp.sum(s_new * _column_tile(q[row, sl]), axis=0, keepdims=True)
        for rp in range(SSM_HEADS // 2):
            r0, r1 = 2 * rp, 2 * rp + 1
            g = r0 // HEADS_PER_GROUP
            sl = slice(rp * LANES, (rp + 1) * LANES)
            gsl = slice(g * SSM_STATE, (g + 1) * SSM_STATE)
            h2 = jnp.concatenate([ss_ref[j, r0], ss_ref[j, r1]], axis=0)
            da2 = jnp.where(top_half, d_a[row, r0:r0 + 1], d_a[row, r1:r1 + 1])
            dt2 = jnp.where(top_half, dt[row, r0:r0 + 1], dt[row, r1:r1 + 1])
            h_new = h2 * da2 + (dt2 * _column_tile(xs[row, sl])) * bm[row, gsl]
            ss_out_ref[j, r0] = h_new[0:SSM_HEAD_DIM]
            ss_out_ref[j, r1] = h_new[SSM_HEAD_DIM:]
            ybuf[row, sl] = jnp.sum((h_new * cm[row, gsl]).T, axis=0, keepdims=True)

    o = obuf[...]
    gate = _silu(u_ref[:, COL_G:COL_G + HG_WIDTH])
    hgn = hgn_ref[...]
    for h in range(HG_HEADS):
        sl = slice(h * HG_DIM, (h + 1) * HG_DIM)
        oh = o[:, sl]
        oh = oh * lax.rsqrt(jnp.mean(oh * oh, axis=-1, keepdims=True) + EPS)
        mix_ref[:, sl] = oh * hgn[:, sl] * gate[:, sl]
    yz = (ybuf[...] + dsk_ref[...] * xs) * _silu(u_ref[:, COL_Z:COL_Z + SSM_WIDTH])
    ssn = ssn_ref[...]
    for g in range(SSM_GROUPS):
        sl = slice(g * GROUP_WIDTH, (g + 1) * GROUP_WIDTH)
        seg = yz[:, sl]
        seg = seg * lax.rsqrt(jnp.mean(seg * seg, axis=-1, keepdims=True) + EPS) * ssn[:, sl]
        mix_ref[:, HG_WIDTH + g * GROUP_WIDTH:HG_WIDTH + (g + 1) * GROUP_WIDTH] = seg


def _sample_mixer(u, u_block0, n, nbs, params, sh, ss, sc_t):
    small = [_resident(p.shape) for p in params]
    return pl.pallas_call(
        functools.partial(_sample_kernel, nbs),
        grid=(n // nbs,),
        in_specs=[pl.BlockSpec((nbs, D_IN_PAD), lambda i: (u_block0 + i, 0))] + small
        + [pl.BlockSpec((nbs,) + sh.shape[1:], lambda i: (i, 0, 0, 0)),
           pl.BlockSpec((nbs,) + ss.shape[1:], lambda i: (i, 0, 0, 0)),
           pl.BlockSpec((CONV_WIDTH - 1, nbs, CONV_DIM), lambda i: (0, i, 0))],
        out_specs=[pl.BlockSpec((nbs, D_MODEL), lambda i: (i, 0)),
                   pl.BlockSpec((nbs,) + sh.shape[1:], lambda i: (i, 0, 0, 0)),
                   pl.BlockSpec((nbs,) + ss.shape[1:], lambda i: (i, 0, 0, 0)),
                   pl.BlockSpec((CONV_WIDTH - 1, nbs, CONV_DIM), lambda i: (0, i, 0))],
        out_shape=[jax.ShapeDtypeStruct((n, D_MODEL), F32),
                   jax.ShapeDtypeStruct(sh.shape, F32),
                   jax.ShapeDtypeStruct(ss.shape, F32),
                   jax.ShapeDtypeStruct(sc_t.shape, F32)],
        scratch_shapes=[pltpu.VMEM((nbs, HG_WIDTH), F32), pltpu.VMEM((nbs, SSM_WIDTH), F32)],
        compiler_params=_compiler_params(1),
        name="sample_mixer",
    )(u, *params, sh, ss, sc_t)


TM_DENSE = 512
TM_SCAN = 512
TM_SMALL = 128
SAMPLES_PER_STEP = 8


def _pad_lanes(row, value=0.0):
    return jnp.pad(row, ((0, 0), (0, LANES - row.shape[1])), constant_values=value)


def kernel(x_prompt, x_sample, state_hgrn, state_ssm, state_conv, meta_tokens, lb_logits, norm_ffn1, w_ffn1_gate, w_ffn1_up, w_ffn1_down, norm_mix, w_in, hg_norm, conv_w, conv_b, dt_bias, a_log, d_skip, ssm_norm, w_out, norm_ffn2, w_ffn2_gate, w_ffn2_up, w_ffn2_down, norm_final):
    bp, seq_p, _ = x_prompt.shape
    n_s = x_sample.shape[0]
    assert x_sample.shape[1] == 1 and n_s == TM_SMALL and seq_p % TM_SCAN == 0 and (bp * seq_p) % TM_DENSE == 0
    layer = 0

    n1, nm, n2 = norm_ffn1[layer][None], norm_mix[layer][None], norm_ffn2[layer][None]
    nf = norm_final[None]
    win =jnp.pad(w_in[layer], ((0, 0), (0, D_IN_PAD - w_in.shape[2]))).astype(BF16)
    wo = w_out[layer].astype(BF16)
    mixer_params = (lb_logits, hg_norm[layer][None], conv_w[layer], conv_b[layer][None],
                    _pad_lanes(dt_bias[layer][None]), _pad_lanes(a_log[layer][None]),
                    jnp.repeat(d_skip[layer], SSM_HEAD_DIM)[None], ssm_norm[layer][None])

    n_pad = TM_SMALL - N_META
    x_small = jnp.concatenate([jnp.zeros((n_pad, D_MODEL), F32), meta_tokens, x_sample[:, 0]], axis=0)
    h1_small, wg1, wu1, wd1 = _ffn1_small(x_small, n1, w_ffn1_gate[layer], w_ffn1_up[layer], w_ffn1_down[layer])
    u_small = _proj_small(h1_small, nm, win)

    zeros_s = jnp.zeros((1,) + S_SHAPE, F32)
    zeros_h = jnp.zeros((1,) + H_SHAPE, F32)
    zeros_c = jnp.zeros((1,) + C_SHAPE, F32)
    _, s_meta, h_meta, c_meta = _scan(u_small, 0, 1, 1, TM_SMALL, n_pad, None, mixer_params,
                                      zeros_s, zeros_h, zeros_c)

    sc_t = jnp.swapaxes(state_conv[layer], 0, 1)
    mix_s, hgrn_s, ssm_s, conv_s_t = _sample_mixer(
        u_small, TM_SMALL // SAMPLES_PER_STEP, n_s, SAMPLES_PER_STEP, mixer_params,
        state_hgrn[layer], state_ssm[layer], sc_t)
    y_s, wg2, wu2, wd2 = _ffn2_small(h1_small, 1, mix_s, wo, n2, w_ffn2_gate[layer], w_ffn2_up[layer],
                                     w_ffn2_down[layer], nf)

    xp = x_prompt.reshape(bp * seq_p, D_MODEL)
    h1_p = _ffn1(xp, n1, wg1, wu1, wd1, TM_DENSE)
    mix_p, hgrn_p, ssm_p, conv_p = _scan_pipelined(h1_p, bp, seq_p // TM_SCAN, TM_SCAN, (nm, win), mixer_params,
                                                   s_meta, h_meta, c_meta)
    y_p = _ffn_out(h1_p, 0, mix_p, wo, n2, wg2, wu2, wd2, nf, TM_DENSE)

    keep = slice(PAD_ROWS - (CONV_WIDTH - 1), PAD_ROWS)
    return (y_p.reshape(bp, seq_p, D_MODEL),
            y_s.reshape(n_s, 1, D_MODEL),
            hgrn_p[None], ssm_p[None], conv_p[:, keep][None],
            hgrn_s[None], ssm_s[None], jnp.swapaxes(conv_s_t, 0, 1)[None])
```

```python
import functools

import jax
import jax.numpy as jnp
from jax import lax
from jax.experimental import pallas as pl
from jax.experimental.pallas import tpu as pltpu

F32 = jnp.float32
BF16 = jnp.bfloat16

D_MODEL = 1024
D_FF = 2816
N_META = 16
HG_WIDTH = 512
HG_HEADS = 4
HG_DIM = 128
SSM_WIDTH = 512
SSM_HEADS = 8
SSM_HEAD_DIM = 64
SSM_GROUPS = 2
SSM_STATE = 128
CONV_WIDTH = 4
CONV_DIM = SSM_WIDTH + 2 * SSM_GROUPS * SSM_STATE
EPS = 1e-6

LANES = 128
SUBLANES = 8
VMEM_LIMIT_BYTES = 56 * 1024 * 1024

COL_Q = 0
COL_F = HG_WIDTH
COL_I = 2 * HG_WIDTH
COL_G = 3 * HG_WIDTH
COL_Z = 4 * HG_WIDTH
COL_XBC = COL_Z + SSM_WIDTH
COL_DT = COL_XBC + CONV_DIM
D_IN_PAD = COL_DT + LANES

FF_TILE = 256
HG_CHUNK = 64
HG_MAX_CHUNK_LOG_DECAY = 80.0
SSM_CHUNK = 128
PAD_ROWS = SUBLANES
HG_PAIRS = HG_HEADS // 2
HEADS_PER_GROUP = SSM_HEADS // SSM_GROUPS
GROUP_WIDTH = SSM_WIDTH // SSM_GROUPS
assert 2 * SSM_HEAD_DIM == LANES and 2 * HG_DIM == FF_TILE


def _dot(a, b):
    return jnp.dot(a, b, preferred_element_type=F32)


def _dot_nt(a, b):
    return lax.dot_general(a, b, (((1,), (1,)), ((), ())), preferred_element_type=F32)


def _dot_tn(a, b):
    return lax.dot_general(a, b, (((0,), (0,)), ((), ())), preferred_element_type=F32)


def _rms(x, w):
    return x * lax.rsqrt(jnp.mean(x * x, axis=-1, keepdims=True) + EPS) * w


def _silu(x):
    return x * jax.nn.sigmoid(x)


def _swiglu(xn, wg_ref, wu_ref, wd_ref):
    acc = jnp.zeros((xn.shape[0], D_MODEL), F32)
    for j in range(D_FF // FF_TILE):
        cols = slice(j * FF_TILE, (j + 1) * FF_TILE)
        g = _dot(xn, wg_ref[:, cols])
        u = _dot(xn, wu_ref[:, cols])
        acc = acc + _dot((_silu(g) * u).astype(BF16), wd_ref[cols, :])
    return acc


def _cumsum_rows(tri, a):
    n = tri.shape[0]
    if a.shape[0] > n:
        return jnp.concatenate([_cumsum_rows(tri, a[r:r + n]) for r in range(0, a.shape[0], n)], axis=0)
    a1 = a.astype(BF16)
    r1 = a - a1.astype(F32)
    a2 = r1.astype(BF16)
    a3 = (r1 - a2.astype(F32)).astype(BF16)
    return _dot(tri, a1) + _dot(tri, a2) + _dot(tri, a3)


def _lower_tri(n):
    row = lax.broadcasted_iota(jnp.int32, (n, n), 0)
    col = lax.broadcasted_iota(jnp.int32, (n, n), 1)
    return row >= col


def _chunked_tri(n, chunk):
    assert chunk & (chunk - 1) == 0
    row = lax.broadcasted_iota(jnp.int32, (n, n), 0)
    col = lax.broadcasted_iota(jnp.int32, (n, n), 1)
    same_chunk = (row ^ col) < chunk
    return ((row >= col) & same_chunk).astype(BF16)


def _tiled_lower_tri(chunk, reps):
    assert chunk & (chunk - 1) == 0
    row = lax.broadcasted_iota(jnp.int32, (chunk, reps * chunk), 0)
    col = lax.broadcasted_iota(jnp.int32, (chunk, reps * chunk), 1)
    return row >= (col & (chunk - 1))


def _block_diag(blocks):
    n = len(blocks)
    r, c = blocks[0].shape
    rows = []
    for i, blk in enumerate(blocks):
        parts = []
        if i:
            parts.append(jnp.zeros((r, c * i), blk.dtype))
        parts.append(blk)
        if i < n - 1:
            parts.append(jnp.zeros((r, c * (n - 1 - i)), blk.dtype))
        rows.append(jnp.concatenate(parts, axis=1))
    return jnp.concatenate(rows, axis=0)


def _forget_lower_bound(lbl):
    l0, l1 = lbl[0:1], lbl[1:2]
    m = jnp.maximum(l0, l1)
    e0, e1 = jnp.exp(l0 - m), jnp.exp(l1 - m)
    return e0 / (e0 + e1)


def _resident(shape):
    nd = len(shape)
    return pl.BlockSpec(shape, lambda *_: (0,) * nd, pipeline_mode=pl.Buffered(1))


def _compiler_params(n_grid_axes, flags=None):
    return pltpu.CompilerParams(dimension_semantics=("arbitrary",) * n_grid_axes,
                                vmem_limit_bytes=VMEM_LIMIT_BYTES, flags=flags)


def _ffn1_kernel(x_ref, n1_ref, wg_ref, wu_ref, wd_ref, h1_ref):
    x = x_ref[...]
    xn = _rms(x, n1_ref[...]).astype(BF16)
    h1_ref[...] = x + 0.5 * _swiglu(xn, wg_ref, wu_ref, wd_ref)


def _ffn1(x, n1, wg, wu, wd, tm):
    n = x.shape[0]
    return pl.pallas_call(
        _ffn1_kernel,
        grid=(n // tm,),
        in_specs=[pl.BlockSpec((tm, D_MODEL), lambda i: (i, 0)),
                  _resident(n1.shape), _resident(wg.shape), _resident(wu.shape), _resident(wd.shape)],
        out_specs=pl.BlockSpec((tm, D_MODEL), lambda i: (i, 0)),
        out_shape=jax.ShapeDtypeStruct((n, D_MODEL), F32),
        compiler_params=_compiler_params(1),
        name="ffn1",
    )(x, n1, wg, wu, wd)


def _ffn_tile_step(xn, wg_ref, wu_ref, wd_ref, wg_out, wu_out, wd_out, acc_scr):
    wg, wu, wd = wg_ref[...].astype(BF16), wu_ref[...].astype(BF16), wd_ref[...].astype(BF16)
    wg_out[...] = wg
    wu_out[...] = wu
    wd_out[...] = wd
    acc_scr[...] += _dot((_silu(_dot(xn, wg)) * _dot(xn, wu)).astype(BF16), wd)


def _ffn1_small_kernel(x_ref, n1_ref, wg_ref, wu_ref, wd_ref, h1_ref, wg_out, wu_out, wd_out, xn_scr, acc_scr):
    j = pl.program_id(0)

    @pl.when(j == 0)
    def _():
        xn_scr[...] = _rms(x_ref[...], n1_ref[...]).astype(BF16)
        acc_scr[...] = jnp.zeros(acc_scr.shape, F32)

    _ffn_tile_step(xn_scr[...], wg_ref, wu_ref, wd_ref, wg_out, wu_out, wd_out, acc_scr)

    @pl.when(j == pl.num_programs(0) - 1)
    def _():
        h1_ref[...] = x_ref[...] + 0.5 * acc_scr[...]


def _ffn2_small_kernel(h1_ref, mix_ref, wo_ref, n2_ref, wg_ref, wu_ref, wd_ref, nf_ref,
                       y_ref, wg_out, wu_out, wd_out, wo_out, h2_scr, hn_scr, acc_scr):
    j = pl.program_id(0)

    @pl.when(j == 0)
    def _():
        wo = wo_ref[...].astype(BF16)
        wo_out[...] = wo
        h2 = h1_ref[...] + _dot(mix_ref[...].astype(BF16), wo)
        h2_scr[...] = h2
        hn_scr[...] = _rms(h2, n2_ref[...]).astype(BF16)
        acc_scr[...] = jnp.zeros(acc_scr.shape, F32)

    _ffn_tile_step(hn_scr[...], wg_ref, wu_ref, wd_ref, wg_out, wu_out, wd_out, acc_scr)

    @pl.when(j == pl.num_programs(0) - 1)
    def _():
        y_ref[...] = _rms(h2_scr[...] + 0.5 * acc_scr[...], nf_ref[...])


def _weight_tile_specs():
    cols = pl.BlockSpec((D_MODEL, FF_TILE), lambda j: (0, j))
    rows = pl.BlockSpec((FF_TILE, D_MODEL), lambda j: (j, 0))
    shapes = [jax.ShapeDtypeStruct((D_MODEL, D_FF), BF16), jax.ShapeDtypeStruct((D_MODEL, D_FF), BF16),
              jax.ShapeDtypeStruct((D_FF, D_MODEL), BF16)]
    return [cols, cols, rows], shapes


def _ffn1_small(x, n1, wg, wu, wd):
    n = x.shape[0]
    wspecs, wshapes = _weight_tile_specs()
    return pl.pallas_call(
        _ffn1_small_kernel,
        grid=(D_FF // FF_TILE,),
        in_specs=[_resident(x.shape), _resident(n1.shape)] + wspecs,
        out_specs=[pl.BlockSpec((n, D_MODEL), lambda j: (0, 0))] + wspecs,
        out_shape=[jax.ShapeDtypeStruct((n, D_MODEL), F32)] + wshapes,
        scratch_shapes=[pltpu.VMEM((n, D_MODEL), BF16), pltpu.VMEM((n, D_MODEL), F32)],
        compiler_params=_compiler_params(1),
        name="ffn1_small",
    )(x, n1, wg, wu, wd)


def _ffn2_small(h1, h1_block, mix, wo, n2, wg, wu, wd, nf):
    n = mix.shape[0]
    wspecs, wshapes = _weight_tile_specs()
    wspecs_out = wspecs + [pl.BlockSpec(wo.shape, lambda j: (0, 0))]
    wshapes = wshapes + [jax.ShapeDtypeStruct(wo.shape, BF16)]
    return pl.pallas_call(
        _ffn2_small_kernel,
        grid=(D_FF // FF_TILE,),
        in_specs=[pl.BlockSpec((n, D_MODEL), lambda j: (h1_block, 0), pipeline_mode=pl.Buffered(1)),
                  _resident(mix.shape), _resident(wo.shape), _resident(n2.shape)] + wspecs
        + [_resident(nf.shape)],
        out_specs=[pl.BlockSpec((n, D_MODEL), lambda j: (0, 0))] + wspecs_out,
        out_shape=[jax.ShapeDtypeStruct((n, D_MODEL), F32)] + wshapes,
        scratch_shapes=[pltpu.VMEM((n, D_MODEL), F32), pltpu.VMEM((n, D_MODEL), BF16),
                        pltpu.VMEM((n, D_MODEL), F32)],
        compiler_params=_compiler_params(1),
        name="ffn2_small",
    )(h1, mix, wo, n2, wg, wu, wd, nf)


W_IN_TILE = 512


def _proj_small_kernel(n_cols, h1_ref, nm_ref, win_ref, u_ref, win_out, hn_scr):
    j = pl.program_id(0)

    @pl.when(j == 0)
    def _():
        hn_scr[...] = _rms(h1_ref[...], nm_ref[...]).astype(BF16)

    col = j * W_IN_TILE + lax.broadcasted_iota(jnp.int32, (1, W_IN_TILE), 1)
    w = jnp.where(col < n_cols, win_ref[...], 0.0).astype(BF16)
    win_out[...] = w
    u_ref[...] = _dot(hn_scr[...], w)


def _proj_small(h1, nm, w_in):
    n = h1.shape[0]
    tile = lambda rows: pl.BlockSpec((rows, W_IN_TILE), lambda j: (0, j))
    return pl.pallas_call(
        functools.partial(_proj_small_kernel, w_in.shape[1]),
        grid=(pl.cdiv(D_IN_PAD, W_IN_TILE),),
        in_specs=[_resident(h1.shape), _resident(nm.shape), tile(D_MODEL)],
        out_specs=[tile(n), tile(D_MODEL)],
        out_shape=[jax.ShapeDtypeStruct((n, D_IN_PAD), F32), jax.ShapeDtypeStruct((D_MODEL, D_IN_PAD), BF16)],
        scratch_shapes=[pltpu.VMEM((n, D_MODEL), BF16)],
        compiler_params=_compiler_params(1),
        name="proj_small",
    )(h1, nm, w_in)


def _ffn_out_kernel(h1_ref, mix_ref, wo_ref, n2_ref, wg_ref, wu_ref, wd_ref, nf_ref, y_ref):
    h2 = h1_ref[...] + _dot(mix_ref[...].astype(BF16), wo_ref[...])
    hn = _rms(h2, n2_ref[...]).astype(BF16)
    h3 = h2 + 0.5 * _swiglu(hn, wg_ref, wu_ref, wd_ref)
    y_ref[...] = _rms(h3, nf_ref[...])


def _ffn_out(h1, h1_block0, mix, wo, n2, wg, wu, wd, nf, tm):
    n = mix.shape[0]
    return pl.pallas_call(
        _ffn_out_kernel,
        grid=(n // tm,),
        in_specs=[pl.BlockSpec((tm, D_MODEL), lambda i: (i + h1_block0, 0)),
                  pl.BlockSpec((tm, D_MODEL), lambda i: (i, 0)),
                  _resident(wo.shape), _resident(n2.shape), _resident(wg.shape), _resident(wu.shape),
                  _resident(wd.shape), _resident(nf.shape)],
        out_specs=pl.BlockSpec((tm, D_MODEL), lambda i: (i, 0)),
        out_shape=jax.ShapeDtypeStruct((n, D_MODEL), F32),
        compiler_params=_compiler_params(1),
        name="ffn_out",
    )(h1, mix, wo, n2, wg, wu, wd, nf)


def _head_slices(a, width):
    return [a[:, i:i + width] for i in range(0, a.shape[1], width)]


def _column_tile(row):
    return jnp.broadcast_to(row, (LANES, LANES)).T


def _store_hgrn_out(rows, o, gate, hgn, mix_ref):
    for h in range(HG_HEADS):
        sl = slice(h * HG_DIM, (h + 1) * HG_DIM)
        oh = o[:, sl]
        oh = oh * lax.rsqrt(jnp.mean(oh * oh, axis=-1, keepdims=True) + EPS)
        mix_ref[rows, sl] = (oh * hgn[:, sl] * gate[:, sl]).astype(mix_ref.dtype)


def _hgrn_log_decay(n_pad, tb, fz, lb, tri):
    logf = jnp.log(lb + (1.0 - lb) * jax.nn.sigmoid(fz))
    if n_pad:
        logf = jnp.where(lax.broadcasted_iota(jnp.int32, (tb, HG_WIDTH), 0) >= n_pad, logf, 0.0)
    b = _cumsum_rows(tri, logf)
    chunk_ends = jnp.concatenate([b[r:r + 1] for r in range(HG_CHUNK - 1, tb, HG_CHUNK)], axis=0)
    return b, jnp.max(-chunk_ends)


def _hgrn_block(n_pad, tb, u_ref, lb, hgn, b, worst_log_decay, mix_ref, st_scr, o_scr):
    chunked_is_safe = worst_log_decay < HG_MAX_CHUNK_LOG_DECAY

    @pl.when(chunked_is_safe)
    def _():
        _hgrn_chunked(n_pad, tb, u_ref, lb, hgn, b, mix_ref, st_scr)

    @pl.when(jnp.logical_not(chunked_is_safe))
    def _():
        _hgrn_per_token(n_pad, tb, u_ref, lb, hgn, mix_ref, st_scr, o_scr)


def _hgrn_per_token(n_pad, tb, u_ref, lb, hgn, mix_ref, st_scr, o_scr):
    for h in range(HG_HEADS):
        st_scr[h] = st_scr[h].T

    def sublane_group(i, carry):
        rows = pl.ds(pl.multiple_of(i * SUBLANES, SUBLANES), SUBLANES)
        fz = u_ref[rows, COL_F:COL_F + HG_WIDTH]
        f = lb + (1.0 - lb) * jax.nn.sigmoid(fz)
        kk = (1.0 - lb) * jax.nn.sigmoid(-fz)
        if n_pad:
            valid = i * SUBLANES + lax.broadcasted_iota(jnp.int32, (SUBLANES, HG_WIDTH), 0) >= n_pad
            f = jnp.where(valid, f, 1.0)
            kk = jnp.where(valid, kk, 0.0)
        q = _silu(u_ref[rows, COL_Q:COL_Q + HG_WIDTH])
        v = u_ref[rows, COL_I:COL_I + HG_WIDTH]
        for h in range(HG_HEADS):
            sl = slice(h * HG_DIM, (h + 1) * HG_DIM)
            s = st_scr[h]
            o_rows = []
            for j in range(SUBLANES):
                r = slice(j, j + 1)
                s = s * _column_tile(f[r, sl]) + _column_tile(kk[r, sl]) * v[r, sl]
                o_rows.append(jnp.sum(s * _column_tile(q[r, sl]), axis=0, keepdims=True))
            st_scr[h] = s
            o_scr[rows, sl] = jnp.concatenate(o_rows, axis=0)
        return carry

    lax.fori_loop(0, tb // SUBLANES, sublane_group, 0)
    for h in range(HG_HEADS):
        st_scr[h] = st_scr[h].T
    _store_hgrn_out(slice(0, tb), o_scr[...], _silu(u_ref[:, COL_G:COL_G + HG_WIDTH]), hgn, mix_ref)


def _hgrn_chunked(n_pad, tb, u_ref, lb, hgn, b, mix_ref, st_scr):
    c = HG_CHUNK
    kk = (1.0 - lb) * jax.nn.sigmoid(-u_ref[:, COL_F:COL_F + HG_WIDTH])
    if n_pad:
        kk = jnp.where(lax.broadcasted_iota(jnp.int32, (tb, HG_WIDTH), 0) >= n_pad, kk, 0.0)
    q = _silu(u_ref[:, COL_Q:COL_Q + HG_WIDTH])
    v = u_ref[:, COL_I:COL_I + HG_WIDTH].astype(BF16)
    gate = _silu(u_ref[:, COL_G:COL_G + HG_WIDTH])
    qt = (q * jnp.exp(b)).astype(BF16)
    kt = (kk * jnp.exp(-b)).astype(BF16)
    causal = _tiled_lower_tri(c, HG_HEADS)

    for r0 in range(0, tb, c):
        rows = slice(r0, r0 + c)
        b_c = b[rows]
        b_last = b_c[c - 1:c, :]
        kh_c = (kk[rows] * jnp.exp(b_last - b_c)).astype(BF16)
        decay = jnp.exp(b_last)
        qt_c = qt[rows]
        kd = _block_diag(_head_slices(kt[rows], HG_DIM))
        vd = _block_diag(_head_slices(v[rows], HG_DIM))
        scores = jnp.where(causal, _dot_nt(qt_c, kd), 0.0).astype(BF16)
        o = _dot(scores, vd)
        o_prev = []
        for p in range(HG_PAIRS):
            lanes = slice(p * 2 * HG_DIM, (p + 1) * 2 * HG_DIM)
            heads = (2 * p, 2 * p + 1)
            st = [st_scr[h] for h in heads]
            o_prev.append(_dot_nt(qt_c[:, lanes], _block_diag([s.astype(BF16) for s in st])))
            upd = _dot_tn(vd[p * 2 * c:(p + 1) * 2 * c, lanes], jnp.concatenate([kh_c[:, lanes]] * 2, axis=0))
            for i, h in enumerate(heads):
                blk = slice(i * HG_DIM, (i + 1) * HG_DIM)
                st_scr[h] = st[i] * decay[:, h * HG_DIM:(h + 1) * HG_DIM] + upd[blk, blk]
        _store_hgrn_out(rows, o + jnp.concatenate(o_prev, axis=1), gate[rows], hgn, mix_ref)


def _pair_columns(a, r0, r1, first_half):
    shape = (a.shape[0], LANES)
    return jnp.where(first_half, jnp.broadcast_to(a[:, r0:r0 + 1], shape), jnp.broadcast_to(a[:, r1:r1 + 1], shape))


def _group_columns(a, g, first_half):
    r = g * HEADS_PER_GROUP
    return jnp.concatenate([_pair_columns(a, r + i, r + i + 1, first_half)
                            for i in range(0, HEADS_PER_GROUP, 2)], axis=1)


def _ssd_block(n_pad, tb, u_ref, col0, cw, cb, dtb, a_neg, dsk, ssn, tri, mix_ref, hg_scr, xpad):
    c = SSM_CHUNK
    conv = cb
    for j in range(CONV_WIDTH):
        off = PAD_ROWS - (CONV_WIDTH - 1) + j
        conv = conv + cw[j:j + 1, :] * xpad[off:off + tb, :]
    act = _silu(conv)
    xs = act[:, 0:SSM_WIDTH]
    bm = act[:, SSM_WIDTH:SSM_WIDTH + SSM_GROUPS * SSM_STATE].astype(BF16)
    cm = act[:, SSM_WIDTH + SSM_GROUPS * SSM_STATE:].astype(BF16)
    z_gate = _silu(u_ref[:, col0:col0 + SSM_WIDTH])

    col_dt = col0 + COL_DT - COL_Z
    dt = jax.nn.softplus(u_ref[:, col_dt:col_dt + LANES] + dtb)
    if n_pad:
        dt = jnp.where(lax.broadcasted_iota(jnp.int32, (tb, LANES), 0) >= n_pad, dt, 0.0)
    cum = _cumsum_rows(tri, dt * a_neg)
    causal = _lower_tri(c)
    first_half = lax.broadcasted_iota(jnp.int32, (1, LANES), 1) < SSM_HEAD_DIM
    zeros = jnp.zeros((c, LANES), BF16)

    for r0 in range(0, tb, c):
        rows = slice(r0, r0 + c)
        cum_c = cum[rows]
        dt_c = dt[rows]
        last = cum_c[c - 1:c, :]
        cum_t = cum_c.T
        dt_t = dt_c.T
        e_cum = jnp.exp(cum_c)
        w_in = dt_c * jnp.exp(last - cum_c)
        e_last = jnp.exp(last)
        for g in range(SSM_GROUPS):
            yield
            glanes = slice(g * GROUP_WIDTH, (g + 1) * GROUP_WIDTH)
            bg = bm[rows, g * SSM_STATE:(g + 1) * SSM_STATE]
            cg = cm[rows, g * SSM_STATE:(g + 1) * SSM_STATE]
            xg = xs[rows, glanes]
            cbt = _dot_nt(cg, bg)
            m_heads = []
            for rr in range(HEADS_PER_GROUP):
                r = g * HEADS_PER_GROUP + rr
                seg = jnp.exp(jnp.where(causal, cum_c[:, r:r + 1] - cum_t[r:r + 1, :], -jnp.inf))
                m_heads.append((cbt * seg * dt_t[r:r + 1, :]).astype(BF16))
            xb = xg.astype(BF16)
            xd_rows = []
            for rr in range(HEADS_PER_GROUP):
                tile = xb[:, (rr // 2) * LANES:(rr // 2 + 1) * LANES]
                tile = jnp.where(first_half if rr % 2 == 0 else ~first_half, tile, jnp.zeros_like(tile))
                xd_rows.append(jnp.concatenate([tile, zeros] if rr < 2 else [zeros, tile], axis=1))
            xd = jnp.concatenate(xd_rows, axis=0)
            hg = hg_scr[g]
            y = (_dot(jnp.concatenate(m_heads, axis=1), xd)
                 + _dot(cg, hg.astype(BF16)) * _group_columns(e_cum, g, first_half))
            xw = (xg * _group_columns(w_in, g, first_half)).astype(BF16)
            hg_scr[g] = hg * _group_columns(e_last, g, first_half) + _dot_tn(bg, xw)
            yz = (y + dsk[:, glanes] * xg) * z_gate[rows, glanes]
            yz = yz * lax.rsqrt(jnp.mean(yz * yz, axis=-1, keepdims=True) + EPS) * ssn[:, glanes]
            mix_ref[rows, HG_WIDTH + g * GROUP_WIDTH:HG_WIDTH + (g + 1) * GROUP_WIDTH] = yz.astype(mix_ref.dtype)


def _interleave(main, side, side_steps):
    for n in side_steps:
        if next(main, StopIteration) is StopIteration:
            break
        for _ in range(n):
            next(side, None)
    for _ in main:
        pass
    for _ in side:
        pass


def _mixer_block(*args):
    for _ in _mixer_steps(*args):
        pass


def _mixer_steps(n_pad, tb, uh_ref, us_ref, us_col0, param_refs, mix_ref, st_scr, hg_scr, xpad, o_scr,
                 log_decay=None):
    lbl_ref, hgn_ref, cw_ref, cb_ref, dtb_ref, alog_ref, dsk_ref, ssn_ref, tri_hg_ref, tri_ssm_ref = param_refs
    lb = _forget_lower_bound(lbl_ref[...])
    if log_decay is None:
        b, worst = _hgrn_log_decay(n_pad, tb, uh_ref[:, COL_F:COL_F + HG_WIDTH], lb, tri_hg_ref[...])
    else:
        b, worst = log_decay[0][...], log_decay[1]
    _hgrn_block(n_pad, tb, uh_ref, lb, hgn_ref[...], b, worst, mix_ref, st_scr, o_scr)
    yield
    col_xbc = us_col0 + COL_XBC - COL_Z
    xbc = us_ref[:, col_xbc:col_xbc + CONV_DIM]
    if n_pad:
        xbc = jnp.where(lax.broadcasted_iota(jnp.int32, (tb, CONV_DIM), 0) >= n_pad, xbc, 0.0)
    xpad[PAD_ROWS:PAD_ROWS + tb, :] = xbc
    yield from _ssd_block(n_pad, tb, us_ref, us_col0, cw_ref[...], cb_ref[...], dtb_ref[...],
                          -jnp.exp(alog_ref[...]), dsk_ref[...], ssn_ref[...], tri_ssm_ref[...],
                          mix_ref, hg_scr, xpad)
    xpad[0:PAD_ROWS, :] = xpad[tb:tb + PAD_ROWS, :]


def _load_state(s0_ref, h0_ref, c0_ref, st_scr, hg_scr, xpad):
    for h in range(HG_HEADS):
        st_scr[h] = s0_ref[0, h].T
    for r in range(0, SSM_HEADS, 2):
        g, lane0 = r // HEADS_PER_GROUP, (r % HEADS_PER_GROUP) * SSM_HEAD_DIM
        hg_scr[g, :, lane0:lane0 + LANES] = jnp.concatenate([h0_ref[0, r], h0_ref[0, r + 1]], axis=0).T
    xpad[0:PAD_ROWS, :] = c0_ref[0]


def _store_state(s_out_ref, h_out_ref, c_out_ref, st_scr, hg_scr, xpad):
    for h in range(HG_HEADS):
        s_out_ref[0, h] = st_scr[h].T
    for r in range(0, SSM_HEADS, 2):
        g, lane0 = r // HEADS_PER_GROUP, (r % HEADS_PER_GROUP) * SSM_HEAD_DIM
        pair = hg_scr[g, :, lane0:lane0 + LANES].T
        h_out_ref[0, r] = pair[0:SSM_HEAD_DIM]
        h_out_ref[0, r + 1] = pair[SSM_HEAD_DIM:]
    c_out_ref[0] = xpad[0:PAD_ROWS, :]


N_MIXER_PARAMS = 10


def _scan_params(params, tb):
    n = min(tb, FF_TILE)
    return tuple(params) + (_chunked_tri(n, HG_CHUNK), _chunked_tri(n, SSM_CHUNK))


PROJ_TILE = 512
PROJ_STEPS_AFTER_SCAN_STEP = (3, 2, 1, 1, 1, 1)
HG_SHAPE = (SSM_GROUPS, SSM_STATE, GROUP_WIDTH)
S_SHAPE = (HG_HEADS, HG_DIM, HG_DIM)
H_SHAPE = (SSM_HEADS, SSM_HEAD_DIM, SSM_STATE)
C_SHAPE = (PAD_ROWS, CONV_DIM)


def _scan_kernel(n_pad, tb, project, *refs):
    if project:
        h1_ref, nm_ref, win_ref = refs[:3]
        refs = refs[3:]
    else:
        u_ref = refs[0]
        refs = refs[1:]
    param_refs = refs[:N_MIXER_PARAMS]
    s0_ref, h0_ref, c0_ref, mix_ref, s_out_ref, h_out_ref, c_out_ref = refs[N_MIXER_PARAMS:N_MIXER_PARAMS + 7]
    scratch = refs[N_MIXER_PARAMS + 7:]
    if project:
        uh_scr, us_scr, st_scr, hg_scr, xpad, o_scr = scratch
    else:
        st_scr, hg_scr, xpad, o_scr = scratch
    t = pl.program_id(1)

    @pl.when(t == 0)
    def _():
        _load_state(s0_ref, h0_ref, c0_ref, st_scr, hg_scr, xpad)

    if project:
        hn = _rms(h1_ref[...], nm_ref[...]).astype(BF16)
        uh_scr[...] = _dot(hn, win_ref[:, 0:COL_Z])
        us_scr[...] = _dot(hn, win_ref[:, COL_Z:])
        _mixer_block(n_pad, tb, uh_scr, us_scr, 0, param_refs, mix_ref, st_scr, hg_scr, xpad, o_scr)
    else:
        _mixer_block(n_pad, tb, u_ref, u_ref, COL_Z, param_refs, mix_ref, st_scr, hg_scr, xpad, o_scr)

    @pl.when(t == pl.num_programs(1) - 1)
    def _():
        _store_state(s_out_ref, h_out_ref, c_out_ref, st_scr, hg_scr, xpad)


def _scan_pipelined_kernel(tb, nt, h1_ref, nm_ref, win_ref, *refs):
    param_refs = refs[:N_MIXER_PARAMS]
    s0_ref, h0_ref, c0_ref, mix_ref, s_out_ref, h_out_ref, c_out_ref = refs[N_MIXER_PARAMS:N_MIXER_PARAMS + 7]
    even, odd = refs[N_MIXER_PARAMS + 7:N_MIXER_PARAMS + 11], refs[N_MIXER_PARAMS + 11:N_MIXER_PARAMS + 15]
    st_scr, hg_scr, xpad, o_scr = refs[N_MIXER_PARAMS + 15:]
    s = pl.program_id(0)
    scanned = jnp.maximum(s - 1, 0)

    @pl.when(s == 0)
    def _():
        uh, us, b, worst = odd
        uh[...] = jnp.zeros(uh.shape, F32)
        us[...] = jnp.zeros(us.shape, F32)
        b[...] = jnp.zeros(b.shape, F32)
        worst[0] = 0.0

    @pl.when(scanned % nt == 0)
    def _():
        _load_state(s0_ref, h0_ref, c0_ref, st_scr, hg_scr, xpad)

    def project_steps(uh_w, us_w, b_w, worst_w):
        hn = _rms(h1_ref[...], nm_ref[...]).astype(BF16)
        for c0 in range(0, D_IN_PAD, PROJ_TILE):
            c1 = min(c0 + PROJ_TILE, D_IN_PAD)
            yield
            tile = _dot(hn, win_ref[:, c0:c1])
            if c0 < COL_Z:
                uh_w[:, c0:c1] = tile
            else:
                us_w[:, c0 - COL_Z:c1 - COL_Z] = tile
            if c0 == COL_F:
                assert c1 == COL_F + HG_WIDTH
                lb = _forget_lower_bound(param_refs[0][...])
                b, worst = _hgrn_log_decay(0, tb, tile, lb, param_refs[N_MIXER_PARAMS - 2][...])
                b_w[...] = b
                worst_w[0] = worst

    def body(write, read):
        uh_r, us_r, b_r, worst_r = read
        _interleave(_mixer_steps(0, tb, uh_r, us_r, 0, param_refs, mix_ref, st_scr, hg_scr, xpad, o_scr,
                                 (b_r, worst_r[0])),
                    project_steps(*write), PROJ_STEPS_AFTER_SCAN_STEP)

    pl.when(s % 2 == 0)(lambda: body(even, odd))
    pl.when(s % 2 == 1)(lambda: body(odd, even))

    @pl.when((s >= 1) & (scanned % nt == nt - 1))
    def _():
        _store_state(s_out_ref, h_out_ref, c_out_ref, st_scr, hg_scr, xpad)


def _scan_pipelined(h1, nb, nt, tb, proj, params, s0, h0, c0):
    n_blocks = nb * nt
    last = n_blocks - 1
    lead = tuple(proj) + _scan_params(params, tb)
    shared = lambda shape: pl.BlockSpec((1,) + shape[1:], lambda s: (0,) * len(shape))
    per_seq = lambda shape: pl.BlockSpec((1,) + shape, lambda s: (jnp.maximum(s - 1, 0) // nt,) + (0,) * len(shape))
    u_bufs = [pltpu.VMEM((tb, COL_Z), F32), pltpu.VMEM((tb, D_IN_PAD - COL_Z), F32),
              pltpu.VMEM((tb, HG_WIDTH), F32), pltpu.SMEM((1,), F32)]
    return pl.pallas_call(
        functools.partial(_scan_pipelined_kernel, tb, nt),
        grid=(n_blocks + 1,),
        in_specs=[pl.BlockSpec((tb, D_MODEL), lambda s: (jnp.minimum(s, last), 0))]
        + [_resident(p.shape) for p in lead]
        + [shared(s0.shape), shared(h0.shape), shared(c0.shape)],
        out_specs=[pl.BlockSpec((tb, D_MODEL), lambda s: (jnp.maximum(s - 1, 0), 0)),
                   per_seq(S_SHAPE), per_seq(H_SHAPE), per_seq(C_SHAPE)],
        out_shape=[jax.ShapeDtypeStruct((n_blocks * tb, D_MODEL), BF16),
                   jax.ShapeDtypeStruct((nb,) + S_SHAPE, F32),
                   jax.ShapeDtypeStruct((nb,) + H_SHAPE, F32),
                   jax.ShapeDtypeStruct((nb,) + C_SHAPE, F32)],
        scratch_shapes=u_bufs + u_bufs + [pltpu.VMEM(S_SHAPE, F32), pltpu.VMEM(HG_SHAPE, F32),
                                          pltpu.VMEM((PAD_ROWS + tb, CONV_DIM), F32),
                                          pltpu.VMEM((tb, HG_WIDTH), F32)],
        compiler_params=_compiler_params(1),
        name="proj_mixer",
    )(h1, *lead, s0, h0, c0)


def _scan(src, src_block0, nb, nt, tb, n_pad, proj, params, s0, h0, c0):
    lead = (tuple(proj) if proj else ()) + _scan_params(params, tb)
    shared = lambda shape: pl.BlockSpec((1,) + shape[1:], lambda b, t: (0,) * len(shape))
    per_seq = lambda shape: pl.BlockSpec((1,) + shape, lambda b, t: (b,) + (0,) * len(shape))
    scratch = [pltpu.VMEM(S_SHAPE, F32), pltpu.VMEM(HG_SHAPE, F32), pltpu.VMEM((PAD_ROWS + tb, CONV_DIM), F32),
               pltpu.VMEM((tb, HG_WIDTH), F32)]
    if proj:
        scratch = [pltpu.VMEM((tb, COL_Z), F32), pltpu.VMEM((tb, D_IN_PAD - COL_Z), F32)] + scratch
    return pl.pallas_call(
        functools.partial(_scan_kernel, n_pad, tb, bool(proj)),
        grid=(nb, nt),
        in_specs=[pl.BlockSpec((tb, src.shape[1]), lambda b, t: (src_block0 + b * nt + t, 0))]
        + [_resident(p.shape) for p in lead]
        + [shared(s0.shape), shared(h0.shape), shared(c0.shape)],
        out_specs=[pl.BlockSpec((tb, D_MODEL), lambda b, t: (b * nt + t, 0)),
                   per_seq(S_SHAPE), per_seq(H_SHAPE), per_seq(C_SHAPE)],
        out_shape=[jax.ShapeDtypeStruct((nb * nt * tb, D_MODEL), BF16),
                   jax.ShapeDtypeStruct((nb,) + S_SHAPE, F32),
                   jax.ShapeDtypeStruct((nb,) + H_SHAPE, F32),
                   jax.ShapeDtypeStruct((nb,) + C_SHAPE, F32)],
        scratch_shapes=scratch,
        compiler_params=_compiler_params(2),
        name="proj_mixer" if proj else "mixer",
    )(src, *lead, s0, h0, c0)


def _sample_kernel(nbs, u_ref, lbl_ref, hgn_ref, cw_ref, cb_ref, dtb_ref, alog_ref, dsk_ref, ssn_ref,
                   sh_ref, ss_ref, sc_ref, mix_ref, sh_out_ref, ss_out_ref, sc_out_ref, obuf, ybuf):
    lb = _forget_lower_bound(lbl_ref[...])
    fz = u_ref[:, COL_F:COL_F + HG_WIDTH]
    f = lb + (1.0 - lb) * jax.nn.sigmoid(fz)
    kk = (1.0 - lb) * jax.nn.sigmoid(-fz)
    q = _silu(u_ref[:, COL_Q:COL_Q + HG_WIDTH])
    v = u_ref[:, COL_I:COL_I + HG_WIDTH]

    xbc = u_ref[:, COL_XBC:COL_XBC + CONV_DIM]
    cw = cw_ref[...]
    conv = cb_ref[...] + cw[CONV_WIDTH - 1:CONV_WIDTH, :] * xbc
    for j in range(CONV_WIDTH - 1):
        conv = conv + cw[j:j + 1, :] * sc_ref[j]
    for j in range(CONV_WIDTH - 2):
        sc_out_ref[j] = sc_ref[j + 1]
    sc_out_ref[CONV_WIDTH - 2] = xbc
    act = _silu(conv)
    xs = act[:, 0:SSM_WIDTH]
    bm = act[:, SSM_WIDTH:SSM_WIDTH + SSM_GROUPS * SSM_STATE]
    cm = act[:, SSM_WIDTH + SSM_GROUPS * SSM_STATE:]
    dt = jax.nn.softplus(u_ref[:, COL_DT:COL_DT + LANES] + dtb_ref[...])
    d_a = jnp.exp(dt * (-jnp.exp(alog_ref[...])))

    top_half = lax.broadcasted_iota(jnp.int32, (LANES, LANES), 0) < SSM_HEAD_DIM
    for j in range(nbs):
        row = slice(j, j + 1)
        for h in range(HG_HEADS):
            sl = slice(h * HG_DIM, (h + 1) * HG_DIM)
            s_new = sh_ref[j, h] * _column_tile(f[row, sl]) + _column_tile(kk[row, sl]) * v[row, sl]
            sh_out_ref[j, h] = s_new
            obuf[row, sl] = jnp.sum(s_new * _column_tile(q[row, sl]), axis=0, keepdims=True)
        for rp in range(SSM_HEADS // 2):
            r0, r1 = 2 * rp, 2 * rp + 1
            g = r0 // HEADS_PER_GROUP
            sl = slice(rp * LANES, (rp + 1) * LANES)
            gsl = slice(g * SSM_STATE, (g + 1) * SSM_STATE)
            h2 = jnp.concatenate([ss_ref[j, r0], ss_ref[j, r1]], axis=0)
            da2 = jnp.where(top_half, d_a[row, r0:r0 + 1], d_a[row, r1:r1 + 1])
            dt2 = jnp.where(top_half, dt[row, r0:r0 + 1], dt[row, r1:r1 + 1])
            h_new = h2 * da2 + (dt2 * _column_tile(xs[row, sl])) * bm[row, gsl]
            ss_out_ref[j, r0] = h_new[0:SSM_HEAD_DIM]
            ss_out_ref[j, r1] = h_new[SSM_HEAD_DIM:]
            ybuf[row, sl] = jnp.sum((h_new * cm[row, gsl]).T, axis=0, keepdims=True)

    o = obuf[...]
    gate = _silu(u_ref[:, COL_G:COL_G + HG_WIDTH])
    hgn = hgn_ref[...]
    for h in range(HG_HEADS):
        sl = slice(h * HG_DIM, (h + 1) * HG_DIM)
        oh = o[:, sl]
        oh = oh * lax.rsqrt(jnp.mean(oh * oh, axis=-1, keepdims=True) + EPS)
        mix_ref[:, sl] = oh * hgn[:, sl] * gate[:, sl]
    yz = (ybuf[...] + dsk_ref[...] * xs) * _silu(u_ref[:, COL_Z:COL_Z + SSM_WIDTH])
    ssn = ssn_ref[...]
    for g in range(SSM_GROUPS):
        sl = slice(g * GROUP_WIDTH, (g + 1) * GROUP_WIDTH)
        seg = yz[:, sl]
        seg = seg * lax.rsqrt(jnp.mean(seg * seg, axis=-1, keepdims=True) + EPS) * ssn[:, sl]
        mix_ref[:, HG_WIDTH + g * GROUP_WIDTH:HG_WIDTH + (g + 1) * GROUP_WIDTH] = seg


def _sample_mixer(u, u_block0, n, nbs, params, sh, ss, sc_t):
    small = [_resident(p.shape) for p in params]
    return pl.pallas_call(
        functools.partial(_sample_kernel, nbs),
        grid=(n // nbs,),
        in_specs=[pl.BlockSpec((nbs, D_IN_PAD), lambda i: (u_block0 + i, 0))] + small
        + [pl.BlockSpec((nbs,) + sh.shape[1:], lambda i: (i, 0, 0, 0)),
           pl.BlockSpec((nbs,) + ss.shape[1:], lambda i: (i, 0, 0, 0)),
           pl.BlockSpec((CONV_WIDTH - 1, nbs, CONV_DIM), lambda i: (0, i, 0))],
        out_specs=[pl.BlockSpec((nbs, D_MODEL), lambda i: (i, 0)),
                   pl.BlockSpec((nbs,) + sh.shape[1:], lambda i: (i, 0, 0, 0)),
                   pl.BlockSpec((nbs,) + ss.shape[1:], lambda i: (i, 0, 0, 0)),
                   pl.BlockSpec((CONV_WIDTH - 1, nbs, CONV_DIM), lambda i: (0, i, 0))],
        out_shape=[jax.ShapeDtypeStruct((n, D_MODEL), F32),
                   jax.ShapeDtypeStruct(sh.shape, F32),
                   jax.ShapeDtypeStruct(ss.shape, F32),
                   jax.ShapeDtypeStruct(sc_t.shape, F32)],
        scratch_shapes=[pltpu.VMEM((nbs, HG_WIDTH), F32), pltpu.VMEM((nbs, SSM_WIDTH), F32)],
        compiler_params=_compiler_params(1),
        name="sample_mixer",
    )(u, *params, sh, ss, sc_t)


TM_DENSE = 512
TM_SCAN = 512
TM_SMALL = 128
SAMPLES_PER_STEP = 8


def _pad_lanes(row, value=0.0):
    return jnp.pad(row, ((0, 0), (0, LANES - row.shape[1])), constant_values=value)


def kernel(x_prompt, x_sample, state_hgrn, state_ssm, state_conv, meta_tokens, lb_logits, norm_ffn1, w_ffn1_gate, w_ffn1_up, w_ffn1_down, norm_mix, w_in, hg_norm, conv_w, conv_b, dt_bias, a_log, d_skip, ssm_norm, w_out, norm_ffn2, w_ffn2_gate, w_ffn2_up, w_ffn2_down, norm_final):
    bp, seq_p, _ = x_prompt.shape
    n_s = x_sample.shape[0]
    assert x_sample.shape[1] == 1 and n_s == TM_SMALL and seq_p % TM_SCAN == 0 and (bp * seq_p) % TM_DENSE == 0
    layer = 0

    n1, nm, n2 = norm_ffn1[layer][None], norm_mix[layer][None], norm_ffn2[layer][None]
    nf = norm_final[None]
    mixer_params = (lb_logits, hg_norm[layer][None], conv_w[layer], conv_b[layer][None],
                    _pad_lanes(dt_bias[layer][None]), _pad_lanes(a_log[layer][None]),
                    jnp.repeat(d_skip[layer], SSM_HEAD_DIM)[None], ssm_norm[layer][None])

    n_pad = TM_SMALL - N_META
    x_small = jnp.concatenate([jnp.zeros((n_pad, D_MODEL), F32), meta_tokens, x_sample[:, 0]], axis=0)
    h1_small, wg1, wu1, wd1 = _ffn1_small(x_small, n1, w_ffn1_gate[layer], w_ffn1_up[layer], w_ffn1_down[layer])
    u_small, win = _proj_small(h1_small, nm, w_in[layer])

    zeros_s = jnp.zeros((1,) + S_SHAPE, F32)
    zeros_h = jnp.zeros((1,) + H_SHAPE, F32)
    zeros_c = jnp.zeros((1,) + C_SHAPE, F32)
    _, s_meta, h_meta, c_meta = _scan(u_small, 0, 1, 1, TM_SMALL, n_pad, None, mixer_params,
                                      zeros_s, zeros_h, zeros_c)

    sc_t = jnp.swapaxes(state_conv[layer], 0, 1)
    mix_s, hgrn_s, ssm_s, conv_s_t = _sample_mixer(
        u_small, TM_SMALL // SAMPLES_PER_STEP, n_s, SAMPLES_PER_STEP, mixer_params,
        state_hgrn[layer], state_ssm[layer], sc_t)
    y_s, wg2, wu2, wd2, wo = _ffn2_small(h1_small, 1, mix_s, w_out[layer], n2, w_ffn2_gate[layer],
                                         w_ffn2_up[layer], w_ffn2_down[layer], nf)

    xp = x_prompt.reshape(bp * seq_p, D_MODEL)
    h1_p = _ffn1(xp, n1, wg1, wu1, wd1, TM_DENSE)
    mix_p, hgrn_p, ssm_p, conv_p = _scan_pipelined(h1_p, bp, seq_p // TM_SCAN, TM_SCAN, (nm, win), mixer_params,
                                                   s_meta, h_meta, c_meta)
    y_p = _ffn_out(h1_p, 0, mix_p, wo, n2, wg2, wu2, wd2, nf, TM_DENSE)

    keep = slice(PAD_ROWS - (CONV_WIDTH - 1), PAD_ROWS)
    return (y_p.reshape(bp, seq_p, D_MODEL),
            y_s.reshape(n_s, 1, D_MODEL),
            hgrn_p[None], ssm_p[None], conv_p[:, keep][None],
            hgrn_s[None], ssm_s[None], jnp.swapaxes(conv_s_t, 0, 1)[None])
```

```python
import functools

import jax
import jax.numpy as jnp
from jax import lax
from jax.experimental import pallas as pl
from jax.experimental.pallas import tpu as pltpu

F32 = jnp.float32
BF16 = jnp.bfloat16

D_MODEL = 1024
D_FF = 2816
N_META = 16
HG_WIDTH = 512
HG_HEADS = 4
HG_DIM = 128
SSM_WIDTH = 512
SSM_HEADS = 8
SSM_HEAD_DIM = 64
SSM_GROUPS = 2
SSM_STATE = 128
CONV_WIDTH = 4
CONV_DIM = SSM_WIDTH + 2 * SSM_GROUPS * SSM_STATE
EPS = 1e-6

LANES = 128
SUBLANES = 8
VMEM_LIMIT_BYTES = 56 * 1024 * 1024

COL_Q = 0
COL_F = HG_WIDTH
COL_I = 2 * HG_WIDTH
COL_G = 3 * HG_WIDTH
COL_Z = 4 * HG_WIDTH
COL_XBC = COL_Z + SSM_WIDTH
COL_DT = COL_XBC + CONV_DIM
D_IN_PAD = COL_DT + LANES

FF_TILE = 256
HG_CHUNK = 64
HG_MAX_CHUNK_LOG_DECAY = 80.0
SSM_CHUNK = 128
PAD_ROWS = SUBLANES
HG_PAIRS = HG_HEADS // 2
HEADS_PER_GROUP = SSM_HEADS // SSM_GROUPS
GROUP_WIDTH = SSM_WIDTH // SSM_GROUPS
assert 2 * SSM_HEAD_DIM == LANES and 2 * HG_DIM == FF_TILE


def _dot(a, b):
    return jnp.dot(a, b, preferred_element_type=F32)


def _dot_nt(a, b):
    return lax.dot_general(a, b, (((1,), (1,)), ((), ())), preferred_element_type=F32)


def _dot_tn(a, b):
    return lax.dot_general(a, b, (((0,), (0,)), ((), ())), preferred_element_type=F32)


def _rms(x, w):
    return x * lax.rsqrt(jnp.mean(x * x, axis=-1, keepdims=True) + EPS) * w


def _silu(x):
    return x * jax.nn.sigmoid(x)


def _swiglu(xn, wg_ref, wu_ref, wd_ref):
    acc = jnp.zeros((xn.shape[0], D_MODEL), F32)
    for j in range(D_FF // FF_TILE):
        cols = slice(j * FF_TILE, (j + 1) * FF_TILE)
        g = _dot(xn, wg_ref[:, cols])
        u = _dot(xn, wu_ref[:, cols])
        acc = acc + _dot((_silu(g) * u).astype(BF16), wd_ref[cols, :])
    return acc


def _cumsum_rows(tri, a):
    n = tri.shape[0]
    if a.shape[0] > n:
        return jnp.concatenate([_cumsum_rows(tri, a[r:r + n]) for r in range(0, a.shape[0], n)], axis=0)
    a1 = a.astype(BF16)
    r1 = a - a1.astype(F32)
    a2 = r1.astype(BF16)
    a3 = (r1 - a2.astype(F32)).astype(BF16)
    return _dot(tri, a1) + _dot(tri, a2) + _dot(tri, a3)


def _lower_tri(n):
    row = lax.broadcasted_iota(jnp.int32, (n, n), 0)
    col = lax.broadcasted_iota(jnp.int32, (n, n), 1)
    return row >= col


def _chunked_tri(n, chunk):
    assert chunk & (chunk - 1) == 0
    row = lax.broadcasted_iota(jnp.int32, (n, n), 0)
    col = lax.broadcasted_iota(jnp.int32, (n, n), 1)
    same_chunk = (row ^ col) < chunk
    return ((row >= col) & same_chunk).astype(BF16)


def _tiled_lower_tri(chunk, reps):
    assert chunk & (chunk - 1) == 0
    row = lax.broadcasted_iota(jnp.int32, (chunk, reps * chunk), 0)
    col = lax.broadcasted_iota(jnp.int32, (chunk, reps * chunk), 1)
    return row >= (col & (chunk - 1))


def _block_diag(blocks):
    n = len(blocks)
    r, c = blocks[0].shape
    rows = []
    for i, blk in enumerate(blocks):
        parts = []
        if i:
            parts.append(jnp.zeros((r, c * i), blk.dtype))
        parts.append(blk)
        if i < n - 1:
            parts.append(jnp.zeros((r, c * (n - 1 - i)), blk.dtype))
        rows.append(jnp.concatenate(parts, axis=1))
    return jnp.concatenate(rows, axis=0)


def _forget_lower_bound(lbl):
    l0, l1 = lbl[0:1], lbl[1:2]
    m = jnp.maximum(l0, l1)
    e0, e1 = jnp.exp(l0 - m), jnp.exp(l1 - m)
    return e0 / (e0 + e1)


def _resident(shape):
    nd = len(shape)
    return pl.BlockSpec(shape, lambda *_: (0,) * nd, pipeline_mode=pl.Buffered(1))


def _compiler_params(n_grid_axes, flags=None):
    return pltpu.CompilerParams(dimension_semantics=("arbitrary",) * n_grid_axes,
                                vmem_limit_bytes=VMEM_LIMIT_BYTES, flags=flags)


def _ffn1_kernel(x_ref, n1_ref, wg_ref, wu_ref, wd_ref, h1_ref):
    x = x_ref[...]
    xn = _rms(x, n1_ref[...]).astype(BF16)
    h1_ref[...] = x + 0.5 * _swiglu(xn, wg_ref, wu_ref, wd_ref)


def _ffn1(x, n1, wg, wu, wd, tm):
    n = x.shape[0]
    return pl.pallas_call(
        _ffn1_kernel,
        grid=(n // tm,),
        in_specs=[pl.BlockSpec((tm, D_MODEL), lambda i: (i, 0)),
                  _resident(n1.shape), _resident(wg.shape), _resident(wu.shape), _resident(wd.shape)],
        out_specs=pl.BlockSpec((tm, D_MODEL), lambda i: (i, 0)),
        out_shape=jax.ShapeDtypeStruct((n, D_MODEL), F32),
        compiler_params=_compiler_params(1),
        name="ffn1",
    )(x, n1, wg, wu, wd)


def _ffn_tile_step(xn, wg_ref, wu_ref, wd_ref, wg_out, wu_out, wd_out, acc_scr):
    wg, wu, wd = wg_ref[...].astype(BF16), wu_ref[...].astype(BF16), wd_ref[...].astype(BF16)
    wg_out[...] = wg
    wu_out[...] = wu
    wd_out[...] = wd
    acc_scr[...] += _dot((_silu(_dot(xn, wg)) * _dot(xn, wu)).astype(BF16), wd)


def _ffn1_small_kernel(x_ref, n1_ref, wg_ref, wu_ref, wd_ref, h1_ref, wg_out, wu_out, wd_out, xn_scr, acc_scr):
    j = pl.program_id(0)

    @pl.when(j == 0)
    def _():
        xn_scr[...] = _rms(x_ref[...], n1_ref[...]).astype(BF16)
        acc_scr[...] = jnp.zeros(acc_scr.shape, F32)

    _ffn_tile_step(xn_scr[...], wg_ref, wu_ref, wd_ref, wg_out, wu_out, wd_out, acc_scr)

    @pl.when(j == pl.num_programs(0) - 1)
    def _():
        h1_ref[...] = x_ref[...] + 0.5 * acc_scr[...]


def _ffn2_small_kernel(h1_ref, mix_ref, wo_ref, n2_ref, wg_ref, wu_ref, wd_ref, nf_ref,
                       y_ref, wg_out, wu_out, wd_out, wo_out, h2_scr, hn_scr, acc_scr):
    j = pl.program_id(0)

    @pl.when(j == 0)
    def _():
        wo = wo_ref[...].astype(BF16)
        wo_out[...] = wo
        h2 = h1_ref[...] + _dot(mix_ref[...].astype(BF16), wo)
        h2_scr[...] = h2
        hn_scr[...] = _rms(h2, n2_ref[...]).astype(BF16)
        acc_scr[...] = jnp.zeros(acc_scr.shape, F32)

    _ffn_tile_step(hn_scr[...], wg_ref, wu_ref, wd_ref, wg_out, wu_out, wd_out, acc_scr)

    @pl.when(j == pl.num_programs(0) - 1)
    def _():
        y_ref[...] = _rms(h2_scr[...] + 0.5 * acc_scr[...], nf_ref[...])


def _weight_tile_specs():
    cols = pl.BlockSpec((D_MODEL, FF_TILE), lambda j: (0, j))
    rows = pl.BlockSpec((FF_TILE, D_MODEL), lambda j: (j, 0))
    shapes = [jax.ShapeDtypeStruct((D_MODEL, D_FF), BF16), jax.ShapeDtypeStruct((D_MODEL, D_FF), BF16),
              jax.ShapeDtypeStruct((D_FF, D_MODEL), BF16)]
    return [cols, cols, rows], shapes


def _ffn1_small(x, n1, wg, wu, wd):
    n = x.shape[0]
    wspecs, wshapes = _weight_tile_specs()
    return pl.pallas_call(
        _ffn1_small_kernel,
        grid=(D_FF // FF_TILE,),
        in_specs=[_resident(x.shape), _resident(n1.shape)] + wspecs,
        out_specs=[pl.BlockSpec((n, D_MODEL), lambda j: (0, 0))] + wspecs,
        out_shape=[jax.ShapeDtypeStruct((n, D_MODEL), F32)] + wshapes,
        scratch_shapes=[pltpu.VMEM((n, D_MODEL), BF16), pltpu.VMEM((n, D_MODEL), F32)],
        compiler_params=_compiler_params(1),
        name="ffn1_small",
    )(x, n1, wg, wu, wd)


def _ffn2_small(h1, h1_block, mix, wo, n2, wg, wu, wd, nf):
    n = mix.shape[0]
    wspecs, wshapes = _weight_tile_specs()
    wspecs_out = wspecs + [pl.BlockSpec(wo.shape, lambda j: (0, 0))]
    wshapes = wshapes + [jax.ShapeDtypeStruct(wo.shape, BF16)]
    return pl.pallas_call(
        _ffn2_small_kernel,
        grid=(D_FF // FF_TILE,),
        in_specs=[pl.BlockSpec((n, D_MODEL), lambda j: (h1_block, 0), pipeline_mode=pl.Buffered(1)),
                  _resident(mix.shape), _resident(wo.shape), _resident(n2.shape)] + wspecs
        + [_resident(nf.shape)],
        out_specs=[pl.BlockSpec((n, D_MODEL), lambda j: (0, 0))] + wspecs_out,
        out_shape=[jax.ShapeDtypeStruct((n, D_MODEL), F32)] + wshapes,
        scratch_shapes=[pltpu.VMEM((n, D_MODEL), F32), pltpu.VMEM((n, D_MODEL), BF16),
                        pltpu.VMEM((n, D_MODEL), F32)],
        compiler_params=_compiler_params(1),
        name="ffn2_small",
    )(h1, mix, wo, n2, wg, wu, wd, nf)


W_IN_TILE = 512


def _proj_small_kernel(n_cols, h1_ref, nm_ref, wt_ref, u_ref, win_out, hn_scr):
    j = pl.program_id(0)

    @pl.when(j == 0)
    def _():
        hn_scr[...] = _rms(h1_ref[...], nm_ref[...]).astype(BF16)

    col = j * W_IN_TILE + lax.broadcasted_iota(jnp.int32, (W_IN_TILE, 1), 0)
    w = jnp.where(col < n_cols, wt_ref[...], 0.0).T.astype(BF16)
    win_out[...] = w
    u_ref[...] = _dot(hn_scr[...], w)


def _proj_small(h1, nm, w_in_t):
    n = h1.shape[0]
    tile = lambda rows: pl.BlockSpec((rows, W_IN_TILE), lambda j: (0, j))
    return pl.pallas_call(
        functools.partial(_proj_small_kernel, w_in_t.shape[0]),
        grid=(pl.cdiv(D_IN_PAD, W_IN_TILE),),
        in_specs=[_resident(h1.shape), _resident(nm.shape), pl.BlockSpec((W_IN_TILE, D_MODEL), lambda j: (j, 0))],
        out_specs=[tile(n), tile(D_MODEL)],
        out_shape=[jax.ShapeDtypeStruct((n, D_IN_PAD), F32), jax.ShapeDtypeStruct((D_MODEL, D_IN_PAD), BF16)],
        scratch_shapes=[pltpu.VMEM((n, D_MODEL), BF16)],
        compiler_params=_compiler_params(1),
        name="proj_small",
    )(h1, nm, w_in_t)


def _ffn_out_kernel(h1_ref, mix_ref, wo_ref, n2_ref, wg_ref, wu_ref, wd_ref, nf_ref, y_ref):
    h2 = h1_ref[...] + _dot(mix_ref[...].astype(BF16), wo_ref[...])
    hn = _rms(h2, n2_ref[...]).astype(BF16)
    h3 = h2 + 0.5 * _swiglu(hn, wg_ref, wu_ref, wd_ref)
    y_ref[...] = _rms(h3, nf_ref[...])


def _ffn_out(h1, h1_block0, mix, wo, n2, wg, wu, wd, nf, tm):
    n = mix.shape[0]
    return pl.pallas_call(
        _ffn_out_kernel,
        grid=(n // tm,),
        in_specs=[pl.BlockSpec((tm, D_MODEL), lambda i: (i + h1_block0, 0)),
                  pl.BlockSpec((tm, D_MODEL), lambda i: (i, 0)),
                  _resident(wo.shape), _resident(n2.shape), _resident(wg.shape), _resident(wu.shape),
                  _resident(wd.shape), _resident(nf.shape)],
        out_specs=pl.BlockSpec((tm, D_MODEL), lambda i: (i, 0)),
        out_shape=jax.ShapeDtypeStruct((n, D_MODEL), F32),
        compiler_params=_compiler_params(1),
        name="ffn_out",
    )(h1, mix, wo, n2, wg, wu, wd, nf)


def _head_slices(a, width):
    return [a[:, i:i + width] for i in range(0, a.shape[1], width)]


def _column_tile(row):
    return jnp.broadcast_to(row, (LANES, LANES)).T


def _store_hgrn_out(rows, o, gate, hgn, mix_ref):
    for h in range(HG_HEADS):
        sl = slice(h * HG_DIM, (h + 1) * HG_DIM)
        oh = o[:, sl]
        oh = oh * lax.rsqrt(jnp.mean(oh * oh, axis=-1, keepdims=True) + EPS)
        mix_ref[rows, sl] = (oh * hgn[:, sl] * gate[:, sl]).astype(mix_ref.dtype)


def _hgrn_log_decay(n_pad, tb, fz, lb, tri):
    logf = jnp.log(lb + (1.0 - lb) * jax.nn.sigmoid(fz))
    if n_pad:
        logf = jnp.where(lax.broadcasted_iota(jnp.int32, (tb, HG_WIDTH), 0) >= n_pad, logf, 0.0)
    b = _cumsum_rows(tri, logf)
    chunk_ends = jnp.concatenate([b[r:r + 1] for r in range(HG_CHUNK - 1, tb, HG_CHUNK)], axis=0)
    return b, jnp.max(-chunk_ends)


def _hgrn_block(n_pad, tb, u_ref, lb, hgn, b, worst_log_decay, mix_ref, st_scr, o_scr):
    chunked_is_safe = worst_log_decay < HG_MAX_CHUNK_LOG_DECAY

    @pl.when(chunked_is_safe)
    def _():
        _hgrn_chunked(n_pad, tb, u_ref, lb, hgn, b, mix_ref, st_scr)

    @pl.when(jnp.logical_not(chunked_is_safe))
    def _():
        _hgrn_per_token(n_pad, tb, u_ref, lb, hgn, mix_ref, st_scr, o_scr)


def _hgrn_per_token(n_pad, tb, u_ref, lb, hgn, mix_ref, st_scr, o_scr):
    for h in range(HG_HEADS):
        st_scr[h] = st_scr[h].T

    def sublane_group(i, carry):
        rows = pl.ds(pl.multiple_of(i * SUBLANES, SUBLANES), SUBLANES)
        fz = u_ref[rows, COL_F:COL_F + HG_WIDTH]
        f = lb + (1.0 - lb) * jax.nn.sigmoid(fz)
        kk = (1.0 - lb) * jax.nn.sigmoid(-fz)
        if n_pad:
            valid = i * SUBLANES + lax.broadcasted_iota(jnp.int32, (SUBLANES, HG_WIDTH), 0) >= n_pad
            f = jnp.where(valid, f, 1.0)
            kk = jnp.where(valid, kk, 0.0)
        q = _silu(u_ref[rows, COL_Q:COL_Q + HG_WIDTH])
        v = u_ref[rows, COL_I:COL_I + HG_WIDTH]
        for h in range(HG_HEADS):
            sl = slice(h * HG_DIM, (h + 1) * HG_DIM)
            s = st_scr[h]
            o_rows = []
            for j in range(SUBLANES):
                r = slice(j, j + 1)
                s = s * _column_tile(f[r, sl]) + _column_tile(kk[r, sl]) * v[r, sl]
                o_rows.append(jnp.sum(s * _column_tile(q[r, sl]), axis=0, keepdims=True))
            st_scr[h] = s
            o_scr[rows, sl] = jnp.concatenate(o_rows, axis=0)
        return carry

    lax.fori_loop(0, tb // SUBLANES, sublane_group, 0)
    for h in range(HG_HEADS):
        st_scr[h] = st_scr[h].T
    _store_hgrn_out(slice(0, tb), o_scr[...], _silu(u_ref[:, COL_G:COL_G + HG_WIDTH]), hgn, mix_ref)


def _hgrn_chunked(n_pad, tb, u_ref, lb, hgn, b, mix_ref, st_scr):
    c = HG_CHUNK
    kk = (1.0 - lb) * jax.nn.sigmoid(-u_ref[:, COL_F:COL_F + HG_WIDTH])
    if n_pad:
        kk = jnp.where(lax.broadcasted_iota(jnp.int32, (tb, HG_WIDTH), 0) >= n_pad, kk, 0.0)
    q = _silu(u_ref[:, COL_Q:COL_Q + HG_WIDTH])
    v = u_ref[:, COL_I:COL_I + HG_WIDTH].astype(BF16)
    gate = _silu(u_ref[:, COL_G:COL_G + HG_WIDTH])
    qt = (q * jnp.exp(b)).astype(BF16)
    kt = (kk * jnp.exp(-b)).astype(BF16)
    causal = _tiled_lower_tri(c, HG_HEADS)

    for r0 in range(0, tb, c):
        rows = slice(r0, r0 + c)
        b_c = b[rows]
        b_last = b_c[c - 1:c, :]
        kh_c = (kk[rows] * jnp.exp(b_last - b_c)).astype(BF16)
        decay = jnp.exp(b_last)
        qt_c = qt[rows]
        kd = _block_diag(_head_slices(kt[rows], HG_DIM))
        vd = _block_diag(_head_slices(v[rows], HG_DIM))
        scores = jnp.where(causal, _dot_nt(qt_c, kd), 0.0).astype(BF16)
        o = _dot(scores, vd)
        o_prev = []
        for p in range(HG_PAIRS):
            lanes = slice(p * 2 * HG_DIM, (p + 1) * 2 * HG_DIM)
            heads = (2 * p, 2 * p + 1)
            st = [st_scr[h] for h in heads]
            o_prev.append(_dot_nt(qt_c[:, lanes], _block_diag([s.astype(BF16) for s in st])))
            upd = _dot_tn(vd[p * 2 * c:(p + 1) * 2 * c, lanes], jnp.concatenate([kh_c[:, lanes]] * 2, axis=0))
            for i, h in enumerate(heads):
                blk = slice(i * HG_DIM, (i + 1) * HG_DIM)
                st_scr[h] = st[i] * decay[:, h * HG_DIM:(h + 1) * HG_DIM] + upd[blk, blk]
        _store_hgrn_out(rows, o + jnp.concatenate(o_prev, axis=1), gate[rows], hgn, mix_ref)


def _pair_columns(a, r0, r1, first_half):
    shape = (a.shape[0], LANES)
    return jnp.where(first_half, jnp.broadcast_to(a[:, r0:r0 + 1], shape), jnp.broadcast_to(a[:, r1:r1 + 1], shape))


def _group_columns(a, g, first_half):
    r = g * HEADS_PER_GROUP
    return jnp.concatenate([_pair_columns(a, r + i, r + i + 1, first_half)
                            for i in range(0, HEADS_PER_GROUP, 2)], axis=1)


def _ssd_block(n_pad, tb, u_ref, col0, cw, cb, dtb, a_neg, dsk, ssn, tri, mix_ref, hg_scr, xpad):
    c = SSM_CHUNK
    conv = cb
    for j in range(CONV_WIDTH):
        off = PAD_ROWS - (CONV_WIDTH - 1) + j
        conv = conv + cw[j:j + 1, :] * xpad[off:off + tb, :]
    act = _silu(conv)
    xs = act[:, 0:SSM_WIDTH]
    bm = act[:, SSM_WIDTH:SSM_WIDTH + SSM_GROUPS * SSM_STATE].astype(BF16)
    cm = act[:, SSM_WIDTH + SSM_GROUPS * SSM_STATE:].astype(BF16)
    z_gate = _silu(u_ref[:, col0:col0 + SSM_WIDTH])

    col_dt = col0 + COL_DT - COL_Z
    dt = jax.nn.softplus(u_ref[:, col_dt:col_dt + LANES] + dtb)
    if n_pad:
        dt = jnp.where(lax.broadcasted_iota(jnp.int32, (tb, LANES), 0) >= n_pad, dt, 0.0)
    cum = _cumsum_rows(tri, dt * a_neg)
    causal = _lower_tri(c)
    first_half = lax.broadcasted_iota(jnp.int32, (1, LANES), 1) < SSM_HEAD_DIM
    zeros = jnp.zeros((c, LANES), BF16)

    for r0 in range(0, tb, c):
        rows = slice(r0, r0 + c)
        cum_c = cum[rows]
        dt_c = dt[rows]
        last = cum_c[c - 1:c, :]
        cum_t = cum_c.T
        dt_t = dt_c.T
        e_cum = jnp.exp(cum_c)
        w_in = dt_c * jnp.exp(last - cum_c)
        e_last = jnp.exp(last)
        for g in range(SSM_GROUPS):
            yield
            glanes = slice(g * GROUP_WIDTH, (g + 1) * GROUP_WIDTH)
            bg = bm[rows, g * SSM_STATE:(g + 1) * SSM_STATE]
            cg = cm[rows, g * SSM_STATE:(g + 1) * SSM_STATE]
            xg = xs[rows, glanes]
            cbt = _dot_nt(cg, bg)
            m_heads = []
            for rr in range(HEADS_PER_GROUP):
                r = g * HEADS_PER_GROUP + rr
                seg = jnp.exp(jnp.where(causal, cum_c[:, r:r + 1] - cum_t[r:r + 1, :], -jnp.inf))
                m_heads.append((cbt * seg * dt_t[r:r + 1, :]).astype(BF16))
            xb = xg.astype(BF16)
            xd_rows = []
            for rr in range(HEADS_PER_GROUP):
                tile = xb[:, (rr // 2) * LANES:(rr // 2 + 1) * LANES]
                tile = jnp.where(first_half if rr % 2 == 0 else ~first_half, tile, jnp.zeros_like(tile))
                xd_rows.append(jnp.concatenate([tile, zeros] if rr < 2 else [zeros, tile], axis=1))
            xd = jnp.concatenate(xd_rows, axis=0)
            hg = hg_scr[g]
            y = (_dot(jnp.concatenate(m_heads, axis=1), xd)
                 + _dot(cg, hg.astype(BF16)) * _group_columns(e_cum, g, first_half))
            xw = (xg * _group_columns(w_in, g, first_half)).astype(BF16)
            hg_scr[g] = hg * _group_columns(e_last, g, first_half) + _dot_tn(bg, xw)
            yz = (y + dsk[:, glanes] * xg) * z_gate[rows, glanes]
            yz = yz * lax.rsqrt(jnp.mean(yz * yz, axis=-1, keepdims=True) + EPS) * ssn[:, glanes]
            mix_ref[rows, HG_WIDTH + g * GROUP_WIDTH:HG_WIDTH + (g + 1) * GROUP_WIDTH] = yz.astype(mix_ref.dtype)


def _interleave(main, side, side_steps):
    for n in side_steps:
        if next(main, StopIteration) is StopIteration:
            break
        for _ in range(n):
            next(side, None)
    for _ in main:
        pass
    for _ in side:
        pass


def _mixer_block(*args):
    for _ in _mixer_steps(*args):
        pass


def _mixer_steps(n_pad, tb, uh_ref, us_ref, us_col0, param_refs, mix_ref, st_scr, hg_scr, xpad, o_scr,
                 log_decay=None):
    lbl_ref, hgn_ref, cw_ref, cb_ref, dtb_ref, alog_ref, dsk_ref, ssn_ref, tri_hg_ref, tri_ssm_ref = param_refs
    lb = _forget_lower_bound(lbl_ref[...])
    if log_decay is None:
        b, worst = _hgrn_log_decay(n_pad, tb, uh_ref[:, COL_F:COL_F + HG_WIDTH], lb, tri_hg_ref[...])
    else:
        b, worst = log_decay[0][...], log_decay[1]
    _hgrn_block(n_pad, tb, uh_ref, lb, hgn_ref[...], b, worst, mix_ref, st_scr, o_scr)
    yield
    col_xbc = us_col0 + COL_XBC - COL_Z
    xbc = us_ref[:, col_xbc:col_xbc + CONV_DIM]
    if n_pad:
        xbc = jnp.where(lax.broadcasted_iota(jnp.int32, (tb, CONV_DIM), 0) >= n_pad, xbc, 0.0)
    xpad[PAD_ROWS:PAD_ROWS + tb, :] = xbc
    yield from _ssd_block(n_pad, tb, us_ref, us_col0, cw_ref[...], cb_ref[...], dtb_ref[...],
                          -jnp.exp(alog_ref[...]), dsk_ref[...], ssn_ref[...], tri_ssm_ref[...],
                          mix_ref, hg_scr, xpad)
    xpad[0:PAD_ROWS, :] = xpad[tb:tb + PAD_ROWS, :]


def _load_state(s0_ref, h0_ref, c0_ref, st_scr, hg_scr, xpad):
    for h in range(HG_HEADS):
        st_scr[h] = s0_ref[0, h].T
    for r in range(0, SSM_HEADS, 2):
        g, lane0 = r // HEADS_PER_GROUP, (r % HEADS_PER_GROUP) * SSM_HEAD_DIM
        hg_scr[g, :, lane0:lane0 + LANES] = jnp.concatenate([h0_ref[0, r], h0_ref[0, r + 1]], axis=0).T
    xpad[0:PAD_ROWS, :] = c0_ref[0]


def _store_state(s_out_ref, h_out_ref, c_out_ref, st_scr, hg_scr, xpad):
    for h in range(HG_HEADS):
        s_out_ref[0, h] = st_scr[h].T
    for r in range(0, SSM_HEADS, 2):
        g, lane0 = r // HEADS_PER_GROUP, (r % HEADS_PER_GROUP) * SSM_HEAD_DIM
        pair = hg_scr[g, :, lane0:lane0 + LANES].T
        h_out_ref[0, r] = pair[0:SSM_HEAD_DIM]
        h_out_ref[0, r + 1] = pair[SSM_HEAD_DIM:]
    c_out_ref[0] = xpad[0:PAD_ROWS, :]


N_MIXER_PARAMS = 10


def _scan_params(params, tb):
    n = min(tb, FF_TILE)
    return tuple(params) + (_chunked_tri(n, HG_CHUNK), _chunked_tri(n, SSM_CHUNK))


PROJ_TILE = 512
PROJ_STEPS_AFTER_SCAN_STEP = (3, 2, 1, 1, 1, 1)
HG_SHAPE = (SSM_GROUPS, SSM_STATE, GROUP_WIDTH)
S_SHAPE = (HG_HEADS, HG_DIM, HG_DIM)
H_SHAPE = (SSM_HEADS, SSM_HEAD_DIM, SSM_STATE)
C_SHAPE = (PAD_ROWS, CONV_DIM)


def _scan_kernel(n_pad, tb, project, *refs):
    if project:
        h1_ref, nm_ref, win_ref = refs[:3]
        refs = refs[3:]
    else:
        u_ref = refs[0]
        refs = refs[1:]
    param_refs = refs[:N_MIXER_PARAMS]
    s0_ref, h0_ref, c0_ref, mix_ref, s_out_ref, h_out_ref, c_out_ref = refs[N_MIXER_PARAMS:N_MIXER_PARAMS + 7]
    scratch = refs[N_MIXER_PARAMS + 7:]
    if project:
        uh_scr, us_scr, st_scr, hg_scr, xpad, o_scr = scratch
    else:
        st_scr, hg_scr, xpad, o_scr = scratch
    t = pl.program_id(1)

    @pl.when(t == 0)
    def _():
        _load_state(s0_ref, h0_ref, c0_ref, st_scr, hg_scr, xpad)

    if project:
        hn = _rms(h1_ref[...], nm_ref[...]).astype(BF16)
        uh_scr[...] = _dot(hn, win_ref[:, 0:COL_Z])
        us_scr[...] = _dot(hn, win_ref[:, COL_Z:])
        _mixer_block(n_pad, tb, uh_scr, us_scr, 0, param_refs, mix_ref, st_scr, hg_scr, xpad, o_scr)
    else:
        _mixer_block(n_pad, tb, u_ref, u_ref, COL_Z, param_refs, mix_ref, st_scr, hg_scr, xpad, o_scr)

    @pl.when(t == pl.num_programs(1) - 1)
    def _():
        _store_state(s_out_ref, h_out_ref, c_out_ref, st_scr, hg_scr, xpad)


def _scan_pipelined_kernel(tb, nt, h1_ref, nm_ref, win_ref, *refs):
    param_refs = refs[:N_MIXER_PARAMS]
    s0_ref, h0_ref, c0_ref, mix_ref, s_out_ref, h_out_ref, c_out_ref = refs[N_MIXER_PARAMS:N_MIXER_PARAMS + 7]
    even, odd = refs[N_MIXER_PARAMS + 7:N_MIXER_PARAMS + 11], refs[N_MIXER_PARAMS + 11:N_MIXER_PARAMS + 15]
    st_scr, hg_scr, xpad, o_scr = refs[N_MIXER_PARAMS + 15:]
    s = pl.program_id(0)
    scanned = jnp.maximum(s - 1, 0)

    @pl.when(s == 0)
    def _():
        uh, us, b, worst = odd
        uh[...] = jnp.zeros(uh.shape, F32)
        us[...] = jnp.zeros(us.shape, F32)
        b[...] = jnp.zeros(b.shape, F32)
        worst[0] = 0.0

    @pl.when(scanned % nt == 0)
    def _():
        _load_state(s0_ref, h0_ref, c0_ref, st_scr, hg_scr, xpad)

    def project_steps(uh_w, us_w, b_w, worst_w):
        hn = _rms(h1_ref[...], nm_ref[...]).astype(BF16)
        for c0 in range(0, D_IN_PAD, PROJ_TILE):
            c1 = min(c0 + PROJ_TILE, D_IN_PAD)
            yield
            tile = _dot(hn, win_ref[:, c0:c1])
            if c0 < COL_Z:
                uh_w[:, c0:c1] = tile
            else:
                us_w[:, c0 - COL_Z:c1 - COL_Z] = tile
            if c0 == COL_F:
                assert c1 == COL_F + HG_WIDTH
                lb = _forget_lower_bound(param_refs[0][...])
                b, worst = _hgrn_log_decay(0, tb, tile, lb, param_refs[N_MIXER_PARAMS - 2][...])
                b_w[...] = b
                worst_w[0] = worst

    def body(write, read):
        uh_r, us_r, b_r, worst_r = read
        _interleave(_mixer_steps(0, tb, uh_r, us_r, 0, param_refs, mix_ref, st_scr, hg_scr, xpad, o_scr,
                                 (b_r, worst_r[0])),
                    project_steps(*write), PROJ_STEPS_AFTER_SCAN_STEP)

    pl.when(s % 2 == 0)(lambda: body(even, odd))
    pl.when(s % 2 == 1)(lambda: body(odd, even))

    @pl.when((s >= 1) & (scanned % nt == nt - 1))
    def _():
        _store_state(s_out_ref, h_out_ref, c_out_ref, st_scr, hg_scr, xpad)


def _scan_pipelined(h1, nb, nt, tb, proj, params, s0, h0, c0):
    n_blocks = nb * nt
    last = n_blocks - 1
    lead = tuple(proj) + _scan_params(params, tb)
    shared = lambda shape: pl.BlockSpec((1,) + shape[1:], lambda s: (0,) * len(shape))
    per_seq = lambda shape: pl.BlockSpec((1,) + shape, lambda s: (jnp.maximum(s - 1, 0) // nt,) + (0,) * len(shape))
    u_bufs = [pltpu.VMEM((tb, COL_Z), F32), pltpu.VMEM((tb, D_IN_PAD - COL_Z), F32),
              pltpu.VMEM((tb, HG_WIDTH), F32), pltpu.SMEM((1,), F32)]
    return pl.pallas_call(
        functools.partial(_scan_pipelined_kernel, tb, nt),
        grid=(n_blocks + 1,),
        in_specs=[pl.BlockSpec((tb, D_MODEL), lambda s: (jnp.minimum(s, last), 0))]
        + [_resident(p.shape) for p in lead]
        + [shared(s0.shape), shared(h0.shape), shared(c0.shape)],
        out_specs=[pl.BlockSpec((tb, D_MODEL), lambda s: (jnp.maximum(s - 1, 0), 0)),
                   per_seq(S_SHAPE), per_seq(H_SHAPE), per_seq(C_SHAPE)],
        out_shape=[jax.ShapeDtypeStruct((n_blocks * tb, D_MODEL), BF16),
                   jax.ShapeDtypeStruct((nb,) + S_SHAPE, F32),
                   jax.ShapeDtypeStruct((nb,) + H_SHAPE, F32),
                   jax.ShapeDtypeStruct((nb,) + C_SHAPE, F32)],
        scratch_shapes=u_bufs + u_bufs + [pltpu.VMEM(S_SHAPE, F32), pltpu.VMEM(HG_SHAPE, F32),
                                          pltpu.VMEM((PAD_ROWS + tb, CONV_DIM), F32),
                                          pltpu.VMEM((tb, HG_WIDTH), F32)],
        compiler_params=_compiler_params(1),
        name="proj_mixer",
    )(h1, *lead, s0, h0, c0)


def _scan(src, src_block0, nb, nt, tb, n_pad, proj, params, s0, h0, c0):
    lead = (tuple(proj) if proj else ()) + _scan_params(params, tb)
    shared = lambda shape: pl.BlockSpec((1,) + shape[1:], lambda b, t: (0,) * len(shape))
    per_seq = lambda shape: pl.BlockSpec((1,) + shape, lambda b, t: (b,) + (0,) * len(shape))
    scratch = [pltpu.VMEM(S_SHAPE, F32), pltpu.VMEM(HG_SHAPE, F32), pltpu.VMEM((PAD_ROWS + tb, CONV_DIM), F32),
               pltpu.VMEM((tb, HG_WIDTH), F32)]
    if proj:
        scratch = [pltpu.VMEM((tb, COL_Z), F32), pltpu.VMEM((tb, D_IN_PAD - COL_Z), F32)] + scratch
    return pl.pallas_call(
        functools.partial(_scan_kernel, n_pad, tb, bool(proj)),
        grid=(nb, nt),
        in_specs=[pl.BlockSpec((tb, src.shape[1]), lambda b, t: (src_block0 + b * nt + t, 0))]
        + [_resident(p.shape) for p in lead]
        + [shared(s0.shape), shared(h0.shape), shared(c0.shape)],
        out_specs=[pl.BlockSpec((tb, D_MODEL), lambda b, t: (b * nt + t, 0)),
                   per_seq(S_SHAPE), per_seq(H_SHAPE), per_seq(C_SHAPE)],
        out_shape=[jax.ShapeDtypeStruct((nb * nt * tb, D_MODEL), BF16),
                   jax.ShapeDtypeStruct((nb,) + S_SHAPE, F32),
                   jax.ShapeDtypeStruct((nb,) + H_SHAPE, F32),
                   jax.ShapeDtypeStruct((nb,) + C_SHAPE, F32)],
        scratch_shapes=scratch,
        compiler_params=_compiler_params(2),
        name="proj_mixer" if proj else "mixer",
    )(src, *lead, s0, h0, c0)


def _sample_kernel(nbs, u_ref, lbl_ref, hgn_ref, cw_ref, cb_ref, dtb_ref, alog_ref, dsk_ref, ssn_ref,
                   sh_ref, ss_ref, sc_ref, mix_ref, sh_out_ref, ss_out_ref, sc_out_ref, obuf, ybuf):
    lb = _forget_lower_bound(lbl_ref[...])
    fz = u_ref[:, COL_F:COL_F + HG_WIDTH]
    f = lb + (1.0 - lb) * jax.nn.sigmoid(fz)
    kk = (1.0 - lb) * jax.nn.sigmoid(-fz)
    q = _silu(u_ref[:, COL_Q:COL_Q + HG_WIDTH])
    v = u_ref[:, COL_I:COL_I + HG_WIDTH]

    xbc = u_ref[:, COL_XBC:COL_XBC + CONV_DIM]
    cw = cw_ref[...]
    conv = cb_ref[...] + cw[CONV_WIDTH - 1:CONV_WIDTH, :] * xbc
    for j in range(CONV_WIDTH - 1):
        conv = conv + cw[j:j + 1, :] * sc_ref[j]
    for j in range(CONV_WIDTH - 2):
        sc_out_ref[j] = sc_ref[j + 1]
    sc_out_ref[CONV_WIDTH - 2] = xbc
    act = _silu(conv)
    xs = act[:, 0:SSM_WIDTH]
    bm = act[:, SSM_WIDTH:SSM_WIDTH + SSM_GROUPS * SSM_STATE]
    cm = act[:, SSM_WIDTH + SSM_GROUPS * SSM_STATE:]
    dt = jax.nn.softplus(u_ref[:, COL_DT:COL_DT + LANES] + dtb_ref[...])
    d_a = jnp.exp(dt * (-jnp.exp(alog_ref[...])))

    top_half = lax.broadcasted_iota(jnp.int32, (LANES, LANES), 0) < SSM_HEAD_DIM
    for j in range(nbs):
        row = slice(j, j + 1)
        for h in range(HG_HEADS):
            sl = slice(h * HG_DIM, (h + 1) * HG_DIM)
            s_new = sh_ref[j, h] * _column_tile(f[row, sl]) + _column_tile(kk[row, sl]) * v[row, sl]
            sh_out_ref[j, h] = s_new
            obuf[row, sl] = jnp.sum(s_new * _column_tile(q[row, sl]), axis=0, keepdims=True)
        for rp in range(SSM_HEADS // 2):
            r0, r1 = 2 * rp, 2 * rp + 1
            g = r0 // HEADS_PER_GROUP
            sl = slice(rp * LANES, (rp + 1) * LANES)
            gsl = slice(g * SSM_STATE, (g + 1) * SSM_STATE)
            h2 = jnp.concatenate([ss_ref[j, r0], ss_ref[j, r1]], axis=0)
            da2 = jnp.where(top_half, d_a[row, r0:r0 + 1], d_a[row, r1:r1 + 1])
            dt2 = jnp.where(top_half, dt[row, r0:r0 + 1], dt[row, r1:r1 + 1])
            h_new = h2 * da2 + (dt2 * _column_tile(xs[row, sl])) * bm[row, gsl]
            ss_out_ref[j, r0] = h_new[0:SSM_HEAD_DIM]
            ss_out_ref[j, r1] = h_new[SSM_HEAD_DIM:]
            ybuf[row, sl] = jnp.sum((h_new * cm[row, gsl]).T, axis=0, keepdims=True)

    o = obuf[...]
    gate = _silu(u_ref[:, COL_G:COL_G + HG_WIDTH])
    hgn = hgn_ref[...]
    for h in range(HG_HEADS):
        sl = slice(h * HG_DIM, (h + 1) * HG_DIM)
        oh = o[:, sl]
        oh = oh * lax.rsqrt(jnp.mean(oh * oh, axis=-1, keepdims=True) + EPS)
        mix_ref[:, sl] = oh * hgn[:, sl] * gate[:, sl]
    yz = (ybuf[...] + dsk_ref[...] * xs) * _silu(u_ref[:, COL_Z:COL_Z + SSM_WIDTH])
    ssn = ssn_ref[...]
    for g in range(SSM_GROUPS):
        sl = slice(g * GROUP_WIDTH, (g + 1) * GROUP_WIDTH)
        seg = yz[:, sl]
        seg = seg * lax.rsqrt(jnp.mean(seg * seg, axis=-1, keepdims=True) + EPS) * ssn[:, sl]
        mix_ref[:, HG_WIDTH + g * GROUP_WIDTH:HG_WIDTH + (g + 1) * GROUP_WIDTH] = seg


def _sample_mixer(u, u_block0, n, nbs, params, sh, ss, sc_t):
    small = [_resident(p.shape) for p in params]
    return pl.pallas_call(
        functools.partial(_sample_kernel, nbs),
        grid=(n // nbs,),
        in_specs=[pl.BlockSpec((nbs, D_IN_PAD), lambda i: (u_block0 + i, 0))] + small
        + [pl.BlockSpec((nbs,) + sh.shape[1:], lambda i: (i, 0, 0, 0)),
           pl.BlockSpec((nbs,) + ss.shape[1:], lambda i: (i, 0, 0, 0)),
           pl.BlockSpec((CONV_WIDTH - 1, nbs, CONV_DIM), lambda i: (0, i, 0))],
        out_specs=[pl.BlockSpec((nbs, D_MODEL), lambda i: (i, 0)),
                   pl.BlockSpec((nbs,) + sh.shape[1:], lambda i: (i, 0, 0, 0)),
                   pl.BlockSpec((nbs,) + ss.shape[1:], lambda i: (i, 0, 0, 0)),
                   pl.BlockSpec((CONV_WIDTH - 1, nbs, CONV_DIM), lambda i: (0, i, 0))],
        out_shape=[jax.ShapeDtypeStruct((n, D_MODEL), F32),
                   jax.ShapeDtypeStruct(sh.shape, F32),
                   jax.ShapeDtypeStruct(ss.shape, F32),
                   jax.ShapeDtypeStruct(sc_t.shape, F32)],
        scratch_shapes=[pltpu.VMEM((nbs, HG_WIDTH), F32), pltpu.VMEM((nbs, SSM_WIDTH), F32)],
        compiler_params=_compiler_params(1),
        name="sample_mixer",
    )(u, *params, sh, ss, sc_t)


TM_DENSE = 512
TM_SCAN = 512
TM_SMALL = 128
SAMPLES_PER_STEP = 8


def _pad_lanes(row, value=0.0):
    return jnp.pad(row, ((0, 0), (0, LANES - row.shape[1])), constant_values=value)


def kernel(x_prompt, x_sample, state_hgrn, state_ssm, state_conv, meta_tokens, lb_logits, norm_ffn1, w_ffn1_gate, w_ffn1_up, w_ffn1_down, norm_mix, w_in, hg_norm, conv_w, conv_b, dt_bias, a_log, d_skip, ssm_norm, w_out, norm_ffn2, w_ffn2_gate, w_ffn2_up, w_ffn2_down, norm_final):
    bp, seq_p, _ = x_prompt.shape
    n_s = x_sample.shape[0]
    assert x_sample.shape[1] == 1 and n_s == TM_SMALL and seq_p % TM_SCAN == 0 and (bp * seq_p) % TM_DENSE == 0
    layer = 0

    n1, nm, n2 = norm_ffn1[layer][None], norm_mix[layer][None], norm_ffn2[layer][None]
    nf = norm_final[None]
    mixer_params = (lb_logits, hg_norm[layer][None], conv_w[layer], conv_b[layer][None],
                    _pad_lanes(dt_bias[layer][None]), _pad_lanes(a_log[layer][None]),
                    jnp.repeat(d_skip[layer], SSM_HEAD_DIM)[None], ssm_norm[layer][None])

    n_pad = TM_SMALL - N_META
    x_small = jnp.concatenate([jnp.zeros((n_pad, D_MODEL), F32), meta_tokens, x_sample[:, 0]], axis=0)
    h1_small, wg1, wu1, wd1 = _ffn1_small(x_small, n1, w_ffn1_gate[layer], w_ffn1_up[layer], w_ffn1_down[layer])
    u_small, win = _proj_small(h1_small, nm, jnp.swapaxes(w_in[layer], 0, 1))

    zeros_s = jnp.zeros((1,) + S_SHAPE, F32)
    zeros_h = jnp.zeros((1,) + H_SHAPE, F32)
    zeros_c = jnp.zeros((1,) + C_SHAPE, F32)
    _, s_meta, h_meta, c_meta = _scan(u_small, 0, 1, 1, TM_SMALL, n_pad, None, mixer_params,
                                      zeros_s, zeros_h, zeros_c)

    sc_t = jnp.swapaxes(state_conv[layer], 0, 1)
    mix_s, hgrn_s, ssm_s, conv_s_t = _sample_mixer(
        u_small, TM_SMALL // SAMPLES_PER_STEP, n_s, SAMPLES_PER_STEP, mixer_params,
        state_hgrn[layer], state_ssm[layer], sc_t)
    y_s, wg2, wu2, wd2, wo = _ffn2_small(h1_small, 1, mix_s, w_out[layer], n2, w_ffn2_gate[layer],
                                         w_ffn2_up[layer], w_ffn2_down[layer], nf)

    xp = x_prompt.reshape(bp * seq_p, D_MODEL)
    h1_p = _ffn1(xp, n1, wg1, wu1, wd1, TM_DENSE)
    mix_p, hgrn_p, ssm_p, conv_p = _scan_pipelined(h1_p, bp, seq_p // TM_SCAN, TM_SCAN, (nm, win), mixer_params,
                                                   s_meta, h_meta, c_meta)
    y_p = _ffn_out(h1_p, 0, mix_p, wo, n2, wg2, wu2, wd2, nf, TM_DENSE)

    keep = slice(PAD_ROWS - (CONV_WIDTH - 1), PAD_ROWS)
    return (y_p.reshape(bp, seq_p, D_MODEL),
            y_s.reshape(n_s, 1, D_MODEL),
            hgrn_p[None], ssm_p[None], conv_p[:, keep][None],
            hgrn_s[None], ssm_s[None], jnp.swapaxes(conv_s_t, 0, 1)[None])
```

```python
import functools

import jax
import jax.numpy as jnp
from jax import lax
from jax.experimental import pallas as pl
from jax.experimental.pallas import tpu as pltpu

F32 = jnp.float32
BF16 = jnp.bfloat16

D_MODEL = 1024
D_FF = 2816
N_META = 16
HG_WIDTH = 512
HG_HEADS = 4
HG_DIM = 128
SSM_WIDTH = 512
SSM_HEADS = 8
SSM_HEAD_DIM = 64
SSM_GROUPS = 2
SSM_STATE = 128
CONV_WIDTH = 4
CONV_DIM = SSM_WIDTH + 2 * SSM_GROUPS * SSM_STATE
EPS = 1e-6

LANES = 128
SUBLANES = 8
VMEM_LIMIT_BYTES = 56 * 1024 * 1024

COL_Q = 0
COL_F = HG_WIDTH
COL_I = 2 * HG_WIDTH
COL_G = 3 * HG_WIDTH
COL_Z = 4 * HG_WIDTH
COL_XBC = COL_Z + SSM_WIDTH
COL_DT = COL_XBC + CONV_DIM
D_IN_PAD = COL_DT + LANES

FF_TILE = 256
HG_CHUNK = 64
HG_MAX_CHUNK_LOG_DECAY = 80.0
SSM_CHUNK = 128
PAD_ROWS = SUBLANES
HG_PAIRS = HG_HEADS // 2
HEADS_PER_GROUP = SSM_HEADS // SSM_GROUPS
GROUP_WIDTH = SSM_WIDTH // SSM_GROUPS
assert 2 * SSM_HEAD_DIM == LANES and 2 * HG_DIM == FF_TILE


def _dot(a, b):
    return jnp.dot(a, b, preferred_element_type=F32)


def _dot_nt(a, b):
    return lax.dot_general(a, b, (((1,), (1,)), ((), ())), preferred_element_type=F32)


def _dot_tn(a, b):
    return lax.dot_general(a, b, (((0,), (0,)), ((), ())), preferred_element_type=F32)


def _rms(x, w):
    return x * lax.rsqrt(jnp.mean(x * x, axis=-1, keepdims=True) + EPS) * w


def _silu(x):
    return x * jax.nn.sigmoid(x)


def _swiglu(xn, wg_ref, wu_ref, wd_ref, side_work=()):
    side_work = list(side_work)
    acc = jnp.zeros((xn.shape[0], D_MODEL), F32)
    for j in range(D_FF // FF_TILE):
        cols = slice(j * FF_TILE, (j + 1) * FF_TILE)
        g = _dot(xn, wg_ref[:, cols])
        u = _dot(xn, wu_ref[:, cols])
        if side_work:
            side_work.pop(0)()
        acc = acc + _dot((_silu(g) * u).astype(BF16), wd_ref[cols, :])
    assert not side_work
    return acc


def _cumsum_rows(tri, a):
    n = tri.shape[0]
    if a.shape[0] > n:
        return jnp.concatenate([_cumsum_rows(tri, a[r:r + n]) for r in range(0, a.shape[0], n)], axis=0)
    a1 = a.astype(BF16)
    r1 = a - a1.astype(F32)
    a2 = r1.astype(BF16)
    a3 = (r1 - a2.astype(F32)).astype(BF16)
    return _dot(tri, a1) + _dot(tri, a2) + _dot(tri, a3)


def _lower_tri(n):
    row = lax.broadcasted_iota(jnp.int32, (n, n), 0)
    col = lax.broadcasted_iota(jnp.int32, (n, n), 1)
    return row >= col


def _chunked_tri(n, chunk):
    assert chunk & (chunk - 1) == 0
    row = lax.broadcasted_iota(jnp.int32, (n, n), 0)
    col = lax.broadcasted_iota(jnp.int32, (n, n), 1)
    same_chunk = (row ^ col) < chunk
    return ((row >= col) & same_chunk).astype(BF16)


def _tiled_lower_tri(chunk, reps):
    assert chunk & (chunk - 1) == 0
    row = lax.broadcasted_iota(jnp.int32, (chunk, reps * chunk), 0)
    col = lax.broadcasted_iota(jnp.int32, (chunk, reps * chunk), 1)
    return row >= (col & (chunk - 1))


def _block_diag(blocks):
    n = len(blocks)
    r, c = blocks[0].shape
    rows = []
    for i, blk in enumerate(blocks):
        parts = []
        if i:
            parts.append(jnp.zeros((r, c * i), blk.dtype))
        parts.append(blk)
        if i < n - 1:
            parts.append(jnp.zeros((r, c * (n - 1 - i)), blk.dtype))
        rows.append(jnp.concatenate(parts, axis=1))
    return jnp.concatenate(rows, axis=0)


def _forget_lower_bound(lbl):
    l0, l1 = lbl[0:1], lbl[1:2]
    m = jnp.maximum(l0, l1)
    e0, e1 = jnp.exp(l0 - m), jnp.exp(l1 - m)
    return e0 / (e0 + e1)


def _resident(shape):
    nd = len(shape)
    return pl.BlockSpec(shape, lambda *_: (0,) * nd, pipeline_mode=pl.Buffered(1))


def _compiler_params(n_grid_axes, flags=None):
    return pltpu.CompilerParams(dimension_semantics=("arbitrary",) * n_grid_axes,
                                vmem_limit_bytes=VMEM_LIMIT_BYTES, flags=flags)


N_CAST = 4


def _ffn1_kernel(x_ref, n1_ref, wg_ref, wu_ref, wd_ref, *refs):
    cast_in, h1_ref, cast_out = refs[:N_CAST], refs[N_CAST], refs[N_CAST + 1:]
    def cast(src, dst):
        def work():
            dst[...] = src[...].astype(BF16)
        return work

    x = x_ref[...]
    xn = _rms(x, n1_ref[...]).astype(BF16)
    casts = [cast(src, dst) for src, dst in zip(cast_in, cast_out)]
    h1_ref[...] = x + 0.5 * _swiglu(xn, wg_ref, wu_ref, wd_ref, casts)


def _ffn1(x, n1, wg, wu, wd, tm, w2g, w2u, w2d, w_out):
    n = x.shape[0]
    steps = n // tm
    n_ff, n_wo = D_FF // FF_TILE, w_out.shape[0] // FF_TILE
    assert steps >= n_ff and steps >= n_wo
    cast_specs = [pl.BlockSpec((D_MODEL, FF_TILE), lambda i: (0, jnp.minimum(i, n_ff - 1))),
                  pl.BlockSpec((D_MODEL, FF_TILE), lambda i: (0, jnp.minimum(i, n_ff - 1))),
                  pl.BlockSpec((FF_TILE, D_MODEL), lambda i: (jnp.minimum(i, n_ff - 1), 0)),
                  pl.BlockSpec((FF_TILE, w_out.shape[1]), lambda i: (jnp.minimum(i, n_wo - 1), 0))]
    cast = (w2g, w2u, w2d, w_out)
    return pl.pallas_call(
        _ffn1_kernel,
        grid=(steps,),
        in_specs=[pl.BlockSpec((tm, D_MODEL), lambda i: (i, 0)),
                  _resident(n1.shape), _resident(wg.shape), _resident(wu.shape), _resident(wd.shape)] + cast_specs,
        out_specs=[pl.BlockSpec((tm, D_MODEL), lambda i: (i, 0))] + cast_specs,
        out_shape=[jax.ShapeDtypeStruct((n, D_MODEL), F32)] + [jax.ShapeDtypeStruct(w.shape, BF16) for w in cast],
        compiler_params=_compiler_params(1),
        name="ffn1",
    )(x, n1, wg, wu, wd, *cast)


def _ffn_tile_step(xn, wg_ref, wu_ref, wd_ref, wg_out, wu_out, wd_out, acc_scr):
    wg, wu, wd = wg_ref[...].astype(BF16), wu_ref[...].astype(BF16), wd_ref[...].astype(BF16)
    wg_out[...] = wg
    wu_out[...] = wu
    wd_out[...] = wd
    acc_scr[...] += _dot((_silu(_dot(xn, wg)) * _dot(xn, wu)).astype(BF16), wd)


def _ffn1_small_kernel(x_ref, n1_ref, wg_ref, wu_ref, wd_ref, h1_ref, wg_out, wu_out, wd_out, xn_scr, acc_scr):
    j = pl.program_id(0)

    @pl.when(j == 0)
    def _():
        xn_scr[...] = _rms(x_ref[...], n1_ref[...]).astype(BF16)
        acc_scr[...] = jnp.zeros(acc_scr.shape, F32)

    _ffn_tile_step(xn_scr[...], wg_ref, wu_ref, wd_ref, wg_out, wu_out, wd_out, acc_scr)

    @pl.when(j == pl.num_programs(0) - 1)
    def _():
        h1_ref[...] = x_ref[...] + 0.5 * acc_scr[...]


def _weight_tile_specs():
    cols = pl.BlockSpec((D_MODEL, FF_TILE), lambda j: (0, j))
    rows = pl.BlockSpec((FF_TILE, D_MODEL), lambda j: (j, 0))
    shapes = [jax.ShapeDtypeStruct((D_MODEL, D_FF), BF16), jax.ShapeDtypeStruct((D_MODEL, D_FF), BF16),
              jax.ShapeDtypeStruct((D_FF, D_MODEL), BF16)]
    return [cols, cols, rows], shapes


def _ffn1_small(x, n1, wg, wu, wd):
    n = x.shape[0]
    wspecs, wshapes = _weight_tile_specs()
    return pl.pallas_call(
        _ffn1_small_kernel,
        grid=(D_FF // FF_TILE,),
        in_specs=[_resident(x.shape), _resident(n1.shape)] + wspecs,
        out_specs=[pl.BlockSpec((n, D_MODEL), lambda j: (0, 0))] + wspecs,
        out_shape=[jax.ShapeDtypeStruct((n, D_MODEL), F32)] + wshapes,
        scratch_shapes=[pltpu.VMEM((n, D_MODEL), BF16), pltpu.VMEM((n, D_MODEL), F32)],
        compiler_params=_compiler_params(1),
        name="ffn1_small",
    )(x, n1, wg, wu, wd)


W_IN_TILE = 512


def _proj_small_kernel(n_cols, h1_ref, nm_ref, wt_ref, u_ref, win_out, hn_scr):
    j = pl.program_id(0)

    @pl.when(j == 0)
    def _():
        hn_scr[...] = _rms(h1_ref[...], nm_ref[...]).astype(BF16)

    col = j * W_IN_TILE + lax.broadcasted_iota(jnp.int32, (W_IN_TILE, 1), 0)
    w = jnp.where(col < n_cols, wt_ref[...], 0.0).T.astype(BF16)
    win_out[...] = w
    u_ref[...] = _dot(hn_scr[...], w)


def _proj_small(h1, nm, w_in_t):
    n = h1.shape[0]
    tile = lambda rows: pl.BlockSpec((rows, W_IN_TILE), lambda j: (0, j))
    return pl.pallas_call(
        functools.partial(_proj_small_kernel, w_in_t.shape[0]),
        grid=(pl.cdiv(D_IN_PAD, W_IN_TILE),),
        in_specs=[_resident(h1.shape), _resident(nm.shape), pl.BlockSpec((W_IN_TILE, D_MODEL), lambda j: (j, 0))],
        out_specs=[tile(n), tile(D_MODEL)],
        out_shape=[jax.ShapeDtypeStruct((n, D_IN_PAD), F32), jax.ShapeDtypeStruct((D_MODEL, D_IN_PAD), BF16)],
        scratch_shapes=[pltpu.VMEM((n, D_MODEL), BF16)],
        compiler_params=_compiler_params(1),
        name="proj_small",
    )(h1, nm, w_in_t)


def _ffn_out_kernel(h1_ref, mix_ref, wo_ref, n2_ref, wg_ref, wu_ref, wd_ref, nf_ref, y_ref):
    h2 = h1_ref[...] + _dot(mix_ref[...].astype(BF16), wo_ref[...])
    hn = _rms(h2, n2_ref[...]).astype(BF16)
    h3 = h2 + 0.5 * _swiglu(hn, wg_ref, wu_ref, wd_ref)
    y_ref[...] = _rms(h3, nf_ref[...])


def _ffn_out(h1, h1_block0, mix, wo, n2, wg, wu, wd, nf, tm):
    n = mix.shape[0]
    return pl.pallas_call(
        _ffn_out_kernel,
        grid=(n // tm,),
        in_specs=[pl.BlockSpec((tm, D_MODEL), lambda i: (i + h1_block0, 0)),
                  pl.BlockSpec((tm, D_MODEL), lambda i: (i, 0)),
                  _resident(wo.shape), _resident(n2.shape), _resident(wg.shape), _resident(wu.shape),
                  _resident(wd.shape), _resident(nf.shape)],
        out_specs=pl.BlockSpec((tm, D_MODEL), lambda i: (i, 0)),
        out_shape=jax.ShapeDtypeStruct((n, D_MODEL), F32),
        compiler_params=_compiler_params(1),
        name="ffn_out",
    )(h1, mix, wo, n2, wg, wu, wd, nf)


def _head_slices(a, width):
    return [a[:, i:i + width] for i in range(0, a.shape[1], width)]


def _column_tile(row):
    return jnp.broadcast_to(row, (LANES, LANES)).T


def _store_hgrn_out(rows, o, gate, hgn, mix_ref):
    for h in range(HG_HEADS):
        sl = slice(h * HG_DIM, (h + 1) * HG_DIM)
        oh = o[:, sl]
        oh = oh * lax.rsqrt(jnp.mean(oh * oh, axis=-1, keepdims=True) + EPS)
        mix_ref[rows, sl] = (oh * hgn[:, sl] * gate[:, sl]).astype(mix_ref.dtype)


def _hgrn_log_decay(n_pad, tb, fz, lb, tri):
    logf = jnp.log(lb + (1.0 - lb) * jax.nn.sigmoid(fz))
    if n_pad:
        logf = jnp.where(lax.broadcasted_iota(jnp.int32, (tb, HG_WIDTH), 0) >= n_pad, logf, 0.0)
    b = _cumsum_rows(tri, logf)
    chunk_ends = jnp.concatenate([b[r:r + 1] for r in range(HG_CHUNK - 1, tb, HG_CHUNK)], axis=0)
    return b, jnp.max(-chunk_ends)


def _hgrn_block(n_pad, tb, u_ref, lb, hgn, b, worst_log_decay, mix_ref, st_scr, o_scr):
    chunked_is_safe = worst_log_decay < HG_MAX_CHUNK_LOG_DECAY

    @pl.when(chunked_is_safe)
    def _():
        _hgrn_chunked(n_pad, tb, u_ref, lb, hgn, b, mix_ref, st_scr)

    @pl.when(jnp.logical_not(chunked_is_safe))
    def _():
        _hgrn_per_token(n_pad, tb, u_ref, lb, hgn, mix_ref, st_scr, o_scr)


def _hgrn_per_token(n_pad, tb, u_ref, lb, hgn, mix_ref, st_scr, o_scr):
    for h in range(HG_HEADS):
        st_scr[h] = st_scr[h].T

    def sublane_group(i, carry):
        rows = pl.ds(pl.multiple_of(i * SUBLANES, SUBLANES), SUBLANES)
        fz = u_ref[rows, COL_F:COL_F + HG_WIDTH]
        f = lb + (1.0 - lb) * jax.nn.sigmoid(fz)
        kk = (1.0 - lb) * jax.nn.sigmoid(-fz)
        if n_pad:
            valid = i * SUBLANES + lax.broadcasted_iota(jnp.int32, (SUBLANES, HG_WIDTH), 0) >= n_pad
            f = jnp.where(valid, f, 1.0)
            kk = jnp.where(valid, kk, 0.0)
        q = _silu(u_ref[rows, COL_Q:COL_Q + HG_WIDTH])
        v = u_ref[rows, COL_I:COL_I + HG_WIDTH]
        for h in range(HG_HEADS):
            sl = slice(h * HG_DIM, (h + 1) * HG_DIM)
            s = st_scr[h]
            o_rows = []
            for j in range(SUBLANES):
                r = slice(j, j + 1)
                s = s * _column_tile(f[r, sl]) + _column_tile(kk[r, sl]) * v[r, sl]
                o_rows.append(jnp.sum(s * _column_tile(q[r, sl]), axis=0, keepdims=True))
            st_scr[h] = s
            o_scr[rows, sl] = jnp.concatenate(o_rows, axis=0)
        return carry

    lax.fori_loop(0, tb // SUBLANES, sublane_group, 0)
    for h in range(HG_HEADS):
        st_scr[h] = st_scr[h].T
    _store_hgrn_out(slice(0, tb), o_scr[...], _silu(u_ref[:, COL_G:COL_G + HG_WIDTH]), hgn, mix_ref)


def _hgrn_chunked(n_pad, tb, u_ref, lb, hgn, b, mix_ref, st_scr):
    c = HG_CHUNK
    kk = (1.0 - lb) * jax.nn.sigmoid(-u_ref[:, COL_F:COL_F + HG_WIDTH])
    if n_pad:
        kk = jnp.where(lax.broadcasted_iota(jnp.int32, (tb, HG_WIDTH), 0) >= n_pad, kk, 0.0)
    q = _silu(u_ref[:, COL_Q:COL_Q + HG_WIDTH])
    v = u_ref[:, COL_I:COL_I + HG_WIDTH].astype(BF16)
    gate = _silu(u_ref[:, COL_G:COL_G + HG_WIDTH])
    qt = (q * jnp.exp(b)).astype(BF16)
    kt = (kk * jnp.exp(-b)).astype(BF16)
    causal = _tiled_lower_tri(c, HG_HEADS)

    for r0 in range(0, tb, c):
        rows = slice(r0, r0 + c)
        b_c = b[rows]
        b_last = b_c[c - 1:c, :]
        kh_c = (kk[rows] * jnp.exp(b_last - b_c)).astype(BF16)
        decay = jnp.exp(b_last)
        qt_c = qt[rows]
        kd = _block_diag(_head_slices(kt[rows], HG_DIM))
        vd = _block_diag(_head_slices(v[rows], HG_DIM))
        scores = jnp.where(causal, _dot_nt(qt_c, kd), 0.0).astype(BF16)
        o = _dot(scores, vd)
        o_prev = []
        for p in range(HG_PAIRS):
            lanes = slice(p * 2 * HG_DIM, (p + 1) * 2 * HG_DIM)
            heads = (2 * p, 2 * p + 1)
            st = [st_scr[h] for h in heads]
            o_prev.append(_dot_nt(qt_c[:, lanes], _block_diag([s.astype(BF16) for s in st])))
            upd = _dot_tn(vd[p * 2 * c:(p + 1) * 2 * c, lanes], jnp.concatenate([kh_c[:, lanes]] * 2, axis=0))
            for i, h in enumerate(heads):
                blk = slice(i * HG_DIM, (i + 1) * HG_DIM)
                st_scr[h] = st[i] * decay[:, h * HG_DIM:(h + 1) * HG_DIM] + upd[blk, blk]
        _store_hgrn_out(rows, o + jnp.concatenate(o_prev, axis=1), gate[rows], hgn, mix_ref)


def _pair_columns(a, r0, r1, first_half):
    shape = (a.shape[0], LANES)
    return jnp.where(first_half, jnp.broadcast_to(a[:, r0:r0 + 1], shape), jnp.broadcast_to(a[:, r1:r1 + 1], shape))


def _group_columns(a, g, first_half):
    r = g * HEADS_PER_GROUP
    return jnp.concatenate([_pair_columns(a, r + i, r + i + 1, first_half)
                            for i in range(0, HEADS_PER_GROUP, 2)], axis=1)


def _ssd_block(n_pad, tb, u_ref, col0, cw, cb, dtb, a_neg, dsk, ssn, tri, mix_ref, hg_scr, xpad):
    c = SSM_CHUNK
    conv = cb
    for j in range(CONV_WIDTH):
        off = PAD_ROWS - (CONV_WIDTH - 1) + j
        conv = conv + cw[j:j + 1, :] * xpad[off:off + tb, :]
    act = _silu(conv)
    xs = act[:, 0:SSM_WIDTH]
    bm = act[:, SSM_WIDTH:SSM_WIDTH + SSM_GROUPS * SSM_STATE].astype(BF16)
    cm = act[:, SSM_WIDTH + SSM_GROUPS * SSM_STATE:].astype(BF16)
    z_gate = _silu(u_ref[:, col0:col0 + SSM_WIDTH])

    col_dt = col0 + COL_DT - COL_Z
    dt = jax.nn.softplus(u_ref[:, col_dt:col_dt + LANES] + dtb)
    if n_pad:
        dt = jnp.where(lax.broadcasted_iota(jnp.int32, (tb, LANES), 0) >= n_pad, dt, 0.0)
    cum = _cumsum_rows(tri, dt * a_neg)
    causal = _lower_tri(c)
    first_half = lax.broadcasted_iota(jnp.int32, (1, LANES), 1) < SSM_HEAD_DIM
    zeros = jnp.zeros((c, LANES), BF16)

    for r0 in range(0, tb, c):
        rows = slice(r0, r0 + c)
        cum_c = cum[rows]
        dt_c = dt[rows]
        last = cum_c[c - 1:c, :]
        cum_t = cum_c.T
        dt_t = dt_c.T
        e_cum = jnp.exp(cum_c)
        w_in = dt_c * jnp.exp(last - cum_c)
        e_last = jnp.exp(last)
        for g in range(SSM_GROUPS):
            yield
            glanes = slice(g * GROUP_WIDTH, (g + 1) * GROUP_WIDTH)
            bg = bm[rows, g * SSM_STATE:(g + 1) * SSM_STATE]
            cg = cm[rows, g * SSM_STATE:(g + 1) * SSM_STATE]
            xg = xs[rows, glanes]
            cbt = _dot_nt(cg, bg)
            m_heads = []
            for rr in range(HEADS_PER_GROUP):
                r = g * HEADS_PER_GROUP + rr
                seg = jnp.exp(jnp.where(causal, cum_c[:, r:r + 1] - cum_t[r:r + 1, :], -jnp.inf))
                m_heads.append((cbt * seg * dt_t[r:r + 1, :]).astype(BF16))
            xb = xg.astype(BF16)
            xd_rows = []
            for rr in range(HEADS_PER_GROUP):
                tile = xb[:, (rr // 2) * LANES:(rr // 2 + 1) * LANES]
                tile = jnp.where(first_half if rr % 2 == 0 else ~first_half, tile, jnp.zeros_like(tile))
                xd_rows.append(jnp.concatenate([tile, zeros] if rr < 2 else [zeros, tile], axis=1))
            xd = jnp.concatenate(xd_rows, axis=0)
            hg = hg_scr[g]
            y = (_dot(jnp.concatenate(m_heads, axis=1), xd)
                 + _dot(cg, hg.astype(BF16)) * _group_columns(e_cum, g, first_half))
            xw = (xg * _group_columns(w_in, g, first_half)).astype(BF16)
            hg_scr[g] = hg * _group_columns(e_last, g, first_half) + _dot_tn(bg, xw)
            yz = (y + dsk[:, glanes] * xg) * z_gate[rows, glanes]
            yz = yz * lax.rsqrt(jnp.mean(yz * yz, axis=-1, keepdims=True) + EPS) * ssn[:, glanes]
            mix_ref[rows, HG_WIDTH + g * GROUP_WIDTH:HG_WIDTH + (g + 1) * GROUP_WIDTH] = yz.astype(mix_ref.dtype)


def _interleave(main, side, side_steps):
    for n in side_steps:
        if next(main, StopIteration) is StopIteration:
            break
        for _ in range(n):
            next(side, None)
    for _ in main:
        pass
    for _ in side:
        pass


def _mixer_block(*args):
    for _ in _mixer_steps(*args):
        pass


def _mixer_steps(n_pad, tb, uh_ref, us_ref, us_col0, param_refs, mix_ref, st_scr, hg_scr, xpad, o_scr,
                 log_decay=None):
    lbl_ref, hgn_ref, cw_ref, cb_ref, dtb_ref, alog_ref, dsk_ref, ssn_ref, tri_hg_ref, tri_ssm_ref = param_refs
    lb = _forget_lower_bound(lbl_ref[...])
    if log_decay is None:
        b, worst = _hgrn_log_decay(n_pad, tb, uh_ref[:, COL_F:COL_F + HG_WIDTH], lb, tri_hg_ref[...])
    else:
        b, worst = log_decay[0][...], log_decay[1]
    _hgrn_block(n_pad, tb, uh_ref, lb, hgn_ref[...], b, worst, mix_ref, st_scr, o_scr)
    yield
    col_xbc = us_col0 + COL_XBC - COL_Z
    xbc = us_ref[:, col_xbc:col_xbc + CONV_DIM]
    if n_pad:
        xbc = jnp.where(lax.broadcasted_iota(jnp.int32, (tb, CONV_DIM), 0) >= n_pad, xbc, 0.0)
    xpad[PAD_ROWS:PAD_ROWS + tb, :] = xbc
    yield from _ssd_block(n_pad, tb, us_ref, us_col0, cw_ref[...], cb_ref[...], dtb_ref[...],
                          -jnp.exp(alog_ref[...]), dsk_ref[...], ssn_ref[...], tri_ssm_ref[...],
                          mix_ref, hg_scr, xpad)
    xpad[0:PAD_ROWS, :] = xpad[tb:tb + PAD_ROWS, :]


def _load_state(s0_ref, h0_ref, c0_ref, st_scr, hg_scr, xpad):
    for h in range(HG_HEADS):
        st_scr[h] = s0_ref[0, h].T
    for r in range(0, SSM_HEADS, 2):
        g, lane0 = r // HEADS_PER_GROUP, (r % HEADS_PER_GROUP) * SSM_HEAD_DIM
        hg_scr[g, :, lane0:lane0 + LANES] = jnp.concatenate([h0_ref[0, r], h0_ref[0, r + 1]], axis=0).T
    xpad[0:PAD_ROWS, :] = c0_ref[0]


def _store_state(s_out_ref, h_out_ref, c_out_ref, st_scr, hg_scr, xpad):
    for h in range(HG_HEADS):
        s_out_ref[0, h] = st_scr[h].T
    for r in range(0, SSM_HEADS, 2):
        g, lane0 = r // HEADS_PER_GROUP, (r % HEADS_PER_GROUP) * SSM_HEAD_DIM
        pair = hg_scr[g, :, lane0:lane0 + LANES].T
        h_out_ref[0, r] = pair[0:SSM_HEAD_DIM]
        h_out_ref[0, r + 1] = pair[SSM_HEAD_DIM:]
    c_out_ref[0] = xpad[0:PAD_ROWS, :]


N_MIXER_PARAMS = 10


def _scan_params(params, tb):
    n = min(tb, FF_TILE)
    return tuple(params) + (_chunked_tri(n, HG_CHUNK), _chunked_tri(n, SSM_CHUNK))


PROJ_TILE = 512
PROJ_STEPS_AFTER_SCAN_STEP = (3, 2, 1, 1, 1, 1)
HG_SHAPE = (SSM_GROUPS, SSM_STATE, GROUP_WIDTH)
S_SHAPE = (HG_HEADS, HG_DIM, HG_DIM)
H_SHAPE = (SSM_HEADS, SSM_HEAD_DIM, SSM_STATE)
C_SHAPE = (PAD_ROWS, CONV_DIM)


def _scan_kernel(n_pad, tb, project, *refs):
    if project:
        h1_ref, nm_ref, win_ref = refs[:3]
        refs = refs[3:]
    else:
        u_ref = refs[0]
        refs = refs[1:]
    param_refs = refs[:N_MIXER_PARAMS]
    s0_ref, h0_ref, c0_ref, mix_ref, s_out_ref, h_out_ref, c_out_ref = refs[N_MIXER_PARAMS:N_MIXER_PARAMS + 7]
    scratch = refs[N_MIXER_PARAMS + 7:]
    if project:
        uh_scr, us_scr, st_scr, hg_scr, xpad, o_scr = scratch
    else:
        st_scr, hg_scr, xpad, o_scr = scratch
    t = pl.program_id(1)

    @pl.when(t == 0)
    def _():
        _load_state(s0_ref, h0_ref, c0_ref, st_scr, hg_scr, xpad)

    if project:
        hn = _rms(h1_ref[...], nm_ref[...]).astype(BF16)
        uh_scr[...] = _dot(hn, win_ref[:, 0:COL_Z])
        us_scr[...] = _dot(hn, win_ref[:, COL_Z:])
        _mixer_block(n_pad, tb, uh_scr, us_scr, 0, param_refs, mix_ref, st_scr, hg_scr, xpad, o_scr)
    else:
        _mixer_block(n_pad, tb, u_ref, u_ref, COL_Z, param_refs, mix_ref, st_scr, hg_scr, xpad, o_scr)

    @pl.when(t == pl.num_programs(1) - 1)
    def _():
        _store_state(s_out_ref, h_out_ref, c_out_ref, st_scr, hg_scr, xpad)


def _scan_pipelined_kernel(tb, nt, h1_ref, nm_ref, win_ref, *refs):
    param_refs = refs[:N_MIXER_PARAMS]
    s0_ref, h0_ref, c0_ref, mix_ref, s_out_ref, h_out_ref, c_out_ref = refs[N_MIXER_PARAMS:N_MIXER_PARAMS + 7]
    even, odd = refs[N_MIXER_PARAMS + 7:N_MIXER_PARAMS + 11], refs[N_MIXER_PARAMS + 11:N_MIXER_PARAMS + 15]
    st_scr, hg_scr, xpad, o_scr = refs[N_MIXER_PARAMS + 15:]
    s = pl.program_id(0)
    scanned = jnp.maximum(s - 1, 0)

    @pl.when(s == 0)
    def _():
        uh, us, b, worst = odd
        uh[...] = jnp.zeros(uh.shape, F32)
        us[...] = jnp.zeros(us.shape, F32)
        b[...] = jnp.zeros(b.shape, F32)
        worst[0] = 0.0

    @pl.when(scanned % nt == 0)
    def _():
        _load_state(s0_ref, h0_ref, c0_ref, st_scr, hg_scr, xpad)

    def project_steps(uh_w, us_w, b_w, worst_w):
        hn = _rms(h1_ref[...], nm_ref[...]).astype(BF16)
        for c0 in range(0, D_IN_PAD, PROJ_TILE):
            c1 = min(c0 + PROJ_TILE, D_IN_PAD)
            yield
            tile = _dot(hn, win_ref[:, c0:c1])
            if c0 < COL_Z:
                uh_w[:, c0:c1] = tile
            else:
                us_w[:, c0 - COL_Z:c1 - COL_Z] = tile
            if c0 == COL_F:
                assert c1 == COL_F + HG_WIDTH
                lb = _forget_lower_bound(param_refs[0][...])
                b, worst = _hgrn_log_decay(0, tb, tile, lb, param_refs[N_MIXER_PARAMS - 2][...])
                b_w[...] = b
                worst_w[0] = worst

    def body(write, read):
        uh_r, us_r, b_r, worst_r = read
        _interleave(_mixer_steps(0, tb, uh_r, us_r, 0, param_refs, mix_ref, st_scr, hg_scr, xpad, o_scr,
                                 (b_r, worst_r[0])),
                    project_steps(*write), PROJ_STEPS_AFTER_SCAN_STEP)

    pl.when(s % 2 == 0)(lambda: body(even, odd))
    pl.when(s % 2 == 1)(lambda: body(odd, even))

    @pl.when((s >= 1) & (scanned % nt == nt - 1))
    def _():
        _store_state(s_out_ref, h_out_ref, c_out_ref, st_scr, hg_scr, xpad)


def _scan_pipelined(h1, nb, nt, tb, proj, params, s0, h0, c0):
    n_blocks = nb * nt
    last = n_blocks - 1
    lead = tuple(proj) + _scan_params(params, tb)
    shared = lambda shape: pl.BlockSpec((1,) + shape[1:], lambda s: (0,) * len(shape))
    per_seq = lambda shape: pl.BlockSpec((1,) + shape, lambda s: (jnp.maximum(s - 1, 0) // nt,) + (0,) * len(shape))
    u_bufs = [pltpu.VMEM((tb, COL_Z), F32), pltpu.VMEM((tb, D_IN_PAD - COL_Z), F32),
              pltpu.VMEM((tb, HG_WIDTH), F32), pltpu.SMEM((1,), F32)]
    return pl.pallas_call(
        functools.partial(_scan_pipelined_kernel, tb, nt),
        grid=(n_blocks + 1,),
        in_specs=[pl.BlockSpec((tb, D_MODEL), lambda s: (jnp.minimum(s, last), 0))]
        + [_resident(p.shape) for p in lead]
        + [shared(s0.shape), shared(h0.shape), shared(c0.shape)],
        out_specs=[pl.BlockSpec((tb, D_MODEL), lambda s: (jnp.maximum(s - 1, 0), 0)),
                   per_seq(S_SHAPE), per_seq(H_SHAPE), per_seq(C_SHAPE)],
        out_shape=[jax.ShapeDtypeStruct((n_blocks * tb, D_MODEL), BF16),
                   jax.ShapeDtypeStruct((nb,) + S_SHAPE, F32),
                   jax.ShapeDtypeStruct((nb,) + H_SHAPE, F32),
                   jax.ShapeDtypeStruct((nb,) + C_SHAPE, F32)],
        scratch_shapes=u_bufs + u_bufs + [pltpu.VMEM(S_SHAPE, F32), pltpu.VMEM(HG_SHAPE, F32),
                                          pltpu.VMEM((PAD_ROWS + tb, CONV_DIM), F32),
                                          pltpu.VMEM((tb, HG_WIDTH), F32)],
        compiler_params=_compiler_params(1),
        name="proj_mixer",
    )(h1, *lead, s0, h0, c0)


def _scan(src, src_block0, nb, nt, tb, n_pad, proj, params, s0, h0, c0):
    lead = (tuple(proj) if proj else ()) + _scan_params(params, tb)
    shared = lambda shape: pl.BlockSpec((1,) + shape[1:], lambda b, t: (0,) * len(shape))
    per_seq = lambda shape: pl.BlockSpec((1,) + shape, lambda b, t: (b,) + (0,) * len(shape))
    scratch = [pltpu.VMEM(S_SHAPE, F32), pltpu.VMEM(HG_SHAPE, F32), pltpu.VMEM((PAD_ROWS + tb, CONV_DIM), F32),
               pltpu.VMEM((tb, HG_WIDTH), F32)]
    if proj:
        scratch = [pltpu.VMEM((tb, COL_Z), F32), pltpu.VMEM((tb, D_IN_PAD - COL_Z), F32)] + scratch
    return pl.pallas_call(
        functools.partial(_scan_kernel, n_pad, tb, bool(proj)),
        grid=(nb, nt),
        in_specs=[pl.BlockSpec((tb, src.shape[1]), lambda b, t: (src_block0 + b * nt + t, 0))]
        + [_resident(p.shape) for p in lead]
        + [shared(s0.shape), shared(h0.shape), shared(c0.shape)],
        out_specs=[pl.BlockSpec((tb, D_MODEL), lambda b, t: (b * nt + t, 0)),
                   per_seq(S_SHAPE), per_seq(H_SHAPE), per_seq(C_SHAPE)],
        out_shape=[jax.ShapeDtypeStruct((nb * nt * tb, D_MODEL), BF16),
                   jax.ShapeDtypeStruct((nb,) + S_SHAPE, F32),
                   jax.ShapeDtypeStruct((nb,) + H_SHAPE, F32),
                   jax.ShapeDtypeStruct((nb,) + C_SHAPE, F32)],
        scratch_shapes=scratch,
        compiler_params=_compiler_params(2),
        name="proj_mixer" if proj else "mixer",
    )(src, *lead, s0, h0, c0)


def _sample_kernel(nbs, u_ref, lbl_ref, hgn_ref, cw_ref, cb_ref, dtb_ref, alog_ref, dsk_ref, ssn_ref,
                   sh_ref, ss_ref, sc_ref, mix_ref, sh_out_ref, ss_out_ref, sc_out_ref, obuf, ybuf):
    lb = _forget_lower_bound(lbl_ref[...])
    fz = u_ref[:, COL_F:COL_F + HG_WIDTH]
    f = lb + (1.0 - lb) * jax.nn.sigmoid(fz)
    kk = (1.0 - lb) * jax.nn.sigmoid(-fz)
    q = _silu(u_ref[:, COL_Q:COL_Q + HG_WIDTH])
    v = u_ref[:, COL_I:COL_I + HG_WIDTH]

    xbc = u_ref[:, COL_XBC:COL_XBC + CONV_DIM]
    cw = cw_ref[...]
    conv = cb_ref[...] + cw[CONV_WIDTH - 1:CONV_WIDTH, :] * xbc
    for j in range(CONV_WIDTH - 1):
        conv = conv + cw[j:j + 1, :] * sc_ref[j]
    for j in range(CONV_WIDTH - 2):
        sc_out_ref[j] = sc_ref[j + 1]
    sc_out_ref[CONV_WIDTH - 2] = xbc
    act = _silu(conv)
    xs = act[:, 0:SSM_WIDTH]
    bm = act[:, SSM_WIDTH:SSM_WIDTH + SSM_GROUPS * SSM_STATE]
    cm = act[:, SSM_WIDTH + SSM_GROUPS * SSM_STATE:]
    dt = jax.nn.softplus(u_ref[:, COL_DT:COL_DT + LANES] + dtb_ref[...])
    d_a = jnp.exp(dt * (-jnp.exp(alog_ref[...])))

    top_half = lax.broadcasted_iota(jnp.int32, (LANES, LANES), 0) < SSM_HEAD_DIM
    for j in range(nbs):
        row = slice(j, j + 1)
        for h in range(HG_HEADS):
            sl = slice(h * HG_DIM, (h + 1) * HG_DIM)
            s_new = sh_ref[j, h] * _column_tile(f[row, sl]) + _column_tile(kk[row, sl]) * v[row, sl]
            sh_out_ref[j, h] = s_new
            obuf[row, sl] = jnp.sum(s_new * _column_tile(q[row, sl]), axis=0, keepdims=True)
        for rp in range(SSM_HEADS // 2):
            r0, r1 = 2 * rp, 2 * rp + 1
            g = r0 // HEADS_PER_GROUP
            sl = slice(rp * LANES, (rp + 1) * LANES)
            gsl = slice(g * SSM_STATE, (g + 1) * SSM_STATE)
            h2 = jnp.concatenate([ss_ref[j, r0], ss_ref[j, r1]], axis=0)
            da2 = jnp.where(top_half, d_a[row, r0:r0 + 1], d_a[row, r1:r1 + 1])
            dt2 = jnp.where(top_half, dt[row, r0:r0 + 1], dt[row, r1:r1 + 1])
            h_new = h2 * da2 + (dt2 * _column_tile(xs[row, sl])) * bm[row, gsl]
            ss_out_ref[j, r0] = h_new[0:SSM_HEAD_DIM]
            ss_out_ref[j, r1] = h_new[SSM_HEAD_DIM:]
            ybuf[row, sl] = jnp.sum((h_new * cm[row, gsl]).T, axis=0, keepdims=True)

    o = obuf[...]
    gate = _silu(u_ref[:, COL_G:COL_G + HG_WIDTH])
    hgn = hgn_ref[...]
    for h in range(HG_HEADS):
        sl = slice(h * HG_DIM, (h + 1) * HG_DIM)
        oh = o[:, sl]
        oh = oh * lax.rsqrt(jnp.mean(oh * oh, axis=-1, keepdims=True) + EPS)
        mix_ref[:, sl] = oh * hgn[:, sl] * gate[:, sl]
    yz = (ybuf[...] + dsk_ref[...] * xs) * _silu(u_ref[:, COL_Z:COL_Z + SSM_WIDTH])
    ssn = ssn_ref[...]
    for g in range(SSM_GROUPS):
        sl = slice(g * GROUP_WIDTH, (g + 1) * GROUP_WIDTH)
        seg = yz[:, sl]
        seg = seg * lax.rsqrt(jnp.mean(seg * seg, axis=-1, keepdims=True) + EPS) * ssn[:, sl]
        mix_ref[:, HG_WIDTH + g * GROUP_WIDTH:HG_WIDTH + (g + 1) * GROUP_WIDTH] = seg


def _sample_mixer(u, u_block0, n, nbs, params, sh, ss, sc_t):
    small = [_resident(p.shape) for p in params]
    return pl.pallas_call(
        functools.partial(_sample_kernel, nbs),
        grid=(n // nbs,),
        in_specs=[pl.BlockSpec((nbs, D_IN_PAD), lambda i: (u_block0 + i, 0))] + small
        + [pl.BlockSpec((nbs,) + sh.shape[1:], lambda i: (i, 0, 0, 0)),
           pl.BlockSpec((nbs,) + ss.shape[1:], lambda i: (i, 0, 0, 0)),
           pl.BlockSpec((CONV_WIDTH - 1, nbs, CONV_DIM), lambda i: (0, i, 0))],
        out_specs=[pl.BlockSpec((nbs, D_MODEL), lambda i: (i, 0)),
                   pl.BlockSpec((nbs,) + sh.shape[1:], lambda i: (i, 0, 0, 0)),
                   pl.BlockSpec((nbs,) + ss.shape[1:], lambda i: (i, 0, 0, 0)),
                   pl.BlockSpec((CONV_WIDTH - 1, nbs, CONV_DIM), lambda i: (0, i, 0))],
        out_shape=[jax.ShapeDtypeStruct((n, D_MODEL), F32),
                   jax.ShapeDtypeStruct(sh.shape, F32),
                   jax.ShapeDtypeStruct(ss.shape, F32),
                   jax.ShapeDtypeStruct(sc_t.shape, F32)],
        scratch_shapes=[pltpu.VMEM((nbs, HG_WIDTH), F32), pltpu.VMEM((nbs, SSM_WIDTH), F32)],
        compiler_params=_compiler_params(1),
        name="sample_mixer",
    )(u, *params, sh, ss, sc_t)


TM_DENSE = 512
TM_SCAN = 512
TM_SMALL = 128
SAMPLES_PER_STEP = 8


def _pad_lanes(row, value=0.0):
    return jnp.pad(row, ((0, 0), (0, LANES - row.shape[1])), constant_values=value)


def kernel(x_prompt, x_sample, state_hgrn, state_ssm, state_conv, meta_tokens, lb_logits, norm_ffn1, w_ffn1_gate, w_ffn1_up, w_ffn1_down, norm_mix, w_in, hg_norm, conv_w, conv_b, dt_bias, a_log, d_skip, ssm_norm, w_out, norm_ffn2, w_ffn2_gate, w_ffn2_up, w_ffn2_down, norm_final):
    bp, seq_p, _ = x_prompt.shape
    n_s = x_sample.shape[0]
    assert x_sample.shape[1] == 1 and n_s == TM_SMALL and seq_p % TM_SCAN == 0 and (bp * seq_p) % TM_DENSE == 0
    layer = 0

    n1, nm, n2 = norm_ffn1[layer][None], norm_mix[layer][None], norm_ffn2[layer][None]
    nf = norm_final[None]
    mixer_params = (lb_logits, hg_norm[layer][None], conv_w[layer], conv_b[layer][None],
                    _pad_lanes(dt_bias[layer][None]), _pad_lanes(a_log[layer][None]),
                    jnp.repeat(d_skip[layer], SSM_HEAD_DIM)[None], ssm_norm[layer][None])

    n_pad = TM_SMALL - N_META
    x_small = jnp.concatenate([jnp.zeros((n_pad, D_MODEL), F32), meta_tokens, x_sample[:, 0]], axis=0)
    h1_small, wg1, wu1, wd1 = _ffn1_small(x_small, n1, w_ffn1_gate[layer], w_ffn1_up[layer], w_ffn1_down[layer])
    u_small, win = _proj_small(h1_small, nm, jnp.swapaxes(w_in[layer], 0, 1))

    zeros_s = jnp.zeros((1,) + S_SHAPE, F32)
    zeros_h = jnp.zeros((1,) + H_SHAPE, F32)
    zeros_c = jnp.zeros((1,) + C_SHAPE, F32)
    _, s_meta, h_meta, c_meta = _scan(u_small, 0, 1, 1, TM_SMALL, n_pad, None, mixer_params,
                                      zeros_s, zeros_h, zeros_c)

    sc_t = jnp.swapaxes(state_conv[layer], 0, 1)
    mix_s, hgrn_s, ssm_s, conv_s_t = _sample_mixer(
        u_small, TM_SMALL // SAMPLES_PER_STEP, n_s, SAMPLES_PER_STEP, mixer_params,
        state_hgrn[layer], state_ssm[layer], sc_t)

    xp = x_prompt.reshape(bp * seq_p, D_MODEL)
    h1_p, wg2, wu2, wd2, wo = _ffn1(xp, n1, wg1, wu1, wd1, TM_DENSE, w_ffn2_gate[layer], w_ffn2_up[layer],
                                    w_ffn2_down[layer], w_out[layer])
    y_s = _ffn_out(h1_small, 1, mix_s, wo, n2, wg2, wu2, wd2, nf, TM_SMALL)
    mix_p, hgrn_p, ssm_p, conv_p = _scan_pipelined(h1_p, bp, seq_p // TM_SCAN, TM_SCAN, (nm, win), mixer_params,
                                                   s_meta, h_meta, c_meta)
    y_p = _ffn_out(h1_p, 0, mix_p, wo, n2, wg2, wu2, wd2, nf, TM_DENSE)

    keep = slice(PAD_ROWS - (CONV_WIDTH - 1), PAD_ROWS)
    return (y_p.reshape(bp, seq_p, D_MODEL),
            y_s.reshape(n_s, 1, D_MODEL),
            hgrn_p[None], ssm_p[None], conv_p[:, keep][None],
            hgrn_s[None], ssm_s[None], jnp.swapaxes(conv_s_t, 0, 1)[None])
```

```python
import functools

import jax
import jax.numpy as jnp
from jax import lax
from jax.experimental import pallas as pl
from jax.experimental.pallas import tpu as pltpu

F32 = jnp.float32
BF16 = jnp.bfloat16

D_MODEL = 1024
D_FF = 2816
N_META = 16
HG_WIDTH = 512
HG_HEADS = 4
HG_DIM = 128
SSM_WIDTH = 512
SSM_HEADS = 8
SSM_HEAD_DIM = 64
SSM_GROUPS = 2
SSM_STATE = 128
CONV_WIDTH = 4
CONV_DIM = SSM_WIDTH + 2 * SSM_GROUPS * SSM_STATE
EPS = 1e-6

LANES = 128
SUBLANES = 8
VMEM_LIMIT_BYTES = 56 * 1024 * 1024

COL_Q = 0
COL_F = HG_WIDTH
COL_I = 2 * HG_WIDTH
COL_G = 3 * HG_WIDTH
COL_Z = 4 * HG_WIDTH
COL_XBC = COL_Z + SSM_WIDTH
COL_DT = COL_XBC + CONV_DIM
D_IN_PAD = COL_DT + LANES

FF_TILE = 256
HG_CHUNK = 64
HG_MAX_CHUNK_LOG_DECAY = 80.0
SSM_CHUNK = 128
PAD_ROWS = SUBLANES
HG_PAIRS = HG_HEADS // 2
HEADS_PER_GROUP = SSM_HEADS // SSM_GROUPS
GROUP_WIDTH = SSM_WIDTH // SSM_GROUPS
assert 2 * SSM_HEAD_DIM == LANES and 2 * HG_DIM == FF_TILE


def _dot(a, b):
    return jnp.dot(a, b, preferred_element_type=F32)


def _dot_nt(a, b):
    return lax.dot_general(a, b, (((1,), (1,)), ((), ())), preferred_element_type=F32)


def _dot_tn(a, b):
    return lax.dot_general(a, b, (((0,), (0,)), ((), ())), preferred_element_type=F32)


def _rms(x, w):
    return x * lax.rsqrt(jnp.mean(x * x, axis=-1, keepdims=True) + EPS) * w


def _silu(x):
    return x * jax.nn.sigmoid(x)


def _swiglu(xn, wg_ref, wu_ref, wd_ref, side_work=()):
    side_work = list(side_work)
    acc = jnp.zeros((xn.shape[0], D_MODEL), F32)
    for j in range(D_FF // FF_TILE):
        cols = slice(j * FF_TILE, (j + 1) * FF_TILE)
        g = _dot(xn, wg_ref[:, cols])
        u = _dot(xn, wu_ref[:, cols])
        if side_work:
            side_work.pop(0)()
        acc = acc + _dot((_silu(g) * u).astype(BF16), wd_ref[cols, :])
    assert not side_work
    return acc


def _cumsum_rows(tri, a):
    n = tri.shape[0]
    if a.shape[0] > n:
        return jnp.concatenate([_cumsum_rows(tri, a[r:r + n]) for r in range(0, a.shape[0], n)], axis=0)
    a1 = a.astype(BF16)
    r1 = a - a1.astype(F32)
    a2 = r1.astype(BF16)
    a3 = (r1 - a2.astype(F32)).astype(BF16)
    return _dot(tri, a1) + _dot(tri, a2) + _dot(tri, a3)


def _lower_tri(n):
    row = lax.broadcasted_iota(jnp.int32, (n, n), 0)
    col = lax.broadcasted_iota(jnp.int32, (n, n), 1)
    return row >= col


def _chunked_tri(n, chunk):
    assert chunk & (chunk - 1) == 0
    row = lax.broadcasted_iota(jnp.int32, (n, n), 0)
    col = lax.broadcasted_iota(jnp.int32, (n, n), 1)
    same_chunk = (row ^ col) < chunk
    return ((row >= col) & same_chunk).astype(BF16)


def _tiled_lower_tri(chunk, reps):
    assert chunk & (chunk - 1) == 0
    row = lax.broadcasted_iota(jnp.int32, (chunk, reps * chunk), 0)
    col = lax.broadcasted_iota(jnp.int32, (chunk, reps * chunk), 1)
    return row >= (col & (chunk - 1))


def _block_diag(blocks):
    n = len(blocks)
    r, c = blocks[0].shape
    rows = []
    for i, blk in enumerate(blocks):
        parts = []
        if i:
            parts.append(jnp.zeros((r, c * i), blk.dtype))
        parts.append(blk)
        if i < n - 1:
            parts.append(jnp.zeros((r, c * (n - 1 - i)), blk.dtype))
        rows.append(jnp.concatenate(parts, axis=1))
    return jnp.concatenate(rows, axis=0)


def _forget_lower_bound(lbl):
    l0, l1 = lbl[0:1], lbl[1:2]
    m = jnp.maximum(l0, l1)
    e0, e1 = jnp.exp(l0 - m), jnp.exp(l1 - m)
    return e0 / (e0 + e1)


def _resident(shape):
    nd = len(shape)
    return pl.BlockSpec(shape, lambda *_: (0,) * nd, pipeline_mode=pl.Buffered(1))


def _compiler_params(n_grid_axes):
    return pltpu.CompilerParams(dimension_semantics=("arbitrary",) * n_grid_axes,
                                vmem_limit_bytes=VMEM_LIMIT_BYTES)


N_CAST = 4


def _ffn1_kernel(x_ref, n1_ref, wg_ref, wu_ref, wd_ref, *refs):
    cast_in, h1_ref, cast_out = refs[:N_CAST], refs[N_CAST], refs[N_CAST + 1:]

    def cast(src, dst):
        def work():
            dst[...] = src[...].astype(BF16)
        return work

    x = x_ref[...]
    xn = _rms(x, n1_ref[...]).astype(BF16)
    casts = [cast(src, dst) for src, dst in zip(cast_in, cast_out)]
    h1_ref[...] = x + 0.5 * _swiglu(xn, wg_ref, wu_ref, wd_ref, casts)


def _ffn1(x, n1, wg, wu, wd, tm, w2g, w2u, w2d, w_out):
    n = x.shape[0]
    steps = n // tm
    n_ff, n_wo = D_FF // FF_TILE, w_out.shape[0] // FF_TILE
    assert steps >= n_ff and steps >= n_wo
    cast_specs = [pl.BlockSpec((D_MODEL, FF_TILE), lambda i: (0, jnp.minimum(i, n_ff - 1))),
                  pl.BlockSpec((D_MODEL, FF_TILE), lambda i: (0, jnp.minimum(i, n_ff - 1))),
                  pl.BlockSpec((FF_TILE, D_MODEL), lambda i: (jnp.minimum(i, n_ff - 1), 0)),
                  pl.BlockSpec((FF_TILE, w_out.shape[1]), lambda i: (jnp.minimum(i, n_wo - 1), 0))]
    cast = (w2g, w2u, w2d, w_out)
    return pl.pallas_call(
        _ffn1_kernel,
        grid=(steps,),
        in_specs=[pl.BlockSpec((tm, D_MODEL), lambda i: (i, 0)),
                  _resident(n1.shape), _resident(wg.shape), _resident(wu.shape), _resident(wd.shape)] + cast_specs,
        out_specs=[pl.BlockSpec((tm, D_MODEL), lambda i: (i, 0))] + cast_specs,
        out_shape=[jax.ShapeDtypeStruct((n, D_MODEL), F32)] + [jax.ShapeDtypeStruct(w.shape, BF16) for w in cast],
        compiler_params=_compiler_params(1),
        name="ffn1",
    )(x, n1, wg, wu, wd, *cast)


def _ffn_tile_step(xn, wg_ref, wu_ref, wd_ref, wg_out, wu_out, wd_out, acc_scr):
    wg, wu, wd = wg_ref[...].astype(BF16), wu_ref[...].astype(BF16), wd_ref[...].astype(BF16)
    wg_out[...] = wg
    wu_out[...] = wu
    wd_out[...] = wd
    acc_scr[...] += _dot((_silu(_dot(xn, wg)) * _dot(xn, wu)).astype(BF16), wd)


def _ffn1_small_kernel(x_ref, n1_ref, wg_ref, wu_ref, wd_ref, h1_ref, wg_out, wu_out, wd_out, xn_scr, acc_scr):
    j = pl.program_id(0)

    @pl.when(j == 0)
    def _():
        xn_scr[...] = _rms(x_ref[...], n1_ref[...]).astype(BF16)
        acc_scr[...] = jnp.zeros(acc_scr.shape, F32)

    _ffn_tile_step(xn_scr[...], wg_ref, wu_ref, wd_ref, wg_out, wu_out, wd_out, acc_scr)

    @pl.when(j == pl.num_programs(0) - 1)
    def _():
        h1_ref[...] = x_ref[...] + 0.5 * acc_scr[...]


def _weight_tile_specs():
    cols = pl.BlockSpec((D_MODEL, FF_TILE), lambda j: (0, j))
    rows = pl.BlockSpec((FF_TILE, D_MODEL), lambda j: (j, 0))
    shapes = [jax.ShapeDtypeStruct((D_MODEL, D_FF), BF16), jax.ShapeDtypeStruct((D_MODEL, D_FF), BF16),
              jax.ShapeDtypeStruct((D_FF, D_MODEL), BF16)]
    return [cols, cols, rows], shapes


def _ffn1_small(x, n1, wg, wu, wd):
    n = x.shape[0]
    wspecs, wshapes = _weight_tile_specs()
    return pl.pallas_call(
        _ffn1_small_kernel,
        grid=(D_FF // FF_TILE,),
        in_specs=[_resident(x.shape), _resident(n1.shape)] + wspecs,
        out_specs=[pl.BlockSpec((n, D_MODEL), lambda j: (0, 0))] + wspecs,
        out_shape=[jax.ShapeDtypeStruct((n, D_MODEL), F32)] + wshapes,
        scratch_shapes=[pltpu.VMEM((n, D_MODEL), BF16), pltpu.VMEM((n, D_MODEL), F32)],
        compiler_params=_compiler_params(1),
        name="ffn1_small",
    )(x, n1, wg, wu, wd)


W_IN_TILE = 1024


def _proj_small_kernel(n_cols, h1_ref, nm_ref, wt_ref, u_ref, win_out, hn_scr):
    j = pl.program_id(0)

    @pl.when(j == 0)
    def _():
        hn_scr[...] = _rms(h1_ref[...], nm_ref[...]).astype(BF16)

    col = j * W_IN_TILE + lax.broadcasted_iota(jnp.int32, (W_IN_TILE, 1), 0)
    w = jnp.where(col < n_cols, wt_ref[...], 0.0).T.astype(BF16)
    win_out[...] = w
    u_ref[...] = _dot(hn_scr[...], w)


def _proj_small(h1, nm, w_in_t):
    n = h1.shape[0]
    tile = lambda rows: pl.BlockSpec((rows, W_IN_TILE), lambda j: (0, j))
    return pl.pallas_call(
        functools.partial(_proj_small_kernel, w_in_t.shape[0]),
        grid=(pl.cdiv(D_IN_PAD, W_IN_TILE),),
        in_specs=[_resident(h1.shape), _resident(nm.shape), pl.BlockSpec((W_IN_TILE, D_MODEL), lambda j: (j, 0))],
        out_specs=[tile(n), tile(D_MODEL)],
        out_shape=[jax.ShapeDtypeStruct((n, D_IN_PAD), F32), jax.ShapeDtypeStruct((D_MODEL, D_IN_PAD), BF16)],
        scratch_shapes=[pltpu.VMEM((n, D_MODEL), BF16)],
        compiler_params=_compiler_params(1),
        name="proj_small",
    )(h1, nm, w_in_t)


def _ffn_out_kernel(h1_ref, mix_ref, wo_ref, n2_ref, wg_ref, wu_ref, wd_ref, nf_ref, y_ref):
    h2 = h1_ref[...] + _dot(mix_ref[...].astype(BF16), wo_ref[...])
    hn = _rms(h2, n2_ref[...]).astype(BF16)
    h3 = h2 + 0.5 * _swiglu(hn, wg_ref, wu_ref, wd_ref)
    y_ref[...] = _rms(h3, nf_ref[...])


def _ffn_out(h1, h1_block0, mix, wo, n2, wg, wu, wd, nf, tm):
    n = mix.shape[0]
    return pl.pallas_call(
        _ffn_out_kernel,
        grid=(n // tm,),
        in_specs=[pl.BlockSpec((tm, D_MODEL), lambda i: (i + h1_block0, 0)),
                  pl.BlockSpec((tm, D_MODEL), lambda i: (i, 0)),
                  _resident(wo.shape), _resident(n2.shape), _resident(wg.shape), _resident(wu.shape),
                  _resident(wd.shape), _resident(nf.shape)],
        out_specs=pl.BlockSpec((tm, D_MODEL), lambda i: (i, 0)),
        out_shape=jax.ShapeDtypeStruct((n, D_MODEL), F32),
        compiler_params=_compiler_params(1),
        name="ffn_out",
    )(h1, mix, wo, n2, wg, wu, wd, nf)


def _head_slices(a, width):
    return [a[:, i:i + width] for i in range(0, a.shape[1], width)]


def _column_tile(row):
    return jnp.broadcast_to(row, (LANES, LANES)).T


def _store_hgrn_out(rows, o, gate, hgn, mix_ref):
    for h in range(HG_HEADS):
        sl = slice(h * HG_DIM, (h + 1) * HG_DIM)
        oh = o[:, sl]
        oh = oh * lax.rsqrt(jnp.mean(oh * oh, axis=-1, keepdims=True) + EPS)
        mix_ref[rows, sl] = (oh * hgn[:, sl] * gate[:, sl]).astype(mix_ref.dtype)


def _hgrn_log_decay(n_pad, tb, fz, lb, tri):
    logf = jnp.log(lb + (1.0 - lb) * jax.nn.sigmoid(fz))
    if n_pad:
        logf = jnp.where(lax.broadcasted_iota(jnp.int32, (tb, HG_WIDTH), 0) >= n_pad, logf, 0.0)
    b = _cumsum_rows(tri, logf)
    chunk_ends = jnp.concatenate([b[r:r + 1] for r in range(HG_CHUNK - 1, tb, HG_CHUNK)], axis=0)
    return b, jnp.max(-chunk_ends)


def _hgrn_block(n_pad, tb, u_ref, lb, hgn, b, worst_log_decay, mix_ref, st_scr, o_scr):
    chunked_is_safe = worst_log_decay < HG_MAX_CHUNK_LOG_DECAY

    @pl.when(chunked_is_safe)
    def _():
        _hgrn_chunked(n_pad, tb, u_ref, lb, hgn, b, mix_ref, st_scr)

    @pl.when(jnp.logical_not(chunked_is_safe))
    def _():
        _hgrn_per_token(n_pad, tb, u_ref, lb, hgn, mix_ref, st_scr, o_scr)


def _hgrn_per_token(n_pad, tb, u_ref, lb, hgn, mix_ref, st_scr, o_scr):
    for h in range(HG_HEADS):
        st_scr[h] = st_scr[h].T

    def sublane_group(i, carry):
        rows = pl.ds(pl.multiple_of(i * SUBLANES, SUBLANES), SUBLANES)
        fz = u_ref[rows, COL_F:COL_F + HG_WIDTH]
        f = lb + (1.0 - lb) * jax.nn.sigmoid(fz)
        kk = (1.0 - lb) * jax.nn.sigmoid(-fz)
        if n_pad:
            valid = i * SUBLANES + lax.broadcasted_iota(jnp.int32, (SUBLANES, HG_WIDTH), 0) >= n_pad
            f = jnp.where(valid, f, 1.0)
            kk = jnp.where(valid, kk, 0.0)
        q = _silu(u_ref[rows, COL_Q:COL_Q + HG_WIDTH])
        v = u_ref[rows, COL_I:COL_I + HG_WIDTH]
        for h in range(HG_HEADS):
            sl = slice(h * HG_DIM, (h + 1) * HG_DIM)
            s = st_scr[h]
            o_rows = []
            for j in range(SUBLANES):
                r = slice(j, j + 1)
                s = s * _column_tile(f[r, sl]) + _column_tile(kk[r, sl]) * v[r, sl]
                o_rows.append(jnp.sum(s * _column_tile(q[r, sl]), axis=0, keepdims=True))
            st_scr[h] = s
            o_scr[rows, sl] = jnp.concatenate(o_rows, axis=0)
        return carry

    lax.fori_loop(0, tb // SUBLANES, sublane_group, 0)
    for h in range(HG_HEADS):
        st_scr[h] = st_scr[h].T
    _store_hgrn_out(slice(0, tb), o_scr[...], _silu(u_ref[:, COL_G:COL_G + HG_WIDTH]), hgn, mix_ref)


def _hgrn_chunked(n_pad, tb, u_ref, lb, hgn, b, mix_ref, st_scr):
    c = HG_CHUNK
    kk = (1.0 - lb) * jax.nn.sigmoid(-u_ref[:, COL_F:COL_F + HG_WIDTH])
    if n_pad:
        kk = jnp.where(lax.broadcasted_iota(jnp.int32, (tb, HG_WIDTH), 0) >= n_pad, kk, 0.0)
    q = _silu(u_ref[:, COL_Q:COL_Q + HG_WIDTH])
    v = u_ref[:, COL_I:COL_I + HG_WIDTH].astype(BF16)
    gate = _silu(u_ref[:, COL_G:COL_G + HG_WIDTH])
    qt = (q * jnp.exp(b)).astype(BF16)
    kt = (kk * jnp.exp(-b)).astype(BF16)
    causal = _tiled_lower_tri(c, HG_HEADS)

    for r0 in range(0, tb, c):
        rows = slice(r0, r0 + c)
        b_c = b[rows]
        b_last = b_c[c - 1:c, :]
        kh_c = (kk[rows] * jnp.exp(b_last - b_c)).astype(BF16)
        decay = jnp.exp(b_last)
        qt_c = qt[rows]
        kd = _block_diag(_head_slices(kt[rows], HG_DIM))
        vd = _block_diag(_head_slices(v[rows], HG_DIM))
        scores = jnp.where(causal, _dot_nt(qt_c, kd), 0.0).astype(BF16)
        o = _dot(scores, vd)
        o_prev = []
        for p in range(HG_PAIRS):
            lanes = slice(p * 2 * HG_DIM, (p + 1) * 2 * HG_DIM)
            heads = (2 * p, 2 * p + 1)
            st = [st_scr[h] for h in heads]
            o_prev.append(_dot_nt(qt_c[:, lanes], _block_diag([s.astype(BF16) for s in st])))
            upd = _dot_tn(vd[p * 2 * c:(p + 1) * 2 * c, lanes], jnp.concatenate([kh_c[:, lanes]] * 2, axis=0))
            for i, h in enumerate(heads):
                blk = slice(i * HG_DIM, (i + 1) * HG_DIM)
                st_scr[h] = st[i] * decay[:, h * HG_DIM:(h + 1) * HG_DIM] + upd[blk, blk]
        _store_hgrn_out(rows, o + jnp.concatenate(o_prev, axis=1), gate[rows], hgn, mix_ref)


def _pair_columns(a, r0, r1, first_half):
    shape = (a.shape[0], LANES)
    return jnp.where(first_half, jnp.broadcast_to(a[:, r0:r0 + 1], shape), jnp.broadcast_to(a[:, r1:r1 + 1], shape))


def _group_columns(a, g, first_half):
    r = g * HEADS_PER_GROUP
    return jnp.concatenate([_pair_columns(a, r + i, r + i + 1, first_half)
                            for i in range(0, HEADS_PER_GROUP, 2)], axis=1)


def _ssd_block(n_pad, tb, u_ref, col0, cw, cb, dtb, a_neg, dsk, ssn, tri, mix_ref, hg_scr, xpad):
    c = SSM_CHUNK
    conv = cb
    for j in range(CONV_WIDTH):
        off = PAD_ROWS - (CONV_WIDTH - 1) + j
        conv = conv + cw[j:j + 1, :] * xpad[off:off + tb, :]
    act = _silu(conv)
    xs = act[:, 0:SSM_WIDTH]
    bm = act[:, SSM_WIDTH:SSM_WIDTH + SSM_GROUPS * SSM_STATE].astype(BF16)
    cm = act[:, SSM_WIDTH + SSM_GROUPS * SSM_STATE:].astype(BF16)
    z_gate = _silu(u_ref[:, col0:col0 + SSM_WIDTH])

    col_dt = col0 + COL_DT - COL_Z
    dt = jax.nn.softplus(u_ref[:, col_dt:col_dt + LANES] + dtb)
    if n_pad:
        dt = jnp.where(lax.broadcasted_iota(jnp.int32, (tb, LANES), 0) >= n_pad, dt, 0.0)
    cum = _cumsum_rows(tri, dt * a_neg)
    causal = _lower_tri(c)
    first_half = lax.broadcasted_iota(jnp.int32, (1, LANES), 1) < SSM_HEAD_DIM
    zeros = jnp.zeros((c, LANES), BF16)

    for r0 in range(0, tb, c):
        rows = slice(r0, r0 + c)
        cum_c = cum[rows]
        dt_c = dt[rows]
        last = cum_c[c - 1:c, :]
        cum_t = cum_c.T
        dt_t = dt_c.T
        e_cum = jnp.exp(cum_c)
        w_in = dt_c * jnp.exp(last - cum_c)
        e_last = jnp.exp(last)
        for g in range(SSM_GROUPS):
            yield
            glanes = slice(g * GROUP_WIDTH, (g + 1) * GROUP_WIDTH)
            bg = bm[rows, g * SSM_STATE:(g + 1) * SSM_STATE]
            cg = cm[rows, g * SSM_STATE:(g + 1) * SSM_STATE]
            xg = xs[rows, glanes]
            cbt = _dot_nt(cg, bg)
            m_heads = []
            for rr in range(HEADS_PER_GROUP):
                r = g * HEADS_PER_GROUP + rr
                seg = jnp.exp(jnp.where(causal, cum_c[:, r:r + 1] - cum_t[r:r + 1, :], -jnp.inf))
                m_heads.append((cbt * seg * dt_t[r:r + 1, :]).astype(BF16))
            xb = xg.astype(BF16)
            xd_rows = []
            for rr in range(HEADS_PER_GROUP):
                tile = xb[:, (rr // 2) * LANES:(rr // 2 + 1) * LANES]
                tile = jnp.where(first_half if rr % 2 == 0 else ~first_half, tile, jnp.zeros_like(tile))
                xd_rows.append(jnp.concatenate([tile, zeros] if rr < 2 else [zeros, tile], axis=1))
            xd = jnp.concatenate(xd_rows, axis=0)
            hg = hg_scr[g]
            y = (_dot(jnp.concatenate(m_heads, axis=1), xd)
                 + _dot(cg, hg.astype(BF16)) * _group_columns(e_cum, g, first_half))
            xw = (xg * _group_columns(w_in, g, first_half)).astype(BF16)
            hg_scr[g] = hg * _group_columns(e_last, g, first_half) + _dot_tn(bg, xw)
            yz = (y + dsk[:, glanes] * xg) * z_gate[rows, glanes]
            yz = yz * lax.rsqrt(jnp.mean(yz * yz, axis=-1, keepdims=True) + EPS) * ssn[:, glanes]
            mix_ref[rows, HG_WIDTH + g * GROUP_WIDTH:HG_WIDTH + (g + 1) * GROUP_WIDTH] = yz.astype(mix_ref.dtype)


def _interleave(main, side, side_steps):
    for n in side_steps:
        if next(main, StopIteration) is StopIteration:
            break
        for _ in range(n):
            next(side, None)
    for _ in main:
        pass
    for _ in side:
        pass


def _mixer_block(*args):
    for _ in _mixer_steps(*args):
        pass


def _mixer_steps(n_pad, tb, uh_ref, us_ref, us_col0, param_refs, mix_ref, st_scr, hg_scr, xpad, o_scr,
                 log_decay=None):
    lbl_ref, hgn_ref, cw_ref, cb_ref, dtb_ref, alog_ref, dsk_ref, ssn_ref, tri_hg_ref, tri_ssm_ref = param_refs
    lb = _forget_lower_bound(lbl_ref[...])
    if log_decay is None:
        b, worst = _hgrn_log_decay(n_pad, tb, uh_ref[:, COL_F:COL_F + HG_WIDTH], lb, tri_hg_ref[...])
    else:
        b, worst = log_decay[0][...], log_decay[1]
    _hgrn_block(n_pad, tb, uh_ref, lb, hgn_ref[...], b, worst, mix_ref, st_scr, o_scr)
    yield
    col_xbc = us_col0 + COL_XBC - COL_Z
    xbc = us_ref[:, col_xbc:col_xbc + CONV_DIM]
    if n_pad:
        xbc = jnp.where(lax.broadcasted_iota(jnp.int32, (tb, CONV_DIM), 0) >= n_pad, xbc, 0.0)
    xpad[PAD_ROWS:PAD_ROWS + tb, :] = xbc
    yield from _ssd_block(n_pad, tb, us_ref, us_col0, cw_ref[...], cb_ref[...], dtb_ref[...],
                          -jnp.exp(alog_ref[...]), dsk_ref[...], ssn_ref[...], tri_ssm_ref[...],
                          mix_ref, hg_scr, xpad)
    xpad[0:PAD_ROWS, :] = xpad[tb:tb + PAD_ROWS, :]


def _load_state(s0_ref, h0_ref, c0_ref, st_scr, hg_scr, xpad):
    for h in range(HG_HEADS):
        st_scr[h] = s0_ref[0, h].T
    for r in range(0, SSM_HEADS, 2):
        g, lane0 = r // HEADS_PER_GROUP, (r % HEADS_PER_GROUP) * SSM_HEAD_DIM
        hg_scr[g, :, lane0:lane0 + LANES] = jnp.concatenate([h0_ref[0, r], h0_ref[0, r + 1]], axis=0).T
    xpad[0:PAD_ROWS, :] = c0_ref[0]


def _store_state(s_out_ref, h_out_ref, c_out_ref, st_scr, hg_scr, xpad):
    for h in range(HG_HEADS):
        s_out_ref[0, h] = st_scr[h].T
    for r in range(0, SSM_HEADS, 2):
        g, lane0 = r // HEADS_PER_GROUP, (r % HEADS_PER_GROUP) * SSM_HEAD_DIM
        pair = hg_scr[g, :, lane0:lane0 + LANES].T
        h_out_ref[0, r] = pair[0:SSM_HEAD_DIM]
        h_out_ref[0, r + 1] = pair[SSM_HEAD_DIM:]
    c_out_ref[0] = xpad[0:PAD_ROWS, :]


N_MIXER_PARAMS = 10


def _scan_params(params, tb):
    n = min(tb, FF_TILE)
    return tuple(params) + (_chunked_tri(n, HG_CHUNK), _chunked_tri(n, SSM_CHUNK))


PROJ_TILE = 512
PROJ_STEPS_AFTER_SCAN_STEP = (3, 1, 1, 1, 1, 1, 1)
HG_SHAPE = (SSM_GROUPS, SSM_STATE, GROUP_WIDTH)
S_SHAPE = (HG_HEADS, HG_DIM, HG_DIM)
H_SHAPE = (SSM_HEADS, SSM_HEAD_DIM, SSM_STATE)
C_SHAPE = (PAD_ROWS, CONV_DIM)


def _scan_kernel(n_pad, tb, u_ref, *refs):
    param_refs = refs[:N_MIXER_PARAMS]
    (s0_ref, h0_ref, c0_ref, mix_ref, s_out_ref, h_out_ref, c_out_ref,
     st_scr, hg_scr, xpad, o_scr) = refs[N_MIXER_PARAMS:]
    t = pl.program_id(1)

    @pl.when(t == 0)
    def _():
        _load_state(s0_ref, h0_ref, c0_ref, st_scr, hg_scr, xpad)

    _mixer_block(n_pad, tb, u_ref, u_ref, COL_Z, param_refs, mix_ref, st_scr, hg_scr, xpad, o_scr)

    @pl.when(t == pl.num_programs(1) - 1)
    def _():
        _store_state(s_out_ref, h_out_ref, c_out_ref, st_scr, hg_scr, xpad)


def _scan_pipelined_kernel(tb, nt, h1_ref, nm_ref, win_ref, *refs):
    param_refs = refs[:N_MIXER_PARAMS]
    s0_ref, h0_ref, c0_ref, mix_ref, s_out_ref, h_out_ref, c_out_ref = refs[N_MIXER_PARAMS:N_MIXER_PARAMS + 7]
    even, odd = refs[N_MIXER_PARAMS + 7:N_MIXER_PARAMS + 11], refs[N_MIXER_PARAMS + 11:N_MIXER_PARAMS + 15]
    st_scr, hg_scr, xpad, o_scr = refs[N_MIXER_PARAMS + 15:]
    s = pl.program_id(0)
    scanned = jnp.maximum(s - 1, 0)

    @pl.when(s == 0)
    def _():
        uh, us, b, worst = odd
        uh[...] = jnp.zeros(uh.shape, F32)
        us[...] = jnp.zeros(us.shape, F32)
        b[...] = jnp.zeros(b.shape, F32)
        worst[0] = 0.0

    @pl.when(scanned % nt == 0)
    def _():
        _load_state(s0_ref, h0_ref, c0_ref, st_scr, hg_scr, xpad)

    def project_steps(uh_w, us_w, b_w, worst_w):
        hn = _rms(h1_ref[...], nm_ref[...]).astype(BF16)
        for c0 in range(0, D_IN_PAD, PROJ_TILE):
            c1 = min(c0 + PROJ_TILE, D_IN_PAD)
            yield
            tile = _dot(hn, win_ref[:, c0:c1])
            if c0 < COL_Z:
                uh_w[:, c0:c1] = tile
            else:
                us_w[:, c0 - COL_Z:c1 - COL_Z] = tile
            if c0 == COL_F:
                assert c1 == COL_F + HG_WIDTH
                lb = _forget_lower_bound(param_refs[0][...])
                b, worst = _hgrn_log_decay(0, tb, tile, lb, param_refs[N_MIXER_PARAMS - 2][...])
                b_w[...] = b
                worst_w[0] = worst

    def body(write, read):
        uh_r, us_r, b_r, worst_r = read
        _interleave(_mixer_steps(0, tb, uh_r, us_r, 0, param_refs, mix_ref, st_scr, hg_scr, xpad, o_scr,
                                 (b_r, worst_r[0])),
                    project_steps(*write), PROJ_STEPS_AFTER_SCAN_STEP)

    pl.when(s % 2 == 0)(lambda: body(even, odd))
    pl.when(s % 2 == 1)(lambda: body(odd, even))

    @pl.when((s >= 1) & (scanned % nt == nt - 1))
    def _():
        _store_state(s_out_ref, h_out_ref, c_out_ref, st_scr, hg_scr, xpad)


def _scan_pipelined(h1, nb, nt, tb, proj, params, s0, h0, c0):
    n_blocks = nb * nt
    last = n_blocks - 1
    lead = tuple(proj) + _scan_params(params, tb)
    shared = lambda shape: pl.BlockSpec((1,) + shape[1:], lambda s: (0,) * len(shape))
    per_seq = lambda shape: pl.BlockSpec((1,) + shape, lambda s: (jnp.maximum(s - 1, 0) // nt,) + (0,) * len(shape))
    u_bufs = [pltpu.VMEM((tb, COL_Z), F32), pltpu.VMEM((tb, D_IN_PAD - COL_Z), F32),
              pltpu.VMEM((tb, HG_WIDTH), F32), pltpu.SMEM((1,), F32)]
    return pl.pallas_call(
        functools.partial(_scan_pipelined_kernel, tb, nt),
        grid=(n_blocks + 1,),
        in_specs=[pl.BlockSpec((tb, D_MODEL), lambda s: (jnp.minimum(s, last), 0))]
        + [_resident(p.shape) for p in lead]
        + [shared(s0.shape), shared(h0.shape), shared(c0.shape)],
        out_specs=[pl.BlockSpec((tb, D_MODEL), lambda s: (jnp.maximum(s - 1, 0), 0)),
                   per_seq(S_SHAPE), per_seq(H_SHAPE), per_seq(C_SHAPE)],
        out_shape=[jax.ShapeDtypeStruct((n_blocks * tb, D_MODEL), BF16),
                   jax.ShapeDtypeStruct((nb,) + S_SHAPE, F32),
                   jax.ShapeDtypeStruct((nb,) + H_SHAPE, F32),
                   jax.ShapeDtypeStruct((nb,) + C_SHAPE, F32)],
        scratch_shapes=u_bufs + u_bufs + [pltpu.VMEM(S_SHAPE, F32), pltpu.VMEM(HG_SHAPE, F32),
                                          pltpu.VMEM((PAD_ROWS + tb, CONV_DIM), F32),
                                          pltpu.VMEM((tb, HG_WIDTH), F32)],
        compiler_params=_compiler_params(1),
        name="proj_mixer",
    )(h1, *lead, s0, h0, c0)


def _scan(u, u_block0, nb, nt, tb, n_pad, params, s0, h0, c0):
    lead = _scan_params(params, tb)
    shared = lambda shape: pl.BlockSpec((1,) + shape[1:], lambda b, t: (0,) * len(shape))
    per_seq = lambda shape: pl.BlockSpec((1,) + shape, lambda b, t: (b,) + (0,) * len(shape))
    scratch = [pltpu.VMEM(S_SHAPE, F32), pltpu.VMEM(HG_SHAPE, F32), pltpu.VMEM((PAD_ROWS + tb, CONV_DIM), F32),
               pltpu.VMEM((tb, HG_WIDTH), F32)]
    return pl.pallas_call(
        functools.partial(_scan_kernel, n_pad, tb),
        grid=(nb, nt),
        in_specs=[pl.BlockSpec((tb, D_IN_PAD), lambda b, t: (u_block0 + b * nt + t, 0))]
        + [_resident(p.shape) for p in lead]
        + [shared(s0.shape), shared(h0.shape), shared(c0.shape)],
        out_specs=[pl.BlockSpec((tb, D_MODEL), lambda b, t: (b * nt + t, 0)),
                   per_seq(S_SHAPE), per_seq(H_SHAPE), per_seq(C_SHAPE)],
        out_shape=[jax.ShapeDtypeStruct((nb * nt * tb, D_MODEL), BF16),
                   jax.ShapeDtypeStruct((nb,) + S_SHAPE, F32),
                   jax.ShapeDtypeStruct((nb,) + H_SHAPE, F32),
                   jax.ShapeDtypeStruct((nb,) + C_SHAPE, F32)],
        scratch_shapes=scratch,
        compiler_params=_compiler_params(2),
        name="mixer",
    )(u, *lead, s0, h0, c0)


def _sample_kernel(nbs, u_ref, lbl_ref, hgn_ref, cw_ref, cb_ref, dtb_ref, alog_ref, dsk_ref, ssn_ref,
                   sh_ref, ss_ref, sc_ref, mix_ref, sh_out_ref, ss_out_ref, sc_out_ref, obuf, ybuf):
    lb = _forget_lower_bound(lbl_ref[...])
    fz = u_ref[:, COL_F:COL_F + HG_WIDTH]
    f = lb + (1.0 - lb) * jax.nn.sigmoid(fz)
    kk = (1.0 - lb) * jax.nn.sigmoid(-fz)
    q = _silu(u_ref[:, COL_Q:COL_Q + HG_WIDTH])
    v = u_ref[:, COL_I:COL_I + HG_WIDTH]

    xbc = u_ref[:, COL_XBC:COL_XBC + CONV_DIM]
    cw = cw_ref[...]
    conv = cb_ref[...] + cw[CONV_WIDTH - 1:CONV_WIDTH, :] * xbc
    for j in range(CONV_WIDTH - 1):
        conv = conv + cw[j:j + 1, :] * sc_ref[j]
    for j in range(CONV_WIDTH - 2):
        sc_out_ref[j] = sc_ref[j + 1]
    sc_out_ref[CONV_WIDTH - 2] = xbc
    act = _silu(conv)
    xs = act[:, 0:SSM_WIDTH]
    bm = act[:, SSM_WIDTH:SSM_WIDTH + SSM_GROUPS * SSM_STATE]
    cm = act[:, SSM_WIDTH + SSM_GROUPS * SSM_STATE:]
    dt = jax.nn.softplus(u_ref[:, COL_DT:COL_DT + LANES] + dtb_ref[...])
    d_a = jnp.exp(dt * (-jnp.exp(alog_ref[...])))

    top_half = lax.broadcasted_iota(jnp.int32, (LANES, LANES), 0) < SSM_HEAD_DIM
    for j in range(nbs):
        row = slice(j, j + 1)
        for h in range(HG_HEADS):
            sl = slice(h * HG_DIM, (h + 1) * HG_DIM)
            s_new = sh_ref[j, h] * _column_tile(f[row, sl]) + _column_tile(kk[row, sl]) * v[row, sl]
            sh_out_ref[j, h] = s_new
            obuf[row, sl] = jnp.sum(s_new * _column_tile(q[row, sl]), axis=0, keepdims=True)
        for rp in range(SSM_HEADS // 2):
            r0, r1 = 2 * rp, 2 * rp + 1
            g = r0 // HEADS_PER_GROUP
            sl = slice(rp * LANES, (rp + 1) * LANES)
            gsl = slice(g * SSM_STATE, (g + 1) * SSM_STATE)
            h2 = jnp.concatenate([ss_ref[j, r0], ss_ref[j, r1]], axis=0)
            da2 = jnp.where(top_half, d_a[row, r0:r0 + 1], d_a[row, r1:r1 + 1])
            dt2 = jnp.where(top_half, dt[row, r0:r0 + 1], dt[row, r1:r1 + 1])
            h_new = h2 * da2 + (dt2 * _column_tile(xs[row, sl])) * bm[row, gsl]
            ss_out_ref[j, r0] = h_new[0:SSM_HEAD_DIM]
            ss_out_ref[j, r1] = h_new[SSM_HEAD_DIM:]
            ybuf[row, sl] = jnp.sum((h_new * cm[row, gsl]).T, axis=0, keepdims=True)

    o = obuf[...]
    gate = _silu(u_ref[:, COL_G:COL_G + HG_WIDTH])
    hgn = hgn_ref[...]
    for h in range(HG_HEADS):
        sl = slice(h * HG_DIM, (h + 1) * HG_DIM)
        oh = o[:, sl]
        oh = oh * lax.rsqrt(jnp.mean(oh * oh, axis=-1, keepdims=True) + EPS)
        mix_ref[:, sl] = oh * hgn[:, sl] * gate[:, sl]
    yz = (ybuf[...] + dsk_ref[...] * xs) * _silu(u_ref[:, COL_Z:COL_Z + SSM_WIDTH])
    ssn = ssn_ref[...]
    for g in range(SSM_GROUPS):
        sl = slice(g * GROUP_WIDTH, (g + 1) * GROUP_WIDTH)
        seg = yz[:, sl]
        seg = seg * lax.rsqrt(jnp.mean(seg * seg, axis=-1, keepdims=True) + EPS) * ssn[:, sl]
        mix_ref[:, HG_WIDTH + g * GROUP_WIDTH:HG_WIDTH + (g + 1) * GROUP_WIDTH] = seg


def _sample_mixer(u, u_block0, n, nbs, params, sh, ss, sc_t):
    small = [_resident(p.shape) for p in params]
    return pl.pallas_call(
        functools.partial(_sample_kernel, nbs),
        grid=(n // nbs,),
        in_specs=[pl.BlockSpec((nbs, D_IN_PAD), lambda i: (u_block0 + i, 0))] + small
        + [pl.BlockSpec((nbs,) + sh.shape[1:], lambda i: (i, 0, 0, 0)),
           pl.BlockSpec((nbs,) + ss.shape[1:], lambda i: (i, 0, 0, 0)),
           pl.BlockSpec((CONV_WIDTH - 1, nbs, CONV_DIM), lambda i: (0, i, 0))],
        out_specs=[pl.BlockSpec((nbs, D_MODEL), lambda i: (i, 0)),
                   pl.BlockSpec((nbs,) + sh.shape[1:], lambda i: (i, 0, 0, 0)),
                   pl.BlockSpec((nbs,) + ss.shape[1:], lambda i: (i, 0, 0, 0)),
                   pl.BlockSpec((CONV_WIDTH - 1, nbs, CONV_DIM), lambda i: (0, i, 0))],
        out_shape=[jax.ShapeDtypeStruct((n, D_MODEL), F32),
                   jax.ShapeDtypeStruct(sh.shape, F32),
                   jax.ShapeDtypeStruct(ss.shape, F32),
                   jax.ShapeDtypeStruct(sc_t.shape, F32)],
        scratch_shapes=[pltpu.VMEM((nbs, HG_WIDTH), F32), pltpu.VMEM((nbs, SSM_WIDTH), F32)],
        compiler_params=_compiler_params(1),
        name="sample_mixer",
    )(u, *params, sh, ss, sc_t)


TM_DENSE = 512
TM_SCAN = 512
TM_SMALL = 128
SAMPLES_PER_STEP = 8


def _pad_lanes(row, value=0.0):
    return jnp.pad(row, ((0, 0), (0, LANES - row.shape[1])), constant_values=value)


def kernel(x_prompt, x_sample, state_hgrn, state_ssm, state_conv, meta_tokens, lb_logits, norm_ffn1, w_ffn1_gate, w_ffn1_up, w_ffn1_down, norm_mix, w_in, hg_norm, conv_w, conv_b, dt_bias, a_log, d_skip, ssm_norm, w_out, norm_ffn2, w_ffn2_gate, w_ffn2_up, w_ffn2_down, norm_final):
    bp, seq_p, _ = x_prompt.shape
    n_s = x_sample.shape[0]
    assert x_sample.shape[1] == 1 and n_s == TM_SMALL and seq_p % TM_SCAN == 0 and (bp * seq_p) % TM_DENSE == 0
    layer = 0

    n1, nm, n2 = norm_ffn1[layer][None], norm_mix[layer][None], norm_ffn2[layer][None]
    nf = norm_final[None]
    mixer_params = (lb_logits, hg_norm[layer][None], conv_w[layer], conv_b[layer][None],
                    _pad_lanes(dt_bias[layer][None]), _pad_lanes(a_log[layer][None]),
                    jnp.repeat(d_skip[layer], SSM_HEAD_DIM)[None], ssm_norm[layer][None])

    n_pad = TM_SMALL - N_META
    x_small = jnp.concatenate([jnp.zeros((n_pad, D_MODEL), F32), meta_tokens, x_sample[:, 0]], axis=0)
    h1_small, wg1, wu1, wd1 = _ffn1_small(x_small, n1, w_ffn1_gate[layer], w_ffn1_up[layer], w_ffn1_down[layer])
    u_small, win = _proj_small(h1_small, nm, jnp.swapaxes(w_in[layer], 0, 1))

    zeros_s = jnp.zeros((1,) + S_SHAPE, F32)
    zeros_h = jnp.zeros((1,) + H_SHAPE, F32)
    zeros_c = jnp.zeros((1,) + C_SHAPE, F32)
    _, s_meta, h_meta, c_meta = _scan(u_small, 0, 1, 1, TM_SMALL, n_pad, mixer_params, zeros_s, zeros_h, zeros_c)

    sc_t = jnp.swapaxes(state_conv[layer], 0, 1)
    mix_s, hgrn_s, ssm_s, conv_s_t = _sample_mixer(
        u_small, TM_SMALL // SAMPLES_PER_STEP, n_s, SAMPLES_PER_STEP, mixer_params,
        state_hgrn[layer], state_ssm[layer], sc_t)

    xp = x_prompt.reshape(bp * seq_p, D_MODEL)
    h1_p, wg2, wu2, wd2, wo = _ffn1(xp, n1, wg1, wu1, wd1, TM_DENSE, w_ffn2_gate[layer], w_ffn2_up[layer],
                                    w_ffn2_down[layer], w_out[layer])
    y_s = _ffn_out(h1_small, 1, mix_s, wo, n2, wg2, wu2, wd2, nf, TM_SMALL)
    mix_p, hgrn_p, ssm_p, conv_p = _scan_pipelined(h1_p, bp, seq_p // TM_SCAN, TM_SCAN, (nm, win), mixer_params,
                                                   s_meta, h_meta, c_meta)
    y_p = _ffn_out(h1_p, 0, mix_p, wo, n2, wg2, wu2, wd2, nf, TM_DENSE)

    keep = slice(PAD_ROWS - (CONV_WIDTH - 1), PAD_ROWS)
    return (y_p.reshape(bp, seq_p, D_MODEL),
            y_s.reshape(n_s, 1, D_MODEL),
            hgrn_p[None], ssm_p[None], conv_p[:, keep][None],
            hgrn_s[None], ssm_s[None], jnp.swapaxes(conv_s_t, 0, 1)[None])
```

```python
import functools

import jax
import jax.numpy as jnp
from jax import lax
from jax.experimental import pallas as pl
from jax.experimental.pallas import tpu as pltpu

F32 = jnp.float32
BF16 = jnp.bfloat16

D_MODEL = 1024
D_FF = 2816
N_META = 16
HG_WIDTH = 512
HG_HEADS = 4
HG_DIM = 128
SSM_WIDTH = 512
SSM_HEADS = 8
SSM_HEAD_DIM = 64
SSM_GROUPS = 2
SSM_STATE = 128
CONV_WIDTH = 4
CONV_DIM = SSM_WIDTH + 2 * SSM_GROUPS * SSM_STATE
EPS = 1e-6

LANES = 128
SUBLANES = 8
VMEM_LIMIT_BYTES = 56 * 1024 * 1024

COL_Q = 0
COL_F = HG_WIDTH
COL_I = 2 * HG_WIDTH
COL_G = 3 * HG_WIDTH
COL_Z = 4 * HG_WIDTH
COL_XBC = COL_Z + SSM_WIDTH
COL_DT = COL_XBC + CONV_DIM
D_IN_PAD = COL_DT + LANES

FF_TILE = 256
HG_CHUNK = 64
HG_MAX_CHUNK_LOG_DECAY = 80.0
SSM_CHUNK = 128
PAD_ROWS = SUBLANES
HG_PAIRS = HG_HEADS // 2
HEADS_PER_GROUP = SSM_HEADS // SSM_GROUPS
GROUP_WIDTH = SSM_WIDTH // SSM_GROUPS
assert 2 * SSM_HEAD_DIM == LANES and 2 * HG_DIM == FF_TILE


def _dot(a, b):
    return jnp.dot(a, b, preferred_element_type=F32)


def _dot_nt(a, b):
    return lax.dot_general(a, b, (((1,), (1,)), ((), ())), preferred_element_type=F32)


def _dot_tn(a, b):
    return lax.dot_general(a, b, (((0,), (0,)), ((), ())), preferred_element_type=F32)


def _rms(x, w):
    return x * lax.rsqrt(jnp.mean(x * x, axis=-1, keepdims=True) + EPS) * w


def _silu(x):
    return x * jax.nn.sigmoid(x)


def _swiglu(xn, wg_ref, wu_ref, wd_ref, side_work=()):
    side_work = list(side_work)
    acc = jnp.zeros((xn.shape[0], D_MODEL), F32)
    for j in range(D_FF // FF_TILE):
        cols = slice(j * FF_TILE, (j + 1) * FF_TILE)
        g = _dot(xn, wg_ref[:, cols])
        u = _dot(xn, wu_ref[:, cols])
        if side_work:
            side_work.pop(0)()
        acc = acc + _dot((_silu(g) * u).astype(BF16), wd_ref[cols, :])
    assert not side_work
    return acc


def _cumsum_rows(tri, a):
    n = tri.shape[0]
    if a.shape[0] > n:
        return jnp.concatenate([_cumsum_rows(tri, a[r:r + n]) for r in range(0, a.shape[0], n)], axis=0)
    a1 = a.astype(BF16)
    r1 = a - a1.astype(F32)
    a2 = r1.astype(BF16)
    a3 = (r1 - a2.astype(F32)).astype(BF16)
    return _dot(tri, a1) + _dot(tri, a2) + _dot(tri, a3)


def _lower_tri(n):
    row = lax.broadcasted_iota(jnp.int32, (n, n), 0)
    col = lax.broadcasted_iota(jnp.int32, (n, n), 1)
    return row >= col


def _chunked_tri(n, chunk):
    assert chunk & (chunk - 1) == 0
    row = lax.broadcasted_iota(jnp.int32, (n, n), 0)
    col = lax.broadcasted_iota(jnp.int32, (n, n), 1)
    same_chunk = (row ^ col) < chunk
    return ((row >= col) & same_chunk).astype(BF16)


def _tiled_lower_tri(chunk, reps):
    assert chunk & (chunk - 1) == 0
    row = lax.broadcasted_iota(jnp.int32, (chunk, reps * chunk), 0)
    col = lax.broadcasted_iota(jnp.int32, (chunk, reps * chunk), 1)
    return row >= (col & (chunk - 1))


def _block_diag(blocks):
    n = len(blocks)
    r, c = blocks[0].shape
    rows = []
    for i, blk in enumerate(blocks):
        parts = []
        if i:
            parts.append(jnp.zeros((r, c * i), blk.dtype))
        parts.append(blk)
        if i < n - 1:
            parts.append(jnp.zeros((r, c * (n - 1 - i)), blk.dtype))
        rows.append(jnp.concatenate(parts, axis=1))
    return jnp.concatenate(rows, axis=0)


def _forget_lower_bound(lbl):
    l0, l1 = lbl[0:1], lbl[1:2]
    m = jnp.maximum(l0, l1)
    e0, e1 = jnp.exp(l0 - m), jnp.exp(l1 - m)
    return e0 / (e0 + e1)


def _resident(shape):
    nd = len(shape)
    return pl.BlockSpec(shape, lambda *_: (0,) * nd, pipeline_mode=pl.Buffered(1))


def _compiler_params(n_grid_axes):
    return pltpu.CompilerParams(dimension_semantics=("arbitrary",) * n_grid_axes,
                                vmem_limit_bytes=VMEM_LIMIT_BYTES)


N_CAST = 4


def _ffn1_kernel(x_ref, n1_ref, wg_ref, wu_ref, wd_ref, *refs):
    cast_in, h1_ref, cast_out = refs[:N_CAST], refs[N_CAST], refs[N_CAST + 1:]

    def cast(src, dst):
        def work():
            dst[...] = src[...].astype(BF16)
        return work

    x = x_ref[...]
    xn = _rms(x, n1_ref[...]).astype(BF16)
    casts = [cast(src, dst) for src, dst in zip(cast_in, cast_out)]
    h1_ref[...] = x + 0.5 * _swiglu(xn, wg_ref, wu_ref, wd_ref, casts)


def _ffn1(x, n1, wg, wu, wd, tm, w2g, w2u, w2d, w_out):
    n = x.shape[0]
    steps = n // tm
    n_ff, n_wo = D_FF // FF_TILE, w_out.shape[0] // FF_TILE
    assert steps >= n_ff and steps >= n_wo
    cast_specs = [pl.BlockSpec((D_MODEL, FF_TILE), lambda i: (0, jnp.minimum(i, n_ff - 1))),
                  pl.BlockSpec((D_MODEL, FF_TILE), lambda i: (0, jnp.minimum(i, n_ff - 1))),
                  pl.BlockSpec((FF_TILE, D_MODEL), lambda i: (jnp.minimum(i, n_ff - 1), 0)),
                  pl.BlockSpec((FF_TILE, w_out.shape[1]), lambda i: (jnp.minimum(i, n_wo - 1), 0))]
    cast = (w2g, w2u, w2d, w_out)
    return pl.pallas_call(
        _ffn1_kernel,
        grid=(steps,),
        in_specs=[pl.BlockSpec((tm, D_MODEL), lambda i: (i, 0)),
                  _resident(n1.shape), _resident(wg.shape), _resident(wu.shape), _resident(wd.shape)] + cast_specs,
        out_specs=[pl.BlockSpec((tm, D_MODEL), lambda i: (i, 0))] + cast_specs,
        out_shape=[jax.ShapeDtypeStruct((n, D_MODEL), F32)] + [jax.ShapeDtypeStruct(w.shape, BF16) for w in cast],
        compiler_params=_compiler_params(1),
        name="ffn1",
    )(x, n1, wg, wu, wd, *cast)


def _ffn_tile_step(xn, wg_ref, wu_ref, wd_ref, wg_out, wu_out, wd_out, acc_scr):
    wg, wu, wd = wg_ref[...].astype(BF16), wu_ref[...].astype(BF16), wd_ref[...].astype(BF16)
    wg_out[...] = wg
    wu_out[...] = wu
    wd_out[...] = wd
    acc_scr[...] += _dot((_silu(_dot(xn, wg)) * _dot(xn, wu)).astype(BF16), wd)


def _ffn1_small_kernel(x_ref, n1_ref, wg_ref, wu_ref, wd_ref, h1_ref, wg_out, wu_out, wd_out, xn_scr, acc_scr):
    j = pl.program_id(0)

    @pl.when(j == 0)
    def _():
        xn_scr[...] = _rms(x_ref[...], n1_ref[...]).astype(BF16)
        acc_scr[...] = jnp.zeros(acc_scr.shape, F32)

    _ffn_tile_step(xn_scr[...], wg_ref, wu_ref, wd_ref, wg_out, wu_out, wd_out, acc_scr)

    @pl.when(j == pl.num_programs(0) - 1)
    def _():
        h1_ref[...] = x_ref[...] + 0.5 * acc_scr[...]


def _weight_tile_specs():
    cols = pl.BlockSpec((D_MODEL, FF_TILE), lambda j: (0, j))
    rows = pl.BlockSpec((FF_TILE, D_MODEL), lambda j: (j, 0))
    shapes = [jax.ShapeDtypeStruct((D_MODEL, D_FF), BF16), jax.ShapeDtypeStruct((D_MODEL, D_FF), BF16),
              jax.ShapeDtypeStruct((D_FF, D_MODEL), BF16)]
    return [cols, cols, rows], shapes


def _ffn1_small(x, n1, wg, wu, wd):
    n = x.shape[0]
    wspecs, wshapes = _weight_tile_specs()
    return pl.pallas_call(
        _ffn1_small_kernel,
        grid=(D_FF // FF_TILE,),
        in_specs=[_resident(x.shape), _resident(n1.shape)] + wspecs,
        out_specs=[pl.BlockSpec((n, D_MODEL), lambda j: (0, 0))] + wspecs,
        out_shape=[jax.ShapeDtypeStruct((n, D_MODEL), F32)] + wshapes,
        scratch_shapes=[pltpu.VMEM((n, D_MODEL), BF16), pltpu.VMEM((n, D_MODEL), F32)],
        compiler_params=_compiler_params(1),
        name="ffn1_small",
    )(x, n1, wg, wu, wd)


W_IN_TILE = 1024


def _proj_small_kernel(n_cols, h1_ref, nm_ref, wt_ref, u_ref, win_out, hn_scr):
    j = pl.program_id(0)

    @pl.when(j == 0)
    def _():
        hn_scr[...] = _rms(h1_ref[...], nm_ref[...]).astype(BF16)

    col = j * W_IN_TILE + lax.broadcasted_iota(jnp.int32, (W_IN_TILE, 1), 0)
    w = jnp.where(col < n_cols, wt_ref[...], 0.0).T.astype(BF16)
    win_out[...] = w
    u_ref[...] = _dot(hn_scr[...], w)


def _proj_small(h1, nm, w_in_t):
    n = h1.shape[0]
    tile = lambda rows: pl.BlockSpec((rows, W_IN_TILE), lambda j: (0, j))
    return pl.pallas_call(
        functools.partial(_proj_small_kernel, w_in_t.shape[0]),
        grid=(pl.cdiv(D_IN_PAD, W_IN_TILE),),
        in_specs=[_resident(h1.shape), _resident(nm.shape), pl.BlockSpec((W_IN_TILE, D_MODEL), lambda j: (j, 0))],
        out_specs=[tile(n), tile(D_MODEL)],
        out_shape=[jax.ShapeDtypeStruct((n, D_IN_PAD), F32), jax.ShapeDtypeStruct((D_MODEL, D_IN_PAD), BF16)],
        scratch_shapes=[pltpu.VMEM((n, D_MODEL), BF16)],
        compiler_params=_compiler_params(1),
        name="proj_small",
    )(h1, nm, w_in_t)


def _ffn_out_kernel(h1_ref, mix_ref, wo_ref, n2_ref, wg_ref, wu_ref, wd_ref, nf_ref, y_ref):
    h2 = h1_ref[...] + _dot(mix_ref[...].astype(BF16), wo_ref[...])
    hn = _rms(h2, n2_ref[...]).astype(BF16)
    h3 = h2 + 0.5 * _swiglu(hn, wg_ref, wu_ref, wd_ref)
    y_ref[...] = _rms(h3, nf_ref[...])


def _ffn_out(h1, h1_block0, mix, wo, n2, wg, wu, wd, nf, tm):
    n = mix.shape[0]
    return pl.pallas_call(
        _ffn_out_kernel,
        grid=(n // tm,),
        in_specs=[pl.BlockSpec((tm, D_MODEL), lambda i: (i + h1_block0, 0)),
                  pl.BlockSpec((tm, D_MODEL), lambda i: (i, 0)),
                  _resident(wo.shape), _resident(n2.shape), _resident(wg.shape), _resident(wu.shape),
                  _resident(wd.shape), _resident(nf.shape)],
        out_specs=pl.BlockSpec((tm, D_MODEL), lambda i: (i, 0)),
        out_shape=jax.ShapeDtypeStruct((n, D_MODEL), F32),
        compiler_params=_compiler_params(1),
        name="ffn_out",
    )(h1, mix, wo, n2, wg, wu, wd, nf)


def _head_slices(a, width):
    return [a[:, i:i + width] for i in range(0, a.shape[1], width)]


def _column_tile(row):
    return jnp.broadcast_to(row, (LANES, LANES)).T


def _store_hgrn_out(rows, o, gate, hgn, mix_ref):
    for h in range(HG_HEADS):
        sl = slice(h * HG_DIM, (h + 1) * HG_DIM)
        oh = o[:, sl]
        oh = oh * lax.rsqrt(jnp.mean(oh * oh, axis=-1, keepdims=True) + EPS)
        mix_ref[rows, sl] = (oh * hgn[:, sl] * gate[:, sl]).astype(mix_ref.dtype)


def _hgrn_log_decay(n_pad, tb, fz, lb, tri):
    logf = jnp.log(lb + (1.0 - lb) * jax.nn.sigmoid(fz))
    if n_pad:
        logf = jnp.where(lax.broadcasted_iota(jnp.int32, (tb, HG_WIDTH), 0) >= n_pad, logf, 0.0)
    b = _cumsum_rows(tri, logf)
    chunk_ends = jnp.concatenate([b[r:r + 1] for r in range(HG_CHUNK - 1, tb, HG_CHUNK)], axis=0)
    return b, jnp.max(-chunk_ends)


def _hgrn_block(n_pad, tb, u_ref, lb, hgn, b, worst_log_decay, mix_ref, st_scr, o_scr):
    chunked_is_safe = worst_log_decay < HG_MAX_CHUNK_LOG_DECAY

    @pl.when(chunked_is_safe)
    def _():
        _hgrn_chunked(n_pad, tb, u_ref, lb, hgn, b, mix_ref, st_scr)

    @pl.when(jnp.logical_not(chunked_is_safe))
    def _():
        _hgrn_per_token(n_pad, tb, u_ref, lb, hgn, mix_ref, st_scr, o_scr)


def _hgrn_per_token(n_pad, tb, u_ref, lb, hgn, mix_ref, st_scr, o_scr):
    for h in range(HG_HEADS):
        st_scr[h] = st_scr[h].T

    def sublane_group(i, carry):
        rows = pl.ds(pl.multiple_of(i * SUBLANES, SUBLANES), SUBLANES)
        fz = u_ref[rows, COL_F:COL_F + HG_WIDTH]
        f = lb + (1.0 - lb) * jax.nn.sigmoid(fz)
        kk = (1.0 - lb) * jax.nn.sigmoid(-fz)
        if n_pad:
            valid = i * SUBLANES + lax.broadcasted_iota(jnp.int32, (SUBLANES, HG_WIDTH), 0) >= n_pad
            f = jnp.where(valid, f, 1.0)
            kk = jnp.where(valid, kk, 0.0)
        q = _silu(u_ref[rows, COL_Q:COL_Q + HG_WIDTH])
        v = u_ref[rows, COL_I:COL_I + HG_WIDTH]
        for h in range(HG_HEADS):
            sl = slice(h * HG_DIM, (h + 1) * HG_DIM)
            s = st_scr[h]
            o_rows = []
            for j in range(SUBLANES):
                r = slice(j, j + 1)
                s = s * _column_tile(f[r, sl]) + _column_tile(kk[r, sl]) * v[r, sl]
                o_rows.append(jnp.sum(s * _column_tile(q[r, sl]), axis=0, keepdims=True))
            st_scr[h] = s
            o_scr[rows, sl] = jnp.concatenate(o_rows, axis=0)
        return carry

    lax.fori_loop(0, tb // SUBLANES, sublane_group, 0)
    for h in range(HG_HEADS):
        st_scr[h] = st_scr[h].T
    _store_hgrn_out(slice(0, tb), o_scr[...], _silu(u_ref[:, COL_G:COL_G + HG_WIDTH]), hgn, mix_ref)


def _hgrn_chunked(n_pad, tb, u_ref, lb, hgn, b, mix_ref, st_scr):
    c = HG_CHUNK
    kk = (1.0 - lb) * jax.nn.sigmoid(-u_ref[:, COL_F:COL_F + HG_WIDTH])
    if n_pad:
        kk = jnp.where(lax.broadcasted_iota(jnp.int32, (tb, HG_WIDTH), 0) >= n_pad, kk, 0.0)
    q = _silu(u_ref[:, COL_Q:COL_Q + HG_WIDTH])
    v = u_ref[:, COL_I:COL_I + HG_WIDTH].astype(BF16)
    gate = _silu(u_ref[:, COL_G:COL_G + HG_WIDTH])
    qt = (q * jnp.exp(b)).astype(BF16)
    kt = (kk * jnp.exp(-b)).astype(BF16)
    causal = _tiled_lower_tri(c, HG_HEADS)

    for r0 in range(0, tb, c):
        rows = slice(r0, r0 + c)
        b_c = b[rows]
        b_last = b_c[c - 1:c, :]
        kh_c = (kk[rows] * jnp.exp(b_last - b_c)).astype(BF16)
        decay = jnp.exp(b_last)
        qt_c = qt[rows]
        kd = _block_diag(_head_slices(kt[rows], HG_DIM))
        vd = _block_diag(_head_slices(v[rows], HG_DIM))
        scores = jnp.where(causal, _dot_nt(qt_c, kd), 0.0).astype(BF16)
        o = _dot(scores, vd)
        o_prev = []
        for p in range(HG_PAIRS):
            lanes = slice(p * 2 * HG_DIM, (p + 1) * 2 * HG_DIM)
            heads = (2 * p, 2 * p + 1)
            st = [st_scr[h] for h in heads]
            o_prev.append(_dot_nt(qt_c[:, lanes], _block_diag([s.astype(BF16) for s in st])))
            upd = _dot_tn(vd[p * 2 * c:(p + 1) * 2 * c, lanes], jnp.concatenate([kh_c[:, lanes]] * 2, axis=0))
            for i, h in enumerate(heads):
                blk = slice(i * HG_DIM, (i + 1) * HG_DIM)
                st_scr[h] = st[i] * decay[:, h * HG_DIM:(h + 1) * HG_DIM] + upd[blk, blk]
        _store_hgrn_out(rows, o + jnp.concatenate(o_prev, axis=1), gate[rows], hgn, mix_ref)


def _pair_columns(a, r0, r1, first_half):
    shape = (a.shape[0], LANES)
    return jnp.where(first_half, jnp.broadcast_to(a[:, r0:r0 + 1], shape), jnp.broadcast_to(a[:, r1:r1 + 1], shape))


def _group_columns(a, g, first_half):
    r = g * HEADS_PER_GROUP
    return jnp.concatenate([_pair_columns(a, r + i, r + i + 1, first_half)
                            for i in range(0, HEADS_PER_GROUP, 2)], axis=1)


def _ssd_block(n_pad, tb, u_ref, col0, cw, cb, dtb, a_neg, dsk, ssn, tri, mix_ref, hg_scr, xpad):
    c = SSM_CHUNK
    conv = cb
    for j in range(CONV_WIDTH):
        off = PAD_ROWS - (CONV_WIDTH - 1) + j
        conv = conv + cw[j:j + 1, :] * xpad[off:off + tb, :]
    act = _silu(conv)
    xs = act[:, 0:SSM_WIDTH]
    bm = act[:, SSM_WIDTH:SSM_WIDTH + SSM_GROUPS * SSM_STATE].astype(BF16)
    cm = act[:, SSM_WIDTH + SSM_GROUPS * SSM_STATE:].astype(BF16)
    z_gate = _silu(u_ref[:, col0:col0 + SSM_WIDTH])

    col_dt = col0 + COL_DT - COL_Z
    dt = jax.nn.softplus(u_ref[:, col_dt:col_dt + LANES] + dtb)
    if n_pad:
        dt = jnp.where(lax.broadcasted_iota(jnp.int32, (tb, LANES), 0) >= n_pad, dt, 0.0)
    cum = _cumsum_rows(tri, dt * a_neg)
    causal = _lower_tri(c)
    first_half = lax.broadcasted_iota(jnp.int32, (1, LANES), 1) < SSM_HEAD_DIM
    zeros = jnp.zeros((c, LANES), BF16)

    for r0 in range(0, tb, c):
        rows = slice(r0, r0 + c)
        cum_c = cum[rows]
        dt_c = dt[rows]
        last = cum_c[c - 1:c, :]
        cum_t = cum_c.T
        dt_t = dt_c.T
        e_cum = jnp.exp(cum_c)
        w_in = dt_c * jnp.exp(last - cum_c)
        e_last = jnp.exp(last)
        for g in range(SSM_GROUPS):
            yield
            glanes = slice(g * GROUP_WIDTH, (g + 1) * GROUP_WIDTH)
            bg = bm[rows, g * SSM_STATE:(g + 1) * SSM_STATE]
            cg = cm[rows, g * SSM_STATE:(g + 1) * SSM_STATE]
            xg = xs[rows, glanes]
            cbt = _dot_nt(cg, bg)
            m_heads = []
            for rr in range(HEADS_PER_GROUP):
                r = g * HEADS_PER_GROUP + rr
                seg = jnp.exp(jnp.where(causal, cum_c[:, r:r + 1] - cum_t[r:r + 1, :], -jnp.inf))
                m_heads.append((cbt * seg * dt_t[r:r + 1, :]).astype(BF16))
            xb = xg.astype(BF16)
            xd_rows = []
            for rr in range(HEADS_PER_GROUP):
                tile = xb[:, (rr // 2) * LANES:(rr // 2 + 1) * LANES]
                tile = jnp.where(first_half if rr % 2 == 0 else ~first_half, tile, jnp.zeros_like(tile))
                xd_rows.append(jnp.concatenate([tile, zeros] if rr < 2 else [zeros, tile], axis=1))
            xd = jnp.concatenate(xd_rows, axis=0)
            hg = hg_scr[g]
            y = (_dot(jnp.concatenate(m_heads, axis=1), xd)
                 + _dot(cg, hg.astype(BF16)) * _group_columns(e_cum, g, first_half))
            xw = (xg * _group_columns(w_in, g, first_half)).astype(BF16)
            hg_scr[g] = hg * _group_columns(e_last, g, first_half) + _dot_tn(bg, xw)
            yz = (y + dsk[:, glanes] * xg) * z_gate[rows, glanes]
            yz = yz * lax.rsqrt(jnp.mean(yz * yz, axis=-1, keepdims=True) + EPS) * ssn[:, glanes]
            mix_ref[rows, HG_WIDTH + g * GROUP_WIDTH:HG_WIDTH + (g + 1) * GROUP_WIDTH] = yz.astype(mix_ref.dtype)


def _interleave(main, side, side_steps):
    for n in side_steps:
        if next(main, StopIteration) is StopIteration:
            break
        for _ in range(n):
            next(side, None)
    for _ in main:
        pass
    for _ in side:
        pass


def _mixer_block(*args):
    for _ in _mixer_steps(*args):
        pass


def _mixer_steps(n_pad, tb, uh_ref, us_ref, us_col0, param_refs, mix_ref, st_scr, hg_scr, xpad, o_scr,
                 log_decay=None):
    lbl_ref, hgn_ref, cw_ref, cb_ref, dtb_ref, alog_ref, dsk_ref, ssn_ref, tri_hg_ref, tri_ssm_ref = param_refs
    lb = _forget_lower_bound(lbl_ref[...])
    if log_decay is None:
        b, worst = _hgrn_log_decay(n_pad, tb, uh_ref[:, COL_F:COL_F + HG_WIDTH], lb, tri_hg_ref[...])
    else:
        b, worst = log_decay[0][...], log_decay[1]
    _hgrn_block(n_pad, tb, uh_ref, lb, hgn_ref[...], b, worst, mix_ref, st_scr, o_scr)
    yield
    col_xbc = us_col0 + COL_XBC - COL_Z
    xbc = us_ref[:, col_xbc:col_xbc + CONV_DIM]
    if n_pad:
        xbc = jnp.where(lax.broadcasted_iota(jnp.int32, (tb, CONV_DIM), 0) >= n_pad, xbc, 0.0)
    xpad[PAD_ROWS:PAD_ROWS + tb, :] = xbc
    yield from _ssd_block(n_pad, tb, us_ref, us_col0, cw_ref[...], cb_ref[...], dtb_ref[...],
                          -jnp.exp(alog_ref[...]), dsk_ref[...], ssn_ref[...], tri_ssm_ref[...],
                          mix_ref, hg_scr, xpad)
    xpad[0:PAD_ROWS, :] = xpad[tb:tb + PAD_ROWS, :]


def _load_state(s0_ref, h0_ref, c0_ref, st_scr, hg_scr, xpad):
    for h in range(HG_HEADS):
        st_scr[h] = s0_ref[0, h].T
    for r in range(0, SSM_HEADS, 2):
        g, lane0 = r // HEADS_PER_GROUP, (r % HEADS_PER_GROUP) * SSM_HEAD_DIM
        hg_scr[g, :, lane0:lane0 + LANES] = jnp.concatenate([h0_ref[0, r], h0_ref[0, r + 1]], axis=0).T
    xpad[0:PAD_ROWS, :] = c0_ref[0]


def _store_state(s_out_ref, h_out_ref, c_out_ref, st_scr, hg_scr, xpad):
    for h in range(HG_HEADS):
        s_out_ref[0, h] = st_scr[h].T
    for r in range(0, SSM_HEADS, 2):
        g, lane0 = r // HEADS_PER_GROUP, (r % HEADS_PER_GROUP) * SSM_HEAD_DIM
        pair = hg_scr[g, :, lane0:lane0 + LANES].T
        h_out_ref[0, r] = pair[0:SSM_HEAD_DIM]
        h_out_ref[0, r + 1] = pair[SSM_HEAD_DIM:]
    c_out_ref[0] = xpad[0:PAD_ROWS, :]


N_MIXER_PARAMS = 10


def _scan_params(params, tb):
    n = min(tb, FF_TILE)
    return tuple(params) + (_chunked_tri(n, HG_CHUNK), _chunked_tri(n, SSM_CHUNK))


PROJ_TILE = 512
PROJ_STEPS_AFTER_SCAN_STEP = (3, 1, 1, 1, 1, 1, 1)
HG_SHAPE = (SSM_GROUPS, SSM_STATE, GROUP_WIDTH)
S_SHAPE = (HG_HEADS, HG_DIM, HG_DIM)
H_SHAPE = (SSM_HEADS, SSM_HEAD_DIM, SSM_STATE)
C_SHAPE = (PAD_ROWS, CONV_DIM)


def _scan_kernel(n_pad, tb, u_ref, *refs):
    param_refs = refs[:N_MIXER_PARAMS]
    (s0_ref, h0_ref, c0_ref, mix_ref, s_out_ref, h_out_ref, c_out_ref,
     st_scr, hg_scr, xpad, o_scr) = refs[N_MIXER_PARAMS:]
    t = pl.program_id(1)

    @pl.when(t == 0)
    def _():
        _load_state(s0_ref, h0_ref, c0_ref, st_scr, hg_scr, xpad)

    _mixer_block(n_pad, tb, u_ref, u_ref, COL_Z, param_refs, mix_ref, st_scr, hg_scr, xpad, o_scr)

    @pl.when(t == pl.num_programs(1) - 1)
    def _():
        _store_state(s_out_ref, h_out_ref, c_out_ref, st_scr, hg_scr, xpad)


def _scan_pipelined_kernel(tb, nt, h1_ref, nm_ref, win_ref, *refs):
    param_refs = refs[:N_MIXER_PARAMS]
    s0_ref, h0_ref, c0_ref, mix_ref, s_out_ref, h_out_ref, c_out_ref = refs[N_MIXER_PARAMS:N_MIXER_PARAMS + 7]
    even, odd = refs[N_MIXER_PARAMS + 7:N_MIXER_PARAMS + 11], refs[N_MIXER_PARAMS + 11:N_MIXER_PARAMS + 15]
    st_scr, hg_scr, xpad, o_scr = refs[N_MIXER_PARAMS + 15:]
    s = pl.program_id(0)
    scanned = jnp.maximum(s - 1, 0)

    @pl.when(s == 0)
    def _():
        uh, us, b, worst = odd
        uh[...] = jnp.zeros(uh.shape, F32)
        us[...] = jnp.zeros(us.shape, F32)
        b[...] = jnp.zeros(b.shape, F32)
        worst[0] = 0.0

    @pl.when(scanned % nt == 0)
    def _():
        _load_state(s0_ref, h0_ref, c0_ref, st_scr, hg_scr, xpad)

    def project_steps(uh_w, us_w, b_w, worst_w):
        hn = _rms(h1_ref[...], nm_ref[...]).astype(BF16)
        for c0 in range(0, D_IN_PAD, PROJ_TILE):
            c1 = min(c0 + PROJ_TILE, D_IN_PAD)
            yield
            tile = _dot(hn, win_ref[:, c0:c1])
            if c0 < COL_Z:
                uh_w[:, c0:c1] = tile
            else:
                us_w[:, c0 - COL_Z:c1 - COL_Z] = tile
            if c0 == COL_F:
                assert c1 == COL_F + HG_WIDTH
                lb = _forget_lower_bound(param_refs[0][...])
                b, worst = _hgrn_log_decay(0, tb, tile, lb, param_refs[N_MIXER_PARAMS - 2][...])
                b_w[...] = b
                worst_w[0] = worst

    def body(write, read):
        uh_r, us_r, b_r, worst_r = read
        _interleave(_mixer_steps(0, tb, uh_r, us_r, 0, param_refs, mix_ref, st_scr, hg_scr, xpad, o_scr,
                                 (b_r, worst_r[0])),
                    project_steps(*write), PROJ_STEPS_AFTER_SCAN_STEP)

    pl.when(s % 2 == 0)(lambda: body(even, odd))
    pl.when(s % 2 == 1)(lambda: body(odd, even))

    @pl.when((s >= 1) & (scanned % nt == nt - 1))
    def _():
        _store_state(s_out_ref, h_out_ref, c_out_ref, st_scr, hg_scr, xpad)


def _scan_pipelined(h1, nb, nt, tb, proj, params, s0, h0, c0):
    n_blocks = nb * nt
    last = n_blocks - 1
    lead = tuple(proj) + _scan_params(params, tb)
    shared = lambda shape: pl.BlockSpec((1,) + shape[1:], lambda s: (0,) * len(shape))
    per_seq = lambda shape: pl.BlockSpec((1,) + shape, lambda s: (jnp.maximum(s - 1, 0) // nt,) + (0,) * len(shape))
    u_bufs = [pltpu.VMEM((tb, COL_Z), F32), pltpu.VMEM((tb, D_IN_PAD - COL_Z), F32),
              pltpu.VMEM((tb, HG_WIDTH), F32), pltpu.SMEM((1,), F32)]
    return pl.pallas_call(
        functools.partial(_scan_pipelined_kernel, tb, nt),
        grid=(n_blocks + 1,),
        in_specs=[pl.BlockSpec((tb, D_MODEL), lambda s: (jnp.minimum(s, last), 0))]
        + [_resident(p.shape) for p in lead]
        + [shared(s0.shape), shared(h0.shape), shared(c0.shape)],
        out_specs=[pl.BlockSpec((tb, D_MODEL), lambda s: (jnp.maximum(s - 1, 0), 0)),
                   per_seq(S_SHAPE), per_seq(H_SHAPE), per_seq(C_SHAPE)],
        out_shape=[jax.ShapeDtypeStruct((n_blocks * tb, D_MODEL), BF16),
                   jax.ShapeDtypeStruct((nb,) + S_SHAPE, F32),
                   jax.ShapeDtypeStruct((nb,) + H_SHAPE, F32),
                   jax.ShapeDtypeStruct((nb,) + C_SHAPE, F32)],
        scratch_shapes=u_bufs + u_bufs + [pltpu.VMEM(S_SHAPE, F32), pltpu.VMEM(HG_SHAPE, F32),
                                          pltpu.VMEM((PAD_ROWS + tb, CONV_DIM), F32),
                                          pltpu.VMEM((tb, HG_WIDTH), F32)],
        compiler_params=_compiler_params(1),
        name="proj_mixer",
    )(h1, *lead, s0, h0, c0)


def _scan(u, u_block0, nb, nt, tb, n_pad, params, s0, h0, c0):
    lead = _scan_params(params, tb)
    shared = lambda shape: pl.BlockSpec((1,) + shape[1:], lambda b, t: (0,) * len(shape))
    per_seq = lambda shape: pl.BlockSpec((1,) + shape, lambda b, t: (b,) + (0,) * len(shape))
    scratch = [pltpu.VMEM(S_SHAPE, F32), pltpu.VMEM(HG_SHAPE, F32), pltpu.VMEM((PAD_ROWS + tb, CONV_DIM), F32),
               pltpu.VMEM((tb, HG_WIDTH), F32)]
    return pl.pallas_call(
        functools.partial(_scan_kernel, n_pad, tb),
        grid=(nb, nt),
        in_specs=[pl.BlockSpec((tb, D_IN_PAD), lambda b, t: (u_block0 + b * nt + t, 0))]
        + [_resident(p.shape) for p in lead]
        + [shared(s0.shape), shared(h0.shape), shared(c0.shape)],
        out_specs=[pl.BlockSpec((tb, D_MODEL), lambda b, t: (b * nt + t, 0)),
                   per_seq(S_SHAPE), per_seq(H_SHAPE), per_seq(C_SHAPE)],
        out_shape=[jax.ShapeDtypeStruct((nb * nt * tb, D_MODEL), BF16),
                   jax.ShapeDtypeStruct((nb,) + S_SHAPE, F32),
                   jax.ShapeDtypeStruct((nb,) + H_SHAPE, F32),
                   jax.ShapeDtypeStruct((nb,) + C_SHAPE, F32)],
        scratch_shapes=scratch,
        compiler_params=_compiler_params(2),
        name="mixer",
    )(u, *lead, s0, h0, c0)


def _sample_kernel(nbs, u_ref, lbl_ref, hgn_ref, cw_ref, cb_ref, dtb_ref, alog_ref, dsk_ref, ssn_ref,
                   sh_ref, ss_ref, sc_ref, mix_ref, sh_out_ref, ss_out_ref, sc_out_ref, obuf, ybuf):
    lb = _forget_lower_bound(lbl_ref[...])
    fz = u_ref[:, COL_F:COL_F + HG_WIDTH]
    f = lb + (1.0 - lb) * jax.nn.sigmoid(fz)
    kk = (1.0 - lb) * jax.nn.sigmoid(-fz)
    q = _silu(u_ref[:, COL_Q:COL_Q + HG_WIDTH])
    v = u_ref[:, COL_I:COL_I + HG_WIDTH]

    xbc = u_ref[:, COL_XBC:COL_XBC + CONV_DIM]
    cw = cw_ref[...]
    conv = cb_ref[...] + cw[CONV_WIDTH - 1:CONV_WIDTH, :] * xbc
    for j in range(CONV_WIDTH - 1):
        conv = conv + cw[j:j + 1, :] * sc_ref[j]
    for j in range(CONV_WIDTH - 2):
        sc_out_ref[j] = sc_ref[j + 1]
    sc_out_ref[CONV_WIDTH - 2] = xbc
    act = _silu(conv)
    xs = act[:, 0:SSM_WIDTH]
    bm = act[:, SSM_WIDTH:SSM_WIDTH + SSM_GROUPS * SSM_STATE]
    cm = act[:, SSM_WIDTH + SSM_GROUPS * SSM_STATE:]
    dt = jax.nn.softplus(u_ref[:, COL_DT:COL_DT + LANES] + dtb_ref[...])
    d_a = jnp.exp(dt * (-jnp.exp(alog_ref[...])))

    top_half = lax.broadcasted_iota(jnp.int32, (LANES, LANES), 0) < SSM_HEAD_DIM
    for j in range(nbs):
        row = slice(j, j + 1)
        for h in range(HG_HEADS):
            sl = slice(h * HG_DIM, (h + 1) * HG_DIM)
            s_new = sh_ref[j, h] * _column_tile(f[row, sl]) + _column_tile(kk[row, sl]) * v[row, sl]
            sh_out_ref[j, h] = s_new
            obuf[row, sl] = jnp.sum(s_new * _column_tile(q[row, sl]), axis=0, keepdims=True)
        for rp in range(SSM_HEADS // 2):
            r0, r1 = 2 * rp, 2 * rp + 1
            g = r0 // HEADS_PER_GROUP
            sl = slice(rp * LANES, (rp + 1) * LANES)
            gsl = slice(g * SSM_STATE, (g + 1) * SSM_STATE)
            h2 = jnp.concatenate([ss_ref[j, r0], ss_ref[j, r1]], axis=0)
            da2 = jnp.where(top_half, d_a[row, r0:r0 + 1], d_a[row, r1:r1 + 1])
            dt2 = jnp.where(top_half, dt[row, r0:r0 + 1], dt[row, r1:r1 + 1])
            h_new = h2 * da2 + (dt2 * _column_tile(xs[row, sl])) * bm[row, gsl]
            ss_out_ref[j, r0] = h_new[0:SSM_HEAD_DIM]
            ss_out_ref[j, r1] = h_new[SSM_HEAD_DIM:]
            ybuf[row, sl] = jnp.sum((h_new * cm[row, gsl]).T, axis=0, keepdims=True)

    o = obuf[...]
    gate = _silu(u_ref[:, COL_G:COL_G + HG_WIDTH])
    hgn = hgn_ref[...]
    for h in range(HG_HEADS):
        sl = slice(h * HG_DIM, (h + 1) * HG_DIM)
        oh = o[:, sl]
        oh = oh * lax.rsqrt(jnp.mean(oh * oh, axis=-1, keepdims=True) + EPS)
        mix_ref[:, sl] = oh * hgn[:, sl] * gate[:, sl]
    yz = (ybuf[...] + dsk_ref[...] * xs) * _silu(u_ref[:, COL_Z:COL_Z + SSM_WIDTH])
    ssn = ssn_ref[...]
    for g in range(SSM_GROUPS):
        sl = slice(g * GROUP_WIDTH, (g + 1) * GROUP_WIDTH)
        seg = yz[:, sl]
        seg = seg * lax.rsqrt(jnp.mean(seg * seg, axis=-1, keepdims=True) + EPS) * ssn[:, sl]
        mix_ref[:, HG_WIDTH + g * GROUP_WIDTH:HG_WIDTH + (g + 1) * GROUP_WIDTH] = seg


def _sample_mixer(u, u_block0, n, nbs, params, sh, ss, sc_t):
    small = [_resident(p.shape) for p in params]
    return pl.pallas_call(
        functools.partial(_sample_kernel, nbs),
        grid=(n // nbs,),
        in_specs=[pl.BlockSpec((nbs, D_IN_PAD), lambda i: (u_block0 + i, 0))] + small
        + [pl.BlockSpec((nbs,) + sh.shape[1:], lambda i: (i, 0, 0, 0)),
           pl.BlockSpec((nbs,) + ss.shape[1:], lambda i: (i, 0, 0, 0)),
           pl.BlockSpec((CONV_WIDTH - 1, nbs, CONV_DIM), lambda i: (0, i, 0))],
        out_specs=[pl.BlockSpec((nbs, D_MODEL), lambda i: (i, 0)),
                   pl.BlockSpec((nbs,) + sh.shape[1:], lambda i: (i, 0, 0, 0)),
                   pl.BlockSpec((nbs,) + ss.shape[1:], lambda i: (i, 0, 0, 0)),
                   pl.BlockSpec((CONV_WIDTH - 1, nbs, CONV_DIM), lambda i: (0, i, 0))],
        out_shape=[jax.ShapeDtypeStruct((n, D_MODEL), F32),
                   jax.ShapeDtypeStruct(sh.shape, F32),
                   jax.ShapeDtypeStruct(ss.shape, F32),
                   jax.ShapeDtypeStruct(sc_t.shape, F32)],
        scratch_shapes=[pltpu.VMEM((nbs, HG_WIDTH), F32), pltpu.VMEM((nbs, SSM_WIDTH), F32)],
        compiler_params=_compiler_params(1),
        name="sample_mixer",
    )(u, *params, sh, ss, sc_t)


TM_DENSE = 512
TM_SCAN = 512
TM_SMALL = 128
SAMPLES_PER_STEP = 16


def _pad_lanes(row, value=0.0):
    return jnp.pad(row, ((0, 0), (0, LANES - row.shape[1])), constant_values=value)


def kernel(x_prompt, x_sample, state_hgrn, state_ssm, state_conv, meta_tokens, lb_logits, norm_ffn1, w_ffn1_gate, w_ffn1_up, w_ffn1_down, norm_mix, w_in, hg_norm, conv_w, conv_b, dt_bias, a_log, d_skip, ssm_norm, w_out, norm_ffn2, w_ffn2_gate, w_ffn2_up, w_ffn2_down, norm_final):
    bp, seq_p, _ = x_prompt.shape
    n_s = x_sample.shape[0]
    assert x_sample.shape[1] == 1 and n_s == TM_SMALL and seq_p % TM_SCAN == 0 and (bp * seq_p) % TM_DENSE == 0
    layer = 0

    n1, nm, n2 = norm_ffn1[layer][None], norm_mix[layer][None], norm_ffn2[layer][None]
    nf = norm_final[None]
    mixer_params = (lb_logits, hg_norm[layer][None], conv_w[layer], conv_b[layer][None],
                    _pad_lanes(dt_bias[layer][None]), _pad_lanes(a_log[layer][None]),
                    jnp.repeat(d_skip[layer], SSM_HEAD_DIM)[None], ssm_norm[layer][None])

    n_pad = TM_SMALL - N_META
    x_small = jnp.concatenate([jnp.zeros((n_pad, D_MODEL), F32), meta_tokens, x_sample[:, 0]], axis=0)
    h1_small, wg1, wu1, wd1 = _ffn1_small(x_small, n1, w_ffn1_gate[layer], w_ffn1_up[layer], w_ffn1_down[layer])
    u_small, win = _proj_small(h1_small, nm, jnp.swapaxes(w_in[layer], 0, 1))

    zeros_s = jnp.zeros((1,) + S_SHAPE, F32)
    zeros_h = jnp.zeros((1,) + H_SHAPE, F32)
    zeros_c = jnp.zeros((1,) + C_SHAPE, F32)
    _, s_meta, h_meta, c_meta = _scan(u_small, 0, 1, 1, TM_SMALL, n_pad, mixer_params, zeros_s, zeros_h, zeros_c)

    sc_t = jnp.swapaxes(state_conv[layer], 0, 1)
    mix_s, hgrn_s, ssm_s, conv_s_t = _sample_mixer(
        u_small, TM_SMALL // SAMPLES_PER_STEP, n_s, SAMPLES_PER_STEP, mixer_params,
        state_hgrn[layer], state_ssm[layer], sc_t)

    xp = x_prompt.reshape(bp * seq_p, D_MODEL)
    h1_p, wg2, wu2, wd2, wo = _ffn1(xp, n1, wg1, wu1, wd1, TM_DENSE, w_ffn2_gate[layer], w_ffn2_up[layer],
                                    w_ffn2_down[layer], w_out[layer])
    y_s = _ffn_out(h1_small, 1, mix_s, wo, n2, wg2, wu2, wd2, nf, TM_SMALL)
    mix_p, hgrn_p, ssm_p, conv_p = _scan_pipelined(h1_p, bp, seq_p // TM_SCAN, TM_SCAN, (nm, win), mixer_params,
                                                   s_meta, h_meta, c_meta)
    y_p = _ffn_out(h1_p, 0, mix_p, wo, n2, wg2, wu2, wd2, nf, TM_DENSE)

    keep = slice(PAD_ROWS - (CONV_WIDTH - 1), PAD_ROWS)
    return (y_p.reshape(bp, seq_p, D_MODEL),
            y_s.reshape(n_s, 1, D_MODEL),
            hgrn_p[None], ssm_p[None], conv_p[:, keep][None],
            hgrn_s[None], ssm_s[None], jnp.swapaxes(conv_s_t, 0, 1)[None])
```

```python
import functools

import jax
import jax.numpy as jnp
from jax import lax
from jax.experimental import pallas as pl
from jax.experimental.pallas import tpu as pltpu

F32 = jnp.float32
BF16 = jnp.bfloat16

D_MODEL = 1024
D_FF = 2816
N_META = 16
HG_WIDTH = 512
HG_HEADS = 4
HG_DIM = 128
SSM_WIDTH = 512
SSM_HEADS = 8
SSM_HEAD_DIM = 64
SSM_GROUPS = 2
SSM_STATE = 128
CONV_WIDTH = 4
CONV_DIM = SSM_WIDTH + 2 * SSM_GROUPS * SSM_STATE
EPS = 1e-6

LANES = 128
SUBLANES = 8
VMEM_LIMIT_BYTES = 56 * 1024 * 1024

COL_Q = 0
COL_F = HG_WIDTH
COL_I = 2 * HG_WIDTH
COL_G = 3 * HG_WIDTH
COL_Z = 4 * HG_WIDTH
COL_XBC = COL_Z + SSM_WIDTH
COL_DT = COL_XBC + CONV_DIM
D_IN_PAD = COL_DT + LANES

FF_TILE = 256
HG_CHUNK = 64
HG_MAX_CHUNK_LOG_DECAY = 80.0
SSM_CHUNK = 128
PAD_ROWS = SUBLANES
HG_PAIRS = HG_HEADS // 2
HEADS_PER_GROUP = SSM_HEADS // SSM_GROUPS
GROUP_WIDTH = SSM_WIDTH // SSM_GROUPS
assert 2 * SSM_HEAD_DIM == LANES and 2 * HG_DIM == FF_TILE


def _dot(a, b):
    return jnp.dot(a, b, preferred_element_type=F32)


def _dot_nt(a, b):
    return lax.dot_general(a, b, (((1,), (1,)), ((), ())), preferred_element_type=F32)


def _dot_tn(a, b):
    return lax.dot_general(a, b, (((0,), (0,)), ((), ())), preferred_element_type=F32)


def _rms(x, w):
    return x * lax.rsqrt(jnp.mean(x * x, axis=-1, keepdims=True) + EPS) * w


def _silu(x):
    return x * jax.nn.sigmoid(x)


def _swiglu(xn, wg_ref, wu_ref, wd_ref, side_work=()):
    side_work = list(side_work)
    acc = jnp.zeros((xn.shape[0], D_MODEL), F32)
    for j in range(D_FF // FF_TILE):
        cols = slice(j * FF_TILE, (j + 1) * FF_TILE)
        g = _dot(xn, wg_ref[:, cols])
        u = _dot(xn, wu_ref[:, cols])
        if side_work:
            side_work.pop(0)()
        acc = acc + _dot((_silu(g) * u).astype(BF16), wd_ref[cols, :])
    assert not side_work
    return acc


def _cumsum_rows(tri, a):
    n = tri.shape[0]
    if a.shape[0] > n:
        return jnp.concatenate([_cumsum_rows(tri, a[r:r + n]) for r in range(0, a.shape[0], n)], axis=0)
    a1 = a.astype(BF16)
    r1 = a - a1.astype(F32)
    a2 = r1.astype(BF16)
    a3 = (r1 - a2.astype(F32)).astype(BF16)
    return _dot(tri, a1) + _dot(tri, a2) + _dot(tri, a3)


def _lower_tri(n):
    row = lax.broadcasted_iota(jnp.int32, (n, n), 0)
    col = lax.broadcasted_iota(jnp.int32, (n, n), 1)
    return row >= col


def _chunked_tri(n, chunk):
    assert chunk & (chunk - 1) == 0
    row = lax.broadcasted_iota(jnp.int32, (n, n), 0)
    col = lax.broadcasted_iota(jnp.int32, (n, n), 1)
    same_chunk = (row ^ col) < chunk
    return ((row >= col) & same_chunk).astype(BF16)


def _tiled_lower_tri(chunk, reps):
    assert chunk & (chunk - 1) == 0
    row = lax.broadcasted_iota(jnp.int32, (chunk, reps * chunk), 0)
    col = lax.broadcasted_iota(jnp.int32, (chunk, reps * chunk), 1)
    return row >= (col & (chunk - 1))


def _block_diag(blocks):
    n = len(blocks)
    r, c = blocks[0].shape
    rows = []
    for i, blk in enumerate(blocks):
        parts = []
        if i:
            parts.append(jnp.zeros((r, c * i), blk.dtype))
        parts.append(blk)
        if i < n - 1:
            parts.append(jnp.zeros((r, c * (n - 1 - i)), blk.dtype))
        rows.append(jnp.concatenate(parts, axis=1))
    return jnp.concatenate(rows, axis=0)


def _forget_lower_bound(lbl):
    l0, l1 = lbl[0:1], lbl[1:2]
    m = jnp.maximum(l0, l1)
    e0, e1 = jnp.exp(l0 - m), jnp.exp(l1 - m)
    return e0 / (e0 + e1)


def _resident(shape):
    nd = len(shape)
    return pl.BlockSpec(shape, lambda *_: (0,) * nd, pipeline_mode=pl.Buffered(1))


def _compiler_params(n_grid_axes):
    return pltpu.CompilerParams(dimension_semantics=("arbitrary",) * n_grid_axes,
                                vmem_limit_bytes=VMEM_LIMIT_BYTES)


N_CAST = 4
CAST_TILE = LANES


def _ffn1_kernel(x_ref, n1_ref, wg_ref, wu_ref, wd_ref, *refs):
    cast_in, h1_ref, cast_out = refs[:N_CAST], refs[N_CAST], refs[N_CAST + 1:]

    def cast(src, dst):
        def work():
            dst[...] = src[...].astype(BF16)
        return work

    x = x_ref[...]
    xn = _rms(x, n1_ref[...]).astype(BF16)
    casts = [cast(src, dst) for src, dst in zip(cast_in, cast_out)]
    h1_ref[...] = x + 0.5 * _swiglu(xn, wg_ref, wu_ref, wd_ref, casts)


def _ffn1(x, n1, wg, wu, wd, tm, w2g, w2u, w2d, w_out):
    n = x.shape[0]
    steps = n // tm
    n_ff, n_wo = D_FF // CAST_TILE, w_out.shape[0] // CAST_TILE
    assert steps >= n_ff and steps >= n_wo
    cast_specs = [pl.BlockSpec((D_MODEL, CAST_TILE), lambda i: (0, jnp.minimum(i, n_ff - 1))),
                  pl.BlockSpec((D_MODEL, CAST_TILE), lambda i: (0, jnp.minimum(i, n_ff - 1))),
                  pl.BlockSpec((CAST_TILE, D_MODEL), lambda i: (jnp.minimum(i, n_ff - 1), 0)),
                  pl.BlockSpec((CAST_TILE, w_out.shape[1]), lambda i: (jnp.minimum(i, n_wo - 1), 0))]
    cast = (w2g, w2u, w2d, w_out)
    return pl.pallas_call(
        _ffn1_kernel,
        grid=(steps,),
        in_specs=[pl.BlockSpec((tm, D_MODEL), lambda i: (i, 0)),
                  _resident(n1.shape), _resident(wg.shape), _resident(wu.shape), _resident(wd.shape)] + cast_specs,
        out_specs=[pl.BlockSpec((tm, D_MODEL), lambda i: (i, 0))] + cast_specs,
        out_shape=[jax.ShapeDtypeStruct((n, D_MODEL), F32)] + [jax.ShapeDtypeStruct(w.shape, BF16) for w in cast],
        compiler_params=_compiler_params(1),
        name="ffn1",
    )(x, n1, wg, wu, wd, *cast)


def _ffn_tile_step(xn, wg_ref, wu_ref, wd_ref, wg_out, wu_out, wd_out, acc_scr):
    wg, wu, wd = wg_ref[...].astype(BF16), wu_ref[...].astype(BF16), wd_ref[...].astype(BF16)
    wg_out[...] = wg
    wu_out[...] = wu
    wd_out[...] = wd
    acc_scr[...] += _dot((_silu(_dot(xn, wg)) * _dot(xn, wu)).astype(BF16), wd)


def _ffn1_small_kernel(x_ref, n1_ref, wg_ref, wu_ref, wd_ref, h1_ref, wg_out, wu_out, wd_out, xn_scr, acc_scr):
    j = pl.program_id(0)

    @pl.when(j == 0)
    def _():
        xn_scr[...] = _rms(x_ref[...], n1_ref[...]).astype(BF16)
        acc_scr[...] = jnp.zeros(acc_scr.shape, F32)

    _ffn_tile_step(xn_scr[...], wg_ref, wu_ref, wd_ref, wg_out, wu_out, wd_out, acc_scr)

    @pl.when(j == pl.num_programs(0) - 1)
    def _():
        h1_ref[...] = x_ref[...] + 0.5 * acc_scr[...]


def _weight_tile_specs():
    cols = pl.BlockSpec((D_MODEL, FF_TILE), lambda j: (0, j))
    rows = pl.BlockSpec((FF_TILE, D_MODEL), lambda j: (j, 0))
    shapes = [jax.ShapeDtypeStruct((D_MODEL, D_FF), BF16), jax.ShapeDtypeStruct((D_MODEL, D_FF), BF16),
              jax.ShapeDtypeStruct((D_FF, D_MODEL), BF16)]
    return [cols, cols, rows], shapes


def _ffn1_small(x, n1, wg, wu, wd):
    n = x.shape[0]
    wspecs, wshapes = _weight_tile_specs()
    return pl.pallas_call(
        _ffn1_small_kernel,
        grid=(D_FF // FF_TILE,),
        in_specs=[_resident(x.shape), _resident(n1.shape)] + wspecs,
        out_specs=[pl.BlockSpec((n, D_MODEL), lambda j: (0, 0))] + wspecs,
        out_shape=[jax.ShapeDtypeStruct((n, D_MODEL), F32)] + wshapes,
        scratch_shapes=[pltpu.VMEM((n, D_MODEL), BF16), pltpu.VMEM((n, D_MODEL), F32)],
        compiler_params=_compiler_params(1),
        name="ffn1_small",
    )(x, n1, wg, wu, wd)


W_IN_TILE = 1024


def _proj_small_kernel(n_cols, h1_ref, nm_ref, wt_ref, u_ref, win_out, hn_scr):
    j = pl.program_id(0)

    @pl.when(j == 0)
    def _():
        hn_scr[...] = _rms(h1_ref[...], nm_ref[...]).astype(BF16)

    col = j * W_IN_TILE + lax.broadcasted_iota(jnp.int32, (W_IN_TILE, 1), 0)
    w = jnp.where(col < n_cols, wt_ref[...], 0.0).T.astype(BF16)
    win_out[...] = w
    u_ref[...] = _dot(hn_scr[...], w)


def _proj_small(h1, nm, w_in_t):
    n = h1.shape[0]
    tile = lambda rows: pl.BlockSpec((rows, W_IN_TILE), lambda j: (0, j))
    return pl.pallas_call(
        functools.partial(_proj_small_kernel, w_in_t.shape[0]),
        grid=(pl.cdiv(D_IN_PAD, W_IN_TILE),),
        in_specs=[_resident(h1.shape), _resident(nm.shape), pl.BlockSpec((W_IN_TILE, D_MODEL), lambda j: (j, 0))],
        out_specs=[tile(n), tile(D_MODEL)],
        out_shape=[jax.ShapeDtypeStruct((n, D_IN_PAD), F32), jax.ShapeDtypeStruct((D_MODEL, D_IN_PAD), BF16)],
        scratch_shapes=[pltpu.VMEM((n, D_MODEL), BF16)],
        compiler_params=_compiler_params(1),
        name="proj_small",
    )(h1, nm, w_in_t)


def _ffn_out_kernel(h1_ref, mix_ref, wo_ref, n2_ref, wg_ref, wu_ref, wd_ref, nf_ref, y_ref):
    h2 = h1_ref[...] + _dot(mix_ref[...].astype(BF16), wo_ref[...])
    hn = _rms(h2, n2_ref[...]).astype(BF16)
    h3 = h2 + 0.5 * _swiglu(hn, wg_ref, wu_ref, wd_ref)
    y_ref[...] = _rms(h3, nf_ref[...])


def _ffn_out(h1, h1_block0, mix, wo, n2, wg, wu, wd, nf, tm):
    n = mix.shape[0]
    return pl.pallas_call(
        _ffn_out_kernel,
        grid=(n // tm,),
        in_specs=[pl.BlockSpec((tm, D_MODEL), lambda i: (i + h1_block0, 0)),
                  pl.BlockSpec((tm, D_MODEL), lambda i: (i, 0)),
                  _resident(wo.shape), _resident(n2.shape), _resident(wg.shape), _resident(wu.shape),
                  _resident(wd.shape), _resident(nf.shape)],
        out_specs=pl.BlockSpec((tm, D_MODEL), lambda i: (i, 0)),
        out_shape=jax.ShapeDtypeStruct((n, D_MODEL), F32),
        compiler_params=_compiler_params(1),
        name="ffn_out",
    )(h1, mix, wo, n2, wg, wu, wd, nf)


def _head_slices(a, width):
    return [a[:, i:i + width] for i in range(0, a.shape[1], width)]


def _column_tile(row):
    return jnp.broadcast_to(row, (LANES, LANES)).T


def _store_hgrn_out(rows, o, gate, hgn, mix_ref):
    for h in range(HG_HEADS):
        sl = slice(h * HG_DIM, (h + 1) * HG_DIM)
        oh = o[:, sl]
        oh = oh * lax.rsqrt(jnp.mean(oh * oh, axis=-1, keepdims=True) + EPS)
        mix_ref[rows, sl] = (oh * hgn[:, sl] * gate[:, sl]).astype(mix_ref.dtype)


def _hgrn_log_decay(n_pad, tb, fz, lb, tri):
    logf = jnp.log(lb + (1.0 - lb) * jax.nn.sigmoid(fz))
    if n_pad:
        logf = jnp.where(lax.broadcasted_iota(jnp.int32, (tb, HG_WIDTH), 0) >= n_pad, logf, 0.0)
    b = _cumsum_rows(tri, logf)
    chunk_ends = jnp.concatenate([b[r:r + 1] for r in range(HG_CHUNK - 1, tb, HG_CHUNK)], axis=0)
    return b, jnp.max(-chunk_ends)


def _hgrn_block(n_pad, tb, u_ref, lb, hgn, b, worst_log_decay, mix_ref, st_scr, o_scr):
    chunked_is_safe = worst_log_decay < HG_MAX_CHUNK_LOG_DECAY

    @pl.when(chunked_is_safe)
    def _():
        _hgrn_chunked(n_pad, tb, u_ref, lb, hgn, b, mix_ref, st_scr)

    @pl.when(jnp.logical_not(chunked_is_safe))
    def _():
        _hgrn_per_token(n_pad, tb, u_ref, lb, hgn, mix_ref, st_scr, o_scr)


def _hgrn_per_token(n_pad, tb, u_ref, lb, hgn, mix_ref, st_scr, o_scr):
    for h in range(HG_HEADS):
        st_scr[h] = st_scr[h].T

    def sublane_group(i, carry):
        rows = pl.ds(pl.multiple_of(i * SUBLANES, SUBLANES), SUBLANES)
        fz = u_ref[rows, COL_F:COL_F + HG_WIDTH]
        f = lb + (1.0 - lb) * jax.nn.sigmoid(fz)
        kk = (1.0 - lb) * jax.nn.sigmoid(-fz)
        if n_pad:
            valid = i * SUBLANES + lax.broadcasted_iota(jnp.int32, (SUBLANES, HG_WIDTH), 0) >= n_pad
            f = jnp.where(valid, f, 1.0)
            kk = jnp.where(valid, kk, 0.0)
        q = _silu(u_ref[rows, COL_Q:COL_Q + HG_WIDTH])
        v = u_ref[rows, COL_I:COL_I + HG_WIDTH]
        for h in range(HG_HEADS):
            sl = slice(h * HG_DIM, (h + 1) * HG_DIM)
            s = st_scr[h]
            o_rows = []
            for j in range(SUBLANES):
                r = slice(j, j + 1)
                s = s * _column_tile(f[r, sl]) + _column_tile(kk[r, sl]) * v[r, sl]
                o_rows.append(jnp.sum(s * _column_tile(q[r, sl]), axis=0, keepdims=True))
            st_scr[h] = s
            o_scr[rows, sl] = jnp.concatenate(o_rows, axis=0)
        return carry

    lax.fori_loop(0, tb // SUBLANES, sublane_group, 0)
    for h in range(HG_HEADS):
        st_scr[h] = st_scr[h].T
    _store_hgrn_out(slice(0, tb), o_scr[...], _silu(u_ref[:, COL_G:COL_G + HG_WIDTH]), hgn, mix_ref)


def _hgrn_chunked(n_pad, tb, u_ref, lb, hgn, b, mix_ref, st_scr):
    c = HG_CHUNK
    kk = (1.0 - lb) * jax.nn.sigmoid(-u_ref[:, COL_F:COL_F + HG_WIDTH])
    if n_pad:
        kk = jnp.where(lax.broadcasted_iota(jnp.int32, (tb, HG_WIDTH), 0) >= n_pad, kk, 0.0)
    q = _silu(u_ref[:, COL_Q:COL_Q + HG_WIDTH])
    v = u_ref[:, COL_I:COL_I + HG_WIDTH].astype(BF16)
    gate = _silu(u_ref[:, COL_G:COL_G + HG_WIDTH])
    qt = (q * jnp.exp(b)).astype(BF16)
    kt = (kk * jnp.exp(-b)).astype(BF16)
    causal = _tiled_lower_tri(c, HG_HEADS)

    for r0 in range(0, tb, c):
        rows = slice(r0, r0 + c)
        b_c = b[rows]
        b_last = b_c[c - 1:c, :]
        kh_c = (kk[rows] * jnp.exp(b_last - b_c)).astype(BF16)
        decay = jnp.exp(b_last)
        qt_c = qt[rows]
        kd = _block_diag(_head_slices(kt[rows], HG_DIM))
        vd = _block_diag(_head_slices(v[rows], HG_DIM))
        scores = jnp.where(causal, _dot_nt(qt_c, kd), 0.0).astype(BF16)
        o = _dot(scores, vd)
        o_prev = []
        for p in range(HG_PAIRS):
            lanes = slice(p * 2 * HG_DIM, (p + 1) * 2 * HG_DIM)
            heads = (2 * p, 2 * p + 1)
            st = [st_scr[h] for h in heads]
            o_prev.append(_dot_nt(qt_c[:, lanes], _block_diag([s.astype(BF16) for s in st])))
            upd = _dot_tn(vd[p * 2 * c:(p + 1) * 2 * c, lanes], jnp.concatenate([kh_c[:, lanes]] * 2, axis=0))
            for i, h in enumerate(heads):
                blk = slice(i * HG_DIM, (i + 1) * HG_DIM)
                st_scr[h] = st[i] * decay[:, h * HG_DIM:(h + 1) * HG_DIM] + upd[blk, blk]
        _store_hgrn_out(rows, o + jnp.concatenate(o_prev, axis=1), gate[rows], hgn, mix_ref)


def _pair_columns(a, r0, r1, first_half):
    shape = (a.shape[0], LANES)
    return jnp.where(first_half, jnp.broadcast_to(a[:, r0:r0 + 1], shape), jnp.broadcast_to(a[:, r1:r1 + 1], shape))


def _group_columns(a, g, first_half):
    r = g * HEADS_PER_GROUP
    return jnp.concatenate([_pair_columns(a, r + i, r + i + 1, first_half)
                            for i in range(0, HEADS_PER_GROUP, 2)], axis=1)


def _ssd_block(n_pad, tb, u_ref, col0, cw, cb, dtb, a_neg, dsk, ssn, tri, mix_ref, hg_scr, xpad):
    c = SSM_CHUNK
    conv = cb
    for j in range(CONV_WIDTH):
        off = PAD_ROWS - (CONV_WIDTH - 1) + j
        conv = conv + cw[j:j + 1, :] * xpad[off:off + tb, :]
    act = _silu(conv)
    xs = act[:, 0:SSM_WIDTH]
    bm = act[:, SSM_WIDTH:SSM_WIDTH + SSM_GROUPS * SSM_STATE].astype(BF16)
    cm = act[:, SSM_WIDTH + SSM_GROUPS * SSM_STATE:].astype(BF16)
    z_gate = _silu(u_ref[:, col0:col0 + SSM_WIDTH])

    col_dt = col0 + COL_DT - COL_Z
    dt = jax.nn.softplus(u_ref[:, col_dt:col_dt + LANES] + dtb)
    if n_pad:
        dt = jnp.where(lax.broadcasted_iota(jnp.int32, (tb, LANES), 0) >= n_pad, dt, 0.0)
    cum = _cumsum_rows(tri, dt * a_neg)
    causal = _lower_tri(c)
    first_half = lax.broadcasted_iota(jnp.int32, (1, LANES), 1) < SSM_HEAD_DIM
    zeros = jnp.zeros((c, LANES), BF16)

    for r0 in range(0, tb, c):
        rows = slice(r0, r0 + c)
        cum_c = cum[rows]
        dt_c = dt[rows]
        last = cum_c[c - 1:c, :]
        cum_t = cum_c.T
        dt_t = dt_c.T
        e_cum = jnp.exp(cum_c)
        w_in = dt_c * jnp.exp(last - cum_c)
        e_last = jnp.exp(last)
        for g in range(SSM_GROUPS):
            yield
            glanes = slice(g * GROUP_WIDTH, (g + 1) * GROUP_WIDTH)
            bg = bm[rows, g * SSM_STATE:(g + 1) * SSM_STATE]
            cg = cm[rows, g * SSM_STATE:(g + 1) * SSM_STATE]
            xg = xs[rows, glanes]
            cbt = _dot_nt(cg, bg)
            m_heads = []
            for rr in range(HEADS_PER_GROUP):
                r = g * HEADS_PER_GROUP + rr
                seg = jnp.exp(jnp.where(causal, cum_c[:, r:r + 1] - cum_t[r:r + 1, :], -jnp.inf))
                m_heads.append((cbt * seg * dt_t[r:r + 1, :]).astype(BF16))
            xb = xg.astype(BF16)
            xd_rows = []
            for rr in range(HEADS_PER_GROUP):
                tile = xb[:, (rr // 2) * LANES:(rr // 2 + 1) * LANES]
                tile = jnp.where(first_half if rr % 2 == 0 else ~first_half, tile, jnp.zeros_like(tile))
                xd_rows.append(jnp.concatenate([tile, zeros] if rr < 2 else [zeros, tile], axis=1))
            xd = jnp.concatenate(xd_rows, axis=0)
            hg = hg_scr[g]
            y = (_dot(jnp.concatenate(m_heads, axis=1), xd)
                 + _dot(cg, hg.astype(BF16)) * _group_columns(e_cum, g, first_half))
            xw = (xg * _group_columns(w_in, g, first_half)).astype(BF16)
            hg_scr[g] = hg * _group_columns(e_last, g, first_half) + _dot_tn(bg, xw)
            yz = (y + dsk[:, glanes] * xg) * z_gate[rows, glanes]
            yz = yz * lax.rsqrt(jnp.mean(yz * yz, axis=-1, keepdims=True) + EPS) * ssn[:, glanes]
            mix_ref[rows, HG_WIDTH + g * GROUP_WIDTH:HG_WIDTH + (g + 1) * GROUP_WIDTH] = yz.astype(mix_ref.dtype)


def _interleave(main, side, side_steps):
    for n in side_steps:
        if next(main, StopIteration) is StopIteration:
            break
        for _ in range(n):
            next(side, None)
    for _ in main:
        pass
    for _ in side:
        pass


def _mixer_block(*args):
    for _ in _mixer_steps(*args):
        pass


def _mixer_steps(n_pad, tb, uh_ref, us_ref, us_col0, param_refs, mix_ref, st_scr, hg_scr, xpad, o_scr,
                 log_decay=None):
    lbl_ref, hgn_ref, cw_ref, cb_ref, dtb_ref, alog_ref, dsk_ref, ssn_ref, tri_hg_ref, tri_ssm_ref = param_refs
    lb = _forget_lower_bound(lbl_ref[...])
    if log_decay is None:
        b, worst = _hgrn_log_decay(n_pad, tb, uh_ref[:, COL_F:COL_F + HG_WIDTH], lb, tri_hg_ref[...])
    else:
        b, worst = log_decay[0][...], log_decay[1]
    _hgrn_block(n_pad, tb, uh_ref, lb, hgn_ref[...], b, worst, mix_ref, st_scr, o_scr)
    yield
    col_xbc = us_col0 + COL_XBC - COL_Z
    xbc = us_ref[:, col_xbc:col_xbc + CONV_DIM]
    if n_pad:
        xbc = jnp.where(lax.broadcasted_iota(jnp.int32, (tb, CONV_DIM), 0) >= n_pad, xbc, 0.0)
    xpad[PAD_ROWS:PAD_ROWS + tb, :] = xbc
    yield from _ssd_block(n_pad, tb, us_ref, us_col0, cw_ref[...], cb_ref[...], dtb_ref[...],
                          -jnp.exp(alog_ref[...]), dsk_ref[...], ssn_ref[...], tri_ssm_ref[...],
                          mix_ref, hg_scr, xpad)
    xpad[0:PAD_ROWS, :] = xpad[tb:tb + PAD_ROWS, :]


def _load_state(s0_ref, h0_ref, c0_ref, st_scr, hg_scr, xpad):
    for h in range(HG_HEADS):
        st_scr[h] = s0_ref[0, h].T
    for r in range(0, SSM_HEADS, 2):
        g, lane0 = r // HEADS_PER_GROUP, (r % HEADS_PER_GROUP) * SSM_HEAD_DIM
        hg_scr[g, :, lane0:lane0 + LANES] = jnp.concatenate([h0_ref[0, r], h0_ref[0, r + 1]], axis=0).T
    xpad[0:PAD_ROWS, :] = c0_ref[0]


def _store_state(s_out_ref, h_out_ref, c_out_ref, st_scr, hg_scr, xpad):
    for h in range(HG_HEADS):
        s_out_ref[0, h] = st_scr[h].T
    for r in range(0, SSM_HEADS, 2):
        g, lane0 = r // HEADS_PER_GROUP, (r % HEADS_PER_GROUP) * SSM_HEAD_DIM
        pair = hg_scr[g, :, lane0:lane0 + LANES].T
        h_out_ref[0, r] = pair[0:SSM_HEAD_DIM]
        h_out_ref[0, r + 1] = pair[SSM_HEAD_DIM:]
    c_out_ref[0] = xpad[0:PAD_ROWS, :]


N_MIXER_PARAMS = 10


def _scan_params(params, tb):
    n = min(tb, FF_TILE)
    return tuple(params) + (_chunked_tri(n, HG_CHUNK), _chunked_tri(n, SSM_CHUNK))


PROJ_TILE = 512
PROJ_STEPS_AFTER_SCAN_STEP = (3, 1, 1, 1, 1, 1, 1)
HG_SHAPE = (SSM_GROUPS, SSM_STATE, GROUP_WIDTH)
S_SHAPE = (HG_HEADS, HG_DIM, HG_DIM)
H_SHAPE = (SSM_HEADS, SSM_HEAD_DIM, SSM_STATE)
C_SHAPE = (PAD_ROWS, CONV_DIM)


def _scan_kernel(n_pad, tb, u_ref, *refs):
    param_refs = refs[:N_MIXER_PARAMS]
    (s0_ref, h0_ref, c0_ref, mix_ref, s_out_ref, h_out_ref, c_out_ref,
     st_scr, hg_scr, xpad, o_scr) = refs[N_MIXER_PARAMS:]
    t = pl.program_id(1)

    @pl.when(t == 0)
    def _():
        _load_state(s0_ref, h0_ref, c0_ref, st_scr, hg_scr, xpad)

    _mixer_block(n_pad, tb, u_ref, u_ref, COL_Z, param_refs, mix_ref, st_scr, hg_scr, xpad, o_scr)

    @pl.when(t == pl.num_programs(1) - 1)
    def _():
        _store_state(s_out_ref, h_out_ref, c_out_ref, st_scr, hg_scr, xpad)


def _scan_pipelined_kernel(tb, nt, h1_ref, nm_ref, win_ref, *refs):
    param_refs = refs[:N_MIXER_PARAMS]
    s0_ref, h0_ref, c0_ref, mix_ref, s_out_ref, h_out_ref, c_out_ref = refs[N_MIXER_PARAMS:N_MIXER_PARAMS + 7]
    even, odd = refs[N_MIXER_PARAMS + 7:N_MIXER_PARAMS + 11], refs[N_MIXER_PARAMS + 11:N_MIXER_PARAMS + 15]
    st_scr, hg_scr, xpad, o_scr = refs[N_MIXER_PARAMS + 15:]
    s = pl.program_id(0)
    scanned = jnp.maximum(s - 1, 0)

    @pl.when(s == 0)
    def _():
        uh, us, b, worst = odd
        uh[...] = jnp.zeros(uh.shape, F32)
        us[...] = jnp.zeros(us.shape, F32)
        b[...] = jnp.zeros(b.shape, F32)
        worst[0] = 0.0

    @pl.when(scanned % nt == 0)
    def _():
        _load_state(s0_ref, h0_ref, c0_ref, st_scr, hg_scr, xpad)

    def project_steps(uh_w, us_w, b_w, worst_w):
        hn = _rms(h1_ref[...], nm_ref[...]).astype(BF16)
        for c0 in range(0, D_IN_PAD, PROJ_TILE):
            c1 = min(c0 + PROJ_TILE, D_IN_PAD)
            yield
            tile = _dot(hn, win_ref[:, c0:c1])
            if c0 < COL_Z:
                uh_w[:, c0:c1] = tile
            else:
                us_w[:, c0 - COL_Z:c1 - COL_Z] = tile
            if c0 == COL_F:
                assert c1 == COL_F + HG_WIDTH
                lb = _forget_lower_bound(param_refs[0][...])
                b, worst = _hgrn_log_decay(0, tb, tile, lb, param_refs[N_MIXER_PARAMS - 2][...])
                b_w[...] = b
                worst_w[0] = worst

    def body(write, read):
        uh_r, us_r, b_r, worst_r = read
        _interleave(_mixer_steps(0, tb, uh_r, us_r, 0, param_refs, mix_ref, st_scr, hg_scr, xpad, o_scr,
                                 (b_r, worst_r[0])),
                    project_steps(*write), PROJ_STEPS_AFTER_SCAN_STEP)

    pl.when(s % 2 == 0)(lambda: body(even, odd))
    pl.when(s % 2 == 1)(lambda: body(odd, even))

    @pl.when((s >= 1) & (scanned % nt == nt - 1))
    def _():
        _store_state(s_out_ref, h_out_ref, c_out_ref, st_scr, hg_scr, xpad)


def _scan_pipelined(h1, nb, nt, tb, proj, params, s0, h0, c0):
    n_blocks = nb * nt
    last = n_blocks - 1
    lead = tuple(proj) + _scan_params(params, tb)
    shared = lambda shape: pl.BlockSpec((1,) + shape[1:], lambda s: (0,) * len(shape))
    per_seq = lambda shape: pl.BlockSpec((1,) + shape, lambda s: (jnp.maximum(s - 1, 0) // nt,) + (0,) * len(shape))
    u_bufs = [pltpu.VMEM((tb, COL_Z), F32), pltpu.VMEM((tb, D_IN_PAD - COL_Z), F32),
              pltpu.VMEM((tb, HG_WIDTH), F32), pltpu.SMEM((1,), F32)]
    return pl.pallas_call(
        functools.partial(_scan_pipelined_kernel, tb, nt),
        grid=(n_blocks + 1,),
        in_specs=[pl.BlockSpec((tb, D_MODEL), lambda s: (jnp.minimum(s, last), 0))]
        + [_resident(p.shape) for p in lead]
        + [shared(s0.shape), shared(h0.shape), shared(c0.shape)],
        out_specs=[pl.BlockSpec((tb, D_MODEL), lambda s: (jnp.maximum(s - 1, 0), 0)),
                   per_seq(S_SHAPE), per_seq(H_SHAPE), per_seq(C_SHAPE)],
        out_shape=[jax.ShapeDtypeStruct((n_blocks * tb, D_MODEL), BF16),
                   jax.ShapeDtypeStruct((nb,) + S_SHAPE, F32),
                   jax.ShapeDtypeStruct((nb,) + H_SHAPE, F32),
                   jax.ShapeDtypeStruct((nb,) + C_SHAPE, F32)],
        scratch_shapes=u_bufs + u_bufs + [pltpu.VMEM(S_SHAPE, F32), pltpu.VMEM(HG_SHAPE, F32),
                                          pltpu.VMEM((PAD_ROWS + tb, CONV_DIM), F32),
                                          pltpu.VMEM((tb, HG_WIDTH), F32)],
        compiler_params=_compiler_params(1),
        name="proj_mixer",
    )(h1, *lead, s0, h0, c0)


def _scan(u, u_block0, nb, nt, tb, n_pad, params, s0, h0, c0):
    lead = _scan_params(params, tb)
    shared = lambda shape: pl.BlockSpec((1,) + shape[1:], lambda b, t: (0,) * len(shape))
    per_seq = lambda shape: pl.BlockSpec((1,) + shape, lambda b, t: (b,) + (0,) * len(shape))
    scratch = [pltpu.VMEM(S_SHAPE, F32), pltpu.VMEM(HG_SHAPE, F32), pltpu.VMEM((PAD_ROWS + tb, CONV_DIM), F32),
               pltpu.VMEM((tb, HG_WIDTH), F32)]
    return pl.pallas_call(
        functools.partial(_scan_kernel, n_pad, tb),
        grid=(nb, nt),
        in_specs=[pl.BlockSpec((tb, D_IN_PAD), lambda b, t: (u_block0 + b * nt + t, 0))]
        + [_resident(p.shape) for p in lead]
        + [shared(s0.shape), shared(h0.shape), shared(c0.shape)],
        out_specs=[pl.BlockSpec((tb, D_MODEL), lambda b, t: (b * nt + t, 0)),
                   per_seq(S_SHAPE), per_seq(H_SHAPE), per_seq(C_SHAPE)],
        out_shape=[jax.ShapeDtypeStruct((nb * nt * tb, D_MODEL), BF16),
                   jax.ShapeDtypeStruct((nb,) + S_SHAPE, F32),
                   jax.ShapeDtypeStruct((nb,) + H_SHAPE, F32),
                   jax.ShapeDtypeStruct((nb,) + C_SHAPE, F32)],
        scratch_shapes=scratch,
        compiler_params=_compiler_params(2),
        name="mixer",
    )(u, *lead, s0, h0, c0)


def _sample_kernel(nbs, u_ref, lbl_ref, hgn_ref, cw_ref, cb_ref, dtb_ref, alog_ref, dsk_ref, ssn_ref,
                   sh_ref, ss_ref, sc_ref, mix_ref, sh_out_ref, ss_out_ref, sc_out_ref, obuf, ybuf):
    lb = _forget_lower_bound(lbl_ref[...])
    fz = u_ref[:, COL_F:COL_F + HG_WIDTH]
    f = lb + (1.0 - lb) * jax.nn.sigmoid(fz)
    kk = (1.0 - lb) * jax.nn.sigmoid(-fz)
    q = _silu(u_ref[:, COL_Q:COL_Q + HG_WIDTH])
    v = u_ref[:, COL_I:COL_I + HG_WIDTH]

    xbc = u_ref[:, COL_XBC:COL_XBC + CONV_DIM]
    cw = cw_ref[...]
    conv = cb_ref[...] + cw[CONV_WIDTH - 1:CONV_WIDTH, :] * xbc
    for j in range(CONV_WIDTH - 1):
        conv = conv + cw[j:j + 1, :] * sc_ref[j]
    for j in range(CONV_WIDTH - 2):
        sc_out_ref[j] = sc_ref[j + 1]
    sc_out_ref[CONV_WIDTH - 2] = xbc
    act = _silu(conv)
    xs = act[:, 0:SSM_WIDTH]
    bm = act[:, SSM_WIDTH:SSM_WIDTH + SSM_GROUPS * SSM_STATE]
    cm = act[:, SSM_WIDTH + SSM_GROUPS * SSM_STATE:]
    dt = jax.nn.softplus(u_ref[:, COL_DT:COL_DT + LANES] + dtb_ref[...])
    d_a = jnp.exp(dt * (-jnp.exp(alog_ref[...])))

    top_half = lax.broadcasted_iota(jnp.int32, (LANES, LANES), 0) < SSM_HEAD_DIM
    for j in range(nbs):
        row = slice(j, j + 1)
        for h in range(HG_HEADS):
            sl = slice(h * HG_DIM, (h + 1) * HG_DIM)
            s_new = sh_ref[j, h] * _column_tile(f[row, sl]) + _column_tile(kk[row, sl]) * v[row, sl]
            sh_out_ref[j, h] = s_new
            obuf[row, sl] = jnp.sum(s_new * _column_tile(q[row, sl]), axis=0, keepdims=True)
        for rp in range(SSM_HEADS // 2):
            r0, r1 = 2 * rp, 2 * rp + 1
            g = r0 // HEADS_PER_GROUP
            sl = slice(rp * LANES, (rp + 1) * LANES)
            gsl = slice(g * SSM_STATE, (g + 1) * SSM_STATE)
            h2 = jnp.concatenate([ss_ref[j, r0], ss_ref[j, r1]], axis=0)
            da2 = jnp.where(top_half, d_a[row, r0:r0 + 1], d_a[row, r1:r1 + 1])
            dt2 = jnp.where(top_half, dt[row, r0:r0 + 1], dt[row, r1:r1 + 1])
            h_new = h2 * da2 + (dt2 * _column_tile(xs[row, sl])) * bm[row, gsl]
            ss_out_ref[j, r0] = h_new[0:SSM_HEAD_DIM]
            ss_out_ref[j, r1] = h_new[SSM_HEAD_DIM:]
            ybuf[row, sl] = jnp.sum((h_new * cm[row, gsl]).T, axis=0, keepdims=True)

    o = obuf[...]
    gate = _silu(u_ref[:, COL_G:COL_G + HG_WIDTH])
    hgn = hgn_ref[...]
    for h in range(HG_HEADS):
        sl = slice(h * HG_DIM, (h + 1) * HG_DIM)
        oh = o[:, sl]
        oh = oh * lax.rsqrt(jnp.mean(oh * oh, axis=-1, keepdims=True) + EPS)
        mix_ref[:, sl] = oh * hgn[:, sl] * gate[:, sl]
    yz = (ybuf[...] + dsk_ref[...] * xs) * _silu(u_ref[:, COL_Z:COL_Z + SSM_WIDTH])
    ssn = ssn_ref[...]
    for g in range(SSM_GROUPS):
        sl = slice(g * GROUP_WIDTH, (g + 1) * GROUP_WIDTH)
        seg = yz[:, sl]
        seg = seg * lax.rsqrt(jnp.mean(seg * seg, axis=-1, keepdims=True) + EPS) * ssn[:, sl]
        mix_ref[:, HG_WIDTH + g * GROUP_WIDTH:HG_WIDTH + (g + 1) * GROUP_WIDTH] = seg


def _sample_mixer(u, u_block0, n, nbs, params, sh, ss, sc_t):
    small = [_resident(p.shape) for p in params]
    return pl.pallas_call(
        functools.partial(_sample_kernel, nbs),
        grid=(n // nbs,),
        in_specs=[pl.BlockSpec((nbs, D_IN_PAD), lambda i: (u_block0 + i, 0))] + small
        + [pl.BlockSpec((nbs,) + sh.shape[1:], lambda i: (i, 0, 0, 0)),
           pl.BlockSpec((nbs,) + ss.shape[1:], lambda i: (i, 0, 0, 0)),
           pl.BlockSpec((CONV_WIDTH - 1, nbs, CONV_DIM), lambda i: (0, i, 0))],
        out_specs=[pl.BlockSpec((nbs, D_MODEL), lambda i: (i, 0)),
                   pl.BlockSpec((nbs,) + sh.shape[1:], lambda i: (i, 0, 0, 0)),
                   pl.BlockSpec((nbs,) + ss.shape[1:], lambda i: (i, 0, 0, 0)),
                   pl.BlockSpec((CONV_WIDTH - 1, nbs, CONV_DIM), lambda i: (0, i, 0))],
        out_shape=[jax.ShapeDtypeStruct((n, D_MODEL), F32),
                   jax.ShapeDtypeStruct(sh.shape, F32),
                   jax.ShapeDtypeStruct(ss.shape, F32),
                   jax.ShapeDtypeStruct(sc_t.shape, F32)],
        scratch_shapes=[pltpu.VMEM((nbs, HG_WIDTH), F32), pltpu.VMEM((nbs, SSM_WIDTH), F32)],
        compiler_params=_compiler_params(1),
        name="sample_mixer",
    )(u, *params, sh, ss, sc_t)


TM_DENSE = 512
TM_SCAN = 512
TM_SMALL = 128
SAMPLES_PER_STEP = 16


def _pad_lanes(row, value=0.0):
    return jnp.pad(row, ((0, 0), (0, LANES - row.shape[1])), constant_values=value)


def kernel(x_prompt, x_sample, state_hgrn, state_ssm, state_conv, meta_tokens, lb_logits, norm_ffn1, w_ffn1_gate, w_ffn1_up, w_ffn1_down, norm_mix, w_in, hg_norm, conv_w, conv_b, dt_bias, a_log, d_skip, ssm_norm, w_out, norm_ffn2, w_ffn2_gate, w_ffn2_up, w_ffn2_down, norm_final):
    bp, seq_p, _ = x_prompt.shape
    n_s = x_sample.shape[0]
    assert x_sample.shape[1] == 1 and n_s == TM_SMALL and seq_p % TM_SCAN == 0 and (bp * seq_p) % TM_DENSE == 0
    layer = 0

    n1, nm, n2 = norm_ffn1[layer][None], norm_mix[layer][None], norm_ffn2[layer][None]
    nf = norm_final[None]
    mixer_params = (lb_logits, hg_norm[layer][None], conv_w[layer], conv_b[layer][None],
                    _pad_lanes(dt_bias[layer][None]), _pad_lanes(a_log[layer][None]),
                    jnp.repeat(d_skip[layer], SSM_HEAD_DIM)[None], ssm_norm[layer][None])

    n_pad = TM_SMALL - N_META
    x_small = jnp.concatenate([jnp.zeros((n_pad, D_MODEL), F32), meta_tokens, x_sample[:, 0]], axis=0)
    h1_small, wg1, wu1, wd1 = _ffn1_small(x_small, n1, w_ffn1_gate[layer], w_ffn1_up[layer], w_ffn1_down[layer])
    u_small, win = _proj_small(h1_small, nm, jnp.swapaxes(w_in[layer], 0, 1))

    zeros_s = jnp.zeros((1,) + S_SHAPE, F32)
    zeros_h = jnp.zeros((1,) + H_SHAPE, F32)
    zeros_c = jnp.zeros((1,) + C_SHAPE, F32)
    _, s_meta, h_meta, c_meta = _scan(u_small, 0, 1, 1, TM_SMALL, n_pad, mixer_params, zeros_s, zeros_h, zeros_c)

    sc_t = jnp.swapaxes(state_conv[layer], 0, 1)
    mix_s, hgrn_s, ssm_s, conv_s_t = _sample_mixer(
        u_small, TM_SMALL // SAMPLES_PER_STEP, n_s, SAMPLES_PER_STEP, mixer_params,
        state_hgrn[layer], state_ssm[layer], sc_t)

    xp = x_prompt.reshape(bp * seq_p, D_MODEL)
    h1_p, wg2, wu2, wd2, wo = _ffn1(xp, n1, wg1, wu1, wd1, TM_DENSE, w_ffn2_gate[layer], w_ffn2_up[layer],
                                    w_ffn2_down[layer], w_out[layer])
    y_s = _ffn_out(h1_small, 1, mix_s, wo, n2, wg2, wu2, wd2, nf, TM_SMALL)
    mix_p, hgrn_p, ssm_p, conv_p = _scan_pipelined(h1_p, bp, seq_p // TM_SCAN, TM_SCAN, (nm, win), mixer_params,
                                                   s_meta, h_meta, c_meta)
    y_p = _ffn_out(h1_p, 0, mix_p, wo, n2, wg2, wu2, wd2, nf, TM_DENSE)

    keep = slice(PAD_ROWS - (CONV_WIDTH - 1), PAD_ROWS)
    return (y_p.reshape(bp, seq_p, D_MODEL),
            y_s.reshape(n_s, 1, D_MODEL),
            hgrn_p[None], ssm_p[None], conv_p[:, keep][None],
            hgrn_s[None], ssm_s[None], jnp.swapaxes(conv_s_t, 0, 1)[None])
```

```python
import functools

import jax
import jax.numpy as jnp
from jax import lax
from jax.experimental import pallas as pl
from jax.experimental.pallas import tpu as pltpu

F32 = jnp.float32
BF16 = jnp.bfloat16

D_MODEL = 1024
D_FF = 2816
N_META = 16
HG_WIDTH = 512
HG_HEADS = 4
HG_DIM = 128
SSM_WIDTH = 512
SSM_HEADS = 8
SSM_HEAD_DIM = 64
SSM_GROUPS = 2
SSM_STATE = 128
CONV_WIDTH = 4
CONV_DIM = SSM_WIDTH + 2 * SSM_GROUPS * SSM_STATE
EPS = 1e-6

LANES = 128
SUBLANES = 8
VMEM_LIMIT_BYTES = 56 * 1024 * 1024

COL_Q = 0
COL_F = HG_WIDTH
COL_I = 2 * HG_WIDTH
COL_G = 3 * HG_WIDTH
COL_Z = 4 * HG_WIDTH
COL_XBC = COL_Z + SSM_WIDTH
COL_DT = COL_XBC + CONV_DIM
D_IN_PAD = COL_DT + LANES

FF_TILE = 256
HG_CHUNK = 64
HG_MAX_CHUNK_LOG_DECAY = 80.0
SSM_CHUNK = 128
PAD_ROWS = SUBLANES
HG_PAIRS = HG_HEADS // 2
HEADS_PER_GROUP = SSM_HEADS // SSM_GROUPS
GROUP_WIDTH = SSM_WIDTH // SSM_GROUPS
assert 2 * SSM_HEAD_DIM == LANES and 2 * HG_DIM == FF_TILE


def _dot(a, b):
    return jnp.dot(a, b, preferred_element_type=F32)


def _dot_nt(a, b):
    return lax.dot_general(a, b, (((1,), (1,)), ((), ())), preferred_element_type=F32)


def _dot_tn(a, b):
    return lax.dot_general(a, b, (((0,), (0,)), ((), ())), preferred_element_type=F32)


def _rms(x, w):
    return x * lax.rsqrt(jnp.mean(x * x, axis=-1, keepdims=True) + EPS) * w


def _silu(x):
    return x * jax.nn.sigmoid(x)


def _swiglu(xn, wg_ref, wu_ref, wd_ref, side_work=()):
    side_work = list(side_work)
    acc = jnp.zeros((xn.shape[0], D_MODEL), F32)
    for j in range(D_FF // FF_TILE):
        cols = slice(j * FF_TILE, (j + 1) * FF_TILE)
        g = _dot(xn, wg_ref[:, cols])
        u = _dot(xn, wu_ref[:, cols])
        if side_work:
            side_work.pop(0)()
        acc = acc + _dot((_silu(g) * u).astype(BF16), wd_ref[cols, :])
    assert not side_work
    return acc


def _cumsum_rows(tri, a):
    n = tri.shape[0]
    if a.shape[0] > n:
        return jnp.concatenate([_cumsum_rows(tri, a[r:r + n]) for r in range(0, a.shape[0], n)], axis=0)
    a1 = a.astype(BF16)
    r1 = a - a1.astype(F32)
    a2 = r1.astype(BF16)
    a3 = (r1 - a2.astype(F32)).astype(BF16)
    return _dot(tri, a1) + _dot(tri, a2) + _dot(tri, a3)


def _lower_tri(n):
    row = lax.broadcasted_iota(jnp.int32, (n, n), 0)
    col = lax.broadcasted_iota(jnp.int32, (n, n), 1)
    return row >= col


def _chunked_tri(n, chunk):
    assert chunk & (chunk - 1) == 0
    row = lax.broadcasted_iota(jnp.int32, (n, n), 0)
    col = lax.broadcasted_iota(jnp.int32, (n, n), 1)
    same_chunk = (row ^ col) < chunk
    return ((row >= col) & same_chunk).astype(BF16)


def _tiled_lower_tri(chunk, reps):
    assert chunk & (chunk - 1) == 0
    row = lax.broadcasted_iota(jnp.int32, (chunk, reps * chunk), 0)
    col = lax.broadcasted_iota(jnp.int32, (chunk, reps * chunk), 1)
    return row >= (col & (chunk - 1))


def _block_diag(blocks):
    n = len(blocks)
    r, c = blocks[0].shape
    rows = []
    for i, blk in enumerate(blocks):
        parts = []
        if i:
            parts.append(jnp.zeros((r, c * i), blk.dtype))
        parts.append(blk)
        if i < n - 1:
            parts.append(jnp.zeros((r, c * (n - 1 - i)), blk.dtype))
        rows.append(jnp.concatenate(parts, axis=1))
    return jnp.concatenate(rows, axis=0)


def _forget_lower_bound(lbl):
    l0, l1 = lbl[0:1], lbl[1:2]
    m = jnp.maximum(l0, l1)
    e0, e1 = jnp.exp(l0 - m), jnp.exp(l1 - m)
    return e0 / (e0 + e1)


def _resident(shape):
    nd = len(shape)
    return pl.BlockSpec(shape, lambda *_: (0,) * nd, pipeline_mode=pl.Buffered(1))


def _compiler_params(n_grid_axes):
    return pltpu.CompilerParams(dimension_semantics=("arbitrary",) * n_grid_axes,
                                vmem_limit_bytes=VMEM_LIMIT_BYTES)


N_CAST = 4
CAST_TILE = LANES


def _ffn1_kernel(x_ref, n1_ref, wg_ref, wu_ref, wd_ref, *refs):
    cast_in, h1_ref, cast_out = refs[:N_CAST], refs[N_CAST], refs[N_CAST + 1:]

    def cast(src, dst):
        def work():
            dst[...] = src[...].astype(BF16)
        return work

    x = x_ref[...]
    xn = _rms(x, n1_ref[...]).astype(BF16)
    casts = [cast(src, dst) for src, dst in zip(cast_in, cast_out)]
    h1_ref[...] = x + 0.5 * _swiglu(xn, wg_ref, wu_ref, wd_ref, casts)


def _ffn1(x, n1, wg, wu, wd, tm, w2g, w2u, w2d, w_out):
    n = x.shape[0]
    steps = n // tm
    n_ff, n_wo = D_FF // CAST_TILE, w_out.shape[0] // CAST_TILE
    assert steps >= n_ff and steps >= n_wo
    cast_specs = [pl.BlockSpec((D_MODEL, CAST_TILE), lambda i: (0, jnp.minimum(i, n_ff - 1))),
                  pl.BlockSpec((D_MODEL, CAST_TILE), lambda i: (0, jnp.minimum(i, n_ff - 1))),
                  pl.BlockSpec((CAST_TILE, D_MODEL), lambda i: (jnp.minimum(i, n_ff - 1), 0)),
                  pl.BlockSpec((CAST_TILE, w_out.shape[1]), lambda i: (jnp.minimum(i, n_wo - 1), 0))]
    cast = (w2g, w2u, w2d, w_out)
    return pl.pallas_call(
        _ffn1_kernel,
        grid=(steps,),
        in_specs=[pl.BlockSpec((tm, D_MODEL), lambda i: (i, 0)),
                  _resident(n1.shape), _resident(wg.shape), _resident(wu.shape), _resident(wd.shape)] + cast_specs,
        out_specs=[pl.BlockSpec((tm, D_MODEL), lambda i: (i, 0))] + cast_specs,
        out_shape=[jax.ShapeDtypeStruct((n, D_MODEL), F32)] + [jax.ShapeDtypeStruct(w.shape, BF16) for w in cast],
        compiler_params=_compiler_params(1),
        name="ffn1",
    )(x, n1, wg, wu, wd, *cast)


def _ffn_tile_step(xn, wg_ref, wu_ref, wd_ref, wg_out, wu_out, wd_out, acc_scr):
    wg, wu, wd = wg_ref[...].astype(BF16), wu_ref[...].astype(BF16), wd_ref[...].astype(BF16)
    wg_out[...] = wg
    wu_out[...] = wu
    wd_out[...] = wd
    acc_scr[...] += _dot((_silu(_dot(xn, wg)) * _dot(xn, wu)).astype(BF16), wd)


def _ffn1_small_kernel(x_ref, n1_ref, wg_ref, wu_ref, wd_ref, h1_ref, wg_out, wu_out, wd_out, xn_scr, acc_scr):
    j = pl.program_id(0)

    @pl.when(j == 0)
    def _():
        xn_scr[...] = _rms(x_ref[...], n1_ref[...]).astype(BF16)
        acc_scr[...] = jnp.zeros(acc_scr.shape, F32)

    _ffn_tile_step(xn_scr[...], wg_ref, wu_ref, wd_ref, wg_out, wu_out, wd_out, acc_scr)

    @pl.when(j == pl.num_programs(0) - 1)
    def _():
        h1_ref[...] = x_ref[...] + 0.5 * acc_scr[...]


def _weight_tile_specs():
    cols = pl.BlockSpec((D_MODEL, FF_TILE), lambda j: (0, j))
    rows = pl.BlockSpec((FF_TILE, D_MODEL), lambda j: (j, 0))
    shapes = [jax.ShapeDtypeStruct((D_MODEL, D_FF), BF16), jax.ShapeDtypeStruct((D_MODEL, D_FF), BF16),
              jax.ShapeDtypeStruct((D_FF, D_MODEL), BF16)]
    return [cols, cols, rows], shapes


def _ffn1_small(x, n1, wg, wu, wd):
    n = x.shape[0]
    wspecs, wshapes = _weight_tile_specs()
    return pl.pallas_call(
        _ffn1_small_kernel,
        grid=(D_FF // FF_TILE,),
        in_specs=[_resident(x.shape), _resident(n1.shape)] + wspecs,
        out_specs=[pl.BlockSpec((n, D_MODEL), lambda j: (0, 0))] + wspecs,
        out_shape=[jax.ShapeDtypeStruct((n, D_MODEL), F32)] + wshapes,
        scratch_shapes=[pltpu.VMEM((n, D_MODEL), BF16), pltpu.VMEM((n, D_MODEL), F32)],
        compiler_params=_compiler_params(1),
        name="ffn1_small",
    )(x, n1, wg, wu, wd)


W_IN_TILE = 1024


def _proj_small_kernel(n_cols, h1_ref, nm_ref, wt_ref, u_ref, win_out, hn_scr):
    j = pl.program_id(0)

    @pl.when(j == 0)
    def _():
        hn_scr[...] = _rms(h1_ref[...], nm_ref[...]).astype(BF16)

    col = j * W_IN_TILE + lax.broadcasted_iota(jnp.int32, (W_IN_TILE, 1), 0)
    w = jnp.where(col < n_cols, wt_ref[...], 0.0).T.astype(BF16)
    win_out[...] = w
    u_ref[...] = _dot(hn_scr[...], w)


def _proj_small(h1, nm, w_in_t):
    n = h1.shape[0]
    tile = lambda rows: pl.BlockSpec((rows, W_IN_TILE), lambda j: (0, j))
    return pl.pallas_call(
        functools.partial(_proj_small_kernel, w_in_t.shape[0]),
        grid=(pl.cdiv(D_IN_PAD, W_IN_TILE),),
        in_specs=[_resident(h1.shape), _resident(nm.shape), pl.BlockSpec((W_IN_TILE, D_MODEL), lambda j: (j, 0))],
        out_specs=[tile(n), tile(D_MODEL)],
        out_shape=[jax.ShapeDtypeStruct((n, D_IN_PAD), F32), jax.ShapeDtypeStruct((D_MODEL, D_IN_PAD), BF16)],
        scratch_shapes=[pltpu.VMEM((n, D_MODEL), BF16)],
        compiler_params=_compiler_params(1),
        name="proj_small",
    )(h1, nm, w_in_t)


def _ffn_out_kernel(h1_ref, mix_ref, wo_ref, n2_ref, wg_ref, wu_ref, wd_ref, nf_ref, y_ref):
    h2 = h1_ref[...] + _dot(mix_ref[...].astype(BF16), wo_ref[...])
    hn = _rms(h2, n2_ref[...]).astype(BF16)
    h3 = h2 + 0.5 * _swiglu(hn, wg_ref, wu_ref, wd_ref)
    y_ref[...] = _rms(h3, nf_ref[...])


def _ffn_out(h1, h1_block0, mix, wo, n2, wg, wu, wd, nf, tm):
    n = mix.shape[0]
    return pl.pallas_call(
        _ffn_out_kernel,
        grid=(n // tm,),
        in_specs=[pl.BlockSpec((tm, D_MODEL), lambda i: (i + h1_block0, 0)),
                  pl.BlockSpec((tm, D_MODEL), lambda i: (i, 0)),
                  _resident(wo.shape), _resident(n2.shape), _resident(wg.shape), _resident(wu.shape),
                  _resident(wd.shape), _resident(nf.shape)],
        out_specs=pl.BlockSpec((tm, D_MODEL), lambda i: (i, 0)),
        out_shape=jax.ShapeDtypeStruct((n, D_MODEL), F32),
        compiler_params=_compiler_params(1),
        name="ffn_out",
    )(h1, mix, wo, n2, wg, wu, wd, nf)


def _head_slices(a, width):
    return [a[:, i:i + width] for i in range(0, a.shape[1], width)]


def _column_tile(row):
    return jnp.broadcast_to(row, (LANES, LANES)).T


def _store_hgrn_out(rows, o, gate, hgn, mix_ref):
    for h in range(HG_HEADS):
        sl = slice(h * HG_DIM, (h + 1) * HG_DIM)
        oh = o[:, sl]
        oh = oh * lax.rsqrt(jnp.mean(oh * oh, axis=-1, keepdims=True) + EPS)
        mix_ref[rows, sl] = (oh * hgn[:, sl] * gate[:, sl]).astype(mix_ref.dtype)


def _hgrn_log_decay(n_pad, tb, fz, lb, tri):
    logf = jnp.log(lb + (1.0 - lb) * jax.nn.sigmoid(fz))
    if n_pad:
        logf = jnp.where(lax.broadcasted_iota(jnp.int32, (tb, HG_WIDTH), 0) >= n_pad, logf, 0.0)
    b = _cumsum_rows(tri, logf)
    chunk_ends = jnp.concatenate([b[r:r + 1] for r in range(HG_CHUNK - 1, tb, HG_CHUNK)], axis=0)
    return b, jnp.max(-chunk_ends)


def _hgrn_block(n_pad, tb, u_ref, lb, hgn, b, worst_log_decay, mix_ref, st_scr, o_scr):
    chunked_is_safe = worst_log_decay < HG_MAX_CHUNK_LOG_DECAY

    @pl.when(chunked_is_safe)
    def _():
        _hgrn_chunked(n_pad, tb, u_ref, lb, hgn, b, mix_ref, st_scr)

    @pl.when(jnp.logical_not(chunked_is_safe))
    def _():
        _hgrn_per_token(n_pad, tb, u_ref, lb, hgn, mix_ref, st_scr, o_scr)


def _hgrn_per_token(n_pad, tb, u_ref, lb, hgn, mix_ref, st_scr, o_scr):
    for h in range(HG_HEADS):
        st_scr[h] = st_scr[h].T

    def sublane_group(i, carry):
        rows = pl.ds(pl.multiple_of(i * SUBLANES, SUBLANES), SUBLANES)
        fz = u_ref[rows, COL_F:COL_F + HG_WIDTH]
        f = lb + (1.0 - lb) * jax.nn.sigmoid(fz)
        kk = (1.0 - lb) * jax.nn.sigmoid(-fz)
        if n_pad:
            valid = i * SUBLANES + lax.broadcasted_iota(jnp.int32, (SUBLANES, HG_WIDTH), 0) >= n_pad
            f = jnp.where(valid, f, 1.0)
            kk = jnp.where(valid, kk, 0.0)
        q = _silu(u_ref[rows, COL_Q:COL_Q + HG_WIDTH])
        v = u_ref[rows, COL_I:COL_I + HG_WIDTH]
        for h in range(HG_HEADS):
            sl = slice(h * HG_DIM, (h + 1) * HG_DIM)
            s = st_scr[h]
            o_rows = []
            for j in range(SUBLANES):
                r = slice(j, j + 1)
                s = s * _column_tile(f[r, sl]) + _column_tile(kk[r, sl]) * v[r, sl]
                o_rows.append(jnp.sum(s * _column_tile(q[r, sl]), axis=0, keepdims=True))
            st_scr[h] = s
            o_scr[rows, sl] = jnp.concatenate(o_rows, axis=0)
        return carry

    lax.fori_loop(0, tb // SUBLANES, sublane_group, 0)
    for h in range(HG_HEADS):
        st_scr[h] = st_scr[h].T
    _store_hgrn_out(slice(0, tb), o_scr[...], _silu(u_ref[:, COL_G:COL_G + HG_WIDTH]), hgn, mix_ref)


def _hgrn_chunked(n_pad, tb, u_ref, lb, hgn, b, mix_ref, st_scr):
    c = HG_CHUNK
    kk = (1.0 - lb) * jax.nn.sigmoid(-u_ref[:, COL_F:COL_F + HG_WIDTH])
    if n_pad:
        kk = jnp.where(lax.broadcasted_iota(jnp.int32, (tb, HG_WIDTH), 0) >= n_pad, kk, 0.0)
    q = _silu(u_ref[:, COL_Q:COL_Q + HG_WIDTH])
    v = u_ref[:, COL_I:COL_I + HG_WIDTH].astype(BF16)
    gate = _silu(u_ref[:, COL_G:COL_G + HG_WIDTH])
    qt = (q * jnp.exp(b)).astype(BF16)
    kt = (kk * jnp.exp(-b)).astype(BF16)
    causal = _tiled_lower_tri(c, HG_HEADS)

    for r0 in range(0, tb, c):
        rows = slice(r0, r0 + c)
        b_c = b[rows]
        b_last = b_c[c - 1:c, :]
        kh_c = (kk[rows] * jnp.exp(b_last - b_c)).astype(BF16)
        decay = jnp.exp(b_last)
        qt_c = qt[rows]
        kd = _block_diag(_head_slices(kt[rows], HG_DIM))
        vd = _block_diag(_head_slices(v[rows], HG_DIM))
        scores = jnp.where(causal, _dot_nt(qt_c, kd), 0.0).astype(BF16)
        o = _dot(scores, vd)
        o_prev = []
        for p in range(HG_PAIRS):
            lanes = slice(p * 2 * HG_DIM, (p + 1) * 2 * HG_DIM)
            heads = (2 * p, 2 * p + 1)
            st = [st_scr[h] for h in heads]
            o_prev.append(_dot_nt(qt_c[:, lanes], _block_diag([s.astype(BF16) for s in st])))
            upd = _dot_tn(vd[p * 2 * c:(p + 1) * 2 * c, lanes], jnp.concatenate([kh_c[:, lanes]] * 2, axis=0))
            for i, h in enumerate(heads):
                blk = slice(i * HG_DIM, (i + 1) * HG_DIM)
                st_scr[h] = st[i] * decay[:, h * HG_DIM:(h + 1) * HG_DIM] + upd[blk, blk]
        _store_hgrn_out(rows, o + jnp.concatenate(o_prev, axis=1), gate[rows], hgn, mix_ref)


def _pair_columns(a, r0, r1, first_half):
    shape = (a.shape[0], LANES)
    return jnp.where(first_half, jnp.broadcast_to(a[:, r0:r0 + 1], shape), jnp.broadcast_to(a[:, r1:r1 + 1], shape))


def _group_columns(a, g, first_half):
    r = g * HEADS_PER_GROUP
    return jnp.concatenate([_pair_columns(a, r + i, r + i + 1, first_half)
                            for i in range(0, HEADS_PER_GROUP, 2)], axis=1)


def _ssd_block(n_pad, tb, u_ref, col0, cw, cb, dtb, a_neg, dsk, ssn, tri, mix_ref, hg_scr, xpad):
    c = SSM_CHUNK
    conv = cb
    for j in range(CONV_WIDTH):
        off = PAD_ROWS - (CONV_WIDTH - 1) + j
        conv = conv + cw[j:j + 1, :] * xpad[off:off + tb, :]
    act = _silu(conv)
    xs = act[:, 0:SSM_WIDTH]
    bm = act[:, SSM_WIDTH:SSM_WIDTH + SSM_GROUPS * SSM_STATE].astype(BF16)
    cm = act[:, SSM_WIDTH + SSM_GROUPS * SSM_STATE:].astype(BF16)
    z_gate = _silu(u_ref[:, col0:col0 + SSM_WIDTH])

    col_dt = col0 + COL_DT - COL_Z
    dt = jax.nn.softplus(u_ref[:, col_dt:col_dt + LANES] + dtb)
    if n_pad:
        dt = jnp.where(lax.broadcasted_iota(jnp.int32, (tb, LANES), 0) >= n_pad, dt, 0.0)
    cum = _cumsum_rows(tri, dt * a_neg)
    causal = _lower_tri(c)
    first_half = lax.broadcasted_iota(jnp.int32, (1, LANES), 1) < SSM_HEAD_DIM
    zeros = jnp.zeros((c, LANES), BF16)

    for r0 in range(0, tb, c):
        rows = slice(r0, r0 + c)
        cum_c = cum[rows]
        dt_c = dt[rows]
        last = cum_c[c - 1:c, :]
        cum_t = cum_c.T
        dt_t = dt_c.T
        e_cum = jnp.exp(cum_c)
        w_in = dt_c * jnp.exp(last - cum_c)
        e_last = jnp.exp(last)
        for g in range(SSM_GROUPS):
            yield
            glanes = slice(g * GROUP_WIDTH, (g + 1) * GROUP_WIDTH)
            bg = bm[rows, g * SSM_STATE:(g + 1) * SSM_STATE]
            cg = cm[rows, g * SSM_STATE:(g + 1) * SSM_STATE]
            xg = xs[rows, glanes]
            cbt = _dot_nt(cg, bg)
            m_heads = []
            for rr in range(HEADS_PER_GROUP):
                r = g * HEADS_PER_GROUP + rr
                seg = jnp.exp(jnp.where(causal, cum_c[:, r:r + 1] - cum_t[r:r + 1, :], -jnp.inf))
                m_heads.append((cbt * seg * dt_t[r:r + 1, :]).astype(BF16))
            xb = xg.astype(BF16)
            xd_rows = []
            for rr in range(HEADS_PER_GROUP):
                tile = xb[:, (rr // 2) * LANES:(rr // 2 + 1) * LANES]
                tile = jnp.where(first_half if rr % 2 == 0 else ~first_half, tile, jnp.zeros_like(tile))
                xd_rows.append(jnp.concatenate([tile, zeros] if rr < 2 else [zeros, tile], axis=1))
            xd = jnp.concatenate(xd_rows, axis=0)
            hg = hg_scr[g]
            y = (_dot(jnp.concatenate(m_heads, axis=1), xd)
                 + _dot(cg, hg.astype(BF16)) * _group_columns(e_cum, g, first_half))
            xw = (xg * _group_columns(w_in, g, first_half)).astype(BF16)
            hg_scr[g] = hg * _group_columns(e_last, g, first_half) + _dot_tn(bg, xw)
            yz = (y + dsk[:, glanes] * xg) * z_gate[rows, glanes]
            yz = yz * lax.rsqrt(jnp.mean(yz * yz, axis=-1, keepdims=True) + EPS) * ssn[:, glanes]
            mix_ref[rows, HG_WIDTH + g * GROUP_WIDTH:HG_WIDTH + (g + 1) * GROUP_WIDTH] = yz.astype(mix_ref.dtype)


def _interleave(main, side, side_steps):
    for n in side_steps:
        if next(main, StopIteration) is StopIteration:
            break
        for _ in range(n):
            next(side, None)
    for _ in main:
        pass
    for _ in side:
        pass


def _mixer_block(*args):
    for _ in _mixer_steps(*args):
        pass


def _mixer_steps(n_pad, tb, uh_ref, us_ref, us_col0, param_refs, mix_ref, st_scr, hg_scr, xpad, o_scr,
                 log_decay=None):
    lbl_ref, hgn_ref, cw_ref, cb_ref, dtb_ref, alog_ref, dsk_ref, ssn_ref, tri_hg_ref, tri_ssm_ref = param_refs
    lb = _forget_lower_bound(lbl_ref[...])
    if log_decay is None:
        b, worst = _hgrn_log_decay(n_pad, tb, uh_ref[:, COL_F:COL_F + HG_WIDTH], lb, tri_hg_ref[...])
    else:
        b, worst = log_decay[0][...], log_decay[1]
    _hgrn_block(n_pad, tb, uh_ref, lb, hgn_ref[...], b, worst, mix_ref, st_scr, o_scr)
    yield
    col_xbc = us_col0 + COL_XBC - COL_Z
    xbc = us_ref[:, col_xbc:col_xbc + CONV_DIM]
    if n_pad:
        xbc = jnp.where(lax.broadcasted_iota(jnp.int32, (tb, CONV_DIM), 0) >= n_pad, xbc, 0.0)
    xpad[PAD_ROWS:PAD_ROWS + tb, :] = xbc
    yield from _ssd_block(n_pad, tb, us_ref, us_col0, cw_ref[...], cb_ref[...], dtb_ref[...],
                          -jnp.exp(alog_ref[...]), dsk_ref[...], ssn_ref[...], tri_ssm_ref[...],
                          mix_ref, hg_scr, xpad)
    xpad[0:PAD_ROWS, :] = xpad[tb:tb + PAD_ROWS, :]


def _load_state(s0_ref, h0_ref, c0_ref, st_scr, hg_scr, xpad):
    for h in range(HG_HEADS):
        st_scr[h] = s0_ref[0, h].T
    for r in range(0, SSM_HEADS, 2):
        g, lane0 = r // HEADS_PER_GROUP, (r % HEADS_PER_GROUP) * SSM_HEAD_DIM
        hg_scr[g, :, lane0:lane0 + LANES] = jnp.concatenate([h0_ref[0, r], h0_ref[0, r + 1]], axis=0).T
    xpad[0:PAD_ROWS, :] = c0_ref[0]


def _store_state(s_out_ref, h_out_ref, c_out_ref, st_scr, hg_scr, xpad):
    for h in range(HG_HEADS):
        s_out_ref[0, h] = st_scr[h].T
    for r in range(0, SSM_HEADS, 2):
        g, lane0 = r // HEADS_PER_GROUP, (r % HEADS_PER_GROUP) * SSM_HEAD_DIM
        pair = hg_scr[g, :, lane0:lane0 + LANES].T
        h_out_ref[0, r] = pair[0:SSM_HEAD_DIM]
        h_out_ref[0, r + 1] = pair[SSM_HEAD_DIM:]
    c_out_ref[0] = xpad[0:PAD_ROWS, :]


N_MIXER_PARAMS = 10


def _scan_params(params, tb):
    n = min(tb, FF_TILE)
    return tuple(params) + (_chunked_tri(n, HG_CHUNK), _chunked_tri(n, SSM_CHUNK))


PROJ_TILE = 512
PROJ_STEPS_AFTER_SCAN_STEP = (3, 1, 1, 1, 1, 1, 1)
HG_SHAPE = (SSM_GROUPS, SSM_STATE, GROUP_WIDTH)
S_SHAPE = (HG_HEADS, HG_DIM, HG_DIM)
H_SHAPE = (SSM_HEADS, SSM_HEAD_DIM, SSM_STATE)
C_SHAPE = (PAD_ROWS, CONV_DIM)


def _scan_kernel(n_pad, tb, u_ref, *refs):
    param_refs = refs[:N_MIXER_PARAMS]
    (s0_ref, h0_ref, c0_ref, mix_ref, s_out_ref, h_out_ref, c_out_ref,
     st_scr, hg_scr, xpad, o_scr) = refs[N_MIXER_PARAMS:]
    t = pl.program_id(1)

    @pl.when(t == 0)
    def _():
        _load_state(s0_ref, h0_ref, c0_ref, st_scr, hg_scr, xpad)

    _mixer_block(n_pad, tb, u_ref, u_ref, COL_Z, param_refs, mix_ref, st_scr, hg_scr, xpad, o_scr)

    @pl.when(t == pl.num_programs(1) - 1)
    def _():
        _store_state(s_out_ref, h_out_ref, c_out_ref, st_scr, hg_scr, xpad)


def _scan_pipelined_kernel(tb, nt, h1_ref, nm_ref, win_ref, *refs):
    param_refs = refs[:N_MIXER_PARAMS]
    s0_ref, h0_ref, c0_ref, mix_ref, s_out_ref, h_out_ref, c_out_ref = refs[N_MIXER_PARAMS:N_MIXER_PARAMS + 7]
    even, odd = refs[N_MIXER_PARAMS + 7:N_MIXER_PARAMS + 11], refs[N_MIXER_PARAMS + 11:N_MIXER_PARAMS + 15]
    st_scr, hg_scr, xpad, o_scr = refs[N_MIXER_PARAMS + 15:]
    s = pl.program_id(0)
    scanned = jnp.maximum(s - 1, 0)

    @pl.when(s == 0)
    def _():
        uh, us, b, worst = odd
        uh[...] = jnp.zeros(uh.shape, F32)
        us[...] = jnp.zeros(us.shape, F32)
        b[...] = jnp.zeros(b.shape, F32)
        worst[0] = 0.0

    @pl.when(scanned % nt == 0)
    def _():
        _load_state(s0_ref, h0_ref, c0_ref, st_scr, hg_scr, xpad)

    def project_steps(uh_w, us_w, b_w, worst_w):
        hn = _rms(h1_ref[...], nm_ref[...]).astype(BF16)
        for c0 in range(0, D_IN_PAD, PROJ_TILE):
            c1 = min(c0 + PROJ_TILE, D_IN_PAD)
            yield
            tile = _dot(hn, win_ref[:, c0:c1])
            if c0 < COL_Z:
                uh_w[:, c0:c1] = tile
            else:
                us_w[:, c0 - COL_Z:c1 - COL_Z] = tile
            if c0 == COL_F:
                assert c1 == COL_F + HG_WIDTH
                lb = _forget_lower_bound(param_refs[0][...])
                b, worst = _hgrn_log_decay(0, tb, tile, lb, param_refs[N_MIXER_PARAMS - 2][...])
                b_w[...] = b
                worst_w[0] = worst

    def body(write, read):
        uh_r, us_r, b_r, worst_r = read
        _interleave(_mixer_steps(0, tb, uh_r, us_r, 0, param_refs, mix_ref, st_scr, hg_scr, xpad, o_scr,
                                 (b_r, worst_r[0])),
                    project_steps(*write), PROJ_STEPS_AFTER_SCAN_STEP)

    pl.when(s % 2 == 0)(lambda: body(even, odd))
    pl.when(s % 2 == 1)(lambda: body(odd, even))

    @pl.when((s >= 1) & (scanned % nt == nt - 1))
    def _():
        _store_state(s_out_ref, h_out_ref, c_out_ref, st_scr, hg_scr, xpad)


def _scan_pipelined(h1, nb, nt, tb, proj, params, s0, h0, c0):
    n_blocks = nb * nt
    last = n_blocks - 1
    lead = tuple(proj) + _scan_params(params, tb)
    shared = lambda shape: pl.BlockSpec((1,) + shape[1:], lambda s: (0,) * len(shape))
    per_seq = lambda shape: pl.BlockSpec((1,) + shape, lambda s: (jnp.maximum(s - 1, 0) // nt,) + (0,) * len(shape))
    u_bufs = [pltpu.VMEM((tb, COL_Z), F32), pltpu.VMEM((tb, D_IN_PAD - COL_Z), F32),
              pltpu.VMEM((tb, HG_WIDTH), F32), pltpu.SMEM((1,), F32)]
    return pl.pallas_call(
        functools.partial(_scan_pipelined_kernel, tb, nt),
        grid=(n_blocks + 1,),
        in_specs=[pl.BlockSpec((tb, D_MODEL), lambda s: (jnp.minimum(s, last), 0))]
        + [_resident(p.shape) for p in lead]
        + [shared(s0.shape), shared(h0.shape), shared(c0.shape)],
        out_specs=[pl.BlockSpec((tb, D_MODEL), lambda s: (jnp.maximum(s - 1, 0), 0)),
                   per_seq(S_SHAPE), per_seq(H_SHAPE), per_seq(C_SHAPE)],
        out_shape=[jax.ShapeDtypeStruct((n_blocks * tb, D_MODEL), BF16),
                   jax.ShapeDtypeStruct((nb,) + S_SHAPE, F32),
                   jax.ShapeDtypeStruct((nb,) + H_SHAPE, F32),
                   jax.ShapeDtypeStruct((nb,) + C_SHAPE, F32)],
        scratch_shapes=u_bufs + u_bufs + [pltpu.VMEM(S_SHAPE, F32), pltpu.VMEM(HG_SHAPE, F32),
                                          pltpu.VMEM((PAD_ROWS + tb, CONV_DIM), F32),
                                          pltpu.VMEM((tb, HG_WIDTH), F32)],
        compiler_params=_compiler_params(1),
        name="proj_mixer",
    )(h1, *lead, s0, h0, c0)


def _scan(u, u_block0, nb, nt, tb, n_pad, params, s0, h0, c0):
    lead = _scan_params(params, tb)
    shared = lambda shape: pl.BlockSpec((1,) + shape[1:], lambda b, t: (0,) * len(shape))
    per_seq = lambda shape: pl.BlockSpec((1,) + shape, lambda b, t: (b,) + (0,) * len(shape))
    scratch = [pltpu.VMEM(S_SHAPE, F32), pltpu.VMEM(HG_SHAPE, F32), pltpu.VMEM((PAD_ROWS + tb, CONV_DIM), F32),
               pltpu.VMEM((tb, HG_WIDTH), F32)]
    return pl.pallas_call(
        functools.partial(_scan_kernel, n_pad, tb),
        grid=(nb, nt),
        in_specs=[pl.BlockSpec((tb, D_IN_PAD), lambda b, t: (u_block0 + b * nt + t, 0))]
        + [_resident(p.shape) for p in lead]
        + [shared(s0.shape), shared(h0.shape), shared(c0.shape)],
        out_specs=[pl.BlockSpec((tb, D_MODEL), lambda b, t: (b * nt + t, 0)),
                   per_seq(S_SHAPE), per_seq(H_SHAPE), per_seq(C_SHAPE)],
        out_shape=[jax.ShapeDtypeStruct((nb * nt * tb, D_MODEL), BF16),
                   jax.ShapeDtypeStruct((nb,) + S_SHAPE, F32),
                   jax.ShapeDtypeStruct((nb,) + H_SHAPE, F32),
                   jax.ShapeDtypeStruct((nb,) + C_SHAPE, F32)],
        scratch_shapes=scratch,
        compiler_params=_compiler_params(2),
        name="mixer",
    )(u, *lead, s0, h0, c0)


def _sample_kernel(nbs, u_ref, lbl_ref, hgn_ref, cw_ref, cb_ref, dtb_ref, alog_ref, dsk_ref, ssn_ref,
                   sh_ref, ss_ref, sc_ref, mix_ref, sh_out_ref, ss_out_ref, sc_out_ref, obuf, ybuf):
    lb = _forget_lower_bound(lbl_ref[...])
    fz = u_ref[:, COL_F:COL_F + HG_WIDTH]
    f = lb + (1.0 - lb) * jax.nn.sigmoid(fz)
    kk = (1.0 - lb) * jax.nn.sigmoid(-fz)
    q = _silu(u_ref[:, COL_Q:COL_Q + HG_WIDTH])
    v = u_ref[:, COL_I:COL_I + HG_WIDTH]

    xbc = u_ref[:, COL_XBC:COL_XBC + CONV_DIM]
    cw = cw_ref[...]
    conv = cb_ref[...] + cw[CONV_WIDTH - 1:CONV_WIDTH, :] * xbc
    for j in range(CONV_WIDTH - 1):
        conv = conv + cw[j:j + 1, :] * sc_ref[j]
    for j in range(CONV_WIDTH - 2):
        sc_out_ref[j] = sc_ref[j + 1]
    sc_out_ref[CONV_WIDTH - 2] = xbc
    act = _silu(conv)
    xs = act[:, 0:SSM_WIDTH]
    bm = act[:, SSM_WIDTH:SSM_WIDTH + SSM_GROUPS * SSM_STATE]
    cm = act[:, SSM_WIDTH + SSM_GROUPS * SSM_STATE:]
    dt = jax.nn.softplus(u_ref[:, COL_DT:COL_DT + LANES] + dtb_ref[...])
    d_a = jnp.exp(dt * (-jnp.exp(alog_ref[...])))

    top_half = lax.broadcasted_iota(jnp.int32, (LANES, LANES), 0) < SSM_HEAD_DIM
    q_bf, cm_bf = q.astype(BF16), cm.astype(BF16)
    for j in range(nbs):
        row = slice(j, j + 1)
        for h in range(HG_HEADS):
            sl = slice(h * HG_DIM, (h + 1) * HG_DIM)
            s_new = sh_ref[j, h] * _column_tile(f[row, sl]) + _column_tile(kk[row, sl]) * v[row, sl]
            sh_out_ref[j, h] = s_new
            obuf[row, sl] = _dot(q_bf[:, sl], s_new.astype(BF16))[row]
        for rp in range(SSM_HEADS // 2):
            r0, r1 = 2 * rp, 2 * rp + 1
            g = r0 // HEADS_PER_GROUP
            sl = slice(rp * LANES, (rp + 1) * LANES)
            gsl = slice(g * SSM_STATE, (g + 1) * SSM_STATE)
            h2 = jnp.concatenate([ss_ref[j, r0], ss_ref[j, r1]], axis=0)
            da2 = jnp.where(top_half, d_a[row, r0:r0 + 1], d_a[row, r1:r1 + 1])
            dt2 = jnp.where(top_half, dt[row, r0:r0 + 1], dt[row, r1:r1 + 1])
            h_new = h2 * da2 + (dt2 * _column_tile(xs[row, sl])) * bm[row, gsl]
            ss_out_ref[j, r0] = h_new[0:SSM_HEAD_DIM]
            ss_out_ref[j, r1] = h_new[SSM_HEAD_DIM:]
            ybuf[row, sl] = _dot_nt(cm_bf[:, gsl], h_new.astype(BF16))[row]

    o = obuf[...]
    gate = _silu(u_ref[:, COL_G:COL_G + HG_WIDTH])
    hgn = hgn_ref[...]
    for h in range(HG_HEADS):
        sl = slice(h * HG_DIM, (h + 1) * HG_DIM)
        oh = o[:, sl]
        oh = oh * lax.rsqrt(jnp.mean(oh * oh, axis=-1, keepdims=True) + EPS)
        mix_ref[:, sl] = oh * hgn[:, sl] * gate[:, sl]
    yz = (ybuf[...] + dsk_ref[...] * xs) * _silu(u_ref[:, COL_Z:COL_Z + SSM_WIDTH])
    ssn = ssn_ref[...]
    for g in range(SSM_GROUPS):
        sl = slice(g * GROUP_WIDTH, (g + 1) * GROUP_WIDTH)
        seg = yz[:, sl]
        seg = seg * lax.rsqrt(jnp.mean(seg * seg, axis=-1, keepdims=True) + EPS) * ssn[:, sl]
        mix_ref[:, HG_WIDTH + g * GROUP_WIDTH:HG_WIDTH + (g + 1) * GROUP_WIDTH] = seg


def _sample_mixer(u, u_block0, n, nbs, params, sh, ss, sc_t):
    small = [_resident(p.shape) for p in params]
    return pl.pallas_call(
        functools.partial(_sample_kernel, nbs),
        grid=(n // nbs,),
        in_specs=[pl.BlockSpec((nbs, D_IN_PAD), lambda i: (u_block0 + i, 0))] + small
        + [pl.BlockSpec((nbs,) + sh.shape[1:], lambda i: (i, 0, 0, 0)),
           pl.BlockSpec((nbs,) + ss.shape[1:], lambda i: (i, 0, 0, 0)),
           pl.BlockSpec((CONV_WIDTH - 1, nbs, CONV_DIM), lambda i: (0, i, 0))],
        out_specs=[pl.BlockSpec((nbs, D_MODEL), lambda i: (i, 0)),
                   pl.BlockSpec((nbs,) + sh.shape[1:], lambda i: (i, 0, 0, 0)),
                   pl.BlockSpec((nbs,) + ss.shape[1:], lambda i: (i, 0, 0, 0)),
                   pl.BlockSpec((CONV_WIDTH - 1, nbs, CONV_DIM), lambda i: (0, i, 0))],
        out_shape=[jax.ShapeDtypeStruct((n, D_MODEL), F32),
                   jax.ShapeDtypeStruct(sh.shape, F32),
                   jax.ShapeDtypeStruct(ss.shape, F32),
                   jax.ShapeDtypeStruct(sc_t.shape, F32)],
        scratch_shapes=[pltpu.VMEM((nbs, HG_WIDTH), F32), pltpu.VMEM((nbs, SSM_WIDTH), F32)],
        compiler_params=_compiler_params(1),
        name="sample_mixer",
    )(u, *params, sh, ss, sc_t)


TM_DENSE = 512
TM_SCAN = 512
TM_SMALL = 128
SAMPLES_PER_STEP = 16


def _pad_lanes(row, value=0.0):
    return jnp.pad(row, ((0, 0), (0, LANES - row.shape[1])), constant_values=value)


def kernel(x_prompt, x_sample, state_hgrn, state_ssm, state_conv, meta_tokens, lb_logits, norm_ffn1, w_ffn1_gate, w_ffn1_up, w_ffn1_down, norm_mix, w_in, hg_norm, conv_w, conv_b, dt_bias, a_log, d_skip, ssm_norm, w_out, norm_ffn2, w_ffn2_gate, w_ffn2_up, w_ffn2_down, norm_final):
    bp, seq_p, _ = x_prompt.shape
    n_s = x_sample.shape[0]
    assert x_sample.shape[1] == 1 and n_s == TM_SMALL and seq_p % TM_SCAN == 0 and (bp * seq_p) % TM_DENSE == 0
    layer = 0

    n1, nm, n2 = norm_ffn1[layer][None], norm_mix[layer][None], norm_ffn2[layer][None]
    nf = norm_final[None]
    mixer_params = (lb_logits, hg_norm[layer][None], conv_w[layer], conv_b[layer][None],
                    _pad_lanes(dt_bias[layer][None]), _pad_lanes(a_log[layer][None]),
                    jnp.repeat(d_skip[layer], SSM_HEAD_DIM)[None], ssm_norm[layer][None])

    n_pad = TM_SMALL - N_META
    x_small = jnp.concatenate([jnp.zeros((n_pad, D_MODEL), F32), meta_tokens, x_sample[:, 0]], axis=0)
    h1_small, wg1, wu1, wd1 = _ffn1_small(x_small, n1, w_ffn1_gate[layer], w_ffn1_up[layer], w_ffn1_down[layer])
    u_small, win = _proj_small(h1_small, nm, jnp.swapaxes(w_in[layer], 0, 1))

    zeros_s = jnp.zeros((1,) + S_SHAPE, F32)
    zeros_h = jnp.zeros((1,) + H_SHAPE, F32)
    zeros_c = jnp.zeros((1,) + C_SHAPE, F32)
    _, s_meta, h_meta, c_meta = _scan(u_small, 0, 1, 1, TM_SMALL, n_pad, mixer_params, zeros_s, zeros_h, zeros_c)

    sc_t = jnp.swapaxes(state_conv[layer], 0, 1)
    mix_s, hgrn_s, ssm_s, conv_s_t = _sample_mixer(
        u_small, TM_SMALL // SAMPLES_PER_STEP, n_s, SAMPLES_PER_STEP, mixer_params,
        state_hgrn[layer], state_ssm[layer], sc_t)

    xp = x_prompt.reshape(bp * seq_p, D_MODEL)
    h1_p, wg2, wu2, wd2, wo = _ffn1(xp, n1, wg1, wu1, wd1, TM_DENSE, w_ffn2_gate[layer], w_ffn2_up[layer],
                                    w_ffn2_down[layer], w_out[layer])
    y_s = _ffn_out(h1_small, 1, mix_s, wo, n2, wg2, wu2, wd2, nf, TM_SMALL)
    mix_p, hgrn_p, ssm_p, conv_p = _scan_pipelined(h1_p, bp, seq_p // TM_SCAN, TM_SCAN, (nm, win), mixer_params,
                                                   s_meta, h_meta, c_meta)
    y_p = _ffn_out(h1_p, 0, mix_p, wo, n2, wg2, wu2, wd2, nf, TM_DENSE)

    keep = slice(PAD_ROWS - (CONV_WIDTH - 1), PAD_ROWS)
    return (y_p.reshape(bp, seq_p, D_MODEL),
            y_s.reshape(n_s, 1, D_MODEL),
            hgrn_p[None], ssm_p[None], conv_p[:, keep][None],
            hgrn_s[None], ssm_s[None], jnp.swapaxes(conv_s_t, 0, 1)[None])
```

```python
import functools

import jax
import jax.numpy as jnp
from jax import lax
from jax.experimental import pallas as pl
from jax.experimental.pallas import tpu as pltpu

F32 = jnp.float32
BF16 = jnp.bfloat16

D_MODEL = 1024
D_FF = 2816
N_META = 16
HG_WIDTH = 512
HG_HEADS = 4
HG_DIM = 128
SSM_WIDTH = 512
SSM_HEADS = 8
SSM_HEAD_DIM = 64
SSM_GROUPS = 2
SSM_STATE = 128
CONV_WIDTH = 4
CONV_DIM = SSM_WIDTH + 2 * SSM_GROUPS * SSM_STATE
EPS = 1e-6

LANES = 128
SUBLANES = 8
VMEM_LIMIT_BYTES = 56 * 1024 * 1024

COL_Q = 0
COL_F = HG_WIDTH
COL_I = 2 * HG_WIDTH
COL_G = 3 * HG_WIDTH
COL_Z = 4 * HG_WIDTH
COL_XBC = COL_Z + SSM_WIDTH
COL_DT = COL_XBC + CONV_DIM
D_IN_PAD = COL_DT + LANES

FF_TILE = 256
HG_CHUNK = 64
HG_MAX_CHUNK_LOG_DECAY = 80.0
SSM_CHUNK = 128
PAD_ROWS = SUBLANES
HG_PAIRS = HG_HEADS // 2
HEADS_PER_GROUP = SSM_HEADS // SSM_GROUPS
GROUP_WIDTH = SSM_WIDTH // SSM_GROUPS
assert 2 * SSM_HEAD_DIM == LANES and 2 * HG_DIM == FF_TILE


def _dot(a, b):
    return jnp.dot(a, b, preferred_element_type=F32)


def _dot_nt(a, b):
    return lax.dot_general(a, b, (((1,), (1,)), ((), ())), preferred_element_type=F32)


def _dot_tn(a, b):
    return lax.dot_general(a, b, (((0,), (0,)), ((), ())), preferred_element_type=F32)


def _rms(x, w):
    return x * lax.rsqrt(jnp.mean(x * x, axis=-1, keepdims=True) + EPS) * w


def _silu(x):
    return x * jax.nn.sigmoid(x)


def _swiglu(xn, wg_ref, wu_ref, wd_ref, side_work=()):
    side_work = list(side_work)
    acc = jnp.zeros((xn.shape[0], D_MODEL), F32)
    for j in range(D_FF // FF_TILE):
        cols = slice(j * FF_TILE, (j + 1) * FF_TILE)
        g = _dot(xn, wg_ref[:, cols])
        u = _dot(xn, wu_ref[:, cols])
        if side_work:
            side_work.pop(0)()
        acc = acc + _dot((_silu(g) * u).astype(BF16), wd_ref[cols, :])
    assert not side_work
    return acc


def _cumsum_rows(tri, a):
    n = tri.shape[0]
    if a.shape[0] > n:
        return jnp.concatenate([_cumsum_rows(tri, a[r:r + n]) for r in range(0, a.shape[0], n)], axis=0)
    a1 = a.astype(BF16)
    r1 = a - a1.astype(F32)
    a2 = r1.astype(BF16)
    a3 = (r1 - a2.astype(F32)).astype(BF16)
    return _dot(tri, a1) + _dot(tri, a2) + _dot(tri, a3)


def _lower_tri(n):
    row = lax.broadcasted_iota(jnp.int32, (n, n), 0)
    col = lax.broadcasted_iota(jnp.int32, (n, n), 1)
    return row >= col


def _chunked_tri(n, chunk):
    assert chunk & (chunk - 1) == 0
    row = lax.broadcasted_iota(jnp.int32, (n, n), 0)
    col = lax.broadcasted_iota(jnp.int32, (n, n), 1)
    same_chunk = (row ^ col) < chunk
    return ((row >= col) & same_chunk).astype(BF16)


def _tiled_lower_tri(chunk, reps):
    assert chunk & (chunk - 1) == 0
    row = lax.broadcasted_iota(jnp.int32, (chunk, reps * chunk), 0)
    col = lax.broadcasted_iota(jnp.int32, (chunk, reps * chunk), 1)
    return row >= (col & (chunk - 1))


def _block_diag(blocks):
    n = len(blocks)
    r, c = blocks[0].shape
    rows = []
    for i, blk in enumerate(blocks):
        parts = []
        if i:
            parts.append(jnp.zeros((r, c * i), blk.dtype))
        parts.append(blk)
        if i < n - 1:
            parts.append(jnp.zeros((r, c * (n - 1 - i)), blk.dtype))
        rows.append(jnp.concatenate(parts, axis=1))
    return jnp.concatenate(rows, axis=0)


def _forget_lower_bound(lbl):
    l0, l1 = lbl[0:1], lbl[1:2]
    m = jnp.maximum(l0, l1)
    e0, e1 = jnp.exp(l0 - m), jnp.exp(l1 - m)
    return e0 / (e0 + e1)


def _resident(shape):
    nd = len(shape)
    return pl.BlockSpec(shape, lambda *_: (0,) * nd, pipeline_mode=pl.Buffered(1))


def _compiler_params(n_grid_axes):
    return pltpu.CompilerParams(dimension_semantics=("arbitrary",) * n_grid_axes,
                                vmem_limit_bytes=VMEM_LIMIT_BYTES)


N_CAST = 4
CAST_TILE = LANES


def _ffn1_kernel(x_ref, n1_ref, wg_ref, wu_ref, wd_ref, *refs):
    cast_in, h1_ref, cast_out = refs[:N_CAST], refs[N_CAST], refs[N_CAST + 1:]

    def cast(src, dst):
        def work():
            dst[...] = src[...].astype(BF16)
        return work

    x = x_ref[...]
    xn = _rms(x, n1_ref[...]).astype(BF16)
    casts = [cast(src, dst) for src, dst in zip(cast_in, cast_out)]
    h1_ref[...] = x + 0.5 * _swiglu(xn, wg_ref, wu_ref, wd_ref, casts)


def _ffn1(x, n1, wg, wu, wd, tm, w2g, w2u, w2d, w_out):
    n = x.shape[0]
    steps = n // tm
    n_ff, n_wo = D_FF // CAST_TILE, w_out.shape[0] // CAST_TILE
    assert steps >= n_ff and steps >= n_wo
    cast_specs = [pl.BlockSpec((D_MODEL, CAST_TILE), lambda i: (0, jnp.minimum(i, n_ff - 1))),
                  pl.BlockSpec((D_MODEL, CAST_TILE), lambda i: (0, jnp.minimum(i, n_ff - 1))),
                  pl.BlockSpec((CAST_TILE, D_MODEL), lambda i: (jnp.minimum(i, n_ff - 1), 0)),
                  pl.BlockSpec((CAST_TILE, w_out.shape[1]), lambda i: (jnp.minimum(i, n_wo - 1), 0))]
    cast = (w2g, w2u, w2d, w_out)
    return pl.pallas_call(
        _ffn1_kernel,
        grid=(steps,),
        in_specs=[pl.BlockSpec((tm, D_MODEL), lambda i: (i, 0)),
                  _resident(n1.shape), _resident(wg.shape), _resident(wu.shape), _resident(wd.shape)] + cast_specs,
        out_specs=[pl.BlockSpec((tm, D_MODEL), lambda i: (i, 0))] + cast_specs,
        out_shape=[jax.ShapeDtypeStruct((n, D_MODEL), F32)] + [jax.ShapeDtypeStruct(w.shape, BF16) for w in cast],
        compiler_params=_compiler_params(1),
        name="ffn1",
    )(x, n1, wg, wu, wd, *cast)


def _ffn_tile_step(xn, wg_ref, wu_ref, wd_ref, wg_out, wu_out, wd_out, acc_scr):
    wg, wu, wd = wg_ref[...].astype(BF16), wu_ref[...].astype(BF16), wd_ref[...].astype(BF16)
    wg_out[...] = wg
    wu_out[...] = wu
    wd_out[...] = wd
    acc_scr[...] += _dot((_silu(_dot(xn, wg)) * _dot(xn, wu)).astype(BF16), wd)


def _ffn1_small_kernel(x_ref, n1_ref, wg_ref, wu_ref, wd_ref, h1_ref, wg_out, wu_out, wd_out, xn_scr, acc_scr):
    j = pl.program_id(0)

    @pl.when(j == 0)
    def _():
        xn_scr[...] = _rms(x_ref[...], n1_ref[...]).astype(BF16)
        acc_scr[...] = jnp.zeros(acc_scr.shape, F32)

    _ffn_tile_step(xn_scr[...], wg_ref, wu_ref, wd_ref, wg_out, wu_out, wd_out, acc_scr)

    @pl.when(j == pl.num_programs(0) - 1)
    def _():
        h1_ref[...] = x_ref[...] + 0.5 * acc_scr[...]


def _weight_tile_specs():
    cols = pl.BlockSpec((D_MODEL, FF_TILE), lambda j: (0, j))
    rows = pl.BlockSpec((FF_TILE, D_MODEL), lambda j: (j, 0))
    shapes = [jax.ShapeDtypeStruct((D_MODEL, D_FF), BF16), jax.ShapeDtypeStruct((D_MODEL, D_FF), BF16),
              jax.ShapeDtypeStruct((D_FF, D_MODEL), BF16)]
    return [cols, cols, rows], shapes


def _ffn1_small(x, n1, wg, wu, wd):
    n = x.shape[0]
    wspecs, wshapes = _weight_tile_specs()
    return pl.pallas_call(
        _ffn1_small_kernel,
        grid=(D_FF // FF_TILE,),
        in_specs=[_resident(x.shape), _resident(n1.shape)] + wspecs,
        out_specs=[pl.BlockSpec((n, D_MODEL), lambda j: (0, 0))] + wspecs,
        out_shape=[jax.ShapeDtypeStruct((n, D_MODEL), F32)] + wshapes,
        scratch_shapes=[pltpu.VMEM((n, D_MODEL), BF16), pltpu.VMEM((n, D_MODEL), F32)],
        compiler_params=_compiler_params(1),
        name="ffn1_small",
    )(x, n1, wg, wu, wd)


W_IN_TILE = 1024


def _proj_small_kernel(n_cols, h1_ref, nm_ref, wt_ref, u_ref, win_out, hn_scr):
    j = pl.program_id(0)

    @pl.when(j == 0)
    def _():
        hn_scr[...] = _rms(h1_ref[...], nm_ref[...]).astype(BF16)

    col = j * W_IN_TILE + lax.broadcasted_iota(jnp.int32, (W_IN_TILE, 1), 0)
    w = jnp.where(col < n_cols, wt_ref[...], 0.0).T.astype(BF16)
    win_out[...] = w
    u_ref[...] = _dot(hn_scr[...], w)


def _proj_small(h1, nm, w_in_t):
    n = h1.shape[0]
    tile = lambda rows: pl.BlockSpec((rows, W_IN_TILE), lambda j: (0, j))
    return pl.pallas_call(
        functools.partial(_proj_small_kernel, w_in_t.shape[0]),
        grid=(pl.cdiv(D_IN_PAD, W_IN_TILE),),
        in_specs=[_resident(h1.shape), _resident(nm.shape), pl.BlockSpec((W_IN_TILE, D_MODEL), lambda j: (j, 0))],
        out_specs=[tile(n), tile(D_MODEL)],
        out_shape=[jax.ShapeDtypeStruct((n, D_IN_PAD), F32), jax.ShapeDtypeStruct((D_MODEL, D_IN_PAD), BF16)],
        scratch_shapes=[pltpu.VMEM((n, D_MODEL), BF16)],
        compiler_params=_compiler_params(1),
        name="proj_small",
    )(h1, nm, w_in_t)


def _ffn_out_kernel(h1_ref, mix_ref, wo_ref, n2_ref, wg_ref, wu_ref, wd_ref, nf_ref, y_ref):
    h2 = h1_ref[...] + _dot(mix_ref[...].astype(BF16), wo_ref[...])
    hn = _rms(h2, n2_ref[...]).astype(BF16)
    h3 = h2 + 0.5 * _swiglu(hn, wg_ref, wu_ref, wd_ref)
    y_ref[...] = _rms(h3, nf_ref[...])


def _ffn_out(h1, h1_block0, mix, wo, n2, wg, wu, wd, nf, tm):
    n = mix.shape[0]
    return pl.pallas_call(
        _ffn_out_kernel,
        grid=(n // tm,),
        in_specs=[pl.BlockSpec((tm, D_MODEL), lambda i: (i + h1_block0, 0)),
                  pl.BlockSpec((tm, D_MODEL), lambda i: (i, 0)),
                  _resident(wo.shape), _resident(n2.shape), _resident(wg.shape), _resident(wu.shape),
                  _resident(wd.shape), _resident(nf.shape)],
        out_specs=pl.BlockSpec((tm, D_MODEL), lambda i: (i, 0)),
        out_shape=jax.ShapeDtypeStruct((n, D_MODEL), F32),
        compiler_params=_compiler_params(1),
        name="ffn_out",
    )(h1, mix, wo, n2, wg, wu, wd, nf)


def _head_slices(a, width):
    return [a[:, i:i + width] for i in range(0, a.shape[1], width)]


def _column_tile(row):
    return jnp.broadcast_to(row, (LANES, LANES)).T


def _store_hgrn_out(rows, o, gate, hgn, mix_ref):
    for h in range(HG_HEADS):
        sl = slice(h * HG_DIM, (h + 1) * HG_DIM)
        oh = o[:, sl]
        oh = oh * lax.rsqrt(jnp.mean(oh * oh, axis=-1, keepdims=True) + EPS)
        mix_ref[rows, sl] = (oh * hgn[:, sl] * gate[:, sl]).astype(mix_ref.dtype)


def _hgrn_log_decay(n_pad, tb, fz, lb, tri):
    logf = jnp.log(lb + (1.0 - lb) * jax.nn.sigmoid(fz))
    if n_pad:
        logf = jnp.where(lax.broadcasted_iota(jnp.int32, (tb, HG_WIDTH), 0) >= n_pad, logf, 0.0)
    b = _cumsum_rows(tri, logf)
    chunk_ends = jnp.concatenate([b[r:r + 1] for r in range(HG_CHUNK - 1, tb, HG_CHUNK)], axis=0)
    return b, jnp.max(-chunk_ends)


def _hgrn_block(n_pad, tb, u_ref, lb, hgn, b, worst_log_decay, mix_ref, st_scr, o_scr):
    chunked_is_safe = worst_log_decay < HG_MAX_CHUNK_LOG_DECAY

    @pl.when(chunked_is_safe)
    def _():
        _hgrn_chunked(n_pad, tb, u_ref, lb, hgn, b, mix_ref, st_scr)

    @pl.when(jnp.logical_not(chunked_is_safe))
    def _():
        _hgrn_per_token(n_pad, tb, u_ref, lb, hgn, mix_ref, st_scr, o_scr)


def _hgrn_per_token(n_pad, tb, u_ref, lb, hgn, mix_ref, st_scr, o_scr):
    for h in range(HG_HEADS):
        st_scr[h] = st_scr[h].T

    def sublane_group(i, carry):
        rows = pl.ds(pl.multiple_of(i * SUBLANES, SUBLANES), SUBLANES)
        fz = u_ref[rows, COL_F:COL_F + HG_WIDTH]
        f = lb + (1.0 - lb) * jax.nn.sigmoid(fz)
        kk = (1.0 - lb) * jax.nn.sigmoid(-fz)
        if n_pad:
            valid = i * SUBLANES + lax.broadcasted_iota(jnp.int32, (SUBLANES, HG_WIDTH), 0) >= n_pad
            f = jnp.where(valid, f, 1.0)
            kk = jnp.where(valid, kk, 0.0)
        q = _silu(u_ref[rows, COL_Q:COL_Q + HG_WIDTH])
        v = u_ref[rows, COL_I:COL_I + HG_WIDTH]
        for h in range(HG_HEADS):
            sl = slice(h * HG_DIM, (h + 1) * HG_DIM)
            s = st_scr[h]
            o_rows = []
            for j in range(SUBLANES):
                r = slice(j, j + 1)
                s = s * _column_tile(f[r, sl]) + _column_tile(kk[r, sl]) * v[r, sl]
                o_rows.append(jnp.sum(s * _column_tile(q[r, sl]), axis=0, keepdims=True))
            st_scr[h] = s
            o_scr[rows, sl] = jnp.concatenate(o_rows, axis=0)
        return carry

    lax.fori_loop(0, tb // SUBLANES, sublane_group, 0)
    for h in range(HG_HEADS):
        st_scr[h] = st_scr[h].T
    _store_hgrn_out(slice(0, tb), o_scr[...], _silu(u_ref[:, COL_G:COL_G + HG_WIDTH]), hgn, mix_ref)


def _hgrn_chunked(n_pad, tb, u_ref, lb, hgn, b, mix_ref, st_scr):
    c = HG_CHUNK
    kk = (1.0 - lb) * jax.nn.sigmoid(-u_ref[:, COL_F:COL_F + HG_WIDTH])
    if n_pad:
        kk = jnp.where(lax.broadcasted_iota(jnp.int32, (tb, HG_WIDTH), 0) >= n_pad, kk, 0.0)
    q = _silu(u_ref[:, COL_Q:COL_Q + HG_WIDTH])
    v = u_ref[:, COL_I:COL_I + HG_WIDTH].astype(BF16)
    gate = _silu(u_ref[:, COL_G:COL_G + HG_WIDTH])
    qt = (q * jnp.exp(b)).astype(BF16)
    kt = (kk * jnp.exp(-b)).astype(BF16)
    causal = _tiled_lower_tri(c, HG_HEADS)

    for r0 in range(0, tb, c):
        rows = slice(r0, r0 + c)
        b_c = b[rows]
        b_last = b_c[c - 1:c, :]
        kh_c = (kk[rows] * jnp.exp(b_last - b_c)).astype(BF16)
        decay = jnp.exp(b_last)
        qt_c = qt[rows]
        kd = _block_diag(_head_slices(kt[rows], HG_DIM))
        vd = _block_diag(_head_slices(v[rows], HG_DIM))
        scores = jnp.where(causal, _dot_nt(qt_c, kd), 0.0).astype(BF16)
        o = _dot(scores, vd)
        o_prev = []
        for p in range(HG_PAIRS):
            lanes = slice(p * 2 * HG_DIM, (p + 1) * 2 * HG_DIM)
            heads = (2 * p, 2 * p + 1)
            st = [st_scr[h] for h in heads]
            o_prev.append(_dot_nt(qt_c[:, lanes], _block_diag([s.astype(BF16) for s in st])))
            upd = _dot_tn(vd[p * 2 * c:(p + 1) * 2 * c, lanes], jnp.concatenate([kh_c[:, lanes]] * 2, axis=0))
            for i, h in enumerate(heads):
                blk = slice(i * HG_DIM, (i + 1) * HG_DIM)
                st_scr[h] = st[i] * decay[:, h * HG_DIM:(h + 1) * HG_DIM] + upd[blk, blk]
        _store_hgrn_out(rows, o + jnp.concatenate(o_prev, axis=1), gate[rows], hgn, mix_ref)


def _pair_columns(a, r0, r1, first_half):
    shape = (a.shape[0], LANES)
    return jnp.where(first_half, jnp.broadcast_to(a[:, r0:r0 + 1], shape), jnp.broadcast_to(a[:, r1:r1 + 1], shape))


def _group_columns(a, g, first_half):
    r = g * HEADS_PER_GROUP
    return jnp.concatenate([_pair_columns(a, r + i, r + i + 1, first_half)
                            for i in range(0, HEADS_PER_GROUP, 2)], axis=1)


def _ssd_block(n_pad, tb, u_ref, col0, cw, cb, dtb, a_neg, dsk, ssn, tri, mix_ref, hg_scr, xpad):
    c = SSM_CHUNK
    conv = cb
    for j in range(CONV_WIDTH):
        off = PAD_ROWS - (CONV_WIDTH - 1) + j
        conv = conv + cw[j:j + 1, :] * xpad[off:off + tb, :]
    act = _silu(conv)
    xs = act[:, 0:SSM_WIDTH]
    bm = act[:, SSM_WIDTH:SSM_WIDTH + SSM_GROUPS * SSM_STATE].astype(BF16)
    cm = act[:, SSM_WIDTH + SSM_GROUPS * SSM_STATE:].astype(BF16)
    z_gate = _silu(u_ref[:, col0:col0 + SSM_WIDTH])

    col_dt = col0 + COL_DT - COL_Z
    dt = jax.nn.softplus(u_ref[:, col_dt:col_dt + LANES] + dtb)
    if n_pad:
        dt = jnp.where(lax.broadcasted_iota(jnp.int32, (tb, LANES), 0) >= n_pad, dt, 0.0)
    cum = _cumsum_rows(tri, dt * a_neg)
    causal = _lower_tri(c)
    first_half = lax.broadcasted_iota(jnp.int32, (1, LANES), 1) < SSM_HEAD_DIM
    zeros = jnp.zeros((c, LANES), BF16)

    for r0 in range(0, tb, c):
        rows = slice(r0, r0 + c)
        cum_c = cum[rows]
        dt_c = dt[rows]
        last = cum_c[c - 1:c, :]
        cum_t = cum_c.T
        dt_t = dt_c.T
        e_cum = jnp.exp(cum_c)
        w_in = dt_c * jnp.exp(last - cum_c)
        e_last = jnp.exp(last)
        for g in range(SSM_GROUPS):
            yield
            glanes = slice(g * GROUP_WIDTH, (g + 1) * GROUP_WIDTH)
            bg = bm[rows, g * SSM_STATE:(g + 1) * SSM_STATE]
            cg = cm[rows, g * SSM_STATE:(g + 1) * SSM_STATE]
            xg = xs[rows, glanes]
            cbt = _dot_nt(cg, bg)
            m_heads = []
            for rr in range(HEADS_PER_GROUP):
                r = g * HEADS_PER_GROUP + rr
                seg = jnp.exp(jnp.where(causal, cum_c[:, r:r + 1] - cum_t[r:r + 1, :], -jnp.inf))
                m_heads.append((cbt * seg * dt_t[r:r + 1, :]).astype(BF16))
            xb = xg.astype(BF16)
            xd_rows = []
            for rr in range(HEADS_PER_GROUP):
                tile = xb[:, (rr // 2) * LANES:(rr // 2 + 1) * LANES]
                tile = jnp.where(first_half if rr % 2 == 0 else ~first_half, tile, jnp.zeros_like(tile))
                xd_rows.append(jnp.concatenate([tile, zeros] if rr < 2 else [zeros, tile], axis=1))
            xd = jnp.concatenate(xd_rows, axis=0)
            hg = hg_scr[g]
            y = (_dot(jnp.concatenate(m_heads, axis=1), xd)
                 + _dot(cg, hg.astype(BF16)) * _group_columns(e_cum, g, first_half))
            xw = (xg * _group_columns(w_in, g, first_half)).astype(BF16)
            hg_scr[g] = hg * _group_columns(e_last, g, first_half) + _dot_tn(bg, xw)
            yz = (y + dsk[:, glanes] * xg) * z_gate[rows, glanes]
            yz = yz * lax.rsqrt(jnp.mean(yz * yz, axis=-1, keepdims=True) + EPS) * ssn[:, glanes]
            mix_ref[rows, HG_WIDTH + g * GROUP_WIDTH:HG_WIDTH + (g + 1) * GROUP_WIDTH] = yz.astype(mix_ref.dtype)


def _interleave(main, side, side_steps):
    for n in side_steps:
        if next(main, StopIteration) is StopIteration:
            break
        for _ in range(n):
            next(side, None)
    for _ in main:
        pass
    for _ in side:
        pass


def _mixer_block(*args):
    for _ in _mixer_steps(*args):
        pass


def _mixer_steps(n_pad, tb, uh_ref, us_ref, us_col0, param_refs, mix_ref, st_scr, hg_scr, xpad, o_scr,
                 log_decay=None):
    lbl_ref, hgn_ref, cw_ref, cb_ref, dtb_ref, alog_ref, dsk_ref, ssn_ref, tri_hg_ref, tri_ssm_ref = param_refs
    lb = _forget_lower_bound(lbl_ref[...])
    if log_decay is None:
        b, worst = _hgrn_log_decay(n_pad, tb, uh_ref[:, COL_F:COL_F + HG_WIDTH], lb, tri_hg_ref[...])
    else:
        b, worst = log_decay[0][...], log_decay[1]
    _hgrn_block(n_pad, tb, uh_ref, lb, hgn_ref[...], b, worst, mix_ref, st_scr, o_scr)
    yield
    col_xbc = us_col0 + COL_XBC - COL_Z
    xbc = us_ref[:, col_xbc:col_xbc + CONV_DIM]
    if n_pad:
        xbc = jnp.where(lax.broadcasted_iota(jnp.int32, (tb, CONV_DIM), 0) >= n_pad, xbc, 0.0)
    xpad[PAD_ROWS:PAD_ROWS + tb, :] = xbc
    yield from _ssd_block(n_pad, tb, us_ref, us_col0, cw_ref[...], cb_ref[...], dtb_ref[...],
                          -jnp.exp(alog_ref[...]), dsk_ref[...], ssn_ref[...], tri_ssm_ref[...],
                          mix_ref, hg_scr, xpad)
    xpad[0:PAD_ROWS, :] = xpad[tb:tb + PAD_ROWS, :]


def _load_state(s0_ref, h0_ref, c0_ref, st_scr, hg_scr, xpad):
    for h in range(HG_HEADS):
        st_scr[h] = s0_ref[0, h].T
    for r in range(0, SSM_HEADS, 2):
        g, lane0 = r // HEADS_PER_GROUP, (r % HEADS_PER_GROUP) * SSM_HEAD_DIM
        hg_scr[g, :, lane0:lane0 + LANES] = jnp.concatenate([h0_ref[0, r], h0_ref[0, r + 1]], axis=0).T
    xpad[0:PAD_ROWS, :] = c0_ref[0]


def _store_state(s_out_ref, h_out_ref, c_out_ref, st_scr, hg_scr, xpad):
    for h in range(HG_HEADS):
        s_out_ref[0, h] = st_scr[h].T
    for r in range(0, SSM_HEADS, 2):
        g, lane0 = r // HEADS_PER_GROUP, (r % HEADS_PER_GROUP) * SSM_HEAD_DIM
        pair = hg_scr[g, :, lane0:lane0 + LANES].T
        h_out_ref[0, r] = pair[0:SSM_HEAD_DIM]
        h_out_ref[0, r + 1] = pair[SSM_HEAD_DIM:]
    c_out_ref[0] = xpad[0:PAD_ROWS, :]


N_MIXER_PARAMS = 10


def _scan_params(params, tb):
    n = min(tb, FF_TILE)
    return tuple(params) + (_chunked_tri(n, HG_CHUNK), _chunked_tri(n, SSM_CHUNK))


PROJ_TILE = 512
PROJ_STEPS_AFTER_SCAN_STEP = (3, 1, 1, 1, 1, 1, 1)
HG_SHAPE = (SSM_GROUPS, SSM_STATE, GROUP_WIDTH)
S_SHAPE = (HG_HEADS, HG_DIM, HG_DIM)
H_SHAPE = (SSM_HEADS, SSM_HEAD_DIM, SSM_STATE)
C_SHAPE = (PAD_ROWS, CONV_DIM)


def _scan_kernel(n_pad, tb, u_ref, *refs):
    param_refs = refs[:N_MIXER_PARAMS]
    (s0_ref, h0_ref, c0_ref, mix_ref, s_out_ref, h_out_ref, c_out_ref,
     st_scr, hg_scr, xpad, o_scr) = refs[N_MIXER_PARAMS:]
    t = pl.program_id(1)

    @pl.when(t == 0)
    def _():
        _load_state(s0_ref, h0_ref, c0_ref, st_scr, hg_scr, xpad)

    _mixer_block(n_pad, tb, u_ref, u_ref, COL_Z, param_refs, mix_ref, st_scr, hg_scr, xpad, o_scr)

    @pl.when(t == pl.num_programs(1) - 1)
    def _():
        _store_state(s_out_ref, h_out_ref, c_out_ref, st_scr, hg_scr, xpad)


def _scan_pipelined_kernel(tb, nt, h1_ref, nm_ref, win_ref, *refs):
    param_refs = refs[:N_MIXER_PARAMS]
    s0_ref, h0_ref, c0_ref, mix_ref, s_out_ref, h_out_ref, c_out_ref = refs[N_MIXER_PARAMS:N_MIXER_PARAMS + 7]
    even, odd = refs[N_MIXER_PARAMS + 7:N_MIXER_PARAMS + 11], refs[N_MIXER_PARAMS + 11:N_MIXER_PARAMS + 15]
    st_scr, hg_scr, xpad, o_scr = refs[N_MIXER_PARAMS + 15:]
    s = pl.program_id(0)
    scanned = jnp.maximum(s - 1, 0)

    @pl.when(s == 0)
    def _():
        uh, us, b, worst = odd
        uh[...] = jnp.zeros(uh.shape, F32)
        us[...] = jnp.zeros(us.shape, F32)
        b[...] = jnp.zeros(b.shape, F32)
        worst[0] = 0.0

    @pl.when(scanned % nt == 0)
    def _():
        _load_state(s0_ref, h0_ref, c0_ref, st_scr, hg_scr, xpad)

    def project_steps(uh_w, us_w, b_w, worst_w):
        hn = _rms(h1_ref[...], nm_ref[...]).astype(BF16)
        for c0 in range(0, D_IN_PAD, PROJ_TILE):
            c1 = min(c0 + PROJ_TILE, D_IN_PAD)
            yield
            tile = _dot(hn, win_ref[:, c0:c1])
            if c0 < COL_Z:
                uh_w[:, c0:c1] = tile
            else:
                us_w[:, c0 - COL_Z:c1 - COL_Z] = tile
            if c0 == COL_F:
                assert c1 == COL_F + HG_WIDTH
                lb = _forget_lower_bound(param_refs[0][...])
                b, worst = _hgrn_log_decay(0, tb, tile, lb, param_refs[N_MIXER_PARAMS - 2][...])
                b_w[...] = b
                worst_w[0] = worst

    def body(write, read):
        uh_r, us_r, b_r, worst_r = read
        _interleave(_mixer_steps(0, tb, uh_r, us_r, 0, param_refs, mix_ref, st_scr, hg_scr, xpad, o_scr,
                                 (b_r, worst_r[0])),
                    project_steps(*write), PROJ_STEPS_AFTER_SCAN_STEP)

    pl.when(s % 2 == 0)(lambda: body(even, odd))
    pl.when(s % 2 == 1)(lambda: body(odd, even))

    @pl.when((s >= 1) & (scanned % nt == nt - 1))
    def _():
        _store_state(s_out_ref, h_out_ref, c_out_ref, st_scr, hg_scr, xpad)


def _scan_pipelined(h1, nb, nt, tb, proj, params, s0, h0, c0):
    n_blocks = nb * nt
    last = n_blocks - 1
    lead = tuple(proj) + _scan_params(params, tb)
    shared = lambda shape: pl.BlockSpec((1,) + shape[1:], lambda s: (0,) * len(shape))
    per_seq = lambda shape: pl.BlockSpec((1,) + shape, lambda s: (jnp.maximum(s - 1, 0) // nt,) + (0,) * len(shape))
    u_bufs = [pltpu.VMEM((tb, COL_Z), F32), pltpu.VMEM((tb, D_IN_PAD - COL_Z), F32),
              pltpu.VMEM((tb, HG_WIDTH), F32), pltpu.SMEM((1,), F32)]
    return pl.pallas_call(
        functools.partial(_scan_pipelined_kernel, tb, nt),
        grid=(n_blocks + 1,),
        in_specs=[pl.BlockSpec((tb, D_MODEL), lambda s: (jnp.minimum(s, last), 0))]
        + [_resident(p.shape) for p in lead]
        + [shared(s0.shape), shared(h0.shape), shared(c0.shape)],
        out_specs=[pl.BlockSpec((tb, D_MODEL), lambda s: (jnp.maximum(s - 1, 0), 0)),
                   per_seq(S_SHAPE), per_seq(H_SHAPE), per_seq(C_SHAPE)],
        out_shape=[jax.ShapeDtypeStruct((n_blocks * tb, D_MODEL), BF16),
                   jax.ShapeDtypeStruct((nb,) + S_SHAPE, F32),
                   jax.ShapeDtypeStruct((nb,) + H_SHAPE, F32),
                   jax.ShapeDtypeStruct((nb,) + C_SHAPE, F32)],
        scratch_shapes=u_bufs + u_bufs + [pltpu.VMEM(S_SHAPE, F32), pltpu.VMEM(HG_SHAPE, F32),
                                          pltpu.VMEM((PAD_ROWS + tb, CONV_DIM), F32),
                                          pltpu.VMEM((tb, HG_WIDTH), F32)],
        compiler_params=_compiler_params(1),
        name="proj_mixer",
    )(h1, *lead, s0, h0, c0)


def _scan(u, u_block0, nb, nt, tb, n_pad, params, s0, h0, c0):
    lead = _scan_params(params, tb)
    shared = lambda shape: pl.BlockSpec((1,) + shape[1:], lambda b, t: (0,) * len(shape))
    per_seq = lambda shape: pl.BlockSpec((1,) + shape, lambda b, t: (b,) + (0,) * len(shape))
    scratch = [pltpu.VMEM(S_SHAPE, F32), pltpu.VMEM(HG_SHAPE, F32), pltpu.VMEM((PAD_ROWS + tb, CONV_DIM), F32),
               pltpu.VMEM((tb, HG_WIDTH), F32)]
    return pl.pallas_call(
        functools.partial(_scan_kernel, n_pad, tb),
        grid=(nb, nt),
        in_specs=[pl.BlockSpec((tb, D_IN_PAD), lambda b, t: (u_block0 + b * nt + t, 0))]
        + [_resident(p.shape) for p in lead]
        + [shared(s0.shape), shared(h0.shape), shared(c0.shape)],
        out_specs=[pl.BlockSpec((tb, D_MODEL), lambda b, t: (b * nt + t, 0)),
                   per_seq(S_SHAPE), per_seq(H_SHAPE), per_seq(C_SHAPE)],
        out_shape=[jax.ShapeDtypeStruct((nb * nt * tb, D_MODEL), BF16),
                   jax.ShapeDtypeStruct((nb,) + S_SHAPE, F32),
                   jax.ShapeDtypeStruct((nb,) + H_SHAPE, F32),
                   jax.ShapeDtypeStruct((nb,) + C_SHAPE, F32)],
        scratch_shapes=scratch,
        compiler_params=_compiler_params(2),
        name="mixer",
    )(u, *lead, s0, h0, c0)


def _sample_kernel(nbs, u_ref, lbl_ref, hgn_ref, cw_ref, cb_ref, dtb_ref, alog_ref, dsk_ref, ssn_ref,
                   sh_ref, ss_ref, sc_ref, mix_ref, sh_out_ref, ss_out_ref, sc_out_ref, obuf, ybuf):
    lb = _forget_lower_bound(lbl_ref[...])
    fz = u_ref[:, COL_F:COL_F + HG_WIDTH]
    f = lb + (1.0 - lb) * jax.nn.sigmoid(fz)
    kk = (1.0 - lb) * jax.nn.sigmoid(-fz)
    q = _silu(u_ref[:, COL_Q:COL_Q + HG_WIDTH])
    v = u_ref[:, COL_I:COL_I + HG_WIDTH]

    xbc = u_ref[:, COL_XBC:COL_XBC + CONV_DIM]
    cw = cw_ref[...]
    conv = cb_ref[...] + cw[CONV_WIDTH - 1:CONV_WIDTH, :] * xbc
    for j in range(CONV_WIDTH - 1):
        conv = conv + cw[j:j + 1, :] * sc_ref[j]
    for j in range(CONV_WIDTH - 2):
        sc_out_ref[j] = sc_ref[j + 1]
    sc_out_ref[CONV_WIDTH - 2] = xbc
    act = _silu(conv)
    xs = act[:, 0:SSM_WIDTH]
    bm = act[:, SSM_WIDTH:SSM_WIDTH + SSM_GROUPS * SSM_STATE]
    cm = act[:, SSM_WIDTH + SSM_GROUPS * SSM_STATE:]
    dt = jax.nn.softplus(u_ref[:, COL_DT:COL_DT + LANES] + dtb_ref[...])
    d_a = jnp.exp(dt * (-jnp.exp(alog_ref[...])))

    top_half = lax.broadcasted_iota(jnp.int32, (LANES, LANES), 0) < SSM_HEAD_DIM
    q_bf, cm_bf = q.astype(BF16), cm.astype(BF16)
    for j in range(nbs):
        row = slice(j, j + 1)
        for h in range(HG_HEADS):
            sl = slice(h * HG_DIM, (h + 1) * HG_DIM)
            s_new = sh_ref[j, h] * _column_tile(f[row, sl]) + _column_tile(kk[row, sl]) * v[row, sl]
            sh_out_ref[j, h] = s_new
            obuf[row, sl] = _dot(q_bf[:, sl], s_new.astype(BF16))[row]
        for rp in range(SSM_HEADS // 2):
            r0, r1 = 2 * rp, 2 * rp + 1
            g = r0 // HEADS_PER_GROUP
            sl = slice(rp * LANES, (rp + 1) * LANES)
            gsl = slice(g * SSM_STATE, (g + 1) * SSM_STATE)
            h2 = jnp.concatenate([ss_ref[j, r0], ss_ref[j, r1]], axis=0)
            da2 = jnp.where(top_half, d_a[row, r0:r0 + 1], d_a[row, r1:r1 + 1])
            dt2 = jnp.where(top_half, dt[row, r0:r0 + 1], dt[row, r1:r1 + 1])
            h_new = h2 * da2 + (dt2 * _column_tile(xs[row, sl])) * bm[row, gsl]
            ss_out_ref[j, r0] = h_new[0:SSM_HEAD_DIM]
            ss_out_ref[j, r1] = h_new[SSM_HEAD_DIM:]
            ybuf[row, sl] = _dot_nt(cm_bf[:, gsl], h_new.astype(BF16))[row]

    o = obuf[...]
    gate = _silu(u_ref[:, COL_G:COL_G + HG_WIDTH])
    hgn = hgn_ref[...]
    for h in range(HG_HEADS):
        sl = slice(h * HG_DIM, (h + 1) * HG_DIM)
        oh = o[:, sl]
        oh = oh * lax.rsqrt(jnp.mean(oh * oh, axis=-1, keepdims=True) + EPS)
        mix_ref[:, sl] = oh * hgn[:, sl] * gate[:, sl]
    yz = (ybuf[...] + dsk_ref[...] * xs) * _silu(u_ref[:, COL_Z:COL_Z + SSM_WIDTH])
    ssn = ssn_ref[...]
    for g in range(SSM_GROUPS):
        sl = slice(g * GROUP_WIDTH, (g + 1) * GROUP_WIDTH)
        seg = yz[:, sl]
        seg = seg * lax.rsqrt(jnp.mean(seg * seg, axis=-1, keepdims=True) + EPS) * ssn[:, sl]
        mix_ref[:, HG_WIDTH + g * GROUP_WIDTH:HG_WIDTH + (g + 1) * GROUP_WIDTH] = seg


def _sample_mixer(u, u_block0, n, nbs, params, sh, ss, sc_t):
    small = [_resident(p.shape) for p in params]
    return pl.pallas_call(
        functools.partial(_sample_kernel, nbs),
        grid=(n // nbs,),
        in_specs=[pl.BlockSpec((nbs, D_IN_PAD), lambda i: (u_block0 + i, 0))] + small
        + [pl.BlockSpec((nbs,) + sh.shape[1:], lambda i: (i, 0, 0, 0)),
           pl.BlockSpec((nbs,) + ss.shape[1:], lambda i: (i, 0, 0, 0)),
           pl.BlockSpec((CONV_WIDTH - 1, nbs, CONV_DIM), lambda i: (0, i, 0))],
        out_specs=[pl.BlockSpec((nbs, D_MODEL), lambda i: (i, 0)),
                   pl.BlockSpec((nbs,) + sh.shape[1:], lambda i: (i, 0, 0, 0)),
                   pl.BlockSpec((nbs,) + ss.shape[1:], lambda i: (i, 0, 0, 0)),
                   pl.BlockSpec((CONV_WIDTH - 1, nbs, CONV_DIM), lambda i: (0, i, 0))],
        out_shape=[jax.ShapeDtypeStruct((n, D_MODEL), F32),
                   jax.ShapeDtypeStruct(sh.shape, F32),
                   jax.ShapeDtypeStruct(ss.shape, F32),
                   jax.ShapeDtypeStruct(sc_t.shape, F32)],
        scratch_shapes=[pltpu.VMEM((nbs, HG_WIDTH), F32), pltpu.VMEM((nbs, SSM_WIDTH), F32)],
        compiler_params=_compiler_params(1),
        name="sample_mixer",
    )(u, *params, sh, ss, sc_t)


TM_DENSE = 512
TM_SCAN = 512
TM_SMALL = 128
SAMPLES_PER_STEP = 8


def _pad_lanes(row, value=0.0):
    return jnp.pad(row, ((0, 0), (0, LANES - row.shape[1])), constant_values=value)


def kernel(x_prompt, x_sample, state_hgrn, state_ssm, state_conv, meta_tokens, lb_logits, norm_ffn1, w_ffn1_gate, w_ffn1_up, w_ffn1_down, norm_mix, w_in, hg_norm, conv_w, conv_b, dt_bias, a_log, d_skip, ssm_norm, w_out, norm_ffn2, w_ffn2_gate, w_ffn2_up, w_ffn2_down, norm_final):
    bp, seq_p, _ = x_prompt.shape
    n_s = x_sample.shape[0]
    assert x_sample.shape[1] == 1 and n_s == TM_SMALL and seq_p % TM_SCAN == 0 and (bp * seq_p) % TM_DENSE == 0
    layer = 0

    n1, nm, n2 = norm_ffn1[layer][None], norm_mix[layer][None], norm_ffn2[layer][None]
    nf = norm_final[None]
    mixer_params = (lb_logits, hg_norm[layer][None], conv_w[layer], conv_b[layer][None],
                    _pad_lanes(dt_bias[layer][None]), _pad_lanes(a_log[layer][None]),
                    jnp.repeat(d_skip[layer], SSM_HEAD_DIM)[None], ssm_norm[layer][None])

    n_pad = TM_SMALL - N_META
    x_small = jnp.concatenate([jnp.zeros((n_pad, D_MODEL), F32), meta_tokens, x_sample[:, 0]], axis=0)
    h1_small, wg1, wu1, wd1 = _ffn1_small(x_small, n1, w_ffn1_gate[layer], w_ffn1_up[layer], w_ffn1_down[layer])
    u_small, win = _proj_small(h1_small, nm, jnp.swapaxes(w_in[layer], 0, 1))

    zeros_s = jnp.zeros((1,) + S_SHAPE, F32)
    zeros_h = jnp.zeros((1,) + H_SHAPE, F32)
    zeros_c = jnp.zeros((1,) + C_SHAPE, F32)
    _, s_meta, h_meta, c_meta = _scan(u_small, 0, 1, 1, TM_SMALL, n_pad, mixer_params, zeros_s, zeros_h, zeros_c)

    sc_t = jnp.swapaxes(state_conv[layer], 0, 1)
    mix_s, hgrn_s, ssm_s, conv_s_t = _sample_mixer(
        u_small, TM_SMALL // SAMPLES_PER_STEP, n_s, SAMPLES_PER_STEP, mixer_params,
        state_hgrn[layer], state_ssm[layer], sc_t)

    xp = x_prompt.reshape(bp * seq_p, D_MODEL)
    h1_p, wg2, wu2, wd2, wo = _ffn1(xp, n1, wg1, wu1, wd1, TM_DENSE, w_ffn2_gate[layer], w_ffn2_up[layer],
                                    w_ffn2_down[layer], w_out[layer])
    y_s = _ffn_out(h1_small, 1, mix_s, wo, n2, wg2, wu2, wd2, nf, TM_SMALL)
    mix_p, hgrn_p, ssm_p, conv_p = _scan_pipelined(h1_p, bp, seq_p // TM_SCAN, TM_SCAN, (nm, win), mixer_params,
                                                   s_meta, h_meta, c_meta)
    y_p = _ffn_out(h1_p, 0, mix_p, wo, n2, wg2, wu2, wd2, nf, TM_DENSE)

    keep = slice(PAD_ROWS - (CONV_WIDTH - 1), PAD_ROWS)
    return (y_p.reshape(bp, seq_p, D_MODEL),
            y_s.reshape(n_s, 1, D_MODEL),
            hgrn_p[None], ssm_p[None], conv_p[:, keep][None],
            hgrn_s[None], ssm_s[None], jnp.swapaxes(conv_s_t, 0, 1)[None])
```
